```python
import jax
import jax.numpy as jnp
from jax import lax
import numpy as np

D_MODEL = 4096
BATCH = 4
SEQ = 4096
DEPTH = 4

CTX_LEN = 256
GRID_W = 64
N_MIXERS = 3
N_GLA = len(range(0, DEPTH, N_MIXERS))
N_MLA = len(range(1, DEPTH, N_MIXERS))
N_RWKV = len(range(2, DEPTH, N_MIXERS))
N_MOD = 6
ADA_RANK = 512
LN_EPS = 1e-6
DEEPNORM_ALPHA = (2.0 * DEPTH) ** 0.25
DEEPNORM_BETA = (8.0 * DEPTH) ** -0.25

GLA_HEADS = 8
GLA_QK = D_MODEL // 2
GLA_DK = GLA_QK // GLA_HEADS
GLA_DV = D_MODEL // GLA_HEADS
GLA_GATE_RANK = 16
GLA_GATE_NORMALIZER = 16.0
GLA_CHUNK = 64
GLA_IN = 2 * GLA_QK + 2 * D_MODEL

MLA_HEADS = D_MODEL // 128
MLA_Q_RANK = 1536
MLA_KV_RANK = 512
MLA_NOPE = 128
MLA_ROPE = 64
MLA_V = 128
MLA_IN = MLA_Q_RANK + MLA_KV_RANK + MLA_ROPE
MLA_QBLOCK = 128
MLA_SCALE = (MLA_NOPE + MLA_ROPE) ** -0.5
ROPE_THETA = 10000.0

RWKV_HEAD = 64
RWKV_HEADS = D_MODEL // RWKV_HEAD
RWKV_DECAY_RANK = max(32, int(round(1.8 * D_MODEL ** 0.5 / 32)) * 32)
RWKV_A_RANK = RWKV_DECAY_RANK
RWKV_GATE_RANK = max(32, int(round(0.6 * D_MODEL ** 0.8 / 32)) * 32)
RWKV_GN_EPS = 64e-5

PEER_HEADS = 8
PEER_NKEYS = 128
PEER_EXPERTS = PEER_NKEYS * PEER_NKEYS
PEER_DKEY = 256
PEER_TOPK = 16
PEER_BLOCK = 128

F32 = jnp.float32

kernel_name = "hybrid_gla_mla_rwkv7_peer_prefix_dit"


def layer_norm(z):
    zf = z.astype(F32)
    zc = zf - jnp.mean(zf, -1, keepdims=True)
    return (zc * lax.rsqrt(jnp.mean(zc * zc, -1, keepdims=True) + LN_EPS)).astype(z.dtype)


def rms_norm(z, gain, eps=1e-6):
    zf = z.astype(F32)
    return (zf * lax.rsqrt(jnp.mean(zf * zf, -1, keepdims=True) + eps) * gain).astype(z.dtype)


def ada_mod(cond, w_down, w_up, b_up):
    m = (jax.nn.silu(cond) @ w_down) @ w_up + b_up
    return m.reshape(m.shape[:-1] + (N_MOD, D_MODEL))


def modulate(z, m, idx):
    return z * (1.0 + m[:, idx + 1][:, None]) + m[:, idx][:, None]


def axial_rope(rows):
    row = jnp.repeat(jnp.arange(rows, dtype=F32), GRID_W)
    col = jnp.tile(jnp.arange(GRID_W, dtype=F32), rows)
    n_freq = MLA_ROPE // 4
    inv = ROPE_THETA ** (-jnp.arange(n_freq, dtype=F32) / n_freq)
    ang = jnp.concatenate([row[:, None] * inv, col[:, None] * inv], -1)
    return jnp.cos(ang), jnp.sin(ang)


def apply_rope(z, cos, sin):
    zp = z.reshape(z.shape[:-1] + (-1, 2)).astype(F32)
    z0, z1 = zp[..., 0], zp[..., 1]
    out = jnp.stack([z0 * cos - z1 * sin, z0 * sin + z1 * cos], -1)
    return out.reshape(z.shape).astype(z.dtype)


def gla_project(z, w_in, w_g1, w_g2, b_g):
    B, L, _ = z.shape
    q, k, v, r = jnp.split(z @ w_in, [GLA_QK, 2 * GLA_QK, 2 * GLA_QK + D_MODEL], axis=-1)

    def heads(t, d):
        return t.reshape(B, L, GLA_HEADS, d).transpose(0, 2, 1, 3)

    gl = jnp.einsum('sblr,srk->sblk', jnp.einsum('bld,sdr->sblr', z, w_g1), w_g2) + b_g[:, None, None, :]
    logdecay = jax.nn.log_sigmoid(gl.astype(F32)) / GLA_GATE_NORMALIZER
    gates = (heads(logdecay[0], GLA_DK), heads(logdecay[1], GLA_DK))
    return heads(q, GLA_DK) * GLA_DK ** -0.5, heads(k, GLA_DK), heads(v, GLA_DV), r, gates


def gla_chunk_scan(q, k, v, g, s0):
    B, H, L, _ = q.shape
    dv = v.shape[-1]
    n = L // GLA_CHUNK
    pos = jnp.arange(GLA_CHUNK)
    tri = (pos[:, None] >= pos[None, :])[:, :, None]

    def chunks(t):
        return jnp.moveaxis(t.astype(F32).reshape(B, H, n, GLA_CHUNK, t.shape[-1]), 2, 0)

    def step(s, inp):
        qc, kc, vc, gc = inp
        b = jnp.cumsum(gc, axis=2)
        b_last = b[:, :, -1]
        diff = b[:, :, :, None, :] - b[:, :, None, :, :]
        decay = jnp.where(tri, jnp.exp(jnp.minimum(diff, 0.0)), 0.0)
        att = jnp.einsum('bhid,bhjd,bhijd->bhij', qc, kc, decay)
        o = jnp.einsum('bhij,bhjv->bhiv', att, vc) + jnp.einsum('bhid,bhdv->bhiv', qc * jnp.exp(b), s)
        s = s * jnp.exp(b_last)[..., None] + jnp.einsum('bhjd,bhjv->bhdv', kc * jnp.exp(b_last[:, :, None] - b), vc)
        return s, o

    s, o = lax.scan(step, s0, (chunks(q), chunks(k), chunks(v), chunks(g)))
    return jnp.moveaxis(o, 0, 2).reshape(B, H, L, dv), s


def gla_bidir(q, k, v, gates, s_init):
    rev = lambda t: jnp.flip(t, axis=2)
    o_f, s_f = gla_chunk_scan(q, k, v, gates[0], s_init[0])
    o_b, s_b = gla_chunk_scan(rev(q), rev(k), rev(v), rev(gates[1]), s_init[1])
    return o_f + rev(o_b), (s_f, s_b)


def gla_out(o, r, norm_g, w_o):
    B, _, L, _ = o.shape
    o = rms_norm(o.transpose(0, 2, 1, 3), norm_g).reshape(B, L, D_MODEL).astype(r.dtype)
    return (o * jax.nn.silu(r)) @ w_o


def gla_mixer(u, uc, w_in, w_g1, w_g2, b_g, norm_g, w_o, need_ctx):
    zeros = jnp.zeros((u.shape[0], GLA_HEADS, GLA_DK, GLA_DV), F32)
    cq, ck, cv, cr, cg = gla_project(uc, w_in, w_g1, w_g2, b_g)
    co, ctx_states = gla_bidir(cq, ck, cv, cg, (zeros, zeros))
    q, k, v, r, g = gla_project(u, w_in, w_g1, w_g2, b_g)
    o, _ = gla_bidir(q, k, v, g, ctx_states)
    yc = gla_out(co, cr, norm_g, w_o) if need_ctx else None
    return gla_out(o, r, norm_g, w_o), yc


def mla_attend(qn, qr, kn, kr, v):
    s = jnp.einsum('bqhd,bkhd->bhqk', qn, kn) + jnp.einsum('bqhr,bkr->bhqk', qr, kr)
    p = jax.nn.softmax(s.astype(F32) * MLA_SCALE, axis=-1).astype(v.dtype)
    return jnp.einsum('bhqk,bkhd->bqhd', p, v)


def mla_mixer(u, uc, w_in, q_norm, kv_norm, w_uq, w_ukv, w_o, cos, sin, need_ctx):
    def latents(z):
        h = z @ w_in
        return (rms_norm(h[..., :MLA_Q_RANK], q_norm),
                rms_norm(h[..., MLA_Q_RANK:MLA_Q_RANK + MLA_KV_RANK], kv_norm),
                h[..., MLA_Q_RANK + MLA_KV_RANK:])

    def queries(cq):
        q = (cq @ w_uq).reshape(cq.shape[:2] + (MLA_HEADS, MLA_NOPE + MLA_ROPE))
        return q[..., :MLA_NOPE], q[..., MLA_NOPE:]

    def keys_values(ckv):
        kv = (ckv @ w_ukv).reshape(ckv.shape[:2] + (MLA_HEADS, MLA_NOPE + MLA_V))
        return kv[..., :MLA_NOPE], kv[..., MLA_NOPE:]

    B, T, _ = u.shape
    cq_x, ckv_x, kr_x = latents(u)
    qn, qr = queries(cq_x)
    qr = apply_rope(qr, cos[:, None], sin[:, None])
    kr_x = apply_rope(kr_x, cos, sin)
    kn_x, v_x = keys_values(ckv_x)
    cq_c, ckv_c, kr_c = latents(uc)
    kn_c, v_c = keys_values(ckv_c)
    kn = jnp.concatenate([kn_c, kn_x], 1)
    kr = jnp.concatenate([kr_c, kr_x], 1)
    v = jnp.concatenate([v_c, v_x], 1)
    nblk = T // MLA_QBLOCK
    blocks = lambda t: jnp.moveaxis(t.reshape((B, nblk, MLA_QBLOCK) + t.shape[2:]), 1, 0)
    o = lax.map(lambda qb: mla_attend(qb[0], qb[1], kn, kr, v), (blocks(qn), blocks(qr)))
    y = jnp.moveaxis(o, 0, 1).reshape(B, T, MLA_HEADS * MLA_V) @ w_o
    yc = None
    if need_ctx:
        qn_c, qr_c = queries(cq_c)
        yc = mla_attend(qn_c, qr_c, kn_c, kr_c, v_c).reshape(B, uc.shape[1], MLA_HEADS * MLA_V) @ w_o
    return y, yc


def centred_shift(z):
    zp = jnp.pad(z, ((0, 0), (1, 1), (0, 0)))
    return 0.5 * (zp[:, :-2] + zp[:, 2:])


def rwkv_project(z, mu, w_r, w_k, w_v, w0, w1, w2, a0, a1, a2, g1, g2, k_k, k_a):
    B, L, _ = z.shape
    heads = lambda t: t.reshape(B, L, RWKV_HEADS, RWKV_HEAD)
    xx = centred_shift(z) - z
    xr, xw, xk, xv, xa, xg = (z + xx * mu[n] for n in range(6))
    r = heads(xr @ w_r)
    k = xk @ w_k
    v = heads(xv @ w_v)
    g = jax.nn.sigmoid(xg @ g1) @ g2
    kk = heads(k * k_k).astype(F32)
    kk = kk * lax.rsqrt(jnp.sum(kk * kk, -1, keepdims=True) + 1e-12)
    dirs = []
    for s in range(2):
        w_log = -jax.nn.softplus(-(w0[s] + jnp.tanh(xw @ w1[s]) @ w2[s]).astype(F32)) - 0.5
        a = jax.nn.sigmoid((a0[s] + (xa @ a1[s]) @ a2[s]).astype(F32))
        k_s = heads(k * (1.0 + (a - 1.0) * k_a))
        dirs.append((heads(jnp.exp(-jnp.exp(w_log))), k_s, -kk, kk * heads(a)))
    return r, v, g, dirs


def rwkv_scan(r, w, k, v, a, b, s0):
    def step(s, inp):
        rt, wt, kt, vt, at, bt = inp
        sa = jnp.einsum('bhvk,bhk->bhv', s, at)
        s = s * wt[:, :, None, :] + sa[..., None] * bt[:, :, None, :] + vt[..., None] * kt[:, :, None, :]
        return s, jnp.einsum('bhvk,bhk->bhv', s, rt)

    xs = tuple(jnp.moveaxis(t.astype(F32), 1, 0) for t in (r, w, k, v, a, b))
    s, y = lax.scan(step, s0, xs)
    return jnp.moveaxis(y, 0, 1), s


def rwkv_bidir(r, v, dirs, s_init):
    ys, states = [], []
    for s, (w, k, a, b) in enumerate(dirs):
        seq = (r, w, k, v, a, b)
        if s == 1:
            seq = tuple(jnp.flip(t, axis=1) for t in seq)
        y, st = rwkv_scan(*seq, s_init[s])
        ys.append(jnp.flip(y, axis=1) if s == 1 else y)
        states.append(st)
    return ys[0] + ys[1], states


def rwkv_out(y, r, v, g, dirs, r_k, ln_g, ln_b, w_o):
    B, L, H, N = y.shape
    yc = y - jnp.mean(y, -1, keepdims=True)
    yn = yc * lax.rsqrt(jnp.mean(yc * yc, -1, keepdims=True) + RWKV_GN_EPS)
    yn = yn * ln_g.reshape(H, N) + ln_b.reshape(H, N)
    bonus = jnp.sum(r * (dirs[0][1] + dirs[1][1]) * r_k, -1, keepdims=True) * v
    return ((yn + bonus).reshape(B, L, D_MODEL).astype(g.dtype) * g) @ w_o


def rwkv_mixer(u, uc, mu, w_r, w_k, w_v, w_o, w0, w1, w2, a0, a1, a2, g1, g2, k_k, k_a, r_k, ln_g, ln_b, need_ctx):
    proj = lambda z: rwkv_project(z, mu, w_r, w_k, w_v, w0, w1, w2, a0, a1, a2, g1, g2, k_k, k_a)
    zeros = jnp.zeros((u.shape[0], RWKV_HEADS, RWKV_HEAD, RWKV_HEAD), F32)
    cr, cv, cg, cdirs = proj(uc)
    cy, ctx_states = rwkv_bidir(cr, cv, cdirs, (zeros, zeros))
    r, v, g, dirs = proj(u)
    y, _ = rwkv_bidir(r, v, dirs, ctx_states)
    yc = rwkv_out(cy, cr, cv, cg, cdirs, r_k, ln_g, ln_b, w_o) if need_ctx else None
    return rwkv_out(y, r, v, g, dirs, r_k, ln_g, ln_b, w_o), yc


def peer_ffn(z, w_q, sub_keys, u_tab, v_tab):
    B, L, D = z.shape

    def block(zb):
        q = (zb @ w_q).reshape(PEER_BLOCK, PEER_HEADS, 2, PEER_DKEY // 2)
        s = jnp.einsum('thpk,pnk->thpn', q, sub_keys).astype(F32)
        half_s, half_i = lax.top_k(s, PEER_TOPK)
        cand_s = (half_s[:, :, 0, :, None] + half_s[:, :, 1, None, :]).reshape(PEER_BLOCK, PEER_HEADS, -1)
        cand_e = (half_i[:, :, 0, :, None] * PEER_NKEYS + half_i[:, :, 1, None, :]).reshape(PEER_BLOCK, PEER_HEADS, -1)
        best_s, best_j = lax.top_k(cand_s, PEER_TOPK)
        expert = jnp.take_along_axis(cand_e, best_j, axis=-1)
        gate = jax.nn.softmax(best_s, axis=-1).astype(zb.dtype)
        act = jax.nn.gelu(jnp.einsum('td,thkd->thk', zb, u_tab[expert]), approximate=False)
        return jnp.einsum('thk,thkd->td', gate * act, v_tab[expert])

    return lax.map(block, z.reshape(-1, PEER_BLOCK, D)).reshape(B, L, D)


def setup_inputs(seed: int = 0) -> dict:
    key = jax.random.key(seed)
    ks = iter(jax.random.split(key, 48))

    def nrm(shape, scale):
        return jax.random.normal(next(ks), shape, F32) * scale

    def uni(shape, lo, hi):
        return jax.random.uniform(next(ks), shape, F32, lo, hi)

    D = D_MODEL
    ist = D ** -0.5
    beta = DEEPNORM_BETA
    return {
        'x': nrm((BATCH, SEQ, D), 1.0),
        'c': nrm((BATCH, D), 1.0),
        'ctx': nrm((BATCH, CTX_LEN, D), 1.0),
        'c_ctx': nrm((D,), 1.0),
        'ada_w_down': nrm((DEPTH, D, ADA_RANK), ist),
        'ada_w_up': nrm((DEPTH, ADA_RANK, N_MOD * D), 0.7 * ADA_RANK ** -0.5),
        'ada_b': nrm((DEPTH, N_MOD * D), 0.02),
        'gla_w_in': nrm((N_GLA, D, GLA_IN), ist),
        'gla_w_g1': nrm((N_GLA, 2, D, GLA_GATE_RANK), ist),
        'gla_w_g2': nrm((N_GLA, 2, GLA_GATE_RANK, GLA_QK), GLA_GATE_RANK ** -0.5),
        'gla_b_g': nrm((N_GLA, 2, GLA_QK), 0.1),
        'gla_norm_g': 1.0 + nrm((N_GLA, GLA_DV), 0.05),
        'gla_w_o': nrm((N_GLA, D, D), ist * beta),
        'mla_w_in': nrm((N_MLA, D, MLA_IN), ist),
        'mla_q_norm': 1.0 + nrm((N_MLA, MLA_Q_RANK), 0.05),
        'mla_kv_norm': 1.0 + nrm((N_MLA, MLA_KV_RANK), 0.05),
        'mla_w_uq': nrm((N_MLA, MLA_Q_RANK, MLA_HEADS * (MLA_NOPE + MLA_ROPE)), MLA_Q_RANK ** -0.5),
        'mla_w_ukv': nrm((N_MLA, MLA_KV_RANK, MLA_HEADS * (MLA_NOPE + MLA_V)), MLA_KV_RANK ** -0.5),
        'mla_w_o': nrm((N_MLA, MLA_HEADS * MLA_V, D), (MLA_HEADS * MLA_V) ** -0.5 * beta),
        'rwkv_mu': uni((N_RWKV, 6, D), 0.0, 1.0),
        'rwkv_w_r': nrm((N_RWKV, D, D), ist),
        'rwkv_w_k': nrm((N_RWKV, D, D), ist),
        'rwkv_w_v': nrm((N_RWKV, D, D), ist),
        'rwkv_w_o': nrm((N_RWKV, D, D), ist * beta),
        'rwkv_w0': uni((N_RWKV, 2, D), -6.0, -1.0),
        'rwkv_w1': nrm((N_RWKV, 2, D, RWKV_DECAY_RANK), ist),
        'rwkv_w2': nrm((N_RWKV, 2, RWKV_DECAY_RANK, D), 0.5 * RWKV_DECAY_RANK ** -0.5),
        'rwkv_a0': nrm((N_RWKV, 2, D), 0.1),
        'rwkv_a1': nrm((N_RWKV, 2, D, RWKV_A_RANK), ist),
        'rwkv_a2': nrm((N_RWKV, 2, RWKV_A_RANK, D), 0.5 * RWKV_A_RANK ** -0.5),
        'rwkv_g1': nrm((N_RWKV, D, RWKV_GATE_RANK), ist),
        'rwkv_g2': nrm((N_RWKV, RWKV_GATE_RANK, D), RWKV_GATE_RANK ** -0.5),
        'rwkv_k_k': 0.85 + nrm((N_RWKV, D), 0.05),
        'rwkv_k_a': 1.0 + nrm((N_RWKV, D), 0.05),
        'rwkv_r_k': nrm((N_RWKV, RWKV_HEADS, RWKV_HEAD), 0.1),
        'rwkv_ln_g': 1.0 + nrm((N_RWKV, D), 0.05),
        'rwkv_ln_b': nrm((N_RWKV, D), 0.02),
        'peer_w_q': nrm((DEPTH, D, PEER_HEADS * PEER_DKEY), ist),
        'peer_sub_keys': nrm((DEPTH, 2, PEER_NKEYS, PEER_DKEY // 2), (PEER_DKEY // 2) ** -0.5),
        'peer_u': nrm((DEPTH, PEER_EXPERTS, D), ist),
        'peer_v': nrm((DEPTH, PEER_EXPERTS, D), beta),
    }


def reference(x, c, ctx, c_ctx, ada_w_down, ada_w_up, ada_b,
              gla_w_in, gla_w_g1, gla_w_g2, gla_b_g, gla_norm_g, gla_w_o,
              mla_w_in, mla_q_norm, mla_kv_norm, mla_w_uq, mla_w_ukv, mla_w_o,
              rwkv_mu, rwkv_w_r, rwkv_w_k, rwkv_w_v, rwkv_w_o, rwkv_w0, rwkv_w1, rwkv_w2,
              rwkv_a0, rwkv_a1, rwkv_a2, rwkv_g1, rwkv_g2, rwkv_k_k, rwkv_k_a, rwkv_r_k, rwkv_ln_g, rwkv_ln_b,
              peer_w_q, peer_sub_keys, peer_u, peer_v):
    rows = x.shape[1] // GRID_W
    cos, sin = axial_rope(rows)
    n_ctx = ctx.shape[1]
    for i in range(DEPTH):
        need_ctx = i < DEPTH - 1
        m_x = ada_mod(c, ada_w_down[i], ada_w_up[i], ada_b[i])
        m_c = ada_mod(c_ctx[None], ada_w_down[i], ada_w_up[i], ada_b[i])
        u, uc = modulate(x, m_x, 0), modulate(ctx, m_c, 0)
        j = i // N_MIXERS
        if i % N_MIXERS == 0:
            y, yc = gla_mixer(u, uc, gla_w_in[j], gla_w_g1[j], gla_w_g2[j], gla_b_g[j], gla_norm_g[j], gla_w_o[j], need_ctx)
        elif i % N_MIXERS == 1:
            y, yc = mla_mixer(u, uc, mla_w_in[j], mla_q_norm[j], mla_kv_norm[j], mla_w_uq[j], mla_w_ukv[j], mla_w_o[j], cos, sin, need_ctx)
        else:
            y, yc = rwkv_mixer(u, uc, rwkv_mu[j], rwkv_w_r[j], rwkv_w_k[j], rwkv_w_v[j], rwkv_w_o[j],
                               rwkv_w0[j], rwkv_w1[j], rwkv_w2[j], rwkv_a0[j], rwkv_a1[j], rwkv_a2[j],
                               rwkv_g1[j], rwkv_g2[j], rwkv_k_k[j], rwkv_k_a[j], rwkv_r_k[j], rwkv_ln_g[j], rwkv_ln_b[j], need_ctx)
        x = layer_norm(DEEPNORM_ALPHA * x + m_x[:, 2][:, None] * y)
        if need_ctx:
            ctx = layer_norm(DEEPNORM_ALPHA * ctx + m_c[:, 2][:, None] * yc)
            h = peer_ffn(jnp.concatenate([modulate(ctx, m_c, 3), modulate(x, m_x, 3)], axis=1),
                         peer_w_q[i], peer_sub_keys[i], peer_u[i], peer_v[i])
            hc, hx = h[:, :n_ctx], h[:, n_ctx:]
            ctx = layer_norm(DEEPNORM_ALPHA * ctx + m_c[:, 5][:, None] * hc)
        else:
            hx = peer_ffn(modulate(x, m_x, 3), peer_w_q[i], peer_sub_keys[i], peer_u[i], peer_v[i])
        x = layer_norm(DEEPNORM_ALPHA * x + m_x[:, 5][:, None] * hx)
    return x
```

```python
import functools

import jax
import jax.numpy as jnp
from jax import lax
from jax.experimental import pallas as pl
from jax.experimental.pallas import tpu as pltpu

F32 = jnp.float32
BF16 = jnp.bfloat16

DEPTH = 4
CTX_LEN = 256
GRID_W = 64
N_MIXERS = 3
N_MOD = 6
LN_EPS = 1e-6
DEEPNORM_ALPHA = (2.0 * DEPTH) ** 0.25

GLA_HEADS = 8
GLA_GATE_RANK = 16
GLA_GATE_NORMALIZER = 16.0

MLA_Q_RANK = 1536
MLA_KV_RANK = 512
MLA_NOPE = 128
MLA_ROPE = 64
MLA_V = 128
ROPE_THETA = 10000.0

RWKV_HEAD = 64
RWKV_GN_EPS = 64e-5

PEER_HEADS = 8
PEER_NKEYS = 128
PEER_DKEY = 256
PEER_TOPK = 16

LANES = 128
SUBLANES = 8
VMEM_LIMIT = 52 * 1024 * 1024

GLA_CHUNK = 64
GLA_SUB = 16
RWKV_CHUNK = 64
RWKV_SUB = 16
NEG = -1e30


def _pick(n, cands):
    for c in cands:
        if n % c == 0:
            return c
    return n


def _cparams(*sem):
    return pltpu.CompilerParams(dimension_semantics=sem, vmem_limit_bytes=VMEM_LIMIT)


def _dot(a, b):
    return jnp.dot(a, b, preferred_element_type=F32)


def _dot_nt(a, b):
    return lax.dot_general(a, b, (((1,), (1,)), ((), ())), preferred_element_type=F32)


def _dot_tn(a, b):
    return lax.dot_general(a, b, (((0,), (0,)), ((), ())), preferred_element_type=F32)


def _split3(x):
    hi = x.astype(BF16)
    r1 = x - hi.astype(F32)
    mid = r1.astype(BF16)
    lo = (r1 - mid.astype(F32)).astype(BF16)
    return hi, mid, lo


def _dot_sel(sel, x):
    s = sel.astype(BF16)
    hi, mid, lo = _split3(x)
    return _dot(s, hi) + _dot(s, mid) + _dot(s, lo)


def _dot_x3(a, b, dims=None):
    ah = a.astype(BF16)
    al = (a - ah.astype(F32)).astype(BF16)
    bh = b.astype(BF16)
    bl = (b - bh.astype(F32)).astype(BF16)
    f = _dot if dims is None else dims
    return f(ah, bh) + f(ah, bl) + f(al, bh)


def _log_sigmoid(x):
    return jnp.minimum(x, 0.0) - jnp.log1p(jnp.exp(-jnp.abs(x)))


def _sigmoid(x):
    return 1.0 / (1.0 + jnp.exp(-x))


def _mm_body(a_ref, b_ref, o_ref):
    o_ref[...] = _dot(a_ref[...].astype(BF16), b_ref[...].astype(BF16)).astype(o_ref.dtype)


def matmul(a, b, out_dtype=F32):
    m, k = a.shape
    k2, n = b.shape
    assert k == k2
    a_bytes = jnp.dtype(a.dtype).itemsize
    tm_cands = (1024, 512, 256, 128, 64, 32, 16) if a_bytes * k <= 8192 else (512, 256, 128, 64, 32, 16)
    tm = _pick(m, tm_cands)
    tn = _pick(n, (512, 256, 128))
    return pl.pallas_call(
        _mm_body,
        grid=(m // tm, n // tn),
        in_specs=[pl.BlockSpec((tm, k), lambda i, j: (i, 0)),
                  pl.BlockSpec((k, tn), lambda i, j: (0, j))],
        out_specs=pl.BlockSpec((tm, tn), lambda i, j: (i, j)),
        out_shape=jax.ShapeDtypeStruct((m, n), out_dtype),
        compiler_params=_cparams("parallel", "parallel"),
    )(a, b)


def _layer_norm(z):
    zc = z - jnp.mean(z, -1, keepdims=True)
    return zc * lax.rsqrt(jnp.mean(zc * zc, -1, keepdims=True) + LN_EPS)


def _ln_mod_body(*refs, gidx, sidx, has_y, y_t, u_t):
    it = iter(refs)
    z_ref = next(it)
    y_ref = next(it) if has_y else None
    mg_ref = next(it) if has_y else None
    mn_ref = next(it) if sidx is not None else None
    zo_ref = next(it) if has_y else None
    u_ref = next(it) if sidx is not None else None
    z = z_ref[0]
    if has_y:
        y = y_ref[...].T if y_t else y_ref[0]
        z = _layer_norm(DEEPNORM_ALPHA * z + mg_ref[0, 0, gidx:gidx + 1, :] * y)
        zo_ref[0] = z
    if sidx is not None:
        u = z * (1.0 + mn_ref[0, 0, sidx + 1:sidx + 2, :]) + mn_ref[0, 0, sidx:sidx + 1, :]
        if u_t:
            u_ref[...] = u.T.astype(u_ref.dtype)
        else:
            u_ref[0] = u.astype(u_ref.dtype)


def ln_mod(z, y=None, mods_g=None, gidx=None, mods_n=None, sidx=None, y_t=False, u_t=False):
    bsz, t, d = z.shape
    tr = _pick(CTX_LEN, (256, 128, 64, 32, 16))
    nt = t // tr
    nctx = CTX_LEN // tr
    has_y = y is not None
    row = pl.BlockSpec((1, tr, d), lambda b, i: (b, i, 0))
    col = pl.BlockSpec((d, tr), lambda b, i: (0, b * nt + i))
    mod = pl.BlockSpec((1, 1, N_MOD, d), lambda b, i: (b, jnp.where(i < nctx, 0, 1), 0, 0))
    ins, in_specs, outs, out_specs = [z], [row], [], []
    if has_y:
        ins += [y, mods_g]
        in_specs += [col if y_t else row, mod]
        outs.append(jax.ShapeDtypeStruct((bsz, t, d), F32))
        out_specs.append(row)
    if sidx is not None:
        ins.append(mods_n)
        in_specs.append(mod)
        outs.append(jax.ShapeDtypeStruct((d, bsz * t) if u_t else (bsz, t, d), BF16))
        out_specs.append(col if u_t else row)
    res = pl.pallas_call(
        functools.partial(_ln_mod_body, gidx=gidx, sidx=sidx, has_y=has_y, y_t=y_t, u_t=u_t),
        grid=(bsz, nt), in_specs=in_specs, out_specs=out_specs, out_shape=outs,
        compiler_params=_cparams("parallel", "parallel"),
    )(*ins)
    res = list(res)
    zo = res.pop(0) if has_y else None
    u = res.pop(0) if sidx is not None else None
    return zo, u


def _gla_body(q_ref, k_ref, v_ref, gl_ref, wg2_ref, bg_ref, o_ref, st_ref, *, rev):
    c = q_ref.shape[1]
    dk = q_ref.shape[2]
    sub = GLA_SUB
    nsub = c // sub

    @pl.when(pl.program_id(2) == 0)
    def _():
        st_ref[...] = jnp.zeros_like(st_ref)

    q = q_ref[0].astype(F32) * (dk ** -0.5)
    k = k_ref[0].astype(F32)
    v = v_ref[0]
    gl = _dot_x3(gl_ref[0], wg2_ref[...]) + bg_ref[...]
    g = _log_sigmoid(gl) * (1.0 / GLA_GATE_NORMALIZER)
    row = lax.broadcasted_iota(jnp.int32, (c, c), 0)
    col = lax.broadcasted_iota(jnp.int32, (c, c), 1)
    tri = (row <= col) if rev else (row >= col)
    b = _dot_sel(tri.astype(F32), g)
    btot = jnp.sum(g, axis=0, keepdims=True)

    st = st_ref[...]
    o = _dot_nt((q * jnp.exp(b)).astype(BF16), st.astype(BF16))

    rowk = lax.broadcasted_iota(jnp.int32, (c, 1), 0)
    blocks = []
    for i in range(nsub):
        lo, hi = i * sub, (i + 1) * sub
        if (rev and i == nsub - 1) or (not rev and i == 0):
            blocks.append(jnp.zeros((sub, c), F32))
            continue
        ref = b[hi:hi + 1] if rev else b[lo - 1:lo]
        qi = q[lo:hi] * jnp.exp(b[lo:hi] - ref)
        other = (rowk >= hi) if rev else (rowk < lo)
        kj = jnp.where(other, k * jnp.exp(jnp.minimum(ref - b, 0.0)), 0.0)
        blocks.append(_dot_nt(qi.astype(BF16), kj.astype(BF16)))
    att = jnp.concatenate(blocks, axis=0)

    rmod = rowk % sub
    for lag in range(sub):
        sh = (c - lag) % c if rev else lag
        ks = pltpu.roll(k, sh, 0) if sh else k
        bs = pltpu.roll(b, sh, 0) if sh else b
        term = jnp.sum(q * ks * jnp.exp(jnp.minimum(b - bs, 0.0)), axis=1, keepdims=True)
        valid = (rmod + lag < sub) if rev else (rmod >= lag)
        hit = (col == row + lag) if rev else (col == row - lag)
        att = att + jnp.where(hit & valid, term, 0.0)

    o = o + _dot(att.astype(BF16), v)
    o_ref[0] = o.astype(o_ref.dtype)

    kd = (k * jnp.exp(btot - b)).astype(BF16)
    st_ref[...] = st * jnp.exp(btot) + _dot_tn(v, kd)


def gla_scan(p, glow, wg2, bg, rev):
    bsz, t, d3 = p.shape
    d = d3 // 3
    h = GLA_HEADS
    dk, dv = (d // 2) // h, d // h
    c = GLA_CHUNK
    nc, ncx = t // c, CTX_LEN // c

    def chunk(s):
        return jnp.where(s < ncx, ncx - 1 - s, nc - 1 - (s - ncx)) if rev else s

    return pl.pallas_call(
        functools.partial(_gla_body, rev=rev),
        grid=(bsz, h, nc),
        in_specs=[pl.BlockSpec((1, c, dk), lambda b, hh, s: (b, chunk(s), hh)),
                  pl.BlockSpec((1, c, dk), lambda b, hh, s: (b, chunk(s), h + hh)),
                  pl.BlockSpec((1, c, dv), lambda b, hh, s: (b, chunk(s), h + hh)),
                  pl.BlockSpec((1, c, LANES), lambda b, hh, s: (b, chunk(s), 0)),
                  pl.BlockSpec((LANES, dk), lambda b, hh, s: (0, hh)),
                  pl.BlockSpec((1, dk), lambda b, hh, s: (0, hh))],
        out_specs=pl.BlockSpec((1, c, dv), lambda b, hh, s: (b, chunk(s), hh)),
        out_shape=jax.ShapeDtypeStruct((bsz, t, d), F32),
        scratch_shapes=[pltpu.VMEM((dv, dk), F32)],
        compiler_params=_cparams("parallel", "parallel", "arbitrary"),
    )(p, p, p, glow, wg2, bg)


def _gla_gate_body(of_ref, ob_ref, r_ref, g_ref, o_ref):
    dv = g_ref.shape[1]
    o = of_ref[0] + ob_ref[0]
    r = r_ref[0].astype(F32)
    outs = []
    for hh in range(o.shape[1] // dv):
        seg = o[:, hh * dv:(hh + 1) * dv]
        outs.append(seg * lax.rsqrt(jnp.mean(seg * seg, -1, keepdims=True) + 1e-6) * g_ref[...])
    o_ref[0] = (jnp.concatenate(outs, axis=1) * (r * _sigmoid(r))).astype(o_ref.dtype)


def gla_gate(o_f, o_b, p, norm_g):
    bsz, t, d = o_f.shape
    tr = _pick(t, (256, 128, 64, 32, 16))
    row = pl.BlockSpec((1, tr, d), lambda b, i: (b, i, 0))
    return pl.pallas_call(
        _gla_gate_body, grid=(bsz, t // tr),
        in_specs=[row, row, pl.BlockSpec((1, tr, d), lambda b, i: (b, i, 2)),
                  pl.BlockSpec((1, norm_g.shape[1]), lambda b, i: (0, 0))],
        out_specs=row, out_shape=jax.ShapeDtypeStruct((bsz, t, d), BF16),
        compiler_params=_cparams("parallel", "parallel"),
    )(o_f, o_b, p, norm_g)


def gla_mixer(u, w_in, w_g1, w_g2, b_g, norm_g, w_o):
    bsz, t, d = u.shape
    qk = d // 2
    u2 = u.reshape(bsz * t, d)
    p = matmul(u2, w_in.astype(BF16), BF16).reshape(bsz, t, 3 * d)
    r = GLA_GATE_RANK
    wg1 = jnp.zeros((d, LANES), F32).at[:, :r].set(w_g1[0]).at[:, r:2 * r].set(w_g1[1])
    glow = matmul(u2, wg1.astype(BF16), F32).reshape(bsz, t, LANES)
    outs = []
    for s in range(2):
        wg2 = jnp.zeros((LANES, qk), F32).at[s * r:(s + 1) * r].set(w_g2[s])
        outs.append(gla_scan(p, glow, wg2, b_g[s][None], rev=(s == 1)))
    gated = gla_gate(outs[0], outs[1], p, norm_g[None])
    return matmul(gated.reshape(bsz * t, d), w_o.astype(BF16), F32).reshape(bsz, t, d)


def _rms(x, gain):
    return x * lax.rsqrt(jnp.mean(x * x, -1, keepdims=True) + 1e-6) * gain


def _rope128(a, c, s):
    return a * c + pltpu.roll(a, LANES // 2, 1) * s


def _mla_norm_body(h_ref, qg_ref, kg_ref, c_ref, s_ref, cq_ref, ckv_ref, kr_ref):
    qr, kvr = cq_ref.shape[2], ckv_ref.shape[2]
    h = h_ref[0]
    cq_ref[0] = _rms(h[:, :qr], qg_ref[...]).astype(BF16)
    ckv_ref[0] = _rms(h[:, qr:qr + kvr], kg_ref[...]).astype(BF16)
    kr_ref[0] = _rope128(h[:, qr + kvr:], c_ref[...], s_ref[...]).astype(BF16)


def _mla_attn_body(q_ref, kn_ref, v_ref, kr_ref, c_ref, s_ref, o_ref, kf_ref, *, nctx_tiles, scale):
    qt = pl.program_id(2)
    ctx = nctx_tiles * q_ref.shape[1]

    @pl.when(qt == 0)
    def _():
        kf_ref[:, :LANES] = kn_ref[0]
        kf_ref[:, LANES:] = kr_ref[0]

    q = q_ref[0].astype(F32)
    qr = _rope128(q[:, LANES:], c_ref[...], s_ref[...])
    qf = (jnp.concatenate([q[:, :LANES], qr], axis=1) * scale).astype(BF16)

    def attend(kf, v):
        s = _dot_nt(qf, kf)
        p = jnp.exp(s - jnp.max(s, -1, keepdims=True))
        l = jnp.sum(p, -1, keepdims=True)
        return (_dot(p.astype(BF16), v) / l).astype(o_ref.dtype)

    @pl.when(qt < nctx_tiles)
    def _():
        o_ref[0] = attend(kf_ref[:ctx], v_ref[0, :ctx])

    @pl.when(qt >= nctx_tiles)
    def _():
        o_ref[0] = attend(kf_ref[...], v_ref[0])


def _rope_tables(t):
    rows = (t - CTX_LEN) // GRID_W
    row = jnp.repeat(jnp.arange(rows, dtype=F32), GRID_W)
    colp = jnp.tile(jnp.arange(GRID_W, dtype=F32), rows)
    n_freq = MLA_ROPE // 4
    inv = ROPE_THETA ** (-jnp.arange(n_freq, dtype=F32) / n_freq)
    ang = jnp.concatenate([row[:, None] * inv, colp[:, None] * inv], -1)
    cos = jnp.concatenate([jnp.ones((CTX_LEN, MLA_ROPE // 2), F32), jnp.cos(ang)], 0)
    sin = jnp.concatenate([jnp.zeros((CTX_LEN, MLA_ROPE // 2), F32), jnp.sin(ang)], 0)
    z = jnp.zeros_like(cos)
    return jnp.concatenate([cos, cos, z, z], 1), jnp.concatenate([-sin, sin, z, z], 1)


def mla_mixer(u, w_in, q_norm, kv_norm, w_uq, w_ukv, w_o):
    bsz, t, d = u.shape
    nh = d // 128
    qr, kvr, rp = MLA_Q_RANK, MLA_KV_RANK, MLA_ROPE
    ev, od = jnp.arange(0, rp, 2), jnp.arange(1, rp, 2)
    perm = jnp.concatenate([ev, od, od, ev])
    w_in_p = jnp.concatenate([w_in[:, :qr + kvr], w_in[:, qr + kvr + perm]], axis=1)
    hw = qr + kvr + LANES
    h = matmul(u.reshape(bsz * t, d), w_in_p.astype(BF16), F32).reshape(bsz, t, hw)
    ctab, stab = _rope_tables(t)

    tr = _pick(t, (256, 128, 64, 32, 16))
    cq, ckv, kr = pl.pallas_call(
        _mla_norm_body, grid=(bsz, t // tr),
        in_specs=[pl.BlockSpec((1, tr, hw), lambda b, i: (b, i, 0)),
                  pl.BlockSpec((1, qr), lambda b, i: (0, 0)),
                  pl.BlockSpec((1, kvr), lambda b, i: (0, 0)),
                  pl.BlockSpec((tr, LANES), lambda b, i: (i, 0)),
                  pl.BlockSpec((tr, LANES), lambda b, i: (i, 0))],
        out_specs=[pl.BlockSpec((1, tr, qr), lambda b, i: (b, i, 0)),
                   pl.BlockSpec((1, tr, kvr), lambda b, i: (b, i, 0)),
                   pl.BlockSpec((1, tr, LANES), lambda b, i: (b, i, 0))],
        out_shape=[jax.ShapeDtypeStruct((bsz, t, qr), BF16),
                   jax.ShapeDtypeStruct((bsz, t, kvr), BF16),
                   jax.ShapeDtypeStruct((bsz, t, LANES), BF16)],
        compiler_params=_cparams("parallel", "parallel"),
    )(h, q_norm[None], kv_norm[None], ctab, stab)

    hq = MLA_NOPE + rp
    qcols = (jnp.arange(nh)[:, None] * hq
             + jnp.concatenate([jnp.arange(MLA_NOPE), MLA_NOPE + perm])[None, :]).reshape(-1)
    q = matmul(cq.reshape(bsz * t, qr), w_uq[:, qcols].astype(BF16), BF16).reshape(bsz, t, nh * 2 * LANES)
    kv = matmul(ckv.reshape(bsz * t, kvr), w_ukv.astype(BF16), BF16).reshape(bsz, t, nh * 2 * LANES)

    tq = _pick(CTX_LEN, (256, 128, 64, 32, 16))
    scale = (MLA_NOPE + rp) ** -0.5
    o = pl.pallas_call(
        functools.partial(_mla_attn_body, nctx_tiles=CTX_LEN // tq, scale=scale),
        grid=(bsz, nh, t // tq),
        in_specs=[pl.BlockSpec((1, tq, 2 * LANES), lambda b, hh, i: (b, i, hh)),
                  pl.BlockSpec((1, t, LANES), lambda b, hh, i: (b, 0, 2 * hh)),
                  pl.BlockSpec((1, t, LANES), lambda b, hh, i: (b, 0, 2 * hh + 1)),
                  pl.BlockSpec((1, t, LANES), lambda b, hh, i: (b, 0, 0)),
                  pl.BlockSpec((tq, LANES), lambda b, hh, i: (i, 0)),
                  pl.BlockSpec((tq, LANES), lambda b, hh, i: (i, 0))],
        out_specs=pl.BlockSpec((1, tq, LANES), lambda b, hh, i: (b, i, hh)),
        out_shape=jax.ShapeDtypeStruct((bsz, t, nh * MLA_V), BF16),
        scratch_shapes=[pltpu.VMEM((t, 2 * LANES), BF16)],
        compiler_params=_cparams("parallel", "parallel", "arbitrary"),
    )(q, kv, kv, kr, ctab, stab)
    return matmul(o.reshape(bsz * t, nh * MLA_V), w_o.astype(BF16), F32).reshape(bsz, t, d)


def _rwkv_mix_body(u_ref, mu_ref, *o_refs):
    t = u_ref.shape[1]
    u = u_ref[0].astype(F32)
    row = lax.broadcasted_iota(jnp.int32, (t, 1), 0)
    prev = jnp.where((row == 0) | (row == CTX_LEN), 0.0, pltpu.roll(u, 1, 0))
    nxt = jnp.where((row == CTX_LEN - 1) | (row == t - 1), 0.0, pltpu.roll(u, t - 1, 0))
    xx = 0.5 * (prev + nxt) - u
    for n, o_ref in enumerate(o_refs):
        o_ref[0] = (u + xx * mu_ref[n:n + 1, :]).astype(o_ref.dtype)


def _head_sum(x, e):
    eb = e.astype(BF16)
    outs = []
    for s in range(x.shape[1] // LANES):
        hi, mid, lo = _split3(x[:, s * LANES:(s + 1) * LANES])
        outs.append(_dot(hi, eb) + _dot(mid, eb) + _dot(lo, eb))
    return jnp.concatenate(outs, axis=1)


def _head_ones():
    r = lax.broadcasted_iota(jnp.int32, (LANES, LANES), 0) // RWKV_HEAD
    c = lax.broadcasted_iota(jnp.int32, (LANES, LANES), 1) // RWKV_HEAD
    return (r == c).astype(F32)


def _rwkv_prep_body(k_ref, hw_ref, ha_ref, w2_ref, a2_ref, w0_ref, a0_ref, kk_ref, ka_ref,
                    na_ref, lw0_ref, lw1_ref, ks0_ref, ks1_ref, bb0_ref, bb1_ref):
    rk = w2_ref.shape[1]
    k = k_ref[0]
    kk = k * kk_ref[...]
    kk = kk * lax.rsqrt(_head_sum(kk * kk, _head_ones()) + 1e-12)
    na_ref[0] = -kk
    hw = jnp.tanh(hw_ref[0])
    ha = ha_ref[0]
    for s, (lw_ref, ks_ref, bb_ref) in enumerate(((lw0_ref, ks0_ref, bb0_ref), (lw1_ref, ks1_ref, bb1_ref))):
        wl = w0_ref[s:s + 1, :] + _dot(hw[:, s * rk:(s + 1) * rk].astype(BF16), w2_ref[s])
        lw_ref[0] = -jnp.exp(_log_sigmoid(wl) - 0.5)
        a = _sigmoid(a0_ref[s:s + 1, :] + _dot(ha[:, s * rk:(s + 1) * rk].astype(BF16), a2_ref[s]))
        ks_ref[0] = k * (1.0 + (a - 1.0) * ka_ref[...])
        bb_ref[0] = kk * a


def _rwkv_scan_body(r_ref, lw_ref, ks_ref, v_ref, na_ref, bb_ref, y_ref, st_ref, *, rev):
    c = r_ref.shape[1]
    n = RWKV_HEAD
    c2 = 2 * c

    @pl.when(pl.program_id(2) == 0)
    def _():
        st_ref[...] = jnp.zeros_like(st_ref)

    lw = lw_ref[0]
    ri = lax.broadcasted_iota(jnp.int32, (c, c), 0)
    ci = lax.broadcasted_iota(jnp.int32, (c, c), 1)
    cs = _dot_sel(((ri <= ci) if rev else (ri >= ci)).astype(F32), lw)
    cm = cs - lw
    ctot = jnp.sum(lw, axis=0, keepdims=True)

    lane_a = lax.broadcasted_iota(jnp.int32, (1, LANES), 1) < n

    def stack(x):
        return jnp.concatenate([jnp.where(lane_a, x, 0.0), jnp.where(lane_a, 0.0, x)], axis=0)

    at = stack(na_ref[0] * jnp.exp(cm))
    rt = stack(r_ref[0] * jnp.exp(cs))
    ecs = jnp.exp(-cs)
    bh = stack(bb_ref[0] * ecs)
    kh = stack(ks_ref[0] * ecs)
    ece = jnp.exp(ctot - cs)
    be = stack(bb_ref[0] * ece)
    ke = stack(ks_ref[0] * ece)
    vs = stack(v_ref[0])

    rr = lax.broadcasted_iota(jnp.int32, (c2, c2), 0)
    cc = lax.broadcasted_iota(jnp.int32, (c2, c2), 1)
    same = (rr // c) == (cc // c)
    tr_, tc_ = rr % c, cc % c
    strict = same & ((tc_ > tr_) if rev else (tc_ < tr_))
    incl = same & ((tc_ >= tr_) if rev else (tc_ <= tr_))
    diag_blk = same & ((tr_ // RWKV_SUB) == (tc_ // RWKV_SUB))
    eye = (rr == cc).astype(F32)

    lhs = jnp.concatenate([at, rt], axis=0)
    rhs = jnp.concatenate([bh, kh], axis=0)
    a4 = _dot_x3(lhs, rhs, _dot_nt)
    nmat = jnp.where(strict, a4[:c2, :c2], 0.0)
    aak = jnp.where(strict, a4[:c2, c2:], 0.0)
    arb = jnp.where(incl, a4[c2:, :c2], 0.0)
    ark = jnp.where(incl, a4[c2:, c2:], 0.0)

    ht = st_ref[...]
    htb = ht.astype(BF16)
    x0 = _dot_nt(at.astype(BF16), htb) + _dot(aak.astype(BF16), vs.astype(BF16))

    nd = jnp.where(diag_blk, nmat, 0.0)
    no = nmat - nd
    tinv = eye + nd
    pw = nd
    for _ in range(3):
        pw = _dot_x3(pw, pw)
        tinv = tinv + _dot_x3(pw, tinv)
    z = _dot_x3(tinv, no)
    u = _dot_x3(tinv, x0)
    u = u + _dot_x3(z, u)
    z2 = _dot_x3(z, z)
    u = u + _dot_x3(z2, u)

    ub = u.astype(BF16)
    vb = vs.astype(BF16)
    ys = _dot_nt(rt.astype(BF16), htb) + _dot(arb.astype(BF16), ub) + _dot(ark.astype(BF16), vb)
    y_ref[0] = ys[:c] + ys[c:]

    hn = ht * jnp.exp(ctot) + _dot_tn(ub, be.astype(BF16)) + _dot_tn(vb, ke.astype(BF16))
    hr = lax.broadcasted_iota(jnp.int32, (LANES, LANES), 0) // n
    hc = lax.broadcasted_iota(jnp.int32, (LANES, LANES), 1) // n
    st_ref[...] = jnp.where(hr == hc, hn, 0.0)


def rwkv_scan(r, lw, ks, v, na, bb, rev):
    bsz, t, d = r.shape
    c = RWKV_CHUNK
    nc, ncx = t // c, CTX_LEN // c

    def chunk(s):
        return jnp.where(s < ncx, ncx - 1 - s, nc - 1 - (s - ncx)) if rev else s

    spec = pl.BlockSpec((1, c, LANES), lambda b, p, s: (b, chunk(s), p))
    return pl.pallas_call(
        functools.partial(_rwkv_scan_body, rev=rev),
        grid=(bsz, d // LANES, nc),
        in_specs=[spec] * 6, out_specs=spec,
        out_shape=jax.ShapeDtypeStruct((bsz, t, d), F32),
        scratch_shapes=[pltpu.VMEM((LANES, LANES), F32)],
        compiler_params=_cparams("parallel", "parallel", "arbitrary"),
    )(r, lw, ks, v, na, bb)


def _rwkv_out_body(y0_ref, y1_ref, r_ref, v_ref, g_ref, ks0_ref, ks1_ref, rk_ref, lg_ref, lb_ref, o_ref):
    e = _head_ones()
    y = y0_ref[0] + y1_ref[0]
    inv_n = 1.0 / RWKV_HEAD
    yc = y - _head_sum(y, e) * inv_n
    yn = yc * lax.rsqrt(_head_sum(yc * yc, e) * inv_n + RWKV_GN_EPS)
    yn = yn * lg_ref[...] + lb_ref[...]
    bonus = _head_sum(r_ref[0] * (ks0_ref[0] + ks1_ref[0]) * rk_ref[...], e) * v_ref[0]
    o_ref[0] = ((yn + bonus) * g_ref[0]).astype(o_ref.dtype)


def rwkv_mixer(u, mu, w_r, w_k, w_v, w_o, w0, w1, w2, a0, a1, a2, g1, g2, k_k, k_a, r_k, ln_g, ln_b):
    bsz, t, d = u.shape
    m = bsz * t
    dc = _pick(d, (256, 128))
    xs = pl.pallas_call(
        _rwkv_mix_body, grid=(bsz, d // dc),
        in_specs=[pl.BlockSpec((1, t, dc), lambda b, j: (b, 0, j)),
                  pl.BlockSpec((6, dc), lambda b, j: (0, j))],
        out_specs=[pl.BlockSpec((1, t, dc), lambda b, j: (b, 0, j))] * 6,
        out_shape=[jax.ShapeDtypeStruct((bsz, t, d), BF16)] * 6,
        compiler_params=_cparams("parallel", "parallel"),
    )(u, mu)
    xr, xw, xk, xv, xa, xg = (x.reshape(m, d) for x in xs)
    r = matmul(xr, w_r.astype(BF16)).reshape(bsz, t, d)
    k = matmul(xk, w_k.astype(BF16)).reshape(bsz, t, d)
    v = matmul(xv, w_v.astype(BF16)).reshape(bsz, t, d)
    rk = w1.shape[2]
    hw = matmul(xw, jnp.concatenate([w1[0], w1[1]], 1).astype(BF16)).reshape(bsz, t, 2 * rk)
    ha = matmul(xa, jnp.concatenate([a1[0], a1[1]], 1).astype(BF16)).reshape(bsz, t, 2 * rk)
    gr = g1.shape[1]
    grp = -(-gr // LANES) * LANES
    g1p = jnp.zeros((d, grp), F32).at[:, :gr].set(g1)
    g2p = jnp.zeros((grp, d), F32).at[:gr].set(g2)
    hg = jax.nn.sigmoid(matmul(xg, g1p.astype(BF16)))
    g = matmul(hg, g2p.astype(BF16)).reshape(bsz, t, d)

    tr = _pick(t, (64, 32, 16))
    row = pl.BlockSpec((1, tr, d), lambda b, i: (b, i, 0))
    low = pl.BlockSpec((1, tr, 2 * rk), lambda b, i: (b, i, 0))
    vec = pl.BlockSpec((1, d), lambda b, i: (0, 0))
    vec2 = pl.BlockSpec((2, d), lambda b, i: (0, 0))
    fact = pl.BlockSpec((2, rk, d), lambda b, i: (0, 0, 0))
    big = jax.ShapeDtypeStruct((bsz, t, d), F32)
    na, lw0, lw1, ks0, ks1, bb0, bb1 = pl.pallas_call(
        _rwkv_prep_body, grid=(bsz, t // tr),
        in_specs=[row, low, low, fact, fact, vec2, vec2, vec, vec],
        out_specs=[row] * 7, out_shape=[big] * 7,
        compiler_params=_cparams("parallel", "parallel"),
    )(k, hw, ha, w2.astype(BF16), a2.astype(BF16), w0, a0, k_k[None], k_a[None])

    y0 = rwkv_scan(r, lw0, ks0, v, na, bb0, rev=False)
    y1 = rwkv_scan(r, lw1, ks1, v, na, bb1, rev=True)

    out = pl.pallas_call(
        _rwkv_out_body, grid=(bsz, t // tr),
        in_specs=[row] * 7 + [vec, vec, vec], out_specs=row,
        out_shape=jax.ShapeDtypeStruct((bsz, t, d), BF16),
        compiler_params=_cparams("parallel", "parallel"),
    )(y0, y1, r, v, g, ks0, ks1, r_k.reshape(1, d), ln_g[None], ln_b[None])
    return matmul(out.reshape(m, d), w_o.astype(BF16)).reshape(bsz, t, d)


def _top_vals(s, kk):
    vals = []
    cur = s
    for _ in range(kk):
        mx = jnp.max(cur, axis=0, keepdims=True)
        vals.append(mx)
        cur = jnp.where(cur >= mx, NEG, cur)
    return jnp.concatenate(vals, axis=0)


def _peer_score_body(q_ref, keys_ref, s1_ref, s2_ref, e2_ref, cf_ref, tau_ref):
    nk = PEER_NKEYS
    kk = PEER_TOPK
    taus = []
    for h in range(PEER_HEADS):
        q1 = q_ref[h * 2 * nk:h * 2 * nk + nk, :].astype(BF16)
        q2 = q_ref[h * 2 * nk + nk:(h + 1) * 2 * nk, :].astype(BF16)
        s1 = _dot(keys_ref[0], q1)
        s2 = _dot(keys_ref[1], q2)
        a1 = _top_vals(s1, kk)
        a2 = _top_vals(s2, kk)
        cand = (a1[:, None, :] + a2[None, :, :]).reshape(kk * kk, -1)
        cv = _top_vals(cand, kk)
        tau = cv[kk - 1:kk]
        zsum = jnp.sum(jnp.where(cand >= tau, jnp.exp(cand - cv[0:1]), 0.0), axis=0, keepdims=True)
        s1_ref[h] = s1
        s2_ref[h] = s2
        e2_ref[h] = jnp.exp(s2 - a2[0:1])
        cf_ref[h] = jnp.exp(s1 - a1[0:1]) / zsum
        taus.append(tau)
    tau_ref[...] = jnp.concatenate(taus, axis=0)


def _gelu(x):
    return 0.5 * x * (1.0 + lax.erf(x * (2.0 ** -0.5)))


def _peer_dense_body(z_ref, u_ref, v_ref, s1_ref, s2_ref, e2_ref, cf_ref, tau_ref, o_ref):
    nk = PEER_NKEYS
    eb = pl.program_id(1)
    te = u_ref.shape[0]

    @pl.when(eb == 0)
    def _():
        o_ref[...] = jnp.zeros_like(o_ref)

    act = _dot(u_ref[...], z_ref[...])
    ws = []
    for ii in range(te // nk):
        i = eb * (te // nk) + ii
        w = jnp.zeros((nk, act.shape[1]), F32)
        for h in range(PEER_HEADS):
            hit = (s2_ref[h] + s1_ref[h, pl.ds(i, 1), :]) >= tau_ref[h:h + 1, :]
            w = w + jnp.where(hit, e2_ref[h], 0.0) * cf_ref[h, pl.ds(i, 1), :]
        ws.append(w)
    g = (jnp.concatenate(ws, axis=0) * _gelu(act)).astype(BF16)
    o_ref[...] += _dot(v_ref[...], g)


def peer_ffn(zt, w_q, sub_keys, u_tab, v_tab):
    d, m = zt.shape
    nh, nk = PEER_HEADS, PEER_NKEYS
    qt = matmul(w_q.T.astype(BF16), zt, F32)
    tt = _pick(m, (256, 128))
    sh = jax.ShapeDtypeStruct((nh, nk, m), F32)
    blk = pl.BlockSpec((nh, nk, tt), lambda i: (0, 0, i))
    s1, s2, e2, cf, tau = pl.pallas_call(
        _peer_score_body, grid=(m // tt,),
        in_specs=[pl.BlockSpec((nh * 2 * nk, tt), lambda i: (0, i)),
                  pl.BlockSpec((2, nk, PEER_DKEY // 2), lambda i: (0, 0, 0))],
        out_specs=[blk, blk, blk, blk, pl.BlockSpec((nh, tt), lambda i: (0, i))],
        out_shape=[sh, sh, sh, sh, jax.ShapeDtypeStruct((nh, m), F32)],
        compiler_params=_cparams("parallel"),
    )(qt, sub_keys.astype(BF16))

    tm = _pick(m, (256, 128))
    te = 512
    ne = u_tab.shape[0]
    sblk = pl.BlockSpec((nh, nk, tm), lambda i, e: (0, 0, i))
    return pl.pallas_call(
        _peer_dense_body, grid=(m // tm, ne // te),
        in_specs=[pl.BlockSpec((d, tm), lambda i, e: (0, i)),
                  pl.BlockSpec((te, d), lambda i, e: (e, 0)),
                  pl.BlockSpec((d, te), lambda i, e: (0, e)),
                  sblk, sblk, sblk, sblk,
                  pl.BlockSpec((nh, tm), lambda i, e: (0, i))],
        out_specs=pl.BlockSpec((d, tm), lambda i, e: (0, i)),
        out_shape=jax.ShapeDtypeStruct((d, m), F32),
        compiler_params=_cparams("parallel", "arbitrary"),
    )(zt, u_tab.astype(BF16), v_tab.T.astype(BF16), s1, s2, e2, cf, tau)


def _ada_mods(c, c_ctx, w_down, w_up, b_up):
    bsz, d = c.shape
    cond = jnp.concatenate([c, c_ctx[None]], 0)
    pad = 16 - cond.shape[0] % 16
    cond = jnp.concatenate([cond, jnp.zeros((pad, d), F32)], 0)
    hid = matmul(jax.nn.silu(cond), w_down.astype(BF16), F32)
    m = (matmul(hid, w_up.astype(BF16), F32) + b_up)[:bsz + 1].reshape(bsz + 1, N_MOD, d)
    return jnp.stack([jnp.broadcast_to(m[bsz], (bsz, N_MOD, d)), m[:bsz]], axis=1)


def kernel(x, c, ctx, c_ctx, ada_w_down, ada_w_up, ada_b,
           gla_w_in, gla_w_g1, gla_w_g2, gla_b_g, gla_norm_g, gla_w_o,
           mla_w_in, mla_q_norm, mla_kv_norm, mla_w_uq, mla_w_ukv, mla_w_o,
           rwkv_mu, rwkv_w_r, rwkv_w_k, rwkv_w_v, rwkv_w_o, rwkv_w0, rwkv_w1, rwkv_w2,
           rwkv_a0, rwkv_a1, rwkv_a2, rwkv_g1, rwkv_g2, rwkv_k_k, rwkv_k_a, rwkv_r_k, rwkv_ln_g, rwkv_ln_b,
           peer_w_q, peer_sub_keys, peer_u, peer_v):
    z = jnp.concatenate([ctx, x], axis=1)
    bsz, t, d = z.shape
    mods = [_ada_mods(c, c_ctx, ada_w_down[i], ada_w_up[i], ada_b[i]) for i in range(DEPTH)]
    _, u = ln_mod(z, mods_n=mods[0], sidx=0)
    for i in range(DEPTH):
        j = i // N_MIXERS
        if i % N_MIXERS == 0:
            y = gla_mixer(u, gla_w_in[j], gla_w_g1[j], gla_w_g2[j], gla_b_g[j], gla_norm_g[j], gla_w_o[j])
        elif i % N_MIXERS == 1:
            y = mla_mixer(u, mla_w_in[j], mla_q_norm[j], mla_kv_norm[j], mla_w_uq[j], mla_w_ukv[j], mla_w_o[j])
        else:
            y = rwkv_mixer(u, rwkv_mu[j], rwkv_w_r[j], rwkv_w_k[j], rwkv_w_v[j], rwkv_w_o[j],
                           rwkv_w0[j], rwkv_w1[j], rwkv_w2[j], rwkv_a0[j], rwkv_a1[j], rwkv_a2[j],
                           rwkv_g1[j], rwkv_g2[j], rwkv_k_k[j], rwkv_k_a[j], rwkv_r_k[j],
                           rwkv_ln_g[j], rwkv_ln_b[j])
        z, ut = ln_mod(z, y, mods[i], 2, mods[i], 3, u_t=True)
        ht = peer_ffn(ut, peer_w_q[i], peer_sub_keys[i], peer_u[i], peer_v[i])
        if i + 1 < DEPTH:
            z, u = ln_mod(z, ht, mods[i], 5, mods[i + 1], 0, y_t=True)
        else:
            z, _ = ln_mod(z, ht, mods[i], 5, y_t=True)
    return z[:, CTX_LEN:]
```

```python
import functools

import jax
import jax.numpy as jnp
from jax import lax
from jax.experimental import pallas as pl
from jax.experimental.pallas import tpu as pltpu

F32 = jnp.float32
BF16 = jnp.bfloat16

DEPTH = 4
CTX_LEN = 256
GRID_W = 64
N_MIXERS = 3
N_MOD = 6
LN_EPS = 1e-6
DEEPNORM_ALPHA = (2.0 * DEPTH) ** 0.25

GLA_HEADS = 8
GLA_GATE_RANK = 16
GLA_GATE_NORMALIZER = 16.0

MLA_Q_RANK = 1536
MLA_KV_RANK = 512
MLA_NOPE = 128
MLA_ROPE = 64
MLA_V = 128
ROPE_THETA = 10000.0

RWKV_HEAD = 64
RWKV_GN_EPS = 64e-5

PEER_HEADS = 8
PEER_NKEYS = 128
PEER_DKEY = 256
PEER_TOPK = 16

LANES = 128
SUBLANES = 8
VMEM_LIMIT = 52 * 1024 * 1024

GLA_CHUNK = 64
GLA_SUB = 16
RWKV_CHUNK = 64
RWKV_SUB = 16
PEER_TOK_SUB = 256
PEER_EXP_SUB = 256
NEG = -1e30


def _pick(n, cands):
    for c in cands:
        if n % c == 0:
            return c
    return n


def _cparams(*sem):
    return pltpu.CompilerParams(dimension_semantics=sem, vmem_limit_bytes=VMEM_LIMIT)


def _dot(a, b):
    return jnp.dot(a, b, preferred_element_type=F32)


def _dot_nt(a, b):
    return lax.dot_general(a, b, (((1,), (1,)), ((), ())), preferred_element_type=F32)


def _dot_tn(a, b):
    return lax.dot_general(a, b, (((0,), (0,)), ((), ())), preferred_element_type=F32)


def _split3(x):
    hi = x.astype(BF16)
    r1 = x - hi.astype(F32)
    mid = r1.astype(BF16)
    lo = (r1 - mid.astype(F32)).astype(BF16)
    return hi, mid, lo


def _dot_sel(sel, x):
    s = sel.astype(BF16)
    hi, mid, lo = _split3(x)
    return _dot(s, hi) + _dot(s, mid) + _dot(s, lo)


def _dot_x3(a, b, dims=None):
    ah = a.astype(BF16)
    al = (a - ah.astype(F32)).astype(BF16)
    bh = b.astype(BF16)
    bl = (b - bh.astype(F32)).astype(BF16)
    f = _dot if dims is None else dims
    return f(ah, bh) + f(ah, bl) + f(al, bh)


def _log_sigmoid(x):
    return jnp.minimum(x, 0.0) - jnp.log1p(jnp.exp(-jnp.abs(x)))


def _sigmoid(x):
    return 1.0 / (1.0 + jnp.exp(-x))


def _mm_body(a_ref, b_ref, o_ref):
    o_ref[...] = _dot(a_ref[...].astype(BF16), b_ref[...].astype(BF16)).astype(o_ref.dtype)


def matmul(a, b, out_dtype=F32):
    m, k = a.shape
    k2, n = b.shape
    assert k == k2
    a_bytes = jnp.dtype(a.dtype).itemsize
    tm_cands = (1024, 512, 256, 128, 64, 32, 16) if a_bytes * k <= 8192 else (512, 256, 128, 64, 32, 16)
    tm = _pick(m, tm_cands)
    tn = _pick(n, (512, 256, 128))
    return pl.pallas_call(
        _mm_body,
        grid=(m // tm, n // tn),
        in_specs=[pl.BlockSpec((tm, k), lambda i, j: (i, 0)),
                  pl.BlockSpec((k, tn), lambda i, j: (0, j))],
        out_specs=pl.BlockSpec((tm, tn), lambda i, j: (i, j)),
        out_shape=jax.ShapeDtypeStruct((m, n), out_dtype),
        compiler_params=_cparams("parallel", "parallel"),
    )(a, b)


def _layer_norm(z):
    zc = z - jnp.mean(z, -1, keepdims=True)
    return zc * lax.rsqrt(jnp.mean(zc * zc, -1, keepdims=True) + LN_EPS)


def _ln_mod_body(*refs, gidx, sidx, has_y, y_t, u_t):
    it = iter(refs)
    z_ref = next(it)
    y_ref = next(it) if has_y else None
    mg_ref = next(it) if has_y else None
    mn_ref = next(it) if sidx is not None else None
    zo_ref = next(it) if has_y else None
    u_ref = next(it) if sidx is not None else None
    z = z_ref[0]
    if has_y:
        y = y_ref[...].T if y_t else y_ref[0]
        z = _layer_norm(DEEPNORM_ALPHA * z + mg_ref[0, 0, gidx:gidx + 1, :] * y)
        zo_ref[0] = z
    if sidx is not None:
        u = z * (1.0 + mn_ref[0, 0, sidx + 1:sidx + 2, :]) + mn_ref[0, 0, sidx:sidx + 1, :]
        if u_t:
            u_ref[...] = u.T.astype(u_ref.dtype)
        else:
            u_ref[0] = u.astype(u_ref.dtype)


def ln_mod(z, y=None, mods_g=None, gidx=None, mods_n=None, sidx=None, y_t=False, u_t=False):
    bsz, t, d = z.shape
    tr = _pick(CTX_LEN, (256, 128, 64, 32, 16))
    nt = t // tr
    nctx = CTX_LEN // tr
    has_y = y is not None
    row = pl.BlockSpec((1, tr, d), lambda b, i: (b, i, 0))
    col = pl.BlockSpec((d, tr), lambda b, i: (0, b * nt + i))
    mod = pl.BlockSpec((1, 1, N_MOD, d), lambda b, i: (b, jnp.where(i < nctx, 0, 1), 0, 0))
    ins, in_specs, outs, out_specs = [z], [row], [], []
    if has_y:
        ins += [y, mods_g]
        in_specs += [col if y_t else row, mod]
        outs.append(jax.ShapeDtypeStruct((bsz, t, d), F32))
        out_specs.append(row)
    if sidx is not None:
        ins.append(mods_n)
        in_specs.append(mod)
        outs.append(jax.ShapeDtypeStruct((d, bsz * t) if u_t else (bsz, t, d), BF16))
        out_specs.append(col if u_t else row)
    res = pl.pallas_call(
        functools.partial(_ln_mod_body, gidx=gidx, sidx=sidx, has_y=has_y, y_t=y_t, u_t=u_t),
        grid=(bsz, nt), in_specs=in_specs, out_specs=out_specs, out_shape=outs,
        compiler_params=_cparams("parallel", "parallel"),
    )(*ins)
    res = list(res)
    zo = res.pop(0) if has_y else None
    u = res.pop(0) if sidx is not None else None
    return zo, u


def _lockstep(gens):
    out = [None] * len(gens)
    live = list(range(len(gens)))
    while live:
        for i in list(live):
            try:
                next(gens[i])
            except StopIteration as done:
                out[i] = done.value
                live.remove(i)
    return out


def _gla_chunk(q, k, v, glow, wg2, bg, st, rev):
    c, dk = q.shape
    sub = GLA_SUB
    nsub = c // sub
    q = q.astype(F32) * (dk ** -0.5)
    k = k.astype(F32)
    gl = _dot_x3(glow, wg2) + bg
    yield
    g = _log_sigmoid(gl) * (1.0 / GLA_GATE_NORMALIZER)
    row = lax.broadcasted_iota(jnp.int32, (c, c), 0)
    col = lax.broadcasted_iota(jnp.int32, (c, c), 1)
    tri = (row <= col) if rev else (row >= col)
    b = _dot_sel(tri.astype(F32), g)
    yield
    btot = jnp.sum(g, axis=0, keepdims=True)

    o = _dot_nt((q * jnp.exp(b)).astype(BF16), st.astype(BF16))

    rowk = lax.broadcasted_iota(jnp.int32, (c, 1), 0)
    blocks = []
    for i in range(nsub):
        lo, hi = i * sub, (i + 1) * sub
        if (rev and i == nsub - 1) or (not rev and i == 0):
            blocks.append(jnp.zeros((sub, c), F32))
            continue
        ref = b[hi:hi + 1] if rev else b[lo - 1:lo]
        qi = q[lo:hi] * jnp.exp(b[lo:hi] - ref)
        other = (rowk >= hi) if rev else (rowk < lo)
        kj = jnp.where(other, k * jnp.exp(jnp.minimum(ref - b, 0.0)), 0.0)
        blocks.append(_dot_nt(qi.astype(BF16), kj.astype(BF16)))
    att = jnp.concatenate(blocks, axis=0)
    yield

    rmod = rowk % sub
    for lag in range(sub):
        sh = (c - lag) % c if rev else lag
        ks = pltpu.roll(k, sh, 0) if sh else k
        bs = pltpu.roll(b, sh, 0) if sh else b
        term = jnp.sum(q * ks * jnp.exp(jnp.minimum(b - bs, 0.0)), axis=1, keepdims=True)
        valid = (rmod + lag < sub) if rev else (rmod >= lag)
        hit = (col == row + lag) if rev else (col == row - lag)
        att = att + jnp.where(hit & valid, term, 0.0)

    o = o + _dot(att.astype(BF16), v)
    yield
    kd = (k * jnp.exp(btot - b)).astype(BF16)
    return o, st * jnp.exp(btot) + _dot_tn(v, kd)


def _gla_body(*refs):
    ins, (of_ref, ob_ref, st_ref) = refs[:12], refs[12:]

    @pl.when(pl.program_id(2) == 0)
    def _():
        st_ref[...] = jnp.zeros_like(st_ref)

    gens = []
    for dr in range(2):
        q_ref, k_ref, v_ref, gl_ref, wg2_ref, bg_ref = ins[6 * dr:6 * dr + 6]
        gens.append(_gla_chunk(q_ref[0], k_ref[0], v_ref[0], gl_ref[0], wg2_ref[...], bg_ref[...],
                               st_ref[dr], rev=(dr == 1)))
    for dr, (o, st) in enumerate(_lockstep(gens)):
        (of_ref, ob_ref)[dr][0] = o
        st_ref[dr] = st


def gla_scan(p, glow, wg2, bg):
    bsz, t, d3 = p.shape
    d = d3 // 3
    h = GLA_HEADS
    dk, dv = (d // 2) // h, d // h
    c = GLA_CHUNK
    nc, ncx = t // c, CTX_LEN // c

    def rchunk(s):
        return jnp.where(s < ncx, ncx - 1 - s, nc - 1 - (s - ncx))

    def specs(chunk):
        return [pl.BlockSpec((1, c, dk), lambda b, hh, s: (b, chunk(s), hh)),
                pl.BlockSpec((1, c, dk), lambda b, hh, s: (b, chunk(s), h + hh)),
                pl.BlockSpec((1, c, dv), lambda b, hh, s: (b, chunk(s), h + hh)),
                pl.BlockSpec((1, c, LANES), lambda b, hh, s: (b, chunk(s), 0)),
                pl.BlockSpec((LANES, dk), lambda b, hh, s: (0, hh)),
                pl.BlockSpec((1, dk), lambda b, hh, s: (0, hh))]

    sh = jax.ShapeDtypeStruct((bsz, t, d), F32)
    return pl.pallas_call(
        _gla_body,
        grid=(bsz, h, nc),
        in_specs=specs(lambda s: s) + specs(rchunk),
        out_specs=[pl.BlockSpec((1, c, dv), lambda b, hh, s: (b, s, hh)),
                   pl.BlockSpec((1, c, dv), lambda b, hh, s: (b, rchunk(s), hh))],
        out_shape=[sh, sh],
        scratch_shapes=[pltpu.VMEM((2, dv, dk), F32)],
        compiler_params=_cparams("parallel", "parallel", "arbitrary"),
    )(p, p, p, glow, wg2[0], bg[0], p, p, p, glow, wg2[1], bg[1])


def _gla_gate_body(of_ref, ob_ref, r_ref, g_ref, o_ref):
    dv = g_ref.shape[1]
    o = of_ref[0] + ob_ref[0]
    r = r_ref[0].astype(F32)
    outs = []
    for hh in range(o.shape[1] // dv):
        seg = o[:, hh * dv:(hh + 1) * dv]
        outs.append(seg * lax.rsqrt(jnp.mean(seg * seg, -1, keepdims=True) + 1e-6) * g_ref[...])
    o_ref[0] = (jnp.concatenate(outs, axis=1) * (r * _sigmoid(r))).astype(o_ref.dtype)


def gla_gate(o_f, o_b, p, norm_g):
    bsz, t, d = o_f.shape
    tr = _pick(t, (256, 128, 64, 32, 16))
    row = pl.BlockSpec((1, tr, d), lambda b, i: (b, i, 0))
    return pl.pallas_call(
        _gla_gate_body, grid=(bsz, t // tr),
        in_specs=[row, row, pl.BlockSpec((1, tr, d), lambda b, i: (b, i, 2)),
                  pl.BlockSpec((1, norm_g.shape[1]), lambda b, i: (0, 0))],
        out_specs=row, out_shape=jax.ShapeDtypeStruct((bsz, t, d), BF16),
        compiler_params=_cparams("parallel", "parallel"),
    )(o_f, o_b, p, norm_g)


def gla_mixer(u, w_in, w_g1, w_g2, b_g, norm_g, w_o):
    bsz, t, d = u.shape
    qk = d // 2
    u2 = u.reshape(bsz * t, d)
    p = matmul(u2, w_in.astype(BF16), BF16).reshape(bsz, t, 3 * d)
    r = GLA_GATE_RANK
    wg1 = jnp.zeros((d, LANES), F32).at[:, :r].set(w_g1[0]).at[:, r:2 * r].set(w_g1[1])
    glow = matmul(u2, wg1.astype(BF16), F32).reshape(bsz, t, LANES)
    wg2 = [jnp.zeros((LANES, qk), F32).at[s * r:(s + 1) * r].set(w_g2[s]) for s in range(2)]
    outs = gla_scan(p, glow, wg2, [b_g[0][None], b_g[1][None]])
    gated = gla_gate(outs[0], outs[1], p, norm_g[None])
    return matmul(gated.reshape(bsz * t, d), w_o.astype(BF16), F32).reshape(bsz, t, d)


def _rms(x, gain):
    return x * lax.rsqrt(jnp.mean(x * x, -1, keepdims=True) + 1e-6) * gain


def _rope128(a, c, s):
    return a * c + pltpu.roll(a, LANES // 2, 1) * s


def _mla_norm_body(h_ref, qg_ref, kg_ref, c_ref, s_ref, cq_ref, ckv_ref, kr_ref):
    qr, kvr = cq_ref.shape[2], ckv_ref.shape[2]
    h = h_ref[0]
    cq_ref[0] = _rms(h[:, :qr], qg_ref[...]).astype(BF16)
    ckv_ref[0] = _rms(h[:, qr:qr + kvr], kg_ref[...]).astype(BF16)
    kr_ref[0] = _rope128(h[:, qr + kvr:], c_ref[...], s_ref[...]).astype(BF16)


def _mla_attn_body(q_ref, kn_ref, v_ref, kr_ref, c_ref, s_ref, o_ref, kf_ref, *, nctx_tiles, scale):
    qt = pl.program_id(2)
    ctx = nctx_tiles * q_ref.shape[1]

    @pl.when(qt == 0)
    def _():
        kf_ref[:, :LANES] = kn_ref[0]
        kf_ref[:, LANES:] = kr_ref[0]

    q = q_ref[0].astype(F32)
    qr = _rope128(q[:, LANES:], c_ref[...], s_ref[...])
    qf = (jnp.concatenate([q[:, :LANES], qr], axis=1) * scale).astype(BF16)

    def attend(kf, v):
        s = _dot_nt(qf, kf)
        p = jnp.exp(s - jnp.max(s, -1, keepdims=True))
        l = jnp.sum(p, -1, keepdims=True)
        return (_dot(p.astype(BF16), v) / l).astype(o_ref.dtype)

    @pl.when(qt < nctx_tiles)
    def _():
        o_ref[0] = attend(kf_ref[:ctx], v_ref[0, :ctx])

    @pl.when(qt >= nctx_tiles)
    def _():
        o_ref[0] = attend(kf_ref[...], v_ref[0])


def _rope_tables(t):
    rows = (t - CTX_LEN) // GRID_W
    row = jnp.repeat(jnp.arange(rows, dtype=F32), GRID_W)
    colp = jnp.tile(jnp.arange(GRID_W, dtype=F32), rows)
    n_freq = MLA_ROPE // 4
    inv = ROPE_THETA ** (-jnp.arange(n_freq, dtype=F32) / n_freq)
    ang = jnp.concatenate([row[:, None] * inv, colp[:, None] * inv], -1)
    cos = jnp.concatenate([jnp.ones((CTX_LEN, MLA_ROPE // 2), F32), jnp.cos(ang)], 0)
    sin = jnp.concatenate([jnp.zeros((CTX_LEN, MLA_ROPE // 2), F32), jnp.sin(ang)], 0)
    z = jnp.zeros_like(cos)
    return jnp.concatenate([cos, cos, z, z], 1), jnp.concatenate([-sin, sin, z, z], 1)


def mla_mixer(u, w_in, q_norm, kv_norm, w_uq, w_ukv, w_o):
    bsz, t, d = u.shape
    nh = d // 128
    qr, kvr, rp = MLA_Q_RANK, MLA_KV_RANK, MLA_ROPE
    ev, od = jnp.arange(0, rp, 2), jnp.arange(1, rp, 2)
    perm = jnp.concatenate([ev, od, od, ev])
    w_in_p = jnp.concatenate([w_in[:, :qr + kvr], w_in[:, qr + kvr + perm]], axis=1)
    hw = qr + kvr + LANES
    h = matmul(u.reshape(bsz * t, d), w_in_p.astype(BF16), F32).reshape(bsz, t, hw)
    ctab, stab = _rope_tables(t)

    tr = _pick(t, (256, 128, 64, 32, 16))
    cq, ckv, kr = pl.pallas_call(
        _mla_norm_body, grid=(bsz, t // tr),
        in_specs=[pl.BlockSpec((1, tr, hw), lambda b, i: (b, i, 0)),
                  pl.BlockSpec((1, qr), lambda b, i: (0, 0)),
                  pl.BlockSpec((1, kvr), lambda b, i: (0, 0)),
                  pl.BlockSpec((tr, LANES), lambda b, i: (i, 0)),
                  pl.BlockSpec((tr, LANES), lambda b, i: (i, 0))],
        out_specs=[pl.BlockSpec((1, tr, qr), lambda b, i: (b, i, 0)),
                   pl.BlockSpec((1, tr, kvr), lambda b, i: (b, i, 0)),
                   pl.BlockSpec((1, tr, LANES), lambda b, i: (b, i, 0))],
        out_shape=[jax.ShapeDtypeStruct((bsz, t, qr), BF16),
                   jax.ShapeDtypeStruct((bsz, t, kvr), BF16),
                   jax.ShapeDtypeStruct((bsz, t, LANES), BF16)],
        compiler_params=_cparams("parallel", "parallel"),
    )(h, q_norm[None], kv_norm[None], ctab, stab)

    hq = MLA_NOPE + rp
    qcols = (jnp.arange(nh)[:, None] * hq
             + jnp.concatenate([jnp.arange(MLA_NOPE), MLA_NOPE + perm])[None, :]).reshape(-1)
    q = matmul(cq.reshape(bsz * t, qr), w_uq[:, qcols].astype(BF16), BF16).reshape(bsz, t, nh * 2 * LANES)
    kv = matmul(ckv.reshape(bsz * t, kvr), w_ukv.astype(BF16), BF16).reshape(bsz, t, nh * 2 * LANES)

    tq = _pick(CTX_LEN, (256, 128, 64, 32, 16))
    scale = (MLA_NOPE + rp) ** -0.5
    o = pl.pallas_call(
        functools.partial(_mla_attn_body, nctx_tiles=CTX_LEN // tq, scale=scale),
        grid=(bsz, nh, t // tq),
        in_specs=[pl.BlockSpec((1, tq, 2 * LANES), lambda b, hh, i: (b, i, hh)),
                  pl.BlockSpec((1, t, LANES), lambda b, hh, i: (b, 0, 2 * hh)),
                  pl.BlockSpec((1, t, LANES), lambda b, hh, i: (b, 0, 2 * hh + 1)),
                  pl.BlockSpec((1, t, LANES), lambda b, hh, i: (b, 0, 0)),
                  pl.BlockSpec((tq, LANES), lambda b, hh, i: (i, 0)),
                  pl.BlockSpec((tq, LANES), lambda b, hh, i: (i, 0))],
        out_specs=pl.BlockSpec((1, tq, LANES), lambda b, hh, i: (b, i, hh)),
        out_shape=jax.ShapeDtypeStruct((bsz, t, nh * MLA_V), BF16),
        scratch_shapes=[pltpu.VMEM((t, 2 * LANES), BF16)],
        compiler_params=_cparams("parallel", "parallel", "arbitrary"),
    )(q, kv, kv, kr, ctab, stab)
    return matmul(o.reshape(bsz * t, nh * MLA_V), w_o.astype(BF16), F32).reshape(bsz, t, d)


def _rwkv_mix_body(u_ref, mu_ref, *o_refs):
    t = u_ref.shape[1]
    u = u_ref[0].astype(F32)
    row = lax.broadcasted_iota(jnp.int32, (t, 1), 0)
    prev = jnp.where((row == 0) | (row == CTX_LEN), 0.0, pltpu.roll(u, 1, 0))
    nxt = jnp.where((row == CTX_LEN - 1) | (row == t - 1), 0.0, pltpu.roll(u, t - 1, 0))
    xx = 0.5 * (prev + nxt) - u
    for n, o_ref in enumerate(o_refs):
        o_ref[0] = (u + xx * mu_ref[n:n + 1, :]).astype(o_ref.dtype)


def _head_sum(x, e):
    eb = e.astype(BF16)
    outs = []
    for s in range(x.shape[1] // LANES):
        hi, mid, lo = _split3(x[:, s * LANES:(s + 1) * LANES])
        outs.append(_dot(hi, eb) + _dot(mid, eb) + _dot(lo, eb))
    return jnp.concatenate(outs, axis=1)


def _head_ones():
    r = lax.broadcasted_iota(jnp.int32, (LANES, LANES), 0) // RWKV_HEAD
    c = lax.broadcasted_iota(jnp.int32, (LANES, LANES), 1) // RWKV_HEAD
    return (r == c).astype(F32)


def _rwkv_prep_body(k_ref, hw_ref, ha_ref, w2_ref, a2_ref, w0_ref, a0_ref, kk_ref, ka_ref,
                    na_ref, lw0_ref, lw1_ref, ks0_ref, ks1_ref, bb0_ref, bb1_ref):
    rk = w2_ref.shape[1]
    k = k_ref[0]
    kk = k * kk_ref[...]
    kk = kk * lax.rsqrt(_head_sum(kk * kk, _head_ones()) + 1e-12)
    na_ref[0] = -kk
    hw = jnp.tanh(hw_ref[0])
    ha = ha_ref[0]
    for s, (lw_ref, ks_ref, bb_ref) in enumerate(((lw0_ref, ks0_ref, bb0_ref), (lw1_ref, ks1_ref, bb1_ref))):
        wl = w0_ref[s:s + 1, :] + _dot(hw[:, s * rk:(s + 1) * rk].astype(BF16), w2_ref[s])
        lw_ref[0] = -jnp.exp(_log_sigmoid(wl) - 0.5)
        a = _sigmoid(a0_ref[s:s + 1, :] + _dot(ha[:, s * rk:(s + 1) * rk].astype(BF16), a2_ref[s]))
        ks_ref[0] = k * (1.0 + (a - 1.0) * ka_ref[...])
        bb_ref[0] = kk * a


def _rwkv_chunk(r, lw, ks, v, na, bb, ht, rev):
    c = r.shape[0]
    n = RWKV_HEAD
    c2 = 2 * c
    ri = lax.broadcasted_iota(jnp.int32, (c, c), 0)
    ci = lax.broadcasted_iota(jnp.int32, (c, c), 1)
    cs = _dot_sel(((ri <= ci) if rev else (ri >= ci)).astype(F32), lw)
    yield
    cm = cs - lw
    ctot = jnp.sum(lw, axis=0, keepdims=True)

    lane_a = lax.broadcasted_iota(jnp.int32, (1, LANES), 1) < n

    def stack(x):
        return jnp.concatenate([jnp.where(lane_a, x, 0.0), jnp.where(lane_a, 0.0, x)], axis=0)

    at = stack(na * jnp.exp(cm))
    rt = stack(r * jnp.exp(cs))
    ecs = jnp.exp(-cs)
    bh = stack(bb * ecs)
    kh = stack(ks * ecs)
    ece = jnp.exp(ctot - cs)
    be = stack(bb * ece)
    ke = stack(ks * ece)
    vs = stack(v)

    rr = lax.broadcasted_iota(jnp.int32, (c2, c2), 0)
    cc = lax.broadcasted_iota(jnp.int32, (c2, c2), 1)
    same = (rr // c) == (cc // c)
    tr_, tc_ = rr % c, cc % c
    strict = same & ((tc_ > tr_) if rev else (tc_ < tr_))
    incl = same & ((tc_ >= tr_) if rev else (tc_ <= tr_))
    diag_blk = same & ((tr_ // RWKV_SUB) == (tc_ // RWKV_SUB))
    eye = (rr == cc).astype(F32)

    lhs = jnp.concatenate([at, rt], axis=0)
    rhs = jnp.concatenate([bh, kh], axis=0)
    a4 = _dot_x3(lhs, rhs, _dot_nt)
    yield
    nmat = jnp.where(strict, a4[:c2, :c2], 0.0)
    aak = jnp.where(strict, a4[:c2, c2:], 0.0)
    arb = jnp.where(incl, a4[c2:, :c2], 0.0)
    ark = jnp.where(incl, a4[c2:, c2:], 0.0)

    htb = ht.astype(BF16)
    x0 = _dot_nt(at.astype(BF16), htb) + _dot(aak.astype(BF16), vs.astype(BF16))
    yield

    nd = jnp.where(diag_blk, nmat, 0.0)
    no = nmat - nd
    tinv = eye + nd
    pw = nd
    for _ in range(3):
        pw = _dot_x3(pw, pw)
        yield
        tinv = tinv + _dot_x3(pw, tinv)
        yield
    z = _dot_x3(tinv, no)
    u = _dot_x3(tinv, x0)
    yield
    z2 = _dot_x3(z, z)
    u = u + _dot_x3(z, u)
    yield
    u = u + _dot_x3(z2, u)
    yield

    ub = u.astype(BF16)
    vb = vs.astype(BF16)
    ys = _dot_nt(rt.astype(BF16), htb) + _dot(arb.astype(BF16), ub) + _dot(ark.astype(BF16), vb)
    yield
    hn = ht * jnp.exp(ctot) + _dot_tn(ub, be.astype(BF16)) + _dot_tn(vb, ke.astype(BF16))
    hr = lax.broadcasted_iota(jnp.int32, (LANES, LANES), 0) // n
    hc = lax.broadcasted_iota(jnp.int32, (LANES, LANES), 1) // n
    return ys[:c] + ys[c:], jnp.where(hr == hc, hn, 0.0)


def _rwkv_scan_body(*refs):
    ins, (y0_ref, y1_ref, st_ref) = refs[:12], refs[12:]

    @pl.when(pl.program_id(2) == 0)
    def _():
        st_ref[...] = jnp.zeros_like(st_ref)

    gens, dest = [], []
    for dr in range(2):
        r_ref, lw_ref, ks_ref, v_ref, na_ref, bb_ref = ins[6 * dr:6 * dr + 6]
        for p in range(r_ref.shape[2] // LANES):
            sl = slice(p * LANES, (p + 1) * LANES)
            gens.append(_rwkv_chunk(r_ref[0, :, sl], lw_ref[0, :, sl], ks_ref[0, :, sl], v_ref[0, :, sl],
                                    na_ref[0, :, sl], bb_ref[0, :, sl], st_ref[dr, p], rev=(dr == 1)))
            dest.append(((y0_ref, y1_ref)[dr], sl, dr, p))
    for (y_ref, sl, dr, p), (y, hn) in zip(dest, _lockstep(gens)):
        y_ref[0, :, sl] = y
        st_ref[dr, p] = hn


def rwkv_scan(r, v, na, lw, ks, bb):
    bsz, t, d = r.shape
    c = RWKV_CHUNK
    nc, ncx = t // c, CTX_LEN // c
    pp = _pick(d // LANES, (4, 2, 1))
    w = pp * LANES

    def rchunk(s):
        return jnp.where(s < ncx, ncx - 1 - s, nc - 1 - (s - ncx))

    fwd = pl.BlockSpec((1, c, w), lambda b, p, s: (b, s, p))
    bwd = pl.BlockSpec((1, c, w), lambda b, p, s: (b, rchunk(s), p))
    sh = jax.ShapeDtypeStruct((bsz, t, d), F32)
    return pl.pallas_call(
        _rwkv_scan_body,
        grid=(bsz, d // w, nc),
        in_specs=[fwd] * 6 + [bwd] * 6, out_specs=[fwd, bwd], out_shape=[sh, sh],
        scratch_shapes=[pltpu.VMEM((2, pp, LANES, LANES), F32)],
        compiler_params=_cparams("parallel", "parallel", "arbitrary"),
    )(r, lw[0], ks[0], v, na, bb[0], r, lw[1], ks[1], v, na, bb[1])


def _rwkv_out_body(y0_ref, y1_ref, r_ref, v_ref, g_ref, ks0_ref, ks1_ref, rk_ref, lg_ref, lb_ref, o_ref):
    e = _head_ones()
    y = y0_ref[0] + y1_ref[0]
    inv_n = 1.0 / RWKV_HEAD
    yc = y - _head_sum(y, e) * inv_n
    yn = yc * lax.rsqrt(_head_sum(yc * yc, e) * inv_n + RWKV_GN_EPS)
    yn = yn * lg_ref[...] + lb_ref[...]
    bonus = _head_sum(r_ref[0] * (ks0_ref[0] + ks1_ref[0]) * rk_ref[...], e) * v_ref[0]
    o_ref[0] = ((yn + bonus) * g_ref[0]).astype(o_ref.dtype)


def rwkv_mixer(u, mu, w_r, w_k, w_v, w_o, w0, w1, w2, a0, a1, a2, g1, g2, k_k, k_a, r_k, ln_g, ln_b):
    bsz, t, d = u.shape
    m = bsz * t
    dc = _pick(d, (256, 128))
    xs = pl.pallas_call(
        _rwkv_mix_body, grid=(bsz, d // dc),
        in_specs=[pl.BlockSpec((1, t, dc), lambda b, j: (b, 0, j)),
                  pl.BlockSpec((6, dc), lambda b, j: (0, j))],
        out_specs=[pl.BlockSpec((1, t, dc), lambda b, j: (b, 0, j))] * 6,
        out_shape=[jax.ShapeDtypeStruct((bsz, t, d), BF16)] * 6,
        compiler_params=_cparams("parallel", "parallel"),
    )(u, mu)
    xr, xw, xk, xv, xa, xg = (x.reshape(m, d) for x in xs)
    r = matmul(xr, w_r.astype(BF16)).reshape(bsz, t, d)
    k = matmul(xk, w_k.astype(BF16)).reshape(bsz, t, d)
    v = matmul(xv, w_v.astype(BF16)).reshape(bsz, t, d)
    rk = w1.shape[2]
    hw = matmul(xw, jnp.concatenate([w1[0], w1[1]], 1).astype(BF16)).reshape(bsz, t, 2 * rk)
    ha = matmul(xa, jnp.concatenate([a1[0], a1[1]], 1).astype(BF16)).reshape(bsz, t, 2 * rk)
    gr = g1.shape[1]
    grp = -(-gr // LANES) * LANES
    g1p = jnp.zeros((d, grp), F32).at[:, :gr].set(g1)
    g2p = jnp.zeros((grp, d), F32).at[:gr].set(g2)
    hg = jax.nn.sigmoid(matmul(xg, g1p.astype(BF16)))
    g = matmul(hg, g2p.astype(BF16)).reshape(bsz, t, d)

    tr = _pick(t, (64, 32, 16))
    row = pl.BlockSpec((1, tr, d), lambda b, i: (b, i, 0))
    low = pl.BlockSpec((1, tr, 2 * rk), lambda b, i: (b, i, 0))
    vec = pl.BlockSpec((1, d), lambda b, i: (0, 0))
    vec2 = pl.BlockSpec((2, d), lambda b, i: (0, 0))
    fact = pl.BlockSpec((2, rk, d), lambda b, i: (0, 0, 0))
    big = jax.ShapeDtypeStruct((bsz, t, d), F32)
    na, lw0, lw1, ks0, ks1, bb0, bb1 = pl.pallas_call(
        _rwkv_prep_body, grid=(bsz, t // tr),
        in_specs=[row, low, low, fact, fact, vec2, vec2, vec, vec],
        out_specs=[row] * 7, out_shape=[big] * 7,
        compiler_params=_cparams("parallel", "parallel"),
    )(k, hw, ha, w2.astype(BF16), a2.astype(BF16), w0, a0, k_k[None], k_a[None])

    y0, y1 = rwkv_scan(r, v, na, (lw0, lw1), (ks0, ks1), (bb0, bb1))

    out = pl.pallas_call(
        _rwkv_out_body, grid=(bsz, t // tr),
        in_specs=[row] * 7 + [vec, vec, vec], out_specs=row,
        out_shape=jax.ShapeDtypeStruct((bsz, t, d), BF16),
        compiler_params=_cparams("parallel", "parallel"),
    )(y0, y1, r, v, g, ks0, ks1, r_k.reshape(1, d), ln_g[None], ln_b[None])
    return matmul(out.reshape(m, d), w_o.astype(BF16)).reshape(bsz, t, d)


def _top_vals(s, kk):
    vals = []
    cur = s
    for _ in range(kk):
        mx = jnp.max(cur, axis=0, keepdims=True)
        vals.append(mx)
        cur = jnp.where(cur >= mx, NEG, cur)
    return jnp.concatenate(vals, axis=0)


def _peer_score_body(q_ref, keys_ref, s1_ref, s2_ref, e2_ref, cf_ref, tau_ref):
    nk = PEER_NKEYS
    kk = PEER_TOPK
    taus = []
    for h in range(PEER_HEADS):
        q1 = q_ref[h * 2 * nk:h * 2 * nk + nk, :].astype(BF16)
        q2 = q_ref[h * 2 * nk + nk:(h + 1) * 2 * nk, :].astype(BF16)
        s1 = _dot(keys_ref[0], q1)
        s2 = _dot(keys_ref[1], q2)
        a1 = _top_vals(s1, kk)
        a2 = _top_vals(s2, kk)
        cand = (a1[:, None, :] + a2[None, :, :]).reshape(kk * kk, -1)
        cv = _top_vals(cand, kk)
        tau = cv[kk - 1:kk]
        zsum = jnp.sum(jnp.where(cand >= tau, jnp.exp(cand - cv[0:1]), 0.0), axis=0, keepdims=True)
        s1_ref[h] = s1
        s2_ref[h] = s2
        e2_ref[h] = jnp.exp(s2 - a2[0:1])
        cf_ref[h] = jnp.exp(s1 - a1[0:1]) / zsum
        taus.append(tau)
    tau_ref[...] = jnp.concatenate(taus, axis=0)


def _gelu(x):
    return 0.5 * x * (1.0 + lax.erf(x * (2.0 ** -0.5)))


def _peer_dense_body(z_ref, u_ref, v_ref, s1_ref, s2_ref, e2_ref, cf_ref, tau_ref, o_ref):
    nk = PEER_NKEYS
    eb = pl.program_id(1)
    te = u_ref.shape[0]

    @pl.when(eb == 0)
    def _():
        o_ref[...] = jnp.zeros_like(o_ref)

    sub = min(PEER_TOK_SUB, z_ref.shape[1])

    esub = min(PEER_EXP_SUB, te)

    def chain(ts, e0):
        act = _dot(u_ref[e0:e0 + esub, :], z_ref[:, ts])
        yield
        ws = []
        for ii in range(e0 // nk, (e0 + esub) // nk):
            i = eb * (te // nk) + ii
            w = jnp.zeros((nk, sub), F32)
            for h in range(PEER_HEADS):
                hit = (s2_ref[h, :, ts] + s1_ref[h, pl.ds(i, 1), ts]) >= tau_ref[h:h + 1, ts]
                w = w + jnp.where(hit, e2_ref[h, :, ts], 0.0) * cf_ref[h, pl.ds(i, 1), ts]
            ws.append(w)
        g = (jnp.concatenate(ws, axis=0) * _gelu(act)).astype(BF16)
        o_ref[:, ts] += _dot(v_ref[:, e0:e0 + esub], g)

    _lockstep([chain(slice(t0, t0 + sub), e0)
               for t0 in range(0, z_ref.shape[1], sub) for e0 in range(0, te, esub)])


def peer_ffn(zt, w_q, sub_keys, u_tab, v_tab):
    d, m = zt.shape
    nh, nk = PEER_HEADS, PEER_NKEYS
    qt = matmul(w_q.T.astype(BF16), zt, F32)
    tt = _pick(m, (256, 128))
    sh = jax.ShapeDtypeStruct((nh, nk, m), F32)
    blk = pl.BlockSpec((nh, nk, tt), lambda i: (0, 0, i))
    s1, s2, e2, cf, tau = pl.pallas_call(
        _peer_score_body, grid=(m // tt,),
        in_specs=[pl.BlockSpec((nh * 2 * nk, tt), lambda i: (0, i)),
                  pl.BlockSpec((2, nk, PEER_DKEY // 2), lambda i: (0, 0, 0))],
        out_specs=[blk, blk, blk, blk, pl.BlockSpec((nh, tt), lambda i: (0, i))],
        out_shape=[sh, sh, sh, sh, jax.ShapeDtypeStruct((nh, m), F32)],
        compiler_params=_cparams("parallel"),
    )(qt, sub_keys.astype(BF16))

    tm = _pick(m, (2 * PEER_TOK_SUB, PEER_TOK_SUB, LANES))
    te = 512
    ne = u_tab.shape[0]
    once = pl.Buffered(1)
    sblk = pl.BlockSpec((nh, nk, tm), lambda i, e: (0, 0, i), pipeline_mode=once)
    return pl.pallas_call(
        _peer_dense_body, grid=(m // tm, ne // te),
        in_specs=[pl.BlockSpec((d, tm), lambda i, e: (0, i), pipeline_mode=once),
                  pl.BlockSpec((te, d), lambda i, e: (e, 0)),
                  pl.BlockSpec((d, te), lambda i, e: (0, e)),
                  sblk, sblk, sblk, sblk,
                  pl.BlockSpec((nh, tm), lambda i, e: (0, i), pipeline_mode=once)],
        out_specs=pl.BlockSpec((d, tm), lambda i, e: (0, i)),
        out_shape=jax.ShapeDtypeStruct((d, m), F32),
        compiler_params=_cparams("parallel", "arbitrary"),
    )(zt, u_tab.astype(BF16), v_tab.T.astype(BF16), s1, s2, e2, cf, tau)


def _ada_mods(c, c_ctx, w_down, w_up, b_up):
    bsz, d = c.shape
    cond = jnp.concatenate([c, c_ctx[None]], 0)
    pad = 16 - cond.shape[0] % 16
    cond = jnp.concatenate([cond, jnp.zeros((pad, d), F32)], 0)
    hid = matmul(jax.nn.silu(cond), w_down.astype(BF16), F32)
    m = (matmul(hid, w_up.astype(BF16), F32) + b_up)[:bsz + 1].reshape(bsz + 1, N_MOD, d)
    return jnp.stack([jnp.broadcast_to(m[bsz], (bsz, N_MOD, d)), m[:bsz]], axis=1)


def kernel(x, c, ctx, c_ctx, ada_w_down, ada_w_up, ada_b,
           gla_w_in, gla_w_g1, gla_w_g2, gla_b_g, gla_norm_g, gla_w_o,
           mla_w_in, mla_q_norm, mla_kv_norm, mla_w_uq, mla_w_ukv, mla_w_o,
           rwkv_mu, rwkv_w_r, rwkv_w_k, rwkv_w_v, rwkv_w_o, rwkv_w0, rwkv_w1, rwkv_w2,
           rwkv_a0, rwkv_a1, rwkv_a2, rwkv_g1, rwkv_g2, rwkv_k_k, rwkv_k_a, rwkv_r_k, rwkv_ln_g, rwkv_ln_b,
           peer_w_q, peer_sub_keys, peer_u, peer_v):
    z = jnp.concatenate([ctx, x], axis=1)
    bsz, t, d = z.shape
    mods = [_ada_mods(c, c_ctx, ada_w_down[i], ada_w_up[i], ada_b[i]) for i in range(DEPTH)]
    _, u = ln_mod(z, mods_n=mods[0], sidx=0)
    for i in range(DEPTH):
        j = i // N_MIXERS
        if i % N_MIXERS == 0:
            y = gla_mixer(u, gla_w_in[j], gla_w_g1[j], gla_w_g2[j], gla_b_g[j], gla_norm_g[j], gla_w_o[j])
        elif i % N_MIXERS == 1:
            y = mla_mixer(u, mla_w_in[j], mla_q_norm[j], mla_kv_norm[j], mla_w_uq[j], mla_w_ukv[j], mla_w_o[j])
        else:
            y = rwkv_mixer(u, rwkv_mu[j], rwkv_w_r[j], rwkv_w_k[j], rwkv_w_v[j], rwkv_w_o[j],
                           rwkv_w0[j], rwkv_w1[j], rwkv_w2[j], rwkv_a0[j], rwkv_a1[j], rwkv_a2[j],
                           rwkv_g1[j], rwkv_g2[j], rwkv_k_k[j], rwkv_k_a[j], rwkv_r_k[j],
                           rwkv_ln_g[j], rwkv_ln_b[j])
        z, ut = ln_mod(z, y, mods[i], 2, mods[i], 3, u_t=True)
        ht = peer_ffn(ut, peer_w_q[i], peer_sub_keys[i], peer_u[i], peer_v[i])
        if i + 1 < DEPTH:
            z, u = ln_mod(z, ht, mods[i], 5, mods[i + 1], 0, y_t=True)
        else:
            z, _ = ln_mod(z, ht, mods[i], 5, y_t=True)
    return z[:, CTX_LEN:]
```

```python
import functools

import jax
import jax.numpy as jnp
from jax import lax
from jax.experimental import pallas as pl
from jax.experimental.pallas import tpu as pltpu

F32 = jnp.float32
BF16 = jnp.bfloat16

DEPTH = 4
CTX_LEN = 256
GRID_W = 64
N_MIXERS = 3
N_MOD = 6
LN_EPS = 1e-6
DEEPNORM_ALPHA = (2.0 * DEPTH) ** 0.25

GLA_HEADS = 8
GLA_GATE_RANK = 16
GLA_GATE_NORMALIZER = 16.0

MLA_Q_RANK = 1536
MLA_KV_RANK = 512
MLA_NOPE = 128
MLA_ROPE = 64
MLA_V = 128
ROPE_THETA = 10000.0

RWKV_HEAD = 64
RWKV_GN_EPS = 64e-5

PEER_HEADS = 8
PEER_NKEYS = 128
PEER_DKEY = 256
PEER_TOPK = 16

LANES = 128
SUBLANES = 8
VMEM_LIMIT = 52 * 1024 * 1024

GLA_CHUNK = 64
GLA_SUB = 16
RWKV_CHUNK = 64
RWKV_SUB = 16
MLA_Q_CHAINS = 1
PEER_TOK_SUB = 256
PEER_EXP_SUB = 256
PEER_STAGES = 8
PEER_ROW_STRIP = 32
NEG = -1e30


def _pick(n, cands):
    for c in cands:
        if n % c == 0:
            return c
    return n


def _cparams(*sem):
    return pltpu.CompilerParams(dimension_semantics=sem, vmem_limit_bytes=VMEM_LIMIT)


def _dot(a, b):
    return jnp.dot(a, b, preferred_element_type=F32)


def _dot_nt(a, b):
    return lax.dot_general(a, b, (((1,), (1,)), ((), ())), preferred_element_type=F32)


def _dot_tn(a, b):
    return lax.dot_general(a, b, (((0,), (0,)), ((), ())), preferred_element_type=F32)


def _split3(x):
    hi = x.astype(BF16)
    r1 = x - hi.astype(F32)
    mid = r1.astype(BF16)
    lo = (r1 - mid.astype(F32)).astype(BF16)
    return hi, mid, lo


def _dot_sel(sel, x):
    s = sel.astype(BF16)
    hi, mid, lo = _split3(x)
    return _dot(s, hi) + _dot(s, mid) + _dot(s, lo)


def _dot_x3(a, b, dims=None):
    ah = a.astype(BF16)
    al = (a - ah.astype(F32)).astype(BF16)
    bh = b.astype(BF16)
    bl = (b - bh.astype(F32)).astype(BF16)
    f = _dot if dims is None else dims
    return f(ah, bh) + f(ah, bl) + f(al, bh)


def _dot_b(a, b):
    return _dot(a.astype(BF16), b.astype(BF16))


def _log_sigmoid(x):
    return jnp.minimum(x, 0.0) - jnp.log1p(jnp.exp(-jnp.abs(x)))


def _sigmoid(x):
    return 1.0 / (1.0 + jnp.exp(-x))


def _mm_body(a_ref, b_ref, o_ref):
    o_ref[...] = _dot(a_ref[...].astype(BF16), b_ref[...].astype(BF16)).astype(o_ref.dtype)


def matmul(a, b, out_dtype=F32):
    m, k = a.shape
    k2, n = b.shape
    assert k == k2
    a_bytes = jnp.dtype(a.dtype).itemsize
    tm_cands = (1024, 512, 256, 128, 64, 32, 16) if a_bytes * k <= 8192 else (512, 256, 128, 64, 32, 16)
    tm = _pick(m, tm_cands)
    tn = _pick(n, (512, 256, 128))
    return pl.pallas_call(
        _mm_body,
        grid=(m // tm, n // tn),
        in_specs=[pl.BlockSpec((tm, k), lambda i, j: (i, 0)),
                  pl.BlockSpec((k, tn), lambda i, j: (0, j))],
        out_specs=pl.BlockSpec((tm, tn), lambda i, j: (i, j)),
        out_shape=jax.ShapeDtypeStruct((m, n), out_dtype),
        compiler_params=_cparams("parallel", "parallel"),
    )(a, b)


def _layer_norm(z):
    zc = z - jnp.mean(z, -1, keepdims=True)
    return zc * lax.rsqrt(jnp.mean(zc * zc, -1, keepdims=True) + LN_EPS)


def _ln_mod_body(*refs, gidx, sidx, has_y, y_t, u_t):
    it = iter(refs)
    z_ref = next(it)
    y_ref = next(it) if has_y else None
    mg_ref = next(it) if has_y else None
    mn_ref = next(it) if sidx is not None else None
    zo_ref = next(it) if has_y else None
    u_ref = next(it) if sidx is not None else None
    z = z_ref[0]
    if has_y:
        y = y_ref[...].T if y_t else y_ref[0]
        z = _layer_norm(DEEPNORM_ALPHA * z + mg_ref[0, 0, gidx:gidx + 1, :] * y)
        zo_ref[0] = z
    if sidx is not None:
        u = z * (1.0 + mn_ref[0, 0, sidx + 1:sidx + 2, :]) + mn_ref[0, 0, sidx:sidx + 1, :]
        if u_t:
            u_ref[...] = u.T.astype(u_ref.dtype)
        else:
            u_ref[0] = u.astype(u_ref.dtype)


def ln_mod(z, y=None, mods_g=None, gidx=None, mods_n=None, sidx=None, y_t=False, u_t=False):
    bsz, t, d = z.shape
    tr = _pick(CTX_LEN, (256, 128, 64, 32, 16))
    nt = t // tr
    nctx = CTX_LEN // tr
    has_y = y is not None
    row = pl.BlockSpec((1, tr, d), lambda b, i: (b, i, 0))
    col = pl.BlockSpec((d, tr), lambda b, i: (0, b * nt + i))
    mod = pl.BlockSpec((1, 1, N_MOD, d), lambda b, i: (b, jnp.where(i < nctx, 0, 1), 0, 0))
    ins, in_specs, outs, out_specs = [z], [row], [], []
    if has_y:
        ins += [y, mods_g]
        in_specs += [col if y_t else row, mod]
        outs.append(jax.ShapeDtypeStruct((bsz, t, d), F32))
        out_specs.append(row)
    if sidx is not None:
        ins.append(mods_n)
        in_specs.append(mod)
        outs.append(jax.ShapeDtypeStruct((d, bsz * t) if u_t else (bsz, t, d), BF16))
        out_specs.append(col if u_t else row)
    res = pl.pallas_call(
        functools.partial(_ln_mod_body, gidx=gidx, sidx=sidx, has_y=has_y, y_t=y_t, u_t=u_t),
        grid=(bsz, nt), in_specs=in_specs, out_specs=out_specs, out_shape=outs,
        compiler_params=_cparams("parallel", "parallel"),
    )(*ins)
    res = list(res)
    zo = res.pop(0) if has_y else None
    u = res.pop(0) if sidx is not None else None
    return zo, u


def _lockstep(gens):
    out = [None] * len(gens)
    live = list(range(len(gens)))
    while live:
        for i in list(live):
            try:
                next(gens[i])
            except StopIteration as done:
                out[i] = done.value
                live.remove(i)
    return out


def _gla_chunk(q, k, v, glow, wg2, bg, st, rev):
    c, dk = q.shape
    sub = GLA_SUB
    nsub = c // sub
    q = q.astype(F32) * (dk ** -0.5)
    k = k.astype(F32)
    gl = _dot_x3(glow, wg2) + bg
    yield
    g = _log_sigmoid(gl) * (1.0 / GLA_GATE_NORMALIZER)
    row = lax.broadcasted_iota(jnp.int32, (c, c), 0)
    col = lax.broadcasted_iota(jnp.int32, (c, c), 1)
    tri = (row <= col) if rev else (row >= col)
    b = _dot_sel(tri.astype(F32), g)
    yield
    btot = jnp.sum(g, axis=0, keepdims=True)

    o = _dot_nt((q * jnp.exp(b)).astype(BF16), st.astype(BF16))

    rowk = lax.broadcasted_iota(jnp.int32, (c, 1), 0)
    blocks = []
    for i in range(nsub):
        lo, hi = i * sub, (i + 1) * sub
        if (rev and i == nsub - 1) or (not rev and i == 0):
            blocks.append(jnp.zeros((sub, c), F32))
            continue
        ref = b[hi:hi + 1] if rev else b[lo - 1:lo]
        qi = q[lo:hi] * jnp.exp(b[lo:hi] - ref)
        other = (rowk >= hi) if rev else (rowk < lo)
        kj = jnp.where(other, k * jnp.exp(jnp.minimum(ref - b, 0.0)), 0.0)
        blocks.append(_dot_nt(qi.astype(BF16), kj.astype(BF16)))
    att = jnp.concatenate(blocks, axis=0)
    yield

    rmod = rowk % sub
    for lag in range(sub):
        sh = (c - lag) % c if rev else lag
        ks = pltpu.roll(k, sh, 0) if sh else k
        bs = pltpu.roll(b, sh, 0) if sh else b
        term = jnp.sum(q * ks * jnp.exp(jnp.minimum(b - bs, 0.0)), axis=1, keepdims=True)
        valid = (rmod + lag < sub) if rev else (rmod >= lag)
        hit = (col == row + lag) if rev else (col == row - lag)
        att = att + jnp.where(hit & valid, term, 0.0)

    o = o + _dot(att.astype(BF16), v)
    yield
    kd = (k * jnp.exp(btot - b)).astype(BF16)
    return o, st * jnp.exp(btot) + _dot_tn(v, kd)


def _gla_body(*refs):
    ins, (of_ref, ob_ref, st_ref) = refs[:12], refs[12:]

    @pl.when(pl.program_id(2) == 0)
    def _():
        st_ref[...] = jnp.zeros_like(st_ref)

    gens = []
    for dr in range(2):
        q_ref, k_ref, v_ref, gl_ref, wg2_ref, bg_ref = ins[6 * dr:6 * dr + 6]
        gens.append(_gla_chunk(q_ref[0], k_ref[0], v_ref[0], gl_ref[0], wg2_ref[...], bg_ref[...],
                               st_ref[dr], rev=(dr == 1)))
    for dr, (o, st) in enumerate(_lockstep(gens)):
        (of_ref, ob_ref)[dr][0] = o
        st_ref[dr] = st


def gla_scan(p, glow, wg2, bg):
    bsz, t, d3 = p.shape
    d = d3 // 3
    h = GLA_HEADS
    dk, dv = (d // 2) // h, d // h
    c = GLA_CHUNK
    nc, ncx = t // c, CTX_LEN // c

    def rchunk(s):
        return jnp.where(s < ncx, ncx - 1 - s, nc - 1 - (s - ncx))

    def specs(chunk):
        return [pl.BlockSpec((1, c, dk), lambda b, hh, s: (b, chunk(s), hh)),
                pl.BlockSpec((1, c, dk), lambda b, hh, s: (b, chunk(s), h + hh)),
                pl.BlockSpec((1, c, dv), lambda b, hh, s: (b, chunk(s), h + hh)),
                pl.BlockSpec((1, c, LANES), lambda b, hh, s: (b, chunk(s), 0)),
                pl.BlockSpec((LANES, dk), lambda b, hh, s: (0, hh)),
                pl.BlockSpec((1, dk), lambda b, hh, s: (0, hh))]

    sh = jax.ShapeDtypeStruct((bsz, t, d), F32)
    return pl.pallas_call(
        _gla_body,
        grid=(bsz, h, nc),
        in_specs=specs(lambda s: s) + specs(rchunk),
        out_specs=[pl.BlockSpec((1, c, dv), lambda b, hh, s: (b, s, hh)),
                   pl.BlockSpec((1, c, dv), lambda b, hh, s: (b, rchunk(s), hh))],
        out_shape=[sh, sh],
        scratch_shapes=[pltpu.VMEM((2, dv, dk), F32)],
        compiler_params=_cparams("parallel", "parallel", "arbitrary"),
    )(p, p, p, glow, wg2[0], bg[0], p, p, p, glow, wg2[1], bg[1])


def _gla_gate_body(of_ref, ob_ref, r_ref, g_ref, o_ref):
    dv = g_ref.shape[1]
    o = of_ref[0] + ob_ref[0]
    r = r_ref[0].astype(F32)
    outs = []
    for hh in range(o.shape[1] // dv):
        seg = o[:, hh * dv:(hh + 1) * dv]
        outs.append(seg * lax.rsqrt(jnp.mean(seg * seg, -1, keepdims=True) + 1e-6) * g_ref[...])
    o_ref[0] = (jnp.concatenate(outs, axis=1) * (r * _sigmoid(r))).astype(o_ref.dtype)


def gla_gate(o_f, o_b, p, norm_g):
    bsz, t, d = o_f.shape
    tr = _pick(t, (256, 128, 64, 32, 16))
    row = pl.BlockSpec((1, tr, d), lambda b, i: (b, i, 0))
    return pl.pallas_call(
        _gla_gate_body, grid=(bsz, t // tr),
        in_specs=[row, row, pl.BlockSpec((1, tr, d), lambda b, i: (b, i, 2)),
                  pl.BlockSpec((1, norm_g.shape[1]), lambda b, i: (0, 0))],
        out_specs=row, out_shape=jax.ShapeDtypeStruct((bsz, t, d), BF16),
        compiler_params=_cparams("parallel", "parallel"),
    )(o_f, o_b, p, norm_g)


def gla_mixer(u, w_in, w_g1, w_g2, b_g, norm_g, w_o):
    bsz, t, d = u.shape
    qk = d // 2
    u2 = u.reshape(bsz * t, d)
    p = matmul(u2, w_in.astype(BF16), BF16).reshape(bsz, t, 3 * d)
    r = GLA_GATE_RANK
    wg1 = jnp.zeros((d, LANES), F32).at[:, :r].set(w_g1[0]).at[:, r:2 * r].set(w_g1[1])
    glow = matmul(u2, wg1.astype(BF16), F32).reshape(bsz, t, LANES)
    wg2 = [jnp.zeros((LANES, qk), F32).at[s * r:(s + 1) * r].set(w_g2[s]) for s in range(2)]
    outs = gla_scan(p, glow, wg2, [b_g[0][None], b_g[1][None]])
    gated = gla_gate(outs[0], outs[1], p, norm_g[None])
    return matmul(gated.reshape(bsz * t, d), w_o.astype(BF16), F32).reshape(bsz, t, d)


def _rms(x, gain):
    return x * lax.rsqrt(jnp.mean(x * x, -1, keepdims=True) + 1e-6) * gain


def _rope128(a, c, s):
    return a * c + pltpu.roll(a, LANES // 2, 1) * s


def _mla_norm_body(h_ref, qg_ref, kg_ref, c_ref, s_ref, cq_ref, ckv_ref, kr_ref):
    qr, kvr = cq_ref.shape[2], ckv_ref.shape[2]
    h = h_ref[0]
    cq_ref[0] = _rms(h[:, :qr], qg_ref[...]).astype(BF16)
    ckv_ref[0] = _rms(h[:, qr:qr + kvr], kg_ref[...]).astype(BF16)
    kr_ref[0] = _rope128(h[:, qr + kvr:], c_ref[...], s_ref[...]).astype(BF16)


def _mla_attn_body(q_ref, kn_ref, v_ref, kr_ref, c_ref, s_ref, o_ref, kf_ref, *, nctx_tiles, scale):
    qt = pl.program_id(2)
    ctx = nctx_tiles * q_ref.shape[1]

    @pl.when(qt == 0)
    def _():
        kf_ref[:, :LANES] = kn_ref[0]
        kf_ref[:, LANES:] = kr_ref[0]

    q = q_ref[0].astype(F32)
    qr = _rope128(q[:, LANES:], c_ref[...], s_ref[...])
    qf = (jnp.concatenate([q[:, :LANES], qr], axis=1) * scale).astype(BF16)

    def attend(kf, v, rows):
        s = _dot_nt(qf[rows], kf)
        yield
        p = jnp.exp(s - jnp.max(s, -1, keepdims=True))
        l = jnp.sum(p, -1, keepdims=True)
        o_ref[0, rows] = (_dot(p.astype(BF16), v) / l).astype(o_ref.dtype)

    nq = qf.shape[0]
    halves = [slice(r0, r0 + nq // MLA_Q_CHAINS) for r0 in range(0, nq, nq // MLA_Q_CHAINS)]

    @pl.when(qt < nctx_tiles)
    def _():
        _lockstep([attend(kf_ref[:ctx], v_ref[0, :ctx], rows) for rows in halves])

    @pl.when(qt >= nctx_tiles)
    def _():
        _lockstep([attend(kf_ref[...], v_ref[0], rows) for rows in halves])


def _rope_tables(t):
    rows = (t - CTX_LEN) // GRID_W
    row = jnp.repeat(jnp.arange(rows, dtype=F32), GRID_W)
    colp = jnp.tile(jnp.arange(GRID_W, dtype=F32), rows)
    n_freq = MLA_ROPE // 4
    inv = ROPE_THETA ** (-jnp.arange(n_freq, dtype=F32) / n_freq)
    ang = jnp.concatenate([row[:, None] * inv, colp[:, None] * inv], -1)
    cos = jnp.concatenate([jnp.ones((CTX_LEN, MLA_ROPE // 2), F32), jnp.cos(ang)], 0)
    sin = jnp.concatenate([jnp.zeros((CTX_LEN, MLA_ROPE // 2), F32), jnp.sin(ang)], 0)
    z = jnp.zeros_like(cos)
    return jnp.concatenate([cos, cos, z, z], 1), jnp.concatenate([-sin, sin, z, z], 1)


def mla_mixer(u, w_in, q_norm, kv_norm, w_uq, w_ukv, w_o):
    bsz, t, d = u.shape
    nh = d // 128
    qr, kvr, rp = MLA_Q_RANK, MLA_KV_RANK, MLA_ROPE
    ev, od = jnp.arange(0, rp, 2), jnp.arange(1, rp, 2)
    perm = jnp.concatenate([ev, od, od, ev])
    w_in_p = jnp.concatenate([w_in[:, :qr + kvr], w_in[:, qr + kvr + perm]], axis=1)
    hw = qr + kvr + LANES
    h = matmul(u.reshape(bsz * t, d), w_in_p.astype(BF16), F32).reshape(bsz, t, hw)
    ctab, stab = _rope_tables(t)

    tr = _pick(t, (256, 128, 64, 32, 16))
    cq, ckv, kr = pl.pallas_call(
        _mla_norm_body, grid=(bsz, t // tr),
        in_specs=[pl.BlockSpec((1, tr, hw), lambda b, i: (b, i, 0)),
                  pl.BlockSpec((1, qr), lambda b, i: (0, 0)),
                  pl.BlockSpec((1, kvr), lambda b, i: (0, 0)),
                  pl.BlockSpec((tr, LANES), lambda b, i: (i, 0)),
                  pl.BlockSpec((tr, LANES), lambda b, i: (i, 0))],
        out_specs=[pl.BlockSpec((1, tr, qr), lambda b, i: (b, i, 0)),
                   pl.BlockSpec((1, tr, kvr), lambda b, i: (b, i, 0)),
                   pl.BlockSpec((1, tr, LANES), lambda b, i: (b, i, 0))],
        out_shape=[jax.ShapeDtypeStruct((bsz, t, qr), BF16),
                   jax.ShapeDtypeStruct((bsz, t, kvr), BF16),
                   jax.ShapeDtypeStruct((bsz, t, LANES), BF16)],
        compiler_params=_cparams("parallel", "parallel"),
    )(h, q_norm[None], kv_norm[None], ctab, stab)

    hq = MLA_NOPE + rp
    qcols = (jnp.arange(nh)[:, None] * hq
             + jnp.concatenate([jnp.arange(MLA_NOPE), MLA_NOPE + perm])[None, :]).reshape(-1)
    q = matmul(cq.reshape(bsz * t, qr), w_uq[:, qcols].astype(BF16), BF16).reshape(bsz, t, nh * 2 * LANES)
    kv = matmul(ckv.reshape(bsz * t, kvr), w_ukv.astype(BF16), BF16).reshape(bsz, t, nh * 2 * LANES)

    tq = _pick(CTX_LEN, (256, 128, 64, 32, 16))
    scale = (MLA_NOPE + rp) ** -0.5
    o = pl.pallas_call(
        functools.partial(_mla_attn_body, nctx_tiles=CTX_LEN // tq, scale=scale),
        grid=(bsz, nh, t // tq),
        in_specs=[pl.BlockSpec((1, tq, 2 * LANES), lambda b, hh, i: (b, i, hh)),
                  pl.BlockSpec((1, t, LANES), lambda b, hh, i: (b, 0, 2 * hh)),
                  pl.BlockSpec((1, t, LANES), lambda b, hh, i: (b, 0, 2 * hh + 1)),
                  pl.BlockSpec((1, t, LANES), lambda b, hh, i: (b, 0, 0)),
                  pl.BlockSpec((tq, LANES), lambda b, hh, i: (i, 0)),
                  pl.BlockSpec((tq, LANES), lambda b, hh, i: (i, 0))],
        out_specs=pl.BlockSpec((1, tq, LANES), lambda b, hh, i: (b, i, hh)),
        out_shape=jax.ShapeDtypeStruct((bsz, t, nh * MLA_V), BF16),
        scratch_shapes=[pltpu.VMEM((t, 2 * LANES), BF16)],
        compiler_params=_cparams("parallel", "parallel", "arbitrary"),
    )(q, kv, kv, kr, ctab, stab)
    return matmul(o.reshape(bsz * t, nh * MLA_V), w_o.astype(BF16), F32).reshape(bsz, t, d)


def _rwkv_mix_body(u_ref, mu_ref, *o_refs):
    t = u_ref.shape[1]
    u = u_ref[0].astype(F32)
    row = lax.broadcasted_iota(jnp.int32, (t, 1), 0)
    prev = jnp.where((row == 0) | (row == CTX_LEN), 0.0, pltpu.roll(u, 1, 0))
    nxt = jnp.where((row == CTX_LEN - 1) | (row == t - 1), 0.0, pltpu.roll(u, t - 1, 0))
    xx = 0.5 * (prev + nxt) - u
    for n, o_ref in enumerate(o_refs):
        o_ref[0] = (u + xx * mu_ref[n:n + 1, :]).astype(o_ref.dtype)


def _head_sum(x, e):
    eb = e.astype(BF16)
    outs = []
    for s in range(x.shape[1] // LANES):
        hi, mid, lo = _split3(x[:, s * LANES:(s + 1) * LANES])
        outs.append(_dot(hi, eb) + _dot(mid, eb) + _dot(lo, eb))
    return jnp.concatenate(outs, axis=1)


def _head_ones():
    r = lax.broadcasted_iota(jnp.int32, (LANES, LANES), 0) // RWKV_HEAD
    c = lax.broadcasted_iota(jnp.int32, (LANES, LANES), 1) // RWKV_HEAD
    return (r == c).astype(F32)


def _rwkv_prep_body(k_ref, hw_ref, ha_ref, w2_ref, a2_ref, w0_ref, a0_ref, kk_ref, ka_ref,
                    na_ref, lw0_ref, lw1_ref, ks0_ref, ks1_ref, bb0_ref, bb1_ref):
    rk = w2_ref.shape[1]
    k = k_ref[0]
    kk = k * kk_ref[...]
    kk = kk * lax.rsqrt(_head_sum(kk * kk, _head_ones()) + 1e-12)
    na_ref[0] = -kk
    hw = jnp.tanh(hw_ref[0])
    ha = ha_ref[0]
    for s, (lw_ref, ks_ref, bb_ref) in enumerate(((lw0_ref, ks0_ref, bb0_ref), (lw1_ref, ks1_ref, bb1_ref))):
        wl = w0_ref[s:s + 1, :] + _dot(hw[:, s * rk:(s + 1) * rk].astype(BF16), w2_ref[s])
        lw_ref[0] = -jnp.exp(_log_sigmoid(wl) - 0.5)
        a = _sigmoid(a0_ref[s:s + 1, :] + _dot(ha[:, s * rk:(s + 1) * rk].astype(BF16), a2_ref[s]))
        ks_ref[0] = k * (1.0 + (a - 1.0) * ka_ref[...])
        bb_ref[0] = kk * a


def _rwkv_chunk(r, lw, ks, v, na, bb, ht, rev):
    c = r.shape[0]
    n = RWKV_HEAD
    c2 = 2 * c
    ri = lax.broadcasted_iota(jnp.int32, (c, c), 0)
    ci = lax.broadcasted_iota(jnp.int32, (c, c), 1)
    cs = _dot_sel(((ri <= ci) if rev else (ri >= ci)).astype(F32), lw)
    yield
    cm = cs - lw
    ctot = jnp.sum(lw, axis=0, keepdims=True)

    lane_a = lax.broadcasted_iota(jnp.int32, (1, LANES), 1) < n

    def stack(x):
        return jnp.concatenate([jnp.where(lane_a, x, 0.0), jnp.where(lane_a, 0.0, x)], axis=0)

    at = stack(na * jnp.exp(cm))
    rt = stack(r * jnp.exp(cs))
    ecs = jnp.exp(-cs)
    bh = stack(bb * ecs)
    kh = stack(ks * ecs)
    ece = jnp.exp(ctot - cs)
    be = stack(bb * ece)
    ke = stack(ks * ece)
    vs = stack(v)

    rr = lax.broadcasted_iota(jnp.int32, (c2, c2), 0)
    cc = lax.broadcasted_iota(jnp.int32, (c2, c2), 1)
    same = (rr // c) == (cc // c)
    tr_, tc_ = rr % c, cc % c
    strict = same & ((tc_ > tr_) if rev else (tc_ < tr_))
    incl = same & ((tc_ >= tr_) if rev else (tc_ <= tr_))
    diag_blk = same & ((tr_ // RWKV_SUB) == (tc_ // RWKV_SUB))
    eye = (rr == cc).astype(F32)

    lhs = jnp.concatenate([at, rt], axis=0)
    rhs = jnp.concatenate([bh, kh], axis=0)
    a4 = _dot_nt(lhs.astype(BF16), rhs.astype(BF16))
    nmat = jnp.where(strict, _dot_x3(at, bh, _dot_nt), 0.0)
    yield
    aak = jnp.where(strict, a4[:c2, c2:], 0.0)
    arb = jnp.where(incl, a4[c2:, :c2], 0.0)
    ark = jnp.where(incl, a4[c2:, c2:], 0.0)

    htb = ht.astype(BF16)
    x0 = _dot_nt(at.astype(BF16), htb) + _dot(aak.astype(BF16), vs.astype(BF16))
    yield

    nd = jnp.where(diag_blk, nmat, 0.0)
    no = nmat - nd
    tm = nd
    pw = nd
    for _ in range(3):
        pw = _dot_b(pw, pw)
        yield
        tm = tm + pw + _dot_b(pw, tm)
        yield
    z = no + _dot_b(tm, no)
    u = x0 + _dot_b(tm, x0)
    yield
    z2 = _dot_b(z, z)
    u = u + _dot_b(z, u)
    yield
    u = u + _dot_b(z2, u)
    yield

    ub = u.astype(BF16)
    vb = vs.astype(BF16)
    ys = _dot_nt(rt.astype(BF16), htb) + _dot(arb.astype(BF16), ub) + _dot(ark.astype(BF16), vb)
    yield
    hn = ht * jnp.exp(ctot) + _dot_tn(ub, be.astype(BF16)) + _dot_tn(vb, ke.astype(BF16))
    hr = lax.broadcasted_iota(jnp.int32, (LANES, LANES), 0) // n
    hc = lax.broadcasted_iota(jnp.int32, (LANES, LANES), 1) // n
    return ys[:c] + ys[c:], jnp.where(hr == hc, hn, 0.0)


def _rwkv_scan_body(*refs):
    ins, (y0_ref, y1_ref, st_ref) = refs[:12], refs[12:]

    @pl.when(pl.program_id(2) == 0)
    def _():
        st_ref[...] = jnp.zeros_like(st_ref)

    gens, dest = [], []
    for dr in range(2):
        r_ref, lw_ref, ks_ref, v_ref, na_ref, bb_ref = ins[6 * dr:6 * dr + 6]
        for p in range(r_ref.shape[2] // LANES):
            sl = slice(p * LANES, (p + 1) * LANES)
            gens.append(_rwkv_chunk(r_ref[0, :, sl], lw_ref[0, :, sl], ks_ref[0, :, sl], v_ref[0, :, sl],
                                    na_ref[0, :, sl], bb_ref[0, :, sl], st_ref[dr, p], rev=(dr == 1)))
            dest.append(((y0_ref, y1_ref)[dr], sl, dr, p))
    for (y_ref, sl, dr, p), (y, hn) in zip(dest, _lockstep(gens)):
        y_ref[0, :, sl] = y
        st_ref[dr, p] = hn


def rwkv_scan(r, v, na, lw, ks, bb):
    bsz, t, d = r.shape
    c = RWKV_CHUNK
    nc, ncx = t // c, CTX_LEN // c
    pp = _pick(d // LANES, (4, 2, 1))
    w = pp * LANES

    def rchunk(s):
        return jnp.where(s < ncx, ncx - 1 - s, nc - 1 - (s - ncx))

    fwd = pl.BlockSpec((1, c, w), lambda b, p, s: (b, s, p))
    bwd = pl.BlockSpec((1, c, w), lambda b, p, s: (b, rchunk(s), p))
    sh = jax.ShapeDtypeStruct((bsz, t, d), F32)
    return pl.pallas_call(
        _rwkv_scan_body,
        grid=(bsz, d // w, nc),
        in_specs=[fwd] * 6 + [bwd] * 6, out_specs=[fwd, bwd], out_shape=[sh, sh],
        scratch_shapes=[pltpu.VMEM((2, pp, LANES, LANES), F32)],
        compiler_params=_cparams("parallel", "parallel", "arbitrary"),
    )(r, lw[0], ks[0], v, na, bb[0], r, lw[1], ks[1], v, na, bb[1])


def _rwkv_out_body(y0_ref, y1_ref, r_ref, v_ref, g_ref, ks0_ref, ks1_ref, rk_ref, lg_ref, lb_ref, o_ref):
    e = _head_ones()
    y = y0_ref[0] + y1_ref[0]
    inv_n = 1.0 / RWKV_HEAD
    yc = y - _head_sum(y, e) * inv_n
    yn = yc * lax.rsqrt(_head_sum(yc * yc, e) * inv_n + RWKV_GN_EPS)
    yn = yn * lg_ref[...] + lb_ref[...]
    bonus = _head_sum(r_ref[0] * (ks0_ref[0] + ks1_ref[0]) * rk_ref[...], e) * v_ref[0]
    o_ref[0] = ((yn + bonus) * g_ref[0]).astype(o_ref.dtype)


def rwkv_mixer(u, mu, w_r, w_k, w_v, w_o, w0, w1, w2, a0, a1, a2, g1, g2, k_k, k_a, r_k, ln_g, ln_b):
    bsz, t, d = u.shape
    m = bsz * t
    dc = _pick(d, (256, 128))
    xs = pl.pallas_call(
        _rwkv_mix_body, grid=(bsz, d // dc),
        in_specs=[pl.BlockSpec((1, t, dc), lambda b, j: (b, 0, j)),
                  pl.BlockSpec((6, dc), lambda b, j: (0, j))],
        out_specs=[pl.BlockSpec((1, t, dc), lambda b, j: (b, 0, j))] * 6,
        out_shape=[jax.ShapeDtypeStruct((bsz, t, d), BF16)] * 6,
        compiler_params=_cparams("parallel", "parallel"),
    )(u, mu)
    xr, xw, xk, xv, xa, xg = (x.reshape(m, d) for x in xs)
    r = matmul(xr, w_r.astype(BF16)).reshape(bsz, t, d)
    k = matmul(xk, w_k.astype(BF16)).reshape(bsz, t, d)
    v = matmul(xv, w_v.astype(BF16)).reshape(bsz, t, d)
    rk = w1.shape[2]
    hw = matmul(xw, jnp.concatenate([w1[0], w1[1]], 1).astype(BF16)).reshape(bsz, t, 2 * rk)
    ha = matmul(xa, jnp.concatenate([a1[0], a1[1]], 1).astype(BF16)).reshape(bsz, t, 2 * rk)
    gr = g1.shape[1]
    grp = -(-gr // LANES) * LANES
    g1p = jnp.zeros((d, grp), F32).at[:, :gr].set(g1)
    g2p = jnp.zeros((grp, d), F32).at[:gr].set(g2)
    hg = jax.nn.sigmoid(matmul(xg, g1p.astype(BF16)))
    g = matmul(hg, g2p.astype(BF16)).reshape(bsz, t, d)

    tr = _pick(t, (64, 32, 16))
    row = pl.BlockSpec((1, tr, d), lambda b, i: (b, i, 0))
    low = pl.BlockSpec((1, tr, 2 * rk), lambda b, i: (b, i, 0))
    vec = pl.BlockSpec((1, d), lambda b, i: (0, 0))
    vec2 = pl.BlockSpec((2, d), lambda b, i: (0, 0))
    fact = pl.BlockSpec((2, rk, d), lambda b, i: (0, 0, 0))
    big = jax.ShapeDtypeStruct((bsz, t, d), F32)
    na, lw0, lw1, ks0, ks1, bb0, bb1 = pl.pallas_call(
        _rwkv_prep_body, grid=(bsz, t // tr),
        in_specs=[row, low, low, fact, fact, vec2, vec2, vec, vec],
        out_specs=[row] * 7, out_shape=[big] * 7,
        compiler_params=_cparams("parallel", "parallel"),
    )(k, hw, ha, w2.astype(BF16), a2.astype(BF16), w0, a0, k_k[None], k_a[None])

    y0, y1 = rwkv_scan(r, v, na, (lw0, lw1), (ks0, ks1), (bb0, bb1))

    out = pl.pallas_call(
        _rwkv_out_body, grid=(bsz, t // tr),
        in_specs=[row] * 7 + [vec, vec, vec], out_specs=row,
        out_shape=jax.ShapeDtypeStruct((bsz, t, d), BF16),
        compiler_params=_cparams("parallel", "parallel"),
    )(y0, y1, r, v, g, ks0, ks1, r_k.reshape(1, d), ln_g[None], ln_b[None])
    return matmul(out.reshape(m, d), w_o.astype(BF16)).reshape(bsz, t, d)


def _top_vals(s, kk):
    vals = []
    cur = s
    for _ in range(kk):
        mx = jnp.max(cur, axis=0, keepdims=True)
        vals.append(mx)
        cur = jnp.where(cur >= mx, NEG, cur)
    return jnp.concatenate(vals, axis=0)


def _peer_score_body(q_ref, keys_ref, thr_ref, s2_ref, e2_ref, cf_ref):
    nk = PEER_NKEYS
    kk = PEER_TOPK
    for h in range(PEER_HEADS):
        q1 = q_ref[h * 2 * nk:h * 2 * nk + nk, :].astype(BF16)
        q2 = q_ref[h * 2 * nk + nk:(h + 1) * 2 * nk, :].astype(BF16)
        s1 = _dot(keys_ref[0], q1)
        s2 = _dot(keys_ref[1], q2)
        a1 = _top_vals(s1, kk + 1)
        a2 = _top_vals(s2, kk + 1)
        cand = (a1[:kk, None, :] + a2[None, :kk, :]).reshape(kk * kk, -1)
        cv = _top_vals(cand, kk + 1)
        c17 = jnp.maximum(cv[kk:kk + 1], jnp.maximum(a1[kk:kk + 1] + a2[0:1], a1[0:1] + a2[kk:kk + 1]))
        theta = 0.5 * (cv[kk - 1:kk] + c17)
        zsum = jnp.sum(jnp.where(cand >= theta, jnp.exp(cand - cv[0:1]), 0.0), axis=0, keepdims=True)
        thr_ref[h] = theta - s1
        s2_ref[h] = s2
        e2_ref[h] = jnp.exp(s2 - a2[0:1])
        cf_ref[h] = jnp.exp(s1 - a1[0:1]) / zsum


def _gelu(x):
    return 0.5 * x * (1.0 + lax.erf(x * (2.0 ** -0.5)))


def _peer_dense_body(z_ref, u_ref, v_ref, thr_ref, s2_ref, e2_ref, cf_ref, o_ref, w_ref):
    nk = PEER_NKEYS
    eb = pl.program_id(1)
    te = u_ref.shape[0]
    n_i = te // nk
    strip = PEER_ROW_STRIP

    @pl.when(eb == 0)
    def _():
        o_ref[...] = jnp.zeros_like(o_ref)

    sub = min(PEER_TOK_SUB, z_ref.shape[1])
    esub = min(PEER_EXP_SUB, te)

    d = z_ref.shape[0]
    kc = d * te // (PEER_STAGES * esub)

    def first_matmul(ts):
        acts = []
        for e0 in range(0, te, esub):
            acc = None
            for k0 in range(0, d, kc):
                part = _dot(u_ref[e0:e0 + esub, k0:k0 + kc], z_ref[k0:k0 + kc, ts])
                acc = part if acc is None else acc + part
                yield
            acts.append(acc)
        return acts

    def gates(ts):
        per_stage = (nk // strip) * PEER_HEADS // PEER_STAGES
        n = 0
        for si in range(nk // strip):
            js = slice(si * strip, (si + 1) * strip)
            w = [None] * n_i
            for h in range(PEER_HEADS):
                s2s = s2_ref[h, js, ts]
                e2s = e2_ref[h, js, ts]
                for ii in range(n_i):
                    i = eb * n_i + ii
                    c = jnp.where(s2s >= thr_ref[h, pl.ds(i, 1), ts], e2s, 0.0) * cf_ref[h, pl.ds(i, 1), ts]
                    w[ii] = c if w[ii] is None else w[ii] + c
                n += 1
                if n % per_stage == 0 and h + 1 < PEER_HEADS:
                    yield
            for ii in range(n_i):
                w_ref[ii * nk + si * strip:ii * nk + (si + 1) * strip, ts] = w[ii].astype(BF16)
            yield

    def activate(ts, acts):
        rows = esub // (PEER_STAGES // len(acts))
        for a, act in enumerate(acts):
            for r0 in range(0, esub, rows):
                rs = slice(a * esub + r0, a * esub + r0 + rows)
                w_ref[rs, ts] = w_ref[rs, ts] * _gelu(act[r0:r0 + rows]).astype(BF16)
                yield

    def second_matmul(ts):
        rows = d // PEER_STAGES
        for r0 in range(0, d, rows):
            o_ref[r0:r0 + rows, ts] += _dot(v_ref[r0:r0 + rows, :], w_ref[:, ts])
            yield

    tiles = [slice(t0, t0 + sub) for t0 in range(0, z_ref.shape[1], sub)]
    acts = {}
    for ph in range(len(tiles) + 2):
        gens, tags = [], []
        if ph < len(tiles):
            gens += [first_matmul(tiles[ph]), gates(tiles[ph])]
            tags += [ph, None]
        if 0 <= ph - 1 < len(tiles):
            gens.append(activate(tiles[ph - 1], acts[ph - 1]))
            tags.append(None)
        if 0 <= ph - 2 < len(tiles):
            gens.append(second_matmul(tiles[ph - 2]))
            tags.append(None)
        for tag, res in zip(tags, _lockstep(gens)):
            if tag is not None:
                acts[tag] = res


def peer_ffn(zt, w_q, sub_keys, u_tab, v_tab):
    d, m = zt.shape
    nh, nk = PEER_HEADS, PEER_NKEYS
    qt = matmul(w_q.T.astype(BF16), zt, F32)
    tt = _pick(m, (256, 128))
    sh = jax.ShapeDtypeStruct((nh, nk, m), F32)
    blk = pl.BlockSpec((nh, nk, tt), lambda i: (0, 0, i))
    thr, s2, e2, cf = pl.pallas_call(
        _peer_score_body, grid=(m // tt,),
        in_specs=[pl.BlockSpec((nh * 2 * nk, tt), lambda i: (0, i)),
                  pl.BlockSpec((2, nk, PEER_DKEY // 2), lambda i: (0, 0, 0))],
        out_specs=[blk, blk, blk, blk], out_shape=[sh, sh, sh, sh],
        compiler_params=_cparams("parallel"),
    )(qt, sub_keys.astype(BF16))

    tm = _pick(m, (2 * PEER_TOK_SUB, PEER_TOK_SUB, LANES))
    te = 512
    ne = u_tab.shape[0]
    once = pl.Buffered(1)
    sblk = pl.BlockSpec((nh, nk, tm), lambda i, e: (0, 0, i), pipeline_mode=once)
    return pl.pallas_call(
        _peer_dense_body, grid=(m // tm, ne // te),
        in_specs=[pl.BlockSpec((d, tm), lambda i, e: (0, i), pipeline_mode=once),
                  pl.BlockSpec((te, d), lambda i, e: (e, 0)),
                  pl.BlockSpec((d, te), lambda i, e: (0, e)),
                  sblk, sblk, sblk, sblk],
        out_specs=pl.BlockSpec((d, tm), lambda i, e: (0, i)),
        out_shape=jax.ShapeDtypeStruct((d, m), F32),
        scratch_shapes=[pltpu.VMEM((te, tm), BF16)],
        compiler_params=_cparams("parallel", "arbitrary"),
    )(zt, u_tab.astype(BF16), v_tab.T.astype(BF16), thr, s2, e2, cf)


def _ada_mods(c, c_ctx, w_down, w_up, b_up):
    bsz, d = c.shape
    cond = jnp.concatenate([c, c_ctx[None]], 0)
    pad = 16 - cond.shape[0] % 16
    cond = jnp.concatenate([cond, jnp.zeros((pad, d), F32)], 0)
    hid = matmul(jax.nn.silu(cond), w_down.astype(BF16), F32)
    m = (matmul(hid, w_up.astype(BF16), F32) + b_up)[:bsz + 1].reshape(bsz + 1, N_MOD, d)
    return jnp.stack([jnp.broadcast_to(m[bsz], (bsz, N_MOD, d)), m[:bsz]], axis=1)


def kernel(x, c, ctx, c_ctx, ada_w_down, ada_w_up, ada_b,
           gla_w_in, gla_w_g1, gla_w_g2, gla_b_g, gla_norm_g, gla_w_o,
           mla_w_in, mla_q_norm, mla_kv_norm, mla_w_uq, mla_w_ukv, mla_w_o,
           rwkv_mu, rwkv_w_r, rwkv_w_k, rwkv_w_v, rwkv_w_o, rwkv_w0, rwkv_w1, rwkv_w2,
           rwkv_a0, rwkv_a1, rwkv_a2, rwkv_g1, rwkv_g2, rwkv_k_k, rwkv_k_a, rwkv_r_k, rwkv_ln_g, rwkv_ln_b,
           peer_w_q, peer_sub_keys, peer_u, peer_v):
    z = jnp.concatenate([ctx, x], axis=1)
    bsz, t, d = z.shape
    mods = [_ada_mods(c, c_ctx, ada_w_down[i], ada_w_up[i], ada_b[i]) for i in range(DEPTH)]
    _, u = ln_mod(z, mods_n=mods[0], sidx=0)
    for i in range(DEPTH):
        j = i // N_MIXERS
        if i % N_MIXERS == 0:
            y = gla_mixer(u, gla_w_in[j], gla_w_g1[j], gla_w_g2[j], gla_b_g[j], gla_norm_g[j], gla_w_o[j])
        elif i % N_MIXERS == 1:
            y = mla_mixer(u, mla_w_in[j], mla_q_norm[j], mla_kv_norm[j], mla_w_uq[j], mla_w_ukv[j], mla_w_o[j])
        else:
            y = rwkv_mixer(u, rwkv_mu[j], rwkv_w_r[j], rwkv_w_k[j], rwkv_w_v[j], rwkv_w_o[j],
                           rwkv_w0[j], rwkv_w1[j], rwkv_w2[j], rwkv_a0[j], rwkv_a1[j], rwkv_a2[j],
                           rwkv_g1[j], rwkv_g2[j], rwkv_k_k[j], rwkv_k_a[j], rwkv_r_k[j],
                           rwkv_ln_g[j], rwkv_ln_b[j])
        z, ut = ln_mod(z, y, mods[i], 2, mods[i], 3, u_t=True)
        ht = peer_ffn(ut, peer_w_q[i], peer_sub_keys[i], peer_u[i], peer_v[i])
        if i + 1 < DEPTH:
            z, u = ln_mod(z, ht, mods[i], 5, mods[i + 1], 0, y_t=True)
        else:
            z, _ = ln_mod(z, ht, mods[i], 5, y_t=True)
    return z[:, CTX_LEN:]
```

```python
import functools

import jax
import jax.numpy as jnp
from jax import lax
from jax.experimental import pallas as pl
from jax.experimental.pallas import tpu as pltpu

F32 = jnp.float32
BF16 = jnp.bfloat16

DEPTH = 4
CTX_LEN = 256
GRID_W = 64
N_MIXERS = 3
N_MOD = 6
LN_EPS = 1e-6
DEEPNORM_ALPHA = (2.0 * DEPTH) ** 0.25

GLA_HEADS = 8
GLA_GATE_RANK = 16
GLA_GATE_NORMALIZER = 16.0

MLA_Q_RANK = 1536
MLA_KV_RANK = 512
MLA_NOPE = 128
MLA_ROPE = 64
MLA_V = 128
ROPE_THETA = 10000.0

RWKV_HEAD = 64
RWKV_GN_EPS = 64e-5

PEER_HEADS = 8
PEER_NKEYS = 128
PEER_DKEY = 256
PEER_TOPK = 16

LANES = 128
SUBLANES = 8
VMEM_LIMIT = 52 * 1024 * 1024

GLA_CHUNK = 64
GLA_HEADS_PER_STEP = 2
GLA_SUB = 4
RWKV_CHUNK = 64
RWKV_SUB = 16
MLA_Q_CHAINS = 1
PEER_TOK_SUB = 256
PEER_EXP_SUB = 256
PEER_STAGES = 8
PEER_ROW_STRIP = 32
NEG = -1e30


def _pick(n, cands):
    for c in cands:
        if n % c == 0:
            return c
    return n


def _cparams(*sem):
    return pltpu.CompilerParams(dimension_semantics=sem, vmem_limit_bytes=VMEM_LIMIT)


def _dot(a, b):
    return jnp.dot(a, b, preferred_element_type=F32)


def _dot_nt(a, b):
    return lax.dot_general(a, b, (((1,), (1,)), ((), ())), preferred_element_type=F32)


def _dot_tn(a, b):
    return lax.dot_general(a, b, (((0,), (0,)), ((), ())), preferred_element_type=F32)


def _split3(x):
    hi = x.astype(BF16)
    r1 = x - hi.astype(F32)
    mid = r1.astype(BF16)
    lo = (r1 - mid.astype(F32)).astype(BF16)
    return hi, mid, lo


def _dot_sel(sel, x):
    s = sel.astype(BF16)
    hi, mid, lo = _split3(x)
    return _dot(s, hi) + _dot(s, mid) + _dot(s, lo)


def _dot_x3(a, b, dims=None):
    ah = a.astype(BF16)
    al = (a - ah.astype(F32)).astype(BF16)
    bh = b.astype(BF16)
    bl = (b - bh.astype(F32)).astype(BF16)
    f = _dot if dims is None else dims
    return f(ah, bh) + f(ah, bl) + f(al, bh)


def _dot_b(a, b):
    return _dot(a.astype(BF16), b.astype(BF16))


def _log_sigmoid(x):
    return jnp.minimum(x, 0.0) - jnp.log1p(jnp.exp(-jnp.abs(x)))


def _sigmoid(x):
    return 1.0 / (1.0 + jnp.exp(-x))


def _mm_body(a_ref, b_ref, o_ref):
    o_ref[...] = _dot(a_ref[...].astype(BF16), b_ref[...].astype(BF16)).astype(o_ref.dtype)


def matmul(a, b, out_dtype=F32):
    m, k = a.shape
    k2, n = b.shape
    assert k == k2
    a_bytes = jnp.dtype(a.dtype).itemsize
    tm_cands = (1024, 512, 256, 128, 64, 32, 16) if a_bytes * k <= 8192 else (512, 256, 128, 64, 32, 16)
    tm = _pick(m, tm_cands)
    tn = _pick(n, (512, 256, 128))
    return pl.pallas_call(
        _mm_body,
        grid=(m // tm, n // tn),
        in_specs=[pl.BlockSpec((tm, k), lambda i, j: (i, 0)),
                  pl.BlockSpec((k, tn), lambda i, j: (0, j))],
        out_specs=pl.BlockSpec((tm, tn), lambda i, j: (i, j)),
        out_shape=jax.ShapeDtypeStruct((m, n), out_dtype),
        compiler_params=_cparams("parallel", "parallel"),
    )(a, b)


def _layer_norm(z):
    zc = z - jnp.mean(z, -1, keepdims=True)
    return zc * lax.rsqrt(jnp.mean(zc * zc, -1, keepdims=True) + LN_EPS)


def _ln_mod_body(*refs, gidx, sidx, has_y, y_t, u_t):
    it = iter(refs)
    z_ref = next(it)
    y_ref = next(it) if has_y else None
    mg_ref = next(it) if has_y else None
    mn_ref = next(it) if sidx is not None else None
    zo_ref = next(it) if has_y else None
    u_ref = next(it) if sidx is not None else None
    z = z_ref[0]
    if has_y:
        y = y_ref[...].T if y_t else y_ref[0]
        z = _layer_norm(DEEPNORM_ALPHA * z + mg_ref[0, 0, gidx:gidx + 1, :] * y)
        zo_ref[0] = z
    if sidx is not None:
        u = z * (1.0 + mn_ref[0, 0, sidx + 1:sidx + 2, :]) + mn_ref[0, 0, sidx:sidx + 1, :]
        if u_t:
            u_ref[...] = u.T.astype(u_ref.dtype)
        else:
            u_ref[0] = u.astype(u_ref.dtype)


def ln_mod(z, y=None, mods_g=None, gidx=None, mods_n=None, sidx=None, y_t=False, u_t=False):
    bsz, t, d = z.shape
    tr = _pick(CTX_LEN, (256, 128, 64, 32, 16))
    nt = t // tr
    nctx = CTX_LEN // tr
    has_y = y is not None
    row = pl.BlockSpec((1, tr, d), lambda b, i: (b, i, 0))
    col = pl.BlockSpec((d, tr), lambda b, i: (0, b * nt + i))
    mod = pl.BlockSpec((1, 1, N_MOD, d), lambda b, i: (b, jnp.where(i < nctx, 0, 1), 0, 0))
    ins, in_specs, outs, out_specs = [z], [row], [], []
    if has_y:
        ins += [y, mods_g]
        in_specs += [col if y_t else row, mod]
        outs.append(jax.ShapeDtypeStruct((bsz, t, d), F32))
        out_specs.append(row)
    if sidx is not None:
        ins.append(mods_n)
        in_specs.append(mod)
        outs.append(jax.ShapeDtypeStruct((d, bsz * t) if u_t else (bsz, t, d), BF16))
        out_specs.append(col if u_t else row)
    res = pl.pallas_call(
        functools.partial(_ln_mod_body, gidx=gidx, sidx=sidx, has_y=has_y, y_t=y_t, u_t=u_t),
        grid=(bsz, nt), in_specs=in_specs, out_specs=out_specs, out_shape=outs,
        compiler_params=_cparams("parallel", "parallel"),
    )(*ins)
    res = list(res)
    zo = res.pop(0) if has_y else None
    u = res.pop(0) if sidx is not None else None
    return zo, u


def _lockstep(gens):
    out = [None] * len(gens)
    live = list(range(len(gens)))
    while live:
        for i in list(live):
            try:
                next(gens[i])
            except StopIteration as done:
                out[i] = done.value
                live.remove(i)
    return out


def _gla_chunk(q, k, v, glow, wg2, bg, st, rev):
    c, dk = q.shape
    sub = GLA_SUB
    q = q.astype(F32) * (dk ** -0.5)
    k = k.astype(F32)
    gl = _dot_x3(glow, wg2) + bg
    yield
    g = _log_sigmoid(gl) * (1.0 / GLA_GATE_NORMALIZER)
    row = lax.broadcasted_iota(jnp.int32, (c, c), 0)
    col = lax.broadcasted_iota(jnp.int32, (c, c), 1)
    tri = (row <= col) if rev else (row >= col)
    b = _dot_sel(tri.astype(F32), g)
    yield
    btot = jnp.sum(g, axis=0, keepdims=True)

    o = _dot_nt((q * jnp.exp(b)).astype(BF16), st.astype(BF16))

    rowk = lax.broadcasted_iota(jnp.int32, (c, 1), 0)
    halves = []
    hsz = c // 2
    while hsz >= sub:
        halves.append(hsz)
        hsz //= 2
    pick = jnp.concatenate(
        [(col == (row // (2 * hf)) * (2 * hf) + (hf if rev else hf - 1)).astype(F32) for hf in halves], axis=0)
    refs = _dot_sel(pick, b)
    yield
    att = jnp.zeros((c, c), F32)
    for lv, hf in enumerate(halves):
        refb = refs[lv * c:(lv + 1) * c]
        late = (rowk % (2 * hf)) >= hf
        q_side, k_side = (~late, late) if rev else (late, ~late)
        qf = jnp.where(q_side, q * jnp.exp(jnp.minimum(b - refb, 0.0)), 0.0)
        kf = jnp.where(k_side, k * jnp.exp(jnp.minimum(refb - b, 0.0)), 0.0)
        same = (row // (2 * hf)) == (col // (2 * hf))
        att = att + jnp.where(same, _dot_nt(qf.astype(BF16), kf.astype(BF16)), 0.0)
    yield

    rmod = rowk % sub
    for lag in range(sub):
        sh = (c - lag) % c if rev else lag
        ks = pltpu.roll(k, sh, 0) if sh else k
        bs = pltpu.roll(b, sh, 0) if sh else b
        term = jnp.sum(q * ks * jnp.exp(jnp.minimum(b - bs, 0.0)), axis=1, keepdims=True)
        valid = (rmod + lag < sub) if rev else (rmod >= lag)
        hit = (col == row + lag) if rev else (col == row - lag)
        att = att + jnp.where(hit & valid, term, 0.0)

    o = o + _dot(att.astype(BF16), v)
    yield
    kd = (k * jnp.exp(btot - b)).astype(BF16)
    return o, st * jnp.exp(btot) + _dot_tn(v, kd)


def _gla_body(*refs):
    ins, (of_ref, ob_ref, st_ref) = refs[:12], refs[12:]

    @pl.when(pl.program_id(2) == 0)
    def _():
        st_ref[...] = jnp.zeros_like(st_ref)

    dv = st_ref.shape[2]
    dk = st_ref.shape[3]
    gens, dest = [], []
    for dr in range(2):
        q_ref, k_ref, v_ref, gl_ref, wg2_ref, bg_ref = ins[6 * dr:6 * dr + 6]
        for hh in range(st_ref.shape[1]):
            ks, vs = slice(hh * dk, (hh + 1) * dk), slice(hh * dv, (hh + 1) * dv)
            gens.append(_gla_chunk(q_ref[0, :, ks], k_ref[0, :, ks], v_ref[0, :, vs], gl_ref[0],
                                   wg2_ref[:, ks], bg_ref[:, ks], st_ref[dr, hh], rev=(dr == 1)))
            dest.append((dr, hh, vs))
    for (dr, hh, vs), (o, st) in zip(dest, _lockstep(gens)):
        (of_ref, ob_ref)[dr][0, :, vs] = o
        st_ref[dr, hh] = st


def gla_scan(p, glow, wg2, bg):
    bsz, t, d3 = p.shape
    d = d3 // 3
    h = GLA_HEADS
    dk, dv = (d // 2) // h, d // h
    c = GLA_CHUNK
    nc, ncx = t // c, CTX_LEN // c

    def rchunk(s):
        return jnp.where(s < ncx, ncx - 1 - s, nc - 1 - (s - ncx))

    hp = _pick(h, (GLA_HEADS_PER_STEP, 1))
    hg = h // hp

    def specs(chunk):
        return [pl.BlockSpec((1, c, hp * dk), lambda b, hh, s: (b, chunk(s), hh)),
                pl.BlockSpec((1, c, hp * dk), lambda b, hh, s: (b, chunk(s), hg + hh)),
                pl.BlockSpec((1, c, hp * dv), lambda b, hh, s: (b, chunk(s), hg + hh)),
                pl.BlockSpec((1, c, LANES), lambda b, hh, s: (b, chunk(s), 0)),
                pl.BlockSpec((LANES, hp * dk), lambda b, hh, s: (0, hh)),
                pl.BlockSpec((1, hp * dk), lambda b, hh, s: (0, hh))]

    sh = jax.ShapeDtypeStruct((bsz, t, d), F32)
    return pl.pallas_call(
        _gla_body,
        grid=(bsz, hg, nc),
        in_specs=specs(lambda s: s) + specs(rchunk),
        out_specs=[pl.BlockSpec((1, c, hp * dv), lambda b, hh, s: (b, s, hh)),
                   pl.BlockSpec((1, c, hp * dv), lambda b, hh, s: (b, rchunk(s), hh))],
        out_shape=[sh, sh],
        scratch_shapes=[pltpu.VMEM((2, hp, dv, dk), F32)],
        compiler_params=_cparams("parallel", "parallel", "arbitrary"),
    )(p, p, p, glow, wg2[0], bg[0], p, p, p, glow, wg2[1], bg[1])


def _gla_gate_body(of_ref, ob_ref, r_ref, g_ref, o_ref):
    dv = g_ref.shape[1]
    o = of_ref[0] + ob_ref[0]
    r = r_ref[0].astype(F32)
    outs = []
    for hh in range(o.shape[1] // dv):
        seg = o[:, hh * dv:(hh + 1) * dv]
        outs.append(seg * lax.rsqrt(jnp.mean(seg * seg, -1, keepdims=True) + 1e-6) * g_ref[...])
    o_ref[0] = (jnp.concatenate(outs, axis=1) * (r * _sigmoid(r))).astype(o_ref.dtype)


def gla_gate(o_f, o_b, p, norm_g):
    bsz, t, d = o_f.shape
    tr = _pick(t, (256, 128, 64, 32, 16))
    row = pl.BlockSpec((1, tr, d), lambda b, i: (b, i, 0))
    return pl.pallas_call(
        _gla_gate_body, grid=(bsz, t // tr),
        in_specs=[row, row, pl.BlockSpec((1, tr, d), lambda b, i: (b, i, 2)),
                  pl.BlockSpec((1, norm_g.shape[1]), lambda b, i: (0, 0))],
        out_specs=row, out_shape=jax.ShapeDtypeStruct((bsz, t, d), BF16),
        compiler_params=_cparams("parallel", "parallel"),
    )(o_f, o_b, p, norm_g)


def gla_mixer(u, w_in, w_g1, w_g2, b_g, norm_g, w_o):
    bsz, t, d = u.shape
    qk = d // 2
    u2 = u.reshape(bsz * t, d)
    p = matmul(u2, w_in.astype(BF16), BF16).reshape(bsz, t, 3 * d)
    r = GLA_GATE_RANK
    wg1 = jnp.zeros((d, LANES), F32).at[:, :r].set(w_g1[0]).at[:, r:2 * r].set(w_g1[1])
    glow = matmul(u2, wg1.astype(BF16), F32).reshape(bsz, t, LANES)
    wg2 = [jnp.zeros((LANES, qk), F32).at[s * r:(s + 1) * r].set(w_g2[s]) for s in range(2)]
    outs = gla_scan(p, glow, wg2, [b_g[0][None], b_g[1][None]])
    gated = gla_gate(outs[0], outs[1], p, norm_g[None])
    return matmul(gated.reshape(bsz * t, d), w_o.astype(BF16), F32).reshape(bsz, t, d)


def _rms(x, gain):
    return x * lax.rsqrt(jnp.mean(x * x, -1, keepdims=True) + 1e-6) * gain


def _rope128(a, c, s):
    return a * c + pltpu.roll(a, LANES // 2, 1) * s


def _mla_norm_body(h_ref, qg_ref, kg_ref, c_ref, s_ref, cq_ref, ckv_ref, kr_ref):
    qr, kvr = cq_ref.shape[2], ckv_ref.shape[2]
    h = h_ref[0]
    cq_ref[0] = _rms(h[:, :qr], qg_ref[...]).astype(BF16)
    ckv_ref[0] = _rms(h[:, qr:qr + kvr], kg_ref[...]).astype(BF16)
    kr_ref[0] = _rope128(h[:, qr + kvr:], c_ref[...], s_ref[...]).astype(BF16)


def _mla_attn_body(q_ref, kn_ref, v_ref, kr_ref, c_ref, s_ref, o_ref, kf_ref, *, nctx_tiles, scale):
    qt = pl.program_id(2)
    ctx = nctx_tiles * q_ref.shape[1]

    @pl.when(qt == 0)
    def _():
        kf_ref[:, :LANES] = kn_ref[0]
        kf_ref[:, LANES:] = kr_ref[0]

    q = q_ref[0].astype(F32)
    qr = _rope128(q[:, LANES:], c_ref[...], s_ref[...])
    qf = (jnp.concatenate([q[:, :LANES], qr], axis=1) * scale).astype(BF16)

    def attend(kf, v, rows):
        s = _dot_nt(qf[rows], kf)
        yield
        p = jnp.exp(s - jnp.max(s, -1, keepdims=True))
        l = jnp.sum(p, -1, keepdims=True)
        o_ref[0, rows] = (_dot(p.astype(BF16), v) / l).astype(o_ref.dtype)

    nq = qf.shape[0]
    halves = [slice(r0, r0 + nq // MLA_Q_CHAINS) for r0 in range(0, nq, nq // MLA_Q_CHAINS)]

    @pl.when(qt < nctx_tiles)
    def _():
        _lockstep([attend(kf_ref[:ctx], v_ref[0, :ctx], rows) for rows in halves])

    @pl.when(qt >= nctx_tiles)
    def _():
        _lockstep([attend(kf_ref[...], v_ref[0], rows) for rows in halves])


def _rope_tables(t):
    rows = (t - CTX_LEN) // GRID_W
    row = jnp.repeat(jnp.arange(rows, dtype=F32), GRID_W)
    colp = jnp.tile(jnp.arange(GRID_W, dtype=F32), rows)
    n_freq = MLA_ROPE // 4
    inv = ROPE_THETA ** (-jnp.arange(n_freq, dtype=F32) / n_freq)
    ang = jnp.concatenate([row[:, None] * inv, colp[:, None] * inv], -1)
    cos = jnp.concatenate([jnp.ones((CTX_LEN, MLA_ROPE // 2), F32), jnp.cos(ang)], 0)
    sin = jnp.concatenate([jnp.zeros((CTX_LEN, MLA_ROPE // 2), F32), jnp.sin(ang)], 0)
    z = jnp.zeros_like(cos)
    return jnp.concatenate([cos, cos, z, z], 1), jnp.concatenate([-sin, sin, z, z], 1)


def mla_mixer(u, w_in, q_norm, kv_norm, w_uq, w_ukv, w_o):
    bsz, t, d = u.shape
    nh = d // 128
    qr, kvr, rp = MLA_Q_RANK, MLA_KV_RANK, MLA_ROPE
    ev, od = jnp.arange(0, rp, 2), jnp.arange(1, rp, 2)
    perm = jnp.concatenate([ev, od, od, ev])
    w_in_p = jnp.concatenate([w_in[:, :qr + kvr], w_in[:, qr + kvr + perm]], axis=1)
    hw = qr + kvr + LANES
    h = matmul(u.reshape(bsz * t, d), w_in_p.astype(BF16), F32).reshape(bsz, t, hw)
    ctab, stab = _rope_tables(t)

    tr = _pick(t, (256, 128, 64, 32, 16))
    cq, ckv, kr = pl.pallas_call(
        _mla_norm_body, grid=(bsz, t // tr),
        in_specs=[pl.BlockSpec((1, tr, hw), lambda b, i: (b, i, 0)),
                  pl.BlockSpec((1, qr), lambda b, i: (0, 0)),
                  pl.BlockSpec((1, kvr), lambda b, i: (0, 0)),
                  pl.BlockSpec((tr, LANES), lambda b, i: (i, 0)),
                  pl.BlockSpec((tr, LANES), lambda b, i: (i, 0))],
        out_specs=[pl.BlockSpec((1, tr, qr), lambda b, i: (b, i, 0)),
                   pl.BlockSpec((1, tr, kvr), lambda b, i: (b, i, 0)),
                   pl.BlockSpec((1, tr, LANES), lambda b, i: (b, i, 0))],
        out_shape=[jax.ShapeDtypeStruct((bsz, t, qr), BF16),
                   jax.ShapeDtypeStruct((bsz, t, kvr), BF16),
                   jax.ShapeDtypeStruct((bsz, t, LANES), BF16)],
        compiler_params=_cparams("parallel", "parallel"),
    )(h, q_norm[None], kv_norm[None], ctab, stab)

    hq = MLA_NOPE + rp
    qcols = (jnp.arange(nh)[:, None] * hq
             + jnp.concatenate([jnp.arange(MLA_NOPE), MLA_NOPE + perm])[None, :]).reshape(-1)
    q = matmul(cq.reshape(bsz * t, qr), w_uq[:, qcols].astype(BF16), BF16).reshape(bsz, t, nh * 2 * LANES)
    kv = matmul(ckv.reshape(bsz * t, kvr), w_ukv.astype(BF16), BF16).reshape(bsz, t, nh * 2 * LANES)

    tq = _pick(CTX_LEN, (256, 128, 64, 32, 16))
    scale = (MLA_NOPE + rp) ** -0.5
    o = pl.pallas_call(
        functools.partial(_mla_attn_body, nctx_tiles=CTX_LEN // tq, scale=scale),
        grid=(bsz, nh, t // tq),
        in_specs=[pl.BlockSpec((1, tq, 2 * LANES), lambda b, hh, i: (b, i, hh)),
                  pl.BlockSpec((1, t, LANES), lambda b, hh, i: (b, 0, 2 * hh)),
                  pl.BlockSpec((1, t, LANES), lambda b, hh, i: (b, 0, 2 * hh + 1)),
                  pl.BlockSpec((1, t, LANES), lambda b, hh, i: (b, 0, 0)),
                  pl.BlockSpec((tq, LANES), lambda b, hh, i: (i, 0)),
                  pl.BlockSpec((tq, LANES), lambda b, hh, i: (i, 0))],
        out_specs=pl.BlockSpec((1, tq, LANES), lambda b, hh, i: (b, i, hh)),
        out_shape=jax.ShapeDtypeStruct((bsz, t, nh * MLA_V), BF16),
        scratch_shapes=[pltpu.VMEM((t, 2 * LANES), BF16)],
        compiler_params=_cparams("parallel", "parallel", "arbitrary"),
    )(q, kv, kv, kr, ctab, stab)
    return matmul(o.reshape(bsz * t, nh * MLA_V), w_o.astype(BF16), F32).reshape(bsz, t, d)


def _rwkv_mix_body(u_ref, mu_ref, *o_refs):
    t = u_ref.shape[1]
    u = u_ref[0].astype(F32)
    row = lax.broadcasted_iota(jnp.int32, (t, 1), 0)
    prev = jnp.where((row == 0) | (row == CTX_LEN), 0.0, pltpu.roll(u, 1, 0))
    nxt = jnp.where((row == CTX_LEN - 1) | (row == t - 1), 0.0, pltpu.roll(u, t - 1, 0))
    xx = 0.5 * (prev + nxt) - u
    for n, o_ref in enumerate(o_refs):
        o_ref[0] = (u + xx * mu_ref[n:n + 1, :]).astype(o_ref.dtype)


def _head_sum(x, e):
    eb = e.astype(BF16)
    outs = []
    for s in range(x.shape[1] // LANES):
        hi, mid, lo = _split3(x[:, s * LANES:(s + 1) * LANES])
        outs.append(_dot(hi, eb) + _dot(mid, eb) + _dot(lo, eb))
    return jnp.concatenate(outs, axis=1)


def _head_ones():
    r = lax.broadcasted_iota(jnp.int32, (LANES, LANES), 0) // RWKV_HEAD
    c = lax.broadcasted_iota(jnp.int32, (LANES, LANES), 1) // RWKV_HEAD
    return (r == c).astype(F32)


def _rwkv_prep_body(k_ref, hw_ref, ha_ref, w2_ref, a2_ref, w0_ref, a0_ref, kk_ref, ka_ref,
                    na_ref, lw0_ref, lw1_ref, ks0_ref, ks1_ref, bb0_ref, bb1_ref):
    rk = w2_ref.shape[1]
    k = k_ref[0]
    kk = k * kk_ref[...]
    kk = kk * lax.rsqrt(_head_sum(kk * kk, _head_ones()) + 1e-12)
    na_ref[0] = -kk
    hw = jnp.tanh(hw_ref[0])
    ha = ha_ref[0]
    for s, (lw_ref, ks_ref, bb_ref) in enumerate(((lw0_ref, ks0_ref, bb0_ref), (lw1_ref, ks1_ref, bb1_ref))):
        wl = w0_ref[s:s + 1, :] + _dot(hw[:, s * rk:(s + 1) * rk].astype(BF16), w2_ref[s])
        lw_ref[0] = -jnp.exp(_log_sigmoid(wl) - 0.5)
        a = _sigmoid(a0_ref[s:s + 1, :] + _dot(ha[:, s * rk:(s + 1) * rk].astype(BF16), a2_ref[s]))
        ks_ref[0] = k * (1.0 + (a - 1.0) * ka_ref[...])
        bb_ref[0] = kk * a


def _rwkv_chunk(r, lw, ks, v, na, bb, ht, rev):
    c = r.shape[0]
    n = RWKV_HEAD
    c2 = 2 * c
    ri = lax.broadcasted_iota(jnp.int32, (c, c), 0)
    ci = lax.broadcasted_iota(jnp.int32, (c, c), 1)
    cs = _dot_sel(((ri <= ci) if rev else (ri >= ci)).astype(F32), lw)
    yield
    cm = cs - lw
    ctot = jnp.sum(lw, axis=0, keepdims=True)

    lane_a = lax.broadcasted_iota(jnp.int32, (1, LANES), 1) < n

    def stack(x):
        return jnp.concatenate([jnp.where(lane_a, x, 0.0), jnp.where(lane_a, 0.0, x)], axis=0)

    at = stack(na * jnp.exp(cm))
    rt = stack(r * jnp.exp(cs))
    ecs = jnp.exp(-cs)
    bh = stack(bb * ecs)
    kh = stack(ks * ecs)
    ece = jnp.exp(ctot - cs)
    be = stack(bb * ece)
    ke = stack(ks * ece)
    vs = stack(v)

    rr = lax.broadcasted_iota(jnp.int32, (c2, c2), 0)
    cc = lax.broadcasted_iota(jnp.int32, (c2, c2), 1)
    same = (rr // c) == (cc // c)
    tr_, tc_ = rr % c, cc % c
    strict = same & ((tc_ > tr_) if rev else (tc_ < tr_))
    incl = same & ((tc_ >= tr_) if rev else (tc_ <= tr_))
    diag_blk = same & ((tr_ // RWKV_SUB) == (tc_ // RWKV_SUB))
    eye = (rr == cc).astype(F32)

    lhs = jnp.concatenate([at, rt], axis=0)
    rhs = jnp.concatenate([bh, kh], axis=0)
    a4 = _dot_nt(lhs.astype(BF16), rhs.astype(BF16))
    nmat = jnp.where(strict, _dot_x3(at, bh, _dot_nt), 0.0)
    yield
    aak = jnp.where(strict, a4[:c2, c2:], 0.0)
    arb = jnp.where(incl, a4[c2:, :c2], 0.0)
    ark = jnp.where(incl, a4[c2:, c2:], 0.0)

    htb = ht.astype(BF16)
    x0 = _dot_nt(at.astype(BF16), htb) + _dot(aak.astype(BF16), vs.astype(BF16))
    yield

    nd = jnp.where(diag_blk, nmat, 0.0)
    no = nmat - nd
    tm = nd
    pw = nd
    for _ in range(3):
        pw = _dot_b(pw, pw)
        yield
        tm = tm + pw + _dot_b(pw, tm)
        yield
    z = no + _dot_b(tm, no)
    u = x0 + _dot_b(tm, x0)
    yield
    z2 = _dot_b(z, z)
    u = u + _dot_b(z, u)
    yield
    u = u + _dot_b(z2, u)
    yield

    ub = u.astype(BF16)
    vb = vs.astype(BF16)
    ys = _dot_nt(rt.astype(BF16), htb) + _dot(arb.astype(BF16), ub) + _dot(ark.astype(BF16), vb)
    yield
    hn = ht * jnp.exp(ctot) + _dot_tn(ub, be.astype(BF16)) + _dot_tn(vb, ke.astype(BF16))
    hr = lax.broadcasted_iota(jnp.int32, (LANES, LANES), 0) // n
    hc = lax.broadcasted_iota(jnp.int32, (LANES, LANES), 1) // n
    return ys[:c] + ys[c:], jnp.where(hr == hc, hn, 0.0)


def _rwkv_scan_body(*refs):
    ins, (y0_ref, y1_ref, st_ref) = refs[:12], refs[12:]

    @pl.when(pl.program_id(2) == 0)
    def _():
        st_ref[...] = jnp.zeros_like(st_ref)

    gens, dest = [], []
    for dr in range(2):
        r_ref, lw_ref, ks_ref, v_ref, na_ref, bb_ref = ins[6 * dr:6 * dr + 6]
        for p in range(r_ref.shape[2] // LANES):
            sl = slice(p * LANES, (p + 1) * LANES)
            gens.append(_rwkv_chunk(r_ref[0, :, sl], lw_ref[0, :, sl], ks_ref[0, :, sl], v_ref[0, :, sl],
                                    na_ref[0, :, sl], bb_ref[0, :, sl], st_ref[dr, p], rev=(dr == 1)))
            dest.append(((y0_ref, y1_ref)[dr], sl, dr, p))
    for (y_ref, sl, dr, p), (y, hn) in zip(dest, _lockstep(gens)):
        y_ref[0, :, sl] = y
        st_ref[dr, p] = hn


def rwkv_scan(r, v, na, lw, ks, bb):
    bsz, t, d = r.shape
    c = RWKV_CHUNK
    nc, ncx = t // c, CTX_LEN // c
    pp = _pick(d // LANES, (4, 2, 1))
    w = pp * LANES

    def rchunk(s):
        return jnp.where(s < ncx, ncx - 1 - s, nc - 1 - (s - ncx))

    fwd = pl.BlockSpec((1, c, w), lambda b, p, s: (b, s, p))
    bwd = pl.BlockSpec((1, c, w), lambda b, p, s: (b, rchunk(s), p))
    sh = jax.ShapeDtypeStruct((bsz, t, d), F32)
    return pl.pallas_call(
        _rwkv_scan_body,
        grid=(bsz, d // w, nc),
        in_specs=[fwd] * 6 + [bwd] * 6, out_specs=[fwd, bwd], out_shape=[sh, sh],
        scratch_shapes=[pltpu.VMEM((2, pp, LANES, LANES), F32)],
        compiler_params=_cparams("parallel", "parallel", "arbitrary"),
    )(r, lw[0], ks[0], v, na, bb[0], r, lw[1], ks[1], v, na, bb[1])


def _rwkv_out_body(y0_ref, y1_ref, r_ref, v_ref, g_ref, ks0_ref, ks1_ref, rk_ref, lg_ref, lb_ref, o_ref):
    e = _head_ones()
    y = y0_ref[0] + y1_ref[0]
    inv_n = 1.0 / RWKV_HEAD
    yc = y - _head_sum(y, e) * inv_n
    yn = yc * lax.rsqrt(_head_sum(yc * yc, e) * inv_n + RWKV_GN_EPS)
    yn = yn * lg_ref[...] + lb_ref[...]
    bonus = _head_sum(r_ref[0] * (ks0_ref[0] + ks1_ref[0]) * rk_ref[...], e) * v_ref[0]
    o_ref[0] = ((yn + bonus) * g_ref[0]).astype(o_ref.dtype)


def rwkv_mixer(u, mu, w_r, w_k, w_v, w_o, w0, w1, w2, a0, a1, a2, g1, g2, k_k, k_a, r_k, ln_g, ln_b):
    bsz, t, d = u.shape
    m = bsz * t
    dc = _pick(d, (256, 128))
    xs = pl.pallas_call(
        _rwkv_mix_body, grid=(bsz, d // dc),
        in_specs=[pl.BlockSpec((1, t, dc), lambda b, j: (b, 0, j)),
                  pl.BlockSpec((6, dc), lambda b, j: (0, j))],
        out_specs=[pl.BlockSpec((1, t, dc), lambda b, j: (b, 0, j))] * 6,
        out_shape=[jax.ShapeDtypeStruct((bsz, t, d), BF16)] * 6,
        compiler_params=_cparams("parallel", "parallel"),
    )(u, mu)
    xr, xw, xk, xv, xa, xg = (x.reshape(m, d) for x in xs)
    r = matmul(xr, w_r.astype(BF16)).reshape(bsz, t, d)
    k = matmul(xk, w_k.astype(BF16)).reshape(bsz, t, d)
    v = matmul(xv, w_v.astype(BF16)).reshape(bsz, t, d)
    rk = w1.shape[2]
    hw = matmul(xw, jnp.concatenate([w1[0], w1[1]], 1).astype(BF16)).reshape(bsz, t, 2 * rk)
    ha = matmul(xa, jnp.concatenate([a1[0], a1[1]], 1).astype(BF16)).reshape(bsz, t, 2 * rk)
    gr = g1.shape[1]
    grp = -(-gr // LANES) * LANES
    g1p = jnp.zeros((d, grp), F32).at[:, :gr].set(g1)
    g2p = jnp.zeros((grp, d), F32).at[:gr].set(g2)
    hg = jax.nn.sigmoid(matmul(xg, g1p.astype(BF16)))
    g = matmul(hg, g2p.astype(BF16)).reshape(bsz, t, d)

    tr = _pick(t, (64, 32, 16))
    row = pl.BlockSpec((1, tr, d), lambda b, i: (b, i, 0))
    low = pl.BlockSpec((1, tr, 2 * rk), lambda b, i: (b, i, 0))
    vec = pl.BlockSpec((1, d), lambda b, i: (0, 0))
    vec2 = pl.BlockSpec((2, d), lambda b, i: (0, 0))
    fact = pl.BlockSpec((2, rk, d), lambda b, i: (0, 0, 0))
    big = jax.ShapeDtypeStruct((bsz, t, d), F32)
    na, lw0, lw1, ks0, ks1, bb0, bb1 = pl.pallas_call(
        _rwkv_prep_body, grid=(bsz, t // tr),
        in_specs=[row, low, low, fact, fact, vec2, vec2, vec, vec],
        out_specs=[row] * 7, out_shape=[big] * 7,
        compiler_params=_cparams("parallel", "parallel"),
    )(k, hw, ha, w2.astype(BF16), a2.astype(BF16), w0, a0, k_k[None], k_a[None])

    y0, y1 = rwkv_scan(r, v, na, (lw0, lw1), (ks0, ks1), (bb0, bb1))

    out = pl.pallas_call(
        _rwkv_out_body, grid=(bsz, t // tr),
        in_specs=[row] * 7 + [vec, vec, vec], out_specs=row,
        out_shape=jax.ShapeDtypeStruct((bsz, t, d), BF16),
        compiler_params=_cparams("parallel", "parallel"),
    )(y0, y1, r, v, g, ks0, ks1, r_k.reshape(1, d), ln_g[None], ln_b[None])
    return matmul(out.reshape(m, d), w_o.astype(BF16)).reshape(bsz, t, d)


def _top_vals(s, kk):
    vals = []
    cur = s
    for _ in range(kk):
        mx = jnp.max(cur, axis=0, keepdims=True)
        vals.append(mx)
        cur = jnp.where(cur >= mx, NEG, cur)
    return jnp.concatenate(vals, axis=0)


def _peer_score_body(q_ref, keys_ref, thr_ref, s2_ref, e2_ref, cf_ref):
    nk = PEER_NKEYS
    kk = PEER_TOPK
    for h in range(PEER_HEADS):
        q1 = q_ref[h * 2 * nk:h * 2 * nk + nk, :].astype(BF16)
        q2 = q_ref[h * 2 * nk + nk:(h + 1) * 2 * nk, :].astype(BF16)
        s1 = _dot(keys_ref[0], q1)
        s2 = _dot(keys_ref[1], q2)
        a1 = _top_vals(s1, kk + 1)
        a2 = _top_vals(s2, kk + 1)
        cand = (a1[:kk, None, :] + a2[None, :kk, :]).reshape(kk * kk, -1)
        cv = _top_vals(cand, kk + 1)
        c17 = jnp.maximum(cv[kk:kk + 1], jnp.maximum(a1[kk:kk + 1] + a2[0:1], a1[0:1] + a2[kk:kk + 1]))
        theta = 0.5 * (cv[kk - 1:kk] + c17)
        zsum = jnp.sum(jnp.where(cand >= theta, jnp.exp(cand - cv[0:1]), 0.0), axis=0, keepdims=True)
        thr_ref[h] = theta - s1
        s2_ref[h] = s2
        e2_ref[h] = jnp.exp(s2 - a2[0:1])
        cf_ref[h] = jnp.exp(s1 - a1[0:1]) / zsum


def _gelu(x):
    return 0.5 * x * (1.0 + lax.erf(x * (2.0 ** -0.5)))


def _peer_dense_body(z_ref, u_ref, v_ref, thr_ref, s2_ref, e2_ref, cf_ref, o_ref, w_ref, *, stages, v_tiled):
    nk = PEER_NKEYS
    eb = pl.program_id(1)
    te = u_ref.shape[0]
    n_i = te // nk
    strip = PEER_ROW_STRIP

    @pl.when(eb == 0)
    def _():
        o_ref[...] = jnp.zeros_like(o_ref)

    sub = min(PEER_TOK_SUB, z_ref.shape[1])
    esub = min(PEER_EXP_SUB, te)

    d = z_ref.shape[0]
    kc = d * te // (stages * esub)

    def first_matmul(ts):
        acts = []
        for e0 in range(0, te, esub):
            acc = None
            for k0 in range(0, d, kc):
                part = _dot(u_ref[e0:e0 + esub, k0:k0 + kc], z_ref[k0:k0 + kc, ts])
                acc = part if acc is None else acc + part
                yield
            acts.append(acc)
        return acts

    def gates(ts):
        per_stage = (nk // strip) * PEER_HEADS // stages
        n = 0
        for si in range(nk // strip):
            js = slice(si * strip, (si + 1) * strip)
            w = [None] * n_i
            for h in range(PEER_HEADS):
                s2s = s2_ref[h, js, ts]
                e2s = e2_ref[h, js, ts]
                for ii in range(n_i):
                    i = eb * n_i + ii
                    c = jnp.where(s2s >= thr_ref[h, pl.ds(i, 1), ts], e2s, 0.0) * cf_ref[h, pl.ds(i, 1), ts]
                    w[ii] = c if w[ii] is None else w[ii] + c
                n += 1
                if n % per_stage == 0 and h + 1 < PEER_HEADS:
                    yield
            for ii in range(n_i):
                w_ref[ii * nk + si * strip:ii * nk + (si + 1) * strip, ts] = w[ii].astype(BF16)
            yield

    def activate(ts, acts):
        rows = esub // (stages // len(acts))
        for a, act in enumerate(acts):
            for r0 in range(0, esub, rows):
                rs = slice(a * esub + r0, a * esub + r0 + rows)
                w_ref[rs, ts] = w_ref[rs, ts] * _gelu(act[r0:r0 + rows]).astype(BF16)
                yield

    def second_matmul(ts):
        rows = d // stages
        for r0 in range(0, d, rows):
            vb = v_ref[0, r0:r0 + rows, :] if v_tiled else v_ref[r0:r0 + rows, :]
            o_ref[r0:r0 + rows, ts] += _dot(vb, w_ref[:, ts])
            yield

    tiles = [slice(t0, t0 + sub) for t0 in range(0, z_ref.shape[1], sub)]
    acts = {}
    for ph in range(len(tiles) + 2):
        gens, tags = [], []
        if ph < len(tiles):
            gens += [first_matmul(tiles[ph]), gates(tiles[ph])]
            tags += [ph, None]
        if 0 <= ph - 1 < len(tiles):
            gens.append(activate(tiles[ph - 1], acts[ph - 1]))
            tags.append(None)
        if 0 <= ph - 2 < len(tiles):
            gens.append(second_matmul(tiles[ph - 2]))
            tags.append(None)
        for tag, res in zip(tags, _lockstep(gens)):
            if tag is not None:
                acts[tag] = res


def peer_ffn(zt, w_q, sub_keys, u_tab, v_tab, stages=PEER_STAGES, v_tiled=False):
    d, m = zt.shape
    nh, nk = PEER_HEADS, PEER_NKEYS
    qt = matmul(w_q.T.astype(BF16), zt, F32)
    tt = _pick(m, (256, 128))
    sh = jax.ShapeDtypeStruct((nh, nk, m), F32)
    blk = pl.BlockSpec((nh, nk, tt), lambda i: (0, 0, i))
    thr, s2, e2, cf = pl.pallas_call(
        _peer_score_body, grid=(m // tt,),
        in_specs=[pl.BlockSpec((nh * 2 * nk, tt), lambda i: (0, i)),
                  pl.BlockSpec((2, nk, PEER_DKEY // 2), lambda i: (0, 0, 0))],
        out_specs=[blk, blk, blk, blk], out_shape=[sh, sh, sh, sh],
        compiler_params=_cparams("parallel"),
    )(qt, sub_keys.astype(BF16))

    tm = _pick(m, (2 * PEER_TOK_SUB, PEER_TOK_SUB, LANES))
    te = 512
    ne = u_tab.shape[0]
    once = pl.Buffered(1)
    sblk = pl.BlockSpec((nh, nk, tm), lambda i, e: (0, 0, i), pipeline_mode=once)
    return pl.pallas_call(
        functools.partial(_peer_dense_body, stages=stages, v_tiled=v_tiled), grid=(m // tm, ne // te),
        in_specs=[pl.BlockSpec((d, tm), lambda i, e: (0, i), pipeline_mode=once),
                  pl.BlockSpec((te, d), lambda i, e: (e, 0)),
                  (pl.BlockSpec((1, d, te), lambda i, e: (e, 0, 0)) if v_tiled
                   else pl.BlockSpec((d, te), lambda i, e: (0, e))),
                  sblk, sblk, sblk, sblk],
        out_specs=pl.BlockSpec((d, tm), lambda i, e: (0, i)),
        out_shape=jax.ShapeDtypeStruct((d, m), F32),
        scratch_shapes=[pltpu.VMEM((te, tm), BF16)],
        compiler_params=_cparams("parallel", "arbitrary"),
    )(zt, u_tab.astype(BF16),
      (v_tab.reshape(ne // te, te, d).transpose(0, 2, 1) if v_tiled else v_tab.T).astype(BF16), thr, s2, e2, cf)


def _ada_mods(c, c_ctx, w_down, w_up, b_up):
    bsz, d = c.shape
    cond = jnp.concatenate([c, c_ctx[None]], 0)
    pad = 16 - cond.shape[0] % 16
    cond = jnp.concatenate([cond, jnp.zeros((pad, d), F32)], 0)
    hid = matmul(jax.nn.silu(cond), w_down.astype(BF16), F32)
    m = (matmul(hid, w_up.astype(BF16), F32) + b_up)[:bsz + 1].reshape(bsz + 1, N_MOD, d)
    return jnp.stack([jnp.broadcast_to(m[bsz], (bsz, N_MOD, d)), m[:bsz]], axis=1)


def kernel(x, c, ctx, c_ctx, ada_w_down, ada_w_up, ada_b,
           gla_w_in, gla_w_g1, gla_w_g2, gla_b_g, gla_norm_g, gla_w_o,
           mla_w_in, mla_q_norm, mla_kv_norm, mla_w_uq, mla_w_ukv, mla_w_o,
           rwkv_mu, rwkv_w_r, rwkv_w_k, rwkv_w_v, rwkv_w_o, rwkv_w0, rwkv_w1, rwkv_w2,
           rwkv_a0, rwkv_a1, rwkv_a2, rwkv_g1, rwkv_g2, rwkv_k_k, rwkv_k_a, rwkv_r_k, rwkv_ln_g, rwkv_ln_b,
           peer_w_q, peer_sub_keys, peer_u, peer_v):
    z = jnp.concatenate([ctx, x], axis=1)
    bsz, t, d = z.shape
    mods = [_ada_mods(c, c_ctx, ada_w_down[i], ada_w_up[i], ada_b[i]) for i in range(DEPTH)]
    _, u = ln_mod(z, mods_n=mods[0], sidx=0)
    for i in range(DEPTH):
        j = i // N_MIXERS
        if i % N_MIXERS == 0:
            y = gla_mixer(u, gla_w_in[j], gla_w_g1[j], gla_w_g2[j], gla_b_g[j], gla_norm_g[j], gla_w_o[j])
        elif i % N_MIXERS == 1:
            y = mla_mixer(u, mla_w_in[j], mla_q_norm[j], mla_kv_norm[j], mla_w_uq[j], mla_w_ukv[j], mla_w_o[j])
        else:
            y = rwkv_mixer(u, rwkv_mu[j], rwkv_w_r[j], rwkv_w_k[j], rwkv_w_v[j], rwkv_w_o[j],
                           rwkv_w0[j], rwkv_w1[j], rwkv_w2[j], rwkv_a0[j], rwkv_a1[j], rwkv_a2[j],
                           rwkv_g1[j], rwkv_g2[j], rwkv_k_k[j], rwkv_k_a[j], rwkv_r_k[j],
                           rwkv_ln_g[j], rwkv_ln_b[j])
        z, ut = ln_mod(z, y, mods[i], 2, mods[i], 3, u_t=True)
        ht = peer_ffn(ut, peer_w_q[i], peer_sub_keys[i], peer_u[i], peer_v[i],
                      stages=(8, 8, 4, 4)[i], v_tiled=(False, True, True, False)[i])
        if i + 1 < DEPTH:
            z, u = ln_mod(z, ht, mods[i], 5, mods[i + 1], 0, y_t=True)
        else:
            z, _ = ln_mod(z, ht, mods[i], 5, y_t=True)
    return z[:, CTX_LEN:]
```

```python
import functools

import jax
import jax.numpy as jnp
from jax import lax
from jax.experimental import pallas as pl
from jax.experimental.pallas import tpu as pltpu

F32 = jnp.float32
BF16 = jnp.bfloat16

DEPTH = 4
CTX_LEN = 256
GRID_W = 64
N_MIXERS = 3
N_MOD = 6
LN_EPS = 1e-6
DEEPNORM_ALPHA = (2.0 * DEPTH) ** 0.25

GLA_HEADS = 8
GLA_GATE_RANK = 16
GLA_GATE_NORMALIZER = 16.0

MLA_Q_RANK = 1536
MLA_KV_RANK = 512
MLA_NOPE = 128
MLA_ROPE = 64
MLA_V = 128
ROPE_THETA = 10000.0

RWKV_HEAD = 64
RWKV_GN_EPS = 64e-5

PEER_HEADS = 8
PEER_NKEYS = 128
PEER_DKEY = 256
PEER_TOPK = 16

LANES = 128
SUBLANES = 8
VMEM_LIMIT = 52 * 1024 * 1024

GLA_CHUNK = 64
GLA_HEADS_PER_STEP = 2
GLA_SUB = 4
RWKV_CHUNK = 64
RWKV_SUB = 16
MLA_Q_CHAINS = 1
PEER_TOK_SUB = 256
PEER_EXP_SUB = 256
PEER_STAGES = 8
PEER_ROW_STRIP = 32
NEG = -1e30


def _pick(n, cands):
    for c in cands:
        if n % c == 0:
            return c
    return n


def _cparams(*sem):
    return pltpu.CompilerParams(dimension_semantics=sem, vmem_limit_bytes=VMEM_LIMIT)


def _dot(a, b):
    return jnp.dot(a, b, preferred_element_type=F32)


def _dot_nt(a, b):
    return lax.dot_general(a, b, (((1,), (1,)), ((), ())), preferred_element_type=F32)


def _dot_tn(a, b):
    return lax.dot_general(a, b, (((0,), (0,)), ((), ())), preferred_element_type=F32)


def _split3(x):
    hi = x.astype(BF16)
    r1 = x - hi.astype(F32)
    mid = r1.astype(BF16)
    lo = (r1 - mid.astype(F32)).astype(BF16)
    return hi, mid, lo


def _dot_sel(sel, x):
    s = sel.astype(BF16)
    hi, mid, lo = _split3(x)
    return _dot(s, hi) + _dot(s, mid) + _dot(s, lo)


def _dot_x3(a, b, dims=None):
    ah = a.astype(BF16)
    al = (a - ah.astype(F32)).astype(BF16)
    bh = b.astype(BF16)
    bl = (b - bh.astype(F32)).astype(BF16)
    f = _dot if dims is None else dims
    return f(ah, bh) + f(ah, bl) + f(al, bh)


def _dot_b(a, b):
    return _dot(a.astype(BF16), b.astype(BF16))


def _log_sigmoid(x):
    return jnp.minimum(x, 0.0) - jnp.log1p(jnp.exp(-jnp.abs(x)))


def _sigmoid(x):
    return 1.0 / (1.0 + jnp.exp(-x))


def _mm_body(a_ref, b_ref, o_ref):
    o_ref[...] = _dot(a_ref[...].astype(BF16), b_ref[...].astype(BF16)).astype(o_ref.dtype)


def matmul(a, b, out_dtype=F32):
    m, k = a.shape
    k2, n = b.shape
    assert k == k2
    a_bytes = jnp.dtype(a.dtype).itemsize
    tm_cands = (1024, 512, 256, 128, 64, 32, 16) if a_bytes * k <= 8192 else (512, 256, 128, 64, 32, 16)
    tm = _pick(m, tm_cands)
    tn = _pick(n, (512, 256, 128))
    return pl.pallas_call(
        _mm_body,
        grid=(m // tm, n // tn),
        in_specs=[pl.BlockSpec((tm, k), lambda i, j: (i, 0)),
                  pl.BlockSpec((k, tn), lambda i, j: (0, j))],
        out_specs=pl.BlockSpec((tm, tn), lambda i, j: (i, j)),
        out_shape=jax.ShapeDtypeStruct((m, n), out_dtype),
        compiler_params=_cparams("parallel", "parallel"),
    )(a, b)


def _layer_norm(z):
    zc = z - jnp.mean(z, -1, keepdims=True)
    return zc * lax.rsqrt(jnp.mean(zc * zc, -1, keepdims=True) + LN_EPS)


def _ln_mod_body(*refs, gidx, sidx, has_y, y_t, u_t):
    it = iter(refs)
    z_ref = next(it)
    y_ref = next(it) if has_y else None
    mg_ref = next(it) if has_y else None
    mn_ref = next(it) if sidx is not None else None
    zo_ref = next(it) if has_y else None
    u_ref = next(it) if sidx is not None else None
    z = z_ref[0]
    if has_y:
        y = y_ref[...].T if y_t else y_ref[0]
        z = _layer_norm(DEEPNORM_ALPHA * z + mg_ref[0, 0, gidx:gidx + 1, :] * y)
        zo_ref[0] = z
    if sidx is not None:
        u = z * (1.0 + mn_ref[0, 0, sidx + 1:sidx + 2, :]) + mn_ref[0, 0, sidx:sidx + 1, :]
        if u_t:
            u_ref[...] = u.T.astype(u_ref.dtype)
        else:
            u_ref[0] = u.astype(u_ref.dtype)


def ln_mod(z, y=None, mods_g=None, gidx=None, mods_n=None, sidx=None, y_t=False, u_t=False):
    bsz, t, d = z.shape
    tr = _pick(CTX_LEN, (256, 128, 64, 32, 16))
    nt = t // tr
    nctx = CTX_LEN // tr
    has_y = y is not None
    row = pl.BlockSpec((1, tr, d), lambda b, i: (b, i, 0))
    col = pl.BlockSpec((d, tr), lambda b, i: (0, b * nt + i))
    mod = pl.BlockSpec((1, 1, N_MOD, d), lambda b, i: (b, jnp.where(i < nctx, 0, 1), 0, 0))
    ins, in_specs, outs, out_specs = [z], [row], [], []
    if has_y:
        ins += [y, mods_g]
        in_specs += [col if y_t else row, mod]
        outs.append(jax.ShapeDtypeStruct((bsz, t, d), F32))
        out_specs.append(row)
    if sidx is not None:
        ins.append(mods_n)
        in_specs.append(mod)
        outs.append(jax.ShapeDtypeStruct((d, bsz * t) if u_t else (bsz, t, d), BF16))
        out_specs.append(col if u_t else row)
    res = pl.pallas_call(
        functools.partial(_ln_mod_body, gidx=gidx, sidx=sidx, has_y=has_y, y_t=y_t, u_t=u_t),
        grid=(bsz, nt), in_specs=in_specs, out_specs=out_specs, out_shape=outs,
        compiler_params=_cparams("parallel", "parallel"),
    )(*ins)
    res = list(res)
    zo = res.pop(0) if has_y else None
    u = res.pop(0) if sidx is not None else None
    return zo, u


def _lockstep(gens):
    out = [None] * len(gens)
    live = list(range(len(gens)))
    while live:
        for i in list(live):
            try:
                next(gens[i])
            except StopIteration as done:
                out[i] = done.value
                live.remove(i)
    return out


def _gla_chunk(q, k, v, glow, wg2, bg, st, rev):
    c, dk = q.shape
    sub = GLA_SUB
    q = q.astype(F32) * (dk ** -0.5)
    k = k.astype(F32)
    gl = _dot_x3(glow, wg2) + bg
    yield
    g = _log_sigmoid(gl) * (1.0 / GLA_GATE_NORMALIZER)
    row = lax.broadcasted_iota(jnp.int32, (c, c), 0)
    col = lax.broadcasted_iota(jnp.int32, (c, c), 1)
    tri = (row <= col) if rev else (row >= col)
    b = _dot_sel(tri.astype(F32), g)
    yield
    btot = jnp.sum(g, axis=0, keepdims=True)

    o = _dot_nt((q * jnp.exp(b)).astype(BF16), st.astype(BF16))

    rowk = lax.broadcasted_iota(jnp.int32, (c, 1), 0)
    halves = []
    hsz = c // 2
    while hsz >= sub:
        halves.append(hsz)
        hsz //= 2
    pick = jnp.concatenate(
        [(col == (row // (2 * hf)) * (2 * hf) + (hf if rev else hf - 1)).astype(F32) for hf in halves], axis=0)
    refs = _dot_sel(pick, b)
    yield
    att = jnp.zeros((c, c), F32)
    for lv, hf in enumerate(halves):
        refb = refs[lv * c:(lv + 1) * c]
        late = (rowk % (2 * hf)) >= hf
        q_side, k_side = (~late, late) if rev else (late, ~late)
        qf = jnp.where(q_side, q * jnp.exp(jnp.minimum(b - refb, 0.0)), 0.0)
        kf = jnp.where(k_side, k * jnp.exp(jnp.minimum(refb - b, 0.0)), 0.0)
        same = (row // (2 * hf)) == (col // (2 * hf))
        att = att + jnp.where(same, _dot_nt(qf.astype(BF16), kf.astype(BF16)), 0.0)
    yield

    rmod = rowk % sub
    for lag in range(sub):
        sh = (c - lag) % c if rev else lag
        ks = pltpu.roll(k, sh, 0) if sh else k
        bs = pltpu.roll(b, sh, 0) if sh else b
        term = jnp.sum(q * ks * jnp.exp(jnp.minimum(b - bs, 0.0)), axis=1, keepdims=True)
        valid = (rmod + lag < sub) if rev else (rmod >= lag)
        hit = (col == row + lag) if rev else (col == row - lag)
        att = att + jnp.where(hit & valid, term, 0.0)

    o = o + _dot(att.astype(BF16), v)
    yield
    kd = (k * jnp.exp(btot - b)).astype(BF16)
    return o, st * jnp.exp(btot) + _dot_tn(v, kd)


def _gla_body(*refs):
    ins, (of_ref, ob_ref, st_ref) = refs[:12], refs[12:]

    @pl.when(pl.program_id(2) == 0)
    def _():
        st_ref[...] = jnp.zeros_like(st_ref)

    dv = st_ref.shape[2]
    dk = st_ref.shape[3]
    gens, dest = [], []
    for dr in range(2):
        q_ref, k_ref, v_ref, gl_ref, wg2_ref, bg_ref = ins[6 * dr:6 * dr + 6]
        for hh in range(st_ref.shape[1]):
            ks, vs = slice(hh * dk, (hh + 1) * dk), slice(hh * dv, (hh + 1) * dv)
            gens.append(_gla_chunk(q_ref[0, :, ks], k_ref[0, :, ks], v_ref[0, :, vs], gl_ref[0],
                                   wg2_ref[:, ks], bg_ref[:, ks], st_ref[dr, hh], rev=(dr == 1)))
            dest.append((dr, hh, vs))
    for (dr, hh, vs), (o, st) in zip(dest, _lockstep(gens)):
        (of_ref, ob_ref)[dr][0, :, vs] = o
        st_ref[dr, hh] = st


def gla_scan(p, glow, wg2, bg):
    bsz, t, d3 = p.shape
    d = d3 // 3
    h = GLA_HEADS
    dk, dv = (d // 2) // h, d // h
    c = GLA_CHUNK
    nc, ncx = t // c, CTX_LEN // c

    def rchunk(s):
        return jnp.where(s < ncx, ncx - 1 - s, nc - 1 - (s - ncx))

    hp = _pick(h, (GLA_HEADS_PER_STEP, 1))
    hg = h // hp

    def specs(chunk):
        return [pl.BlockSpec((1, c, hp * dk), lambda b, hh, s: (b, chunk(s), hh)),
                pl.BlockSpec((1, c, hp * dk), lambda b, hh, s: (b, chunk(s), hg + hh)),
                pl.BlockSpec((1, c, hp * dv), lambda b, hh, s: (b, chunk(s), hg + hh)),
                pl.BlockSpec((1, c, LANES), lambda b, hh, s: (b, chunk(s), 0)),
                pl.BlockSpec((LANES, hp * dk), lambda b, hh, s: (0, hh)),
                pl.BlockSpec((1, hp * dk), lambda b, hh, s: (0, hh))]

    sh = jax.ShapeDtypeStruct((bsz, t, d), F32)
    return pl.pallas_call(
        _gla_body,
        grid=(bsz, hg, nc),
        in_specs=specs(lambda s: s) + specs(rchunk),
        out_specs=[pl.BlockSpec((1, c, hp * dv), lambda b, hh, s: (b, s, hh)),
                   pl.BlockSpec((1, c, hp * dv), lambda b, hh, s: (b, rchunk(s), hh))],
        out_shape=[sh, sh],
        scratch_shapes=[pltpu.VMEM((2, hp, dv, dk), F32)],
        compiler_params=_cparams("parallel", "parallel", "arbitrary"),
    )(p, p, p, glow, wg2[0], bg[0], p, p, p, glow, wg2[1], bg[1])


def _gla_gate_body(of_ref, ob_ref, r_ref, g_ref, o_ref):
    dv = g_ref.shape[1]
    o = of_ref[0] + ob_ref[0]
    r = r_ref[0].astype(F32)
    outs = []
    for hh in range(o.shape[1] // dv):
        seg = o[:, hh * dv:(hh + 1) * dv]
        outs.append(seg * lax.rsqrt(jnp.mean(seg * seg, -1, keepdims=True) + 1e-6) * g_ref[...])
    o_ref[0] = (jnp.concatenate(outs, axis=1) * (r * _sigmoid(r))).astype(o_ref.dtype)


def gla_gate(o_f, o_b, p, norm_g):
    bsz, t, d = o_f.shape
    tr = _pick(t, (256, 128, 64, 32, 16))
    row = pl.BlockSpec((1, tr, d), lambda b, i: (b, i, 0))
    return pl.pallas_call(
        _gla_gate_body, grid=(bsz, t // tr),
        in_specs=[row, row, pl.BlockSpec((1, tr, d), lambda b, i: (b, i, 2)),
                  pl.BlockSpec((1, norm_g.shape[1]), lambda b, i: (0, 0))],
        out_specs=row, out_shape=jax.ShapeDtypeStruct((bsz, t, d), BF16),
        compiler_params=_cparams("parallel", "parallel"),
    )(o_f, o_b, p, norm_g)


def gla_mixer(u, w_in, w_g1, w_g2, b_g, norm_g, w_o):
    bsz, t, d = u.shape
    qk = d // 2
    u2 = u.reshape(bsz * t, d)
    p = matmul(u2, w_in.astype(BF16), BF16).reshape(bsz, t, 3 * d)
    r = GLA_GATE_RANK
    wg1 = jnp.zeros((d, LANES), F32).at[:, :r].set(w_g1[0]).at[:, r:2 * r].set(w_g1[1])
    glow = matmul(u2, wg1.astype(BF16), F32).reshape(bsz, t, LANES)
    wg2 = [jnp.zeros((LANES, qk), F32).at[s * r:(s + 1) * r].set(w_g2[s]) for s in range(2)]
    outs = gla_scan(p, glow, wg2, [b_g[0][None], b_g[1][None]])
    gated = gla_gate(outs[0], outs[1], p, norm_g[None])
    return matmul(gated.reshape(bsz * t, d), w_o.astype(BF16), F32).reshape(bsz, t, d)


def _rms(x, gain):
    return x * lax.rsqrt(jnp.mean(x * x, -1, keepdims=True) + 1e-6) * gain


def _rope128(a, c, s):
    return a * c + pltpu.roll(a, LANES // 2, 1) * s


def _mla_norm_body(h_ref, qg_ref, kg_ref, c_ref, s_ref, cq_ref, ckv_ref, kr_ref):
    qr, kvr = cq_ref.shape[2], ckv_ref.shape[2]
    h = h_ref[0]
    cq_ref[0] = _rms(h[:, :qr], qg_ref[...]).astype(BF16)
    ckv_ref[0] = _rms(h[:, qr:qr + kvr], kg_ref[...]).astype(BF16)
    kr_ref[0] = _rope128(h[:, qr + kvr:], c_ref[...], s_ref[...]).astype(BF16)


def _mla_attn_body(q_ref, kn_ref, v_ref, kr_ref, c_ref, s_ref, o_ref, kf_ref, *, nctx_tiles, scale):
    qt = pl.program_id(2)
    ctx = nctx_tiles * q_ref.shape[1]

    @pl.when(qt == 0)
    def _():
        kf_ref[:, :LANES] = kn_ref[0]
        kf_ref[:, LANES:] = kr_ref[0]

    q = q_ref[0].astype(F32)
    qr = _rope128(q[:, LANES:], c_ref[...], s_ref[...])
    qf = (jnp.concatenate([q[:, :LANES], qr], axis=1) * scale).astype(BF16)

    def attend(kf, v, rows):
        s = _dot_nt(qf[rows], kf)
        yield
        p = jnp.exp(s - jnp.max(s, -1, keepdims=True))
        l = jnp.sum(p, -1, keepdims=True)
        o_ref[0, rows] = (_dot(p.astype(BF16), v) / l).astype(o_ref.dtype)

    nq = qf.shape[0]
    halves = [slice(r0, r0 + nq // MLA_Q_CHAINS) for r0 in range(0, nq, nq // MLA_Q_CHAINS)]

    @pl.when(qt < nctx_tiles)
    def _():
        _lockstep([attend(kf_ref[:ctx], v_ref[0, :ctx], rows) for rows in halves])

    @pl.when(qt >= nctx_tiles)
    def _():
        _lockstep([attend(kf_ref[...], v_ref[0], rows) for rows in halves])


def _rope_tables(t):
    rows = (t - CTX_LEN) // GRID_W
    row = jnp.repeat(jnp.arange(rows, dtype=F32), GRID_W)
    colp = jnp.tile(jnp.arange(GRID_W, dtype=F32), rows)
    n_freq = MLA_ROPE // 4
    inv = ROPE_THETA ** (-jnp.arange(n_freq, dtype=F32) / n_freq)
    ang = jnp.concatenate([row[:, None] * inv, colp[:, None] * inv], -1)
    cos = jnp.concatenate([jnp.ones((CTX_LEN, MLA_ROPE // 2), F32), jnp.cos(ang)], 0)
    sin = jnp.concatenate([jnp.zeros((CTX_LEN, MLA_ROPE // 2), F32), jnp.sin(ang)], 0)
    z = jnp.zeros_like(cos)
    return jnp.concatenate([cos, cos, z, z], 1), jnp.concatenate([-sin, sin, z, z], 1)


def mla_mixer(u, w_in, q_norm, kv_norm, w_uq, w_ukv, w_o):
    bsz, t, d = u.shape
    nh = d // 128
    qr, kvr, rp = MLA_Q_RANK, MLA_KV_RANK, MLA_ROPE
    ev, od = jnp.arange(0, rp, 2), jnp.arange(1, rp, 2)
    perm = jnp.concatenate([ev, od, od, ev])
    w_in_p = jnp.concatenate([w_in[:, :qr + kvr], w_in[:, qr + kvr + perm]], axis=1)
    hw = qr + kvr + LANES
    h = matmul(u.reshape(bsz * t, d), w_in_p.astype(BF16), F32).reshape(bsz, t, hw)
    ctab, stab = _rope_tables(t)

    tr = _pick(t, (256, 128, 64, 32, 16))
    cq, ckv, kr = pl.pallas_call(
        _mla_norm_body, grid=(bsz, t // tr),
        in_specs=[pl.BlockSpec((1, tr, hw), lambda b, i: (b, i, 0)),
                  pl.BlockSpec((1, qr), lambda b, i: (0, 0)),
                  pl.BlockSpec((1, kvr), lambda b, i: (0, 0)),
                  pl.BlockSpec((tr, LANES), lambda b, i: (i, 0)),
                  pl.BlockSpec((tr, LANES), lambda b, i: (i, 0))],
        out_specs=[pl.BlockSpec((1, tr, qr), lambda b, i: (b, i, 0)),
                   pl.BlockSpec((1, tr, kvr), lambda b, i: (b, i, 0)),
                   pl.BlockSpec((1, tr, LANES), lambda b, i: (b, i, 0))],
        out_shape=[jax.ShapeDtypeStruct((bsz, t, qr), BF16),
                   jax.ShapeDtypeStruct((bsz, t, kvr), BF16),
                   jax.ShapeDtypeStruct((bsz, t, LANES), BF16)],
        compiler_params=_cparams("parallel", "parallel"),
    )(h, q_norm[None], kv_norm[None], ctab, stab)

    hq = MLA_NOPE + rp
    qcols = (jnp.arange(nh)[:, None] * hq
             + jnp.concatenate([jnp.arange(MLA_NOPE), MLA_NOPE + perm])[None, :]).reshape(-1)
    q = matmul(cq.reshape(bsz * t, qr), w_uq[:, qcols].astype(BF16), BF16).reshape(bsz, t, nh * 2 * LANES)
    kv = matmul(ckv.reshape(bsz * t, kvr), w_ukv.astype(BF16), BF16).reshape(bsz, t, nh * 2 * LANES)

    tq = _pick(CTX_LEN, (256, 128, 64, 32, 16))
    scale = (MLA_NOPE + rp) ** -0.5
    o = pl.pallas_call(
        functools.partial(_mla_attn_body, nctx_tiles=CTX_LEN // tq, scale=scale),
        grid=(bsz, nh, t // tq),
        in_specs=[pl.BlockSpec((1, tq, 2 * LANES), lambda b, hh, i: (b, i, hh)),
                  pl.BlockSpec((1, t, LANES), lambda b, hh, i: (b, 0, 2 * hh)),
                  pl.BlockSpec((1, t, LANES), lambda b, hh, i: (b, 0, 2 * hh + 1)),
                  pl.BlockSpec((1, t, LANES), lambda b, hh, i: (b, 0, 0)),
                  pl.BlockSpec((tq, LANES), lambda b, hh, i: (i, 0)),
                  pl.BlockSpec((tq, LANES), lambda b, hh, i: (i, 0))],
        out_specs=pl.BlockSpec((1, tq, LANES), lambda b, hh, i: (b, i, hh)),
        out_shape=jax.ShapeDtypeStruct((bsz, t, nh * MLA_V), BF16),
        scratch_shapes=[pltpu.VMEM((t, 2 * LANES), BF16)],
        compiler_params=_cparams("parallel", "parallel", "arbitrary"),
    )(q, kv, kv, kr, ctab, stab)
    return matmul(o.reshape(bsz * t, nh * MLA_V), w_o.astype(BF16), F32).reshape(bsz, t, d)


def _rwkv_mix_body(u_ref, mu_ref, *o_refs):
    t = u_ref.shape[1]
    u = u_ref[0].astype(F32)
    row = lax.broadcasted_iota(jnp.int32, (t, 1), 0)
    prev = jnp.where((row == 0) | (row == CTX_LEN), 0.0, pltpu.roll(u, 1, 0))
    nxt = jnp.where((row == CTX_LEN - 1) | (row == t - 1), 0.0, pltpu.roll(u, t - 1, 0))
    xx = 0.5 * (prev + nxt) - u
    for n, o_ref in enumerate(o_refs):
        o_ref[0] = (u + xx * mu_ref[n:n + 1, :]).astype(o_ref.dtype)


def _head_sum(x, e):
    eb = e.astype(BF16)
    outs = []
    for s in range(x.shape[1] // LANES):
        hi, mid, lo = _split3(x[:, s * LANES:(s + 1) * LANES])
        outs.append(_dot(hi, eb) + _dot(mid, eb) + _dot(lo, eb))
    return jnp.concatenate(outs, axis=1)


def _head_ones():
    r = lax.broadcasted_iota(jnp.int32, (LANES, LANES), 0) // RWKV_HEAD
    c = lax.broadcasted_iota(jnp.int32, (LANES, LANES), 1) // RWKV_HEAD
    return (r == c).astype(F32)


def _rwkv_prep_body(k_ref, hw_ref, ha_ref, w2_ref, a2_ref, w0_ref, a0_ref, kk_ref, ka_ref,
                    na_ref, lw0_ref, lw1_ref, ks0_ref, ks1_ref, bb0_ref, bb1_ref):
    rk = w2_ref.shape[1]
    k = k_ref[0]
    kk = k * kk_ref[...]
    kk = kk * lax.rsqrt(_head_sum(kk * kk, _head_ones()) + 1e-12)
    na_ref[0] = -kk
    hw = jnp.tanh(hw_ref[0])
    ha = ha_ref[0]
    for s, (lw_ref, ks_ref, bb_ref) in enumerate(((lw0_ref, ks0_ref, bb0_ref), (lw1_ref, ks1_ref, bb1_ref))):
        wl = w0_ref[s:s + 1, :] + _dot(hw[:, s * rk:(s + 1) * rk].astype(BF16), w2_ref[s])
        lw_ref[0] = -jnp.exp(_log_sigmoid(wl) - 0.5)
        a = _sigmoid(a0_ref[s:s + 1, :] + _dot(ha[:, s * rk:(s + 1) * rk].astype(BF16), a2_ref[s]))
        ks_ref[0] = k * (1.0 + (a - 1.0) * ka_ref[...])
        bb_ref[0] = kk * a


def _rwkv_chunk(r, lw, ks, v, na, bb, ht, rev):
    c = r.shape[0]
    n = RWKV_HEAD
    c2 = 2 * c
    ri = lax.broadcasted_iota(jnp.int32, (c, c), 0)
    ci = lax.broadcasted_iota(jnp.int32, (c, c), 1)
    cs = _dot_sel(((ri <= ci) if rev else (ri >= ci)).astype(F32), lw)
    yield
    cm = cs - lw
    ctot = jnp.sum(lw, axis=0, keepdims=True)

    lane_a = lax.broadcasted_iota(jnp.int32, (1, LANES), 1) < n

    def stack(x):
        return jnp.concatenate([jnp.where(lane_a, x, 0.0), jnp.where(lane_a, 0.0, x)], axis=0)

    at = stack(na * jnp.exp(cm))
    rt = stack(r * jnp.exp(cs))
    ecs = jnp.exp(-cs)
    bh = stack(bb * ecs)
    kh = stack(ks * ecs)
    ece = jnp.exp(ctot - cs)
    be = stack(bb * ece)
    ke = stack(ks * ece)
    vs = stack(v)

    rr = lax.broadcasted_iota(jnp.int32, (c2, c2), 0)
    cc = lax.broadcasted_iota(jnp.int32, (c2, c2), 1)
    same = (rr // c) == (cc // c)
    tr_, tc_ = rr % c, cc % c
    strict = same & ((tc_ > tr_) if rev else (tc_ < tr_))
    incl = same & ((tc_ >= tr_) if rev else (tc_ <= tr_))
    diag_blk = same & ((tr_ // RWKV_SUB) == (tc_ // RWKV_SUB))
    eye = (rr == cc).astype(F32)

    lhs = jnp.concatenate([at, rt], axis=0)
    rhs = jnp.concatenate([bh, kh], axis=0)
    a4 = _dot_nt(lhs.astype(BF16), rhs.astype(BF16))
    nmat = jnp.where(strict, _dot_x3(at, bh, _dot_nt), 0.0)
    yield
    aak = jnp.where(strict, a4[:c2, c2:], 0.0)
    arb = jnp.where(incl, a4[c2:, :c2], 0.0)
    ark = jnp.where(incl, a4[c2:, c2:], 0.0)

    htb = ht.astype(BF16)
    x0 = _dot_nt(at.astype(BF16), htb) + _dot(aak.astype(BF16), vs.astype(BF16))
    yield

    nd = jnp.where(diag_blk, nmat, 0.0)
    no = nmat - nd
    tm = nd
    pw = nd
    for _ in range(3):
        pw = _dot_b(pw, pw)
        yield
        tm = tm + pw + _dot_b(pw, tm)
        yield
    z = no + _dot_b(tm, no)
    u = x0 + _dot_b(tm, x0)
    yield
    z2 = _dot_b(z, z)
    u = u + _dot_b(z, u)
    yield
    u = u + _dot_b(z2, u)
    yield

    ub = u.astype(BF16)
    vb = vs.astype(BF16)
    ys = _dot_nt(rt.astype(BF16), htb) + _dot(arb.astype(BF16), ub) + _dot(ark.astype(BF16), vb)
    yield
    hn = ht * jnp.exp(ctot) + _dot_tn(ub, be.astype(BF16)) + _dot_tn(vb, ke.astype(BF16))
    hr = lax.broadcasted_iota(jnp.int32, (LANES, LANES), 0) // n
    hc = lax.broadcasted_iota(jnp.int32, (LANES, LANES), 1) // n
    return ys[:c] + ys[c:], jnp.where(hr == hc, hn, 0.0)


def _rwkv_scan_body(*refs):
    ins, (y0_ref, y1_ref, st_ref) = refs[:12], refs[12:]

    @pl.when(pl.program_id(2) == 0)
    def _():
        st_ref[...] = jnp.zeros_like(st_ref)

    gens, dest = [], []
    for dr in range(2):
        r_ref, lw_ref, ks_ref, v_ref, na_ref, bb_ref = ins[6 * dr:6 * dr + 6]
        for p in range(r_ref.shape[2] // LANES):
            sl = slice(p * LANES, (p + 1) * LANES)
            gens.append(_rwkv_chunk(r_ref[0, :, sl], lw_ref[0, :, sl], ks_ref[0, :, sl], v_ref[0, :, sl],
                                    na_ref[0, :, sl], bb_ref[0, :, sl], st_ref[dr, p], rev=(dr == 1)))
            dest.append(((y0_ref, y1_ref)[dr], sl, dr, p))
    for (y_ref, sl, dr, p), (y, hn) in zip(dest, _lockstep(gens)):
        y_ref[0, :, sl] = y
        st_ref[dr, p] = hn


def rwkv_scan(r, v, na, lw, ks, bb):
    bsz, t, d = r.shape
    c = RWKV_CHUNK
    nc, ncx = t // c, CTX_LEN // c
    pp = _pick(d // LANES, (4, 2, 1))
    w = pp * LANES

    def rchunk(s):
        return jnp.where(s < ncx, ncx - 1 - s, nc - 1 - (s - ncx))

    fwd = pl.BlockSpec((1, c, w), lambda b, p, s: (b, s, p))
    bwd = pl.BlockSpec((1, c, w), lambda b, p, s: (b, rchunk(s), p))
    sh = jax.ShapeDtypeStruct((bsz, t, d), F32)
    return pl.pallas_call(
        _rwkv_scan_body,
        grid=(bsz, d // w, nc),
        in_specs=[fwd] * 6 + [bwd] * 6, out_specs=[fwd, bwd], out_shape=[sh, sh],
        scratch_shapes=[pltpu.VMEM((2, pp, LANES, LANES), F32)],
        compiler_params=_cparams("parallel", "parallel", "arbitrary"),
    )(r, lw[0], ks[0], v, na, bb[0], r, lw[1], ks[1], v, na, bb[1])


def _rwkv_out_body(y0_ref, y1_ref, r_ref, v_ref, g_ref, ks0_ref, ks1_ref, rk_ref, lg_ref, lb_ref, o_ref):
    e = _head_ones()
    y = y0_ref[0] + y1_ref[0]
    inv_n = 1.0 / RWKV_HEAD
    yc = y - _head_sum(y, e) * inv_n
    yn = yc * lax.rsqrt(_head_sum(yc * yc, e) * inv_n + RWKV_GN_EPS)
    yn = yn * lg_ref[...] + lb_ref[...]
    bonus = _head_sum(r_ref[0] * (ks0_ref[0] + ks1_ref[0]) * rk_ref[...], e) * v_ref[0]
    o_ref[0] = ((yn + bonus) * g_ref[0]).astype(o_ref.dtype)


def rwkv_mixer(u, mu, w_r, w_k, w_v, w_o, w0, w1, w2, a0, a1, a2, g1, g2, k_k, k_a, r_k, ln_g, ln_b):
    bsz, t, d = u.shape
    m = bsz * t
    dc = _pick(d, (256, 128))
    xs = pl.pallas_call(
        _rwkv_mix_body, grid=(bsz, d // dc),
        in_specs=[pl.BlockSpec((1, t, dc), lambda b, j: (b, 0, j)),
                  pl.BlockSpec((6, dc), lambda b, j: (0, j))],
        out_specs=[pl.BlockSpec((1, t, dc), lambda b, j: (b, 0, j))] * 6,
        out_shape=[jax.ShapeDtypeStruct((bsz, t, d), BF16)] * 6,
        compiler_params=_cparams("parallel", "parallel"),
    )(u, mu)
    xr, xw, xk, xv, xa, xg = (x.reshape(m, d) for x in xs)
    r = matmul(xr, w_r.astype(BF16)).reshape(bsz, t, d)
    k = matmul(xk, w_k.astype(BF16)).reshape(bsz, t, d)
    v = matmul(xv, w_v.astype(BF16)).reshape(bsz, t, d)
    rk = w1.shape[2]
    hw = matmul(xw, jnp.concatenate([w1[0], w1[1]], 1).astype(BF16)).reshape(bsz, t, 2 * rk)
    ha = matmul(xa, jnp.concatenate([a1[0], a1[1]], 1).astype(BF16)).reshape(bsz, t, 2 * rk)
    gr = g1.shape[1]
    grp = -(-gr // LANES) * LANES
    g1p = jnp.zeros((d, grp), F32).at[:, :gr].set(g1)
    g2p = jnp.zeros((grp, d), F32).at[:gr].set(g2)
    hg = jax.nn.sigmoid(matmul(xg, g1p.astype(BF16)))
    g = matmul(hg, g2p.astype(BF16)).reshape(bsz, t, d)

    tr = _pick(t, (64, 32, 16))
    row = pl.BlockSpec((1, tr, d), lambda b, i: (b, i, 0))
    low = pl.BlockSpec((1, tr, 2 * rk), lambda b, i: (b, i, 0))
    vec = pl.BlockSpec((1, d), lambda b, i: (0, 0))
    vec2 = pl.BlockSpec((2, d), lambda b, i: (0, 0))
    fact = pl.BlockSpec((2, rk, d), lambda b, i: (0, 0, 0))
    big = jax.ShapeDtypeStruct((bsz, t, d), F32)
    na, lw0, lw1, ks0, ks1, bb0, bb1 = pl.pallas_call(
        _rwkv_prep_body, grid=(bsz, t // tr),
        in_specs=[row, low, low, fact, fact, vec2, vec2, vec, vec],
        out_specs=[row] * 7, out_shape=[big] * 7,
        compiler_params=_cparams("parallel", "parallel"),
    )(k, hw, ha, w2.astype(BF16), a2.astype(BF16), w0, a0, k_k[None], k_a[None])

    y0, y1 = rwkv_scan(r, v, na, (lw0, lw1), (ks0, ks1), (bb0, bb1))

    out = pl.pallas_call(
        _rwkv_out_body, grid=(bsz, t // tr),
        in_specs=[row] * 7 + [vec, vec, vec], out_specs=row,
        out_shape=jax.ShapeDtypeStruct((bsz, t, d), BF16),
        compiler_params=_cparams("parallel", "parallel"),
    )(y0, y1, r, v, g, ks0, ks1, r_k.reshape(1, d), ln_g[None], ln_b[None])
    return matmul(out.reshape(m, d), w_o.astype(BF16)).reshape(bsz, t, d)


def _top_vals(s, kk):
    vals = []
    cur = s
    for _ in range(kk):
        mx = jnp.max(cur, axis=0, keepdims=True)
        vals.append(mx)
        cur = jnp.where(cur >= mx, NEG, cur)
    return jnp.concatenate(vals, axis=0)


def _peer_score_body(q_ref, keys_ref, thr_ref, s2_ref, e2_ref, cf_ref):
    nk = PEER_NKEYS
    kk = PEER_TOPK
    for h in range(PEER_HEADS):
        q1 = q_ref[h * 2 * nk:h * 2 * nk + nk, :].astype(BF16)
        q2 = q_ref[h * 2 * nk + nk:(h + 1) * 2 * nk, :].astype(BF16)
        s1 = _dot(keys_ref[0], q1)
        s2 = _dot(keys_ref[1], q2)
        a1 = _top_vals(s1, kk + 1)
        a2 = _top_vals(s2, kk + 1)
        cand = (a1[:kk, None, :] + a2[None, :kk, :]).reshape(kk * kk, -1)
        cv = _top_vals(cand, kk + 1)
        c17 = jnp.maximum(cv[kk:kk + 1], jnp.maximum(a1[kk:kk + 1] + a2[0:1], a1[0:1] + a2[kk:kk + 1]))
        theta = 0.5 * (cv[kk - 1:kk] + c17)
        zsum = jnp.sum(jnp.where(cand >= theta, jnp.exp(cand - cv[0:1]), 0.0), axis=0, keepdims=True)
        thr_ref[h] = theta - s1
        s2_ref[h] = s2
        e2_ref[h] = jnp.exp(s2 - a2[0:1])
        cf_ref[h] = jnp.exp(s1 - a1[0:1]) / zsum


def _gelu(x):
    return 0.5 * x * (1.0 + lax.erf(x * (2.0 ** -0.5)))


def _peer_dense_body(z_ref, u_ref, v_ref, thr_ref, s2_ref, e2_ref, cf_ref, o_ref, w_ref, *, stages, v_tiled, esub):
    nk = PEER_NKEYS
    eb = pl.program_id(1)
    te = u_ref.shape[0]
    n_i = te // nk
    strip = PEER_ROW_STRIP * 4 // max(n_i, 4)

    @pl.when(eb == 0)
    def _():
        o_ref[...] = jnp.zeros_like(o_ref)

    sub = min(PEER_TOK_SUB, z_ref.shape[1])
    esub = min(esub, te)

    d = z_ref.shape[0]
    kc = d * te // (stages * esub)

    def first_matmul(ts):
        acts = []
        for e0 in range(0, te, esub):
            acc = None
            for k0 in range(0, d, kc):
                part = _dot(u_ref[e0:e0 + esub, k0:k0 + kc], z_ref[k0:k0 + kc, ts])
                acc = part if acc is None else acc + part
                yield
            acts.append(acc)
        return acts

    def gates(ts):
        per_stage = (nk // strip) * PEER_HEADS // stages
        n = 0
        for si in range(nk // strip):
            js = slice(si * strip, (si + 1) * strip)
            w = [None] * n_i
            for h in range(PEER_HEADS):
                s2s = s2_ref[h, js, ts]
                e2s = e2_ref[h, js, ts]
                for ii in range(n_i):
                    i = eb * n_i + ii
                    c = jnp.where(s2s >= thr_ref[h, pl.ds(i, 1), ts], e2s, 0.0) * cf_ref[h, pl.ds(i, 1), ts]
                    w[ii] = c if w[ii] is None else w[ii] + c
                n += 1
                if n % per_stage == 0 and h + 1 < PEER_HEADS:
                    yield
            for ii in range(n_i):
                w_ref[ii * nk + si * strip:ii * nk + (si + 1) * strip, ts] = w[ii].astype(BF16)
            yield

    def activate(ts, acts):
        rows = esub // (stages // len(acts))
        for a, act in enumerate(acts):
            for r0 in range(0, esub, rows):
                rs = slice(a * esub + r0, a * esub + r0 + rows)
                w_ref[rs, ts] = w_ref[rs, ts] * _gelu(act[r0:r0 + rows]).astype(BF16)
                yield

    def second_matmul(ts):
        rows = d // stages
        for r0 in range(0, d, rows):
            vb = v_ref[0, r0:r0 + rows, :] if v_tiled else v_ref[r0:r0 + rows, :]
            o_ref[r0:r0 + rows, ts] += _dot(vb, w_ref[:, ts])
            yield

    tiles = [slice(t0, t0 + sub) for t0 in range(0, z_ref.shape[1], sub)]
    acts = {}
    for ph in range(len(tiles) + 2):
        gens, tags = [], []
        if ph < len(tiles):
            gens += [first_matmul(tiles[ph]), gates(tiles[ph])]
            tags += [ph, None]
        if 0 <= ph - 1 < len(tiles):
            gens.append(activate(tiles[ph - 1], acts[ph - 1]))
            tags.append(None)
        if 0 <= ph - 2 < len(tiles):
            gens.append(second_matmul(tiles[ph - 2]))
            tags.append(None)
        for tag, res in zip(tags, _lockstep(gens)):
            if tag is not None:
                acts[tag] = res


def peer_ffn(zt, w_q, sub_keys, u_tab, v_tab, stages=PEER_STAGES, v_tiled=False, esub=PEER_EXP_SUB, te=512,
             out_once=False):
    d, m = zt.shape
    nh, nk = PEER_HEADS, PEER_NKEYS
    qt = matmul(w_q.T.astype(BF16), zt, F32)
    tt = _pick(m, (256, 128))
    sh = jax.ShapeDtypeStruct((nh, nk, m), F32)
    blk = pl.BlockSpec((nh, nk, tt), lambda i: (0, 0, i))
    thr, s2, e2, cf = pl.pallas_call(
        _peer_score_body, grid=(m // tt,),
        in_specs=[pl.BlockSpec((nh * 2 * nk, tt), lambda i: (0, i)),
                  pl.BlockSpec((2, nk, PEER_DKEY // 2), lambda i: (0, 0, 0))],
        out_specs=[blk, blk, blk, blk], out_shape=[sh, sh, sh, sh],
        compiler_params=_cparams("parallel"),
    )(qt, sub_keys.astype(BF16))

    tm = _pick(m, (2 * PEER_TOK_SUB, PEER_TOK_SUB, LANES))
    ne = u_tab.shape[0]
    once = pl.Buffered(1)
    sblk = pl.BlockSpec((nh, nk, tm), lambda i, e: (0, 0, i), pipeline_mode=once)
    return pl.pallas_call(
        functools.partial(_peer_dense_body, stages=stages, v_tiled=v_tiled, esub=esub), grid=(m // tm, ne // te),
        in_specs=[pl.BlockSpec((d, tm), lambda i, e: (0, i), pipeline_mode=once),
                  pl.BlockSpec((te, d), lambda i, e: (e, 0)),
                  (pl.BlockSpec((1, d, te), lambda i, e: (e, 0, 0)) if v_tiled
                   else pl.BlockSpec((d, te), lambda i, e: (0, e))),
                  sblk, sblk, sblk, sblk],
        out_specs=(pl.BlockSpec((d, tm), lambda i, e: (0, i), pipeline_mode=once) if out_once
                   else pl.BlockSpec((d, tm), lambda i, e: (0, i))),
        out_shape=jax.ShapeDtypeStruct((d, m), F32),
        scratch_shapes=[pltpu.VMEM((te, tm), BF16)],
        compiler_params=pltpu.CompilerParams(dimension_semantics=("parallel", "arbitrary"),
                                             vmem_limit_bytes=(60 if out_once else 52) * 1024 * 1024),
    )(zt, u_tab.astype(BF16),
      (v_tab.reshape(ne // te, te, d).transpose(0, 2, 1) if v_tiled else v_tab.T).astype(BF16), thr, s2, e2, cf)


def _ada_mods(c, c_ctx, w_down, w_up, b_up):
    bsz, d = c.shape
    cond = jnp.concatenate([c, c_ctx[None]], 0)
    pad = 16 - cond.shape[0] % 16
    cond = jnp.concatenate([cond, jnp.zeros((pad, d), F32)], 0)
    hid = matmul(jax.nn.silu(cond), w_down.astype(BF16), F32)
    m = (matmul(hid, w_up.astype(BF16), F32) + b_up)[:bsz + 1].reshape(bsz + 1, N_MOD, d)
    return jnp.stack([jnp.broadcast_to(m[bsz], (bsz, N_MOD, d)), m[:bsz]], axis=1)


def kernel(x, c, ctx, c_ctx, ada_w_down, ada_w_up, ada_b,
           gla_w_in, gla_w_g1, gla_w_g2, gla_b_g, gla_norm_g, gla_w_o,
           mla_w_in, mla_q_norm, mla_kv_norm, mla_w_uq, mla_w_ukv, mla_w_o,
           rwkv_mu, rwkv_w_r, rwkv_w_k, rwkv_w_v, rwkv_w_o, rwkv_w0, rwkv_w1, rwkv_w2,
           rwkv_a0, rwkv_a1, rwkv_a2, rwkv_g1, rwkv_g2, rwkv_k_k, rwkv_k_a, rwkv_r_k, rwkv_ln_g, rwkv_ln_b,
           peer_w_q, peer_sub_keys, peer_u, peer_v):
    z = jnp.concatenate([ctx, x], axis=1)
    bsz, t, d = z.shape
    mods = [_ada_mods(c, c_ctx, ada_w_down[i], ada_w_up[i], ada_b[i]) for i in range(DEPTH)]
    _, u = ln_mod(z, mods_n=mods[0], sidx=0)
    for i in range(DEPTH):
        j = i // N_MIXERS
        if i % N_MIXERS == 0:
            y = gla_mixer(u, gla_w_in[j], gla_w_g1[j], gla_w_g2[j], gla_b_g[j], gla_norm_g[j], gla_w_o[j])
        elif i % N_MIXERS == 1:
            y = mla_mixer(u, mla_w_in[j], mla_q_norm[j], mla_kv_norm[j], mla_w_uq[j], mla_w_ukv[j], mla_w_o[j])
        else:
            y = rwkv_mixer(u, rwkv_mu[j], rwkv_w_r[j], rwkv_w_k[j], rwkv_w_v[j], rwkv_w_o[j],
                           rwkv_w0[j], rwkv_w1[j], rwkv_w2[j], rwkv_a0[j], rwkv_a1[j], rwkv_a2[j],
                           rwkv_g1[j], rwkv_g2[j], rwkv_k_k[j], rwkv_k_a[j], rwkv_r_k[j],
                           rwkv_ln_g[j], rwkv_ln_b[j])
        z, ut = ln_mod(z, y, mods[i], 2, mods[i], 3, u_t=True)
        ht = peer_ffn(ut, peer_w_q[i], peer_sub_keys[i], peer_u[i], peer_v[i],
                      stages=(4, 4, 2, 4)[i], esub=(256, 512, 512, 1024)[i], te=(512, 512, 512, 1024)[i],
                      out_once=(i == 3))
        if i + 1 < DEPTH:
            z, u = ln_mod(z, ht, mods[i], 5, mods[i + 1], 0, y_t=True)
        else:
            z, _ = ln_mod(z, ht, mods[i], 5, y_t=True)
    return z[:, CTX_LEN:]
```

```python
import functools

import jax
import jax.numpy as jnp
from jax import lax
from jax.experimental import pallas as pl
from jax.experimental.pallas import tpu as pltpu

F32 = jnp.float32
BF16 = jnp.bfloat16

DEPTH = 4
CTX_LEN = 256
GRID_W = 64
N_MIXERS = 3
N_MOD = 6
LN_EPS = 1e-6
DEEPNORM_ALPHA = (2.0 * DEPTH) ** 0.25

GLA_HEADS = 8
GLA_GATE_RANK = 16
GLA_GATE_NORMALIZER = 16.0

MLA_Q_RANK = 1536
MLA_KV_RANK = 512
MLA_NOPE = 128
MLA_ROPE = 64
MLA_V = 128
ROPE_THETA = 10000.0

RWKV_HEAD = 64
RWKV_GN_EPS = 64e-5

PEER_HEADS = 8
PEER_NKEYS = 128
PEER_DKEY = 256
PEER_TOPK = 16

LANES = 128
SUBLANES = 8
VMEM_LIMIT = 52 * 1024 * 1024

GLA_CHUNK = 64
GLA_HEADS_PER_STEP = 2
GLA_SUB = 4
RWKV_CHUNK = 64
RWKV_SUB = 16
MLA_KEY_CHAINS = 2
LOG2E = 1.4426950408889634
PEER_TOK_SUB = 256
PEER_EXP_BLOCK = 512
PEER_STAGES = 4
PEER_ROW_STRIP = 32
NEG = -1e30


def _pick(n, cands):
    for c in cands:
        if n % c == 0:
            return c
    return n


def _cparams(*sem):
    return pltpu.CompilerParams(dimension_semantics=sem, vmem_limit_bytes=VMEM_LIMIT)


def _dot(a, b):
    return jnp.dot(a, b, preferred_element_type=F32)


def _dot_nt(a, b):
    return lax.dot_general(a, b, (((1,), (1,)), ((), ())), preferred_element_type=F32)


def _dot_tn(a, b):
    return lax.dot_general(a, b, (((0,), (0,)), ((), ())), preferred_element_type=F32)


def _split3(x):
    hi = x.astype(BF16)
    r1 = x - hi.astype(F32)
    mid = r1.astype(BF16)
    lo = (r1 - mid.astype(F32)).astype(BF16)
    return hi, mid, lo


def _dot_sel(sel, x):
    s = sel.astype(BF16)
    hi, mid, lo = _split3(x)
    return _dot(s, hi) + _dot(s, mid) + _dot(s, lo)


def _dot_x3(a, b, dims=None):
    ah = a.astype(BF16)
    al = (a - ah.astype(F32)).astype(BF16)
    bh = b.astype(BF16)
    bl = (b - bh.astype(F32)).astype(BF16)
    f = _dot if dims is None else dims
    return f(ah, bh) + f(ah, bl) + f(al, bh)


def _dot_b(a, b):
    return _dot(a.astype(BF16), b.astype(BF16))


def _log_sigmoid(x):
    return jnp.minimum(x, 0.0) - jnp.log1p(jnp.exp(-jnp.abs(x)))


def _sigmoid(x):
    return 1.0 / (1.0 + jnp.exp(-x))


def _mm_body(a_ref, b_ref, o_ref):
    o_ref[...] = _dot(a_ref[...].astype(BF16), b_ref[...].astype(BF16)).astype(o_ref.dtype)


def matmul(a, b, out_dtype=F32):
    m, k = a.shape
    k2, n = b.shape
    assert k == k2
    a_bytes = jnp.dtype(a.dtype).itemsize
    tm_cands = (1024, 512, 256, 128, 64, 32, 16) if a_bytes * k <= 8192 else (512, 256, 128, 64, 32, 16)
    tm = _pick(m, tm_cands)
    tn = _pick(n, (512, 256, 128))
    return pl.pallas_call(
        _mm_body,
        grid=(m // tm, n // tn),
        in_specs=[pl.BlockSpec((tm, k), lambda i, j: (i, 0)),
                  pl.BlockSpec((k, tn), lambda i, j: (0, j))],
        out_specs=pl.BlockSpec((tm, tn), lambda i, j: (i, j)),
        out_shape=jax.ShapeDtypeStruct((m, n), out_dtype),
        compiler_params=_cparams("parallel", "parallel"),
    )(a, b)


def _layer_norm(z):
    zc = z - jnp.mean(z, -1, keepdims=True)
    return zc * lax.rsqrt(jnp.mean(zc * zc, -1, keepdims=True) + LN_EPS)


def _ln_mod_body(*refs, gidx, sidx, has_y, y_t, u_t):
    it = iter(refs)
    z_ref = next(it)
    y_ref = next(it) if has_y else None
    mg_ref = next(it) if has_y else None
    mn_ref = next(it) if sidx is not None else None
    zo_ref = next(it) if has_y else None
    u_ref = next(it) if sidx is not None else None
    z = z_ref[0]
    if has_y:
        y = y_ref[...].T if y_t else y_ref[0]
        z = _layer_norm(DEEPNORM_ALPHA * z + mg_ref[0, 0, gidx:gidx + 1, :] * y)
        zo_ref[0] = z
    if sidx is not None:
        u = z * (1.0 + mn_ref[0, 0, sidx + 1:sidx + 2, :]) + mn_ref[0, 0, sidx:sidx + 1, :]
        if u_t:
            u_ref[...] = u.T.astype(u_ref.dtype)
        else:
            u_ref[0] = u.astype(u_ref.dtype)


def ln_mod(z, y=None, mods_g=None, gidx=None, mods_n=None, sidx=None, y_t=False, u_t=False):
    bsz, t, d = z.shape
    tr = _pick(CTX_LEN, (256, 128, 64, 32, 16))
    nt = t // tr
    nctx = CTX_LEN // tr
    has_y = y is not None
    row = pl.BlockSpec((1, tr, d), lambda b, i: (b, i, 0))
    col = pl.BlockSpec((d, tr), lambda b, i: (0, b * nt + i))
    mod = pl.BlockSpec((1, 1, N_MOD, d), lambda b, i: (b, jnp.where(i < nctx, 0, 1), 0, 0))
    ins, in_specs, outs, out_specs = [z], [row], [], []
    if has_y:
        ins += [y, mods_g]
        in_specs += [col if y_t else row, mod]
        outs.append(jax.ShapeDtypeStruct((bsz, t, d), F32))
        out_specs.append(row)
    if sidx is not None:
        ins.append(mods_n)
        in_specs.append(mod)
        outs.append(jax.ShapeDtypeStruct((d, bsz * t) if u_t else (bsz, t, d), BF16))
        out_specs.append(col if u_t else row)
    res = pl.pallas_call(
        functools.partial(_ln_mod_body, gidx=gidx, sidx=sidx, has_y=has_y, y_t=y_t, u_t=u_t),
        grid=(bsz, nt), in_specs=in_specs, out_specs=out_specs, out_shape=outs,
        compiler_params=_cparams("parallel", "parallel"),
    )(*ins)
    res = list(res)
    zo = res.pop(0) if has_y else None
    u = res.pop(0) if sidx is not None else None
    return zo, u


def _lockstep(gens):
    out = [None] * len(gens)
    live = list(range(len(gens)))
    while live:
        for i in list(live):
            try:
                next(gens[i])
            except StopIteration as done:
                out[i] = done.value
                live.remove(i)
    return out


def _gla_chunk(q, k, v, glow, wg2, bg, st, rev):
    c, dk = q.shape
    sub = GLA_SUB
    q = q.astype(F32) * (dk ** -0.5)
    k = k.astype(F32)
    gl = _dot_x3(glow, wg2) + bg
    yield
    g = _log_sigmoid(gl) * (1.0 / GLA_GATE_NORMALIZER)
    row = lax.broadcasted_iota(jnp.int32, (c, c), 0)
    col = lax.broadcasted_iota(jnp.int32, (c, c), 1)
    tri = (row <= col) if rev else (row >= col)
    b = _dot_sel(tri.astype(F32), g)
    yield
    btot = jnp.sum(g, axis=0, keepdims=True)

    o = _dot_nt((q * jnp.exp(b)).astype(BF16), st.astype(BF16))

    rowk = lax.broadcasted_iota(jnp.int32, (c, 1), 0)
    halves = []
    hsz = c // 2
    while hsz >= sub:
        halves.append(hsz)
        hsz //= 2
    pick = jnp.concatenate(
        [(col == (row // (2 * hf)) * (2 * hf) + (hf if rev else hf - 1)).astype(F32) for hf in halves], axis=0)
    refs = _dot_sel(pick, b)
    yield
    att = jnp.zeros((c, c), F32)
    for lv, hf in enumerate(halves):
        refb = refs[lv * c:(lv + 1) * c]
        late = (rowk % (2 * hf)) >= hf
        q_side, k_side = (~late, late) if rev else (late, ~late)
        qf = jnp.where(q_side, q * jnp.exp(jnp.minimum(b - refb, 0.0)), 0.0)
        kf = jnp.where(k_side, k * jnp.exp(jnp.minimum(refb - b, 0.0)), 0.0)
        same = (row // (2 * hf)) == (col // (2 * hf))
        att = att + jnp.where(same, _dot_nt(qf.astype(BF16), kf.astype(BF16)), 0.0)
    yield

    rmod = rowk % sub
    for lag in range(sub):
        sh = (c - lag) % c if rev else lag
        ks = pltpu.roll(k, sh, 0) if sh else k
        bs = pltpu.roll(b, sh, 0) if sh else b
        term = jnp.sum(q * ks * jnp.exp(jnp.minimum(b - bs, 0.0)), axis=1, keepdims=True)
        valid = (rmod + lag < sub) if rev else (rmod >= lag)
        hit = (col == row + lag) if rev else (col == row - lag)
        att = att + jnp.where(hit & valid, term, 0.0)

    o = o + _dot(att.astype(BF16), v)
    yield
    kd = (k * jnp.exp(btot - b)).astype(BF16)
    return o, st * jnp.exp(btot) + _dot_tn(v, kd)


def _gla_body(*refs):
    ins, (of_ref, ob_ref, st_ref) = refs[:12], refs[12:]

    @pl.when(pl.program_id(2) == 0)
    def _():
        st_ref[...] = jnp.zeros_like(st_ref)

    dv = st_ref.shape[2]
    dk = st_ref.shape[3]
    gens, dest = [], []
    for dr in range(2):
        q_ref, k_ref, v_ref, gl_ref, wg2_ref, bg_ref = ins[6 * dr:6 * dr + 6]
        for hh in range(st_ref.shape[1]):
            ks, vs = slice(hh * dk, (hh + 1) * dk), slice(hh * dv, (hh + 1) * dv)
            gens.append(_gla_chunk(q_ref[0, :, ks], k_ref[0, :, ks], v_ref[0, :, vs], gl_ref[0],
                                   wg2_ref[:, ks], bg_ref[:, ks], st_ref[dr, hh], rev=(dr == 1)))
            dest.append((dr, hh, vs))
    for (dr, hh, vs), (o, st) in zip(dest, _lockstep(gens)):
        (of_ref, ob_ref)[dr][0, :, vs] = o.astype(of_ref.dtype)
        st_ref[dr, hh] = st


def gla_scan(p, glow, wg2, bg):
    bsz, t, d3 = p.shape
    d = d3 // 3
    h = GLA_HEADS
    dk, dv = (d // 2) // h, d // h
    c = GLA_CHUNK
    nc, ncx = t // c, CTX_LEN // c

    def rchunk(s):
        return jnp.where(s < ncx, ncx - 1 - s, nc - 1 - (s - ncx))

    hp = _pick(h, (GLA_HEADS_PER_STEP, 1))
    hg = h // hp

    def specs(chunk):
        return [pl.BlockSpec((1, c, hp * dk), lambda b, hh, s: (b, chunk(s), hh)),
                pl.BlockSpec((1, c, hp * dk), lambda b, hh, s: (b, chunk(s), hg + hh)),
                pl.BlockSpec((1, c, hp * dv), lambda b, hh, s: (b, chunk(s), hg + hh)),
                pl.BlockSpec((1, c, LANES), lambda b, hh, s: (b, chunk(s), 0)),
                pl.BlockSpec((LANES, hp * dk), lambda b, hh, s: (0, hh)),
                pl.BlockSpec((1, hp * dk), lambda b, hh, s: (0, hh))]

    sh = jax.ShapeDtypeStruct((bsz, t, d), BF16)
    return pl.pallas_call(
        _gla_body,
        grid=(bsz, hg, nc),
        in_specs=specs(lambda s: s) + specs(rchunk),
        out_specs=[pl.BlockSpec((1, c, hp * dv), lambda b, hh, s: (b, s, hh)),
                   pl.BlockSpec((1, c, hp * dv), lambda b, hh, s: (b, rchunk(s), hh))],
        out_shape=[sh, sh],
        scratch_shapes=[pltpu.VMEM((2, hp, dv, dk), F32)],
        compiler_params=_cparams("parallel", "parallel", "arbitrary"),
    )(p, p, p, glow, wg2[0], bg[0], p, p, p, glow, wg2[1], bg[1])


def _gla_gate_body(of_ref, ob_ref, r_ref, g_ref, o_ref):
    dv = g_ref.shape[1]
    o = of_ref[0].astype(F32) + ob_ref[0].astype(F32)
    r = r_ref[0].astype(F32)
    outs = []
    for hh in range(o.shape[1] // dv):
        seg = o[:, hh * dv:(hh + 1) * dv]
        outs.append(seg * lax.rsqrt(jnp.mean(seg * seg, -1, keepdims=True) + 1e-6) * g_ref[...])
    o_ref[0] = (jnp.concatenate(outs, axis=1) * (r * _sigmoid(r))).astype(o_ref.dtype)


def gla_gate(o_f, o_b, p, norm_g):
    bsz, t, d = o_f.shape
    tr = _pick(t, (256, 128, 64, 32, 16))
    row = pl.BlockSpec((1, tr, d), lambda b, i: (b, i, 0))
    return pl.pallas_call(
        _gla_gate_body, grid=(bsz, t // tr),
        in_specs=[row, row, pl.BlockSpec((1, tr, d), lambda b, i: (b, i, 2)),
                  pl.BlockSpec((1, norm_g.shape[1]), lambda b, i: (0, 0))],
        out_specs=row, out_shape=jax.ShapeDtypeStruct((bsz, t, d), BF16),
        compiler_params=_cparams("parallel", "parallel"),
    )(o_f, o_b, p, norm_g)


def gla_mixer(u, w_in, w_g1, w_g2, b_g, norm_g, w_o):
    bsz, t, d = u.shape
    qk = d // 2
    u2 = u.reshape(bsz * t, d)
    p = matmul(u2, w_in.astype(BF16), BF16).reshape(bsz, t, 3 * d)
    r = GLA_GATE_RANK
    wg1 = jnp.zeros((d, LANES), F32).at[:, :r].set(w_g1[0]).at[:, r:2 * r].set(w_g1[1])
    glow = matmul(u2, wg1.astype(BF16), F32).reshape(bsz, t, LANES)
    wg2 = [jnp.zeros((LANES, qk), F32).at[s * r:(s + 1) * r].set(w_g2[s]) for s in range(2)]
    outs = gla_scan(p, glow, wg2, [b_g[0][None], b_g[1][None]])
    gated = gla_gate(outs[0], outs[1], p, norm_g[None])
    return matmul(gated.reshape(bsz * t, d), w_o.astype(BF16), F32).reshape(bsz, t, d)


def _rms(x, gain):
    return x * lax.rsqrt(jnp.mean(x * x, -1, keepdims=True) + 1e-6) * gain


def _rope128(a, c, s):
    return a * c + pltpu.roll(a, LANES // 2, 1) * s


def _mla_norm_body(h_ref, qg_ref, kg_ref, c_ref, s_ref, cq_ref, ckv_ref, kr_ref):
    qr, kvr = cq_ref.shape[2], ckv_ref.shape[2]
    h = h_ref[0]
    cq_ref[0] = _rms(h[:, :qr], qg_ref[...]).astype(BF16)
    ckv_ref[0] = _rms(h[:, qr:qr + kvr], kg_ref[...]).astype(BF16)
    kr_ref[0] = _rope128(h[:, qr + kvr:], c_ref[...], s_ref[...]).astype(BF16)


def _mla_attn_body(q_ref, kn_ref, v_ref, kr_ref, c_ref, s_ref, o_ref, kf_ref, vf_ref, *, nctx_tiles, scale):
    qt = pl.program_id(2)
    t = kf_ref.shape[0]
    ctx = nctx_tiles * q_ref.shape[1]

    @pl.when(qt == 0)
    def _():
        kf_ref[:, :LANES] = kn_ref[0]
        kf_ref[:, LANES:] = kr_ref[0]
        vf_ref[:, :LANES] = v_ref[0]
        vf_ref[:, LANES:] = jnp.ones((t, LANES), BF16)

    q = q_ref[0].astype(F32)
    qr = _rope128(q[:, LANES:], c_ref[...], s_ref[...])
    qf = (jnp.concatenate([q[:, :LANES], qr], axis=1) * (scale * LOG2E)).astype(BF16)

    def attend(keys):
        s = _dot_nt(qf, kf_ref[keys])
        yield
        m = jnp.max(s, -1, keepdims=True)
        p = jnp.exp2(s - m).astype(BF16)
        return m, _dot(p, vf_ref[keys])

    def finish(parts):
        m = functools.reduce(jnp.maximum, [pm for pm, _ in parts])
        acc = sum(jnp.exp2(pm - m) * po for pm, po in parts)
        o_ref[0] = (acc[:, :LANES] / acc[:, LANES:LANES + 1]).astype(o_ref.dtype)

    @pl.when(qt < nctx_tiles)
    def _():
        finish(_lockstep([attend(slice(0, ctx))]))

    @pl.when(qt >= nctx_tiles)
    def _():
        step = t // MLA_KEY_CHAINS
        finish(_lockstep([attend(slice(k0, k0 + step)) for k0 in range(0, t, step)]))


def _rope_tables(t):
    rows = (t - CTX_LEN) // GRID_W
    row = jnp.repeat(jnp.arange(rows, dtype=F32), GRID_W)
    colp = jnp.tile(jnp.arange(GRID_W, dtype=F32), rows)
    n_freq = MLA_ROPE // 4
    inv = ROPE_THETA ** (-jnp.arange(n_freq, dtype=F32) / n_freq)
    ang = jnp.concatenate([row[:, None] * inv, colp[:, None] * inv], -1)
    cos = jnp.concatenate([jnp.ones((CTX_LEN, MLA_ROPE // 2), F32), jnp.cos(ang)], 0)
    sin = jnp.concatenate([jnp.zeros((CTX_LEN, MLA_ROPE // 2), F32), jnp.sin(ang)], 0)
    z = jnp.zeros_like(cos)
    return jnp.concatenate([cos, cos, z, z], 1), jnp.concatenate([-sin, sin, z, z], 1)


def mla_mixer(u, w_in, q_norm, kv_norm, w_uq, w_ukv, w_o):
    bsz, t, d = u.shape
    nh = d // 128
    qr, kvr, rp = MLA_Q_RANK, MLA_KV_RANK, MLA_ROPE
    ev, od = jnp.arange(0, rp, 2), jnp.arange(1, rp, 2)
    perm = jnp.concatenate([ev, od, od, ev])
    w_in_p = jnp.concatenate([w_in[:, :qr + kvr], w_in[:, qr + kvr + perm]], axis=1)
    hw = qr + kvr + LANES
    h = matmul(u.reshape(bsz * t, d), w_in_p.astype(BF16), F32).reshape(bsz, t, hw)
    ctab, stab = _rope_tables(t)

    tr = _pick(t, (256, 128, 64, 32, 16))
    cq, ckv, kr = pl.pallas_call(
        _mla_norm_body, grid=(bsz, t // tr),
        in_specs=[pl.BlockSpec((1, tr, hw), lambda b, i: (b, i, 0)),
                  pl.BlockSpec((1, qr), lambda b, i: (0, 0)),
                  pl.BlockSpec((1, kvr), lambda b, i: (0, 0)),
                  pl.BlockSpec((tr, LANES), lambda b, i: (i, 0)),
                  pl.BlockSpec((tr, LANES), lambda b, i: (i, 0))],
        out_specs=[pl.BlockSpec((1, tr, qr), lambda b, i: (b, i, 0)),
                   pl.BlockSpec((1, tr, kvr), lambda b, i: (b, i, 0)),
                   pl.BlockSpec((1, tr, LANES), lambda b, i: (b, i, 0))],
        out_shape=[jax.ShapeDtypeStruct((bsz, t, qr), BF16),
                   jax.ShapeDtypeStruct((bsz, t, kvr), BF16),
                   jax.ShapeDtypeStruct((bsz, t, LANES), BF16)],
        compiler_params=_cparams("parallel", "parallel"),
    )(h, q_norm[None], kv_norm[None], ctab, stab)

    hq = MLA_NOPE + rp
    qcols = (jnp.arange(nh)[:, None] * hq
             + jnp.concatenate([jnp.arange(MLA_NOPE), MLA_NOPE + perm])[None, :]).reshape(-1)
    q = matmul(cq.reshape(bsz * t, qr), w_uq[:, qcols].astype(BF16), BF16).reshape(bsz, t, nh * 2 * LANES)
    kv = matmul(ckv.reshape(bsz * t, kvr), w_ukv.astype(BF16), BF16).reshape(bsz, t, nh * 2 * LANES)

    tq = _pick(CTX_LEN, (256, 128, 64, 32, 16))
    scale = (MLA_NOPE + rp) ** -0.5
    o = pl.pallas_call(
        functools.partial(_mla_attn_body, nctx_tiles=CTX_LEN // tq, scale=scale),
        grid=(bsz, nh, t // tq),
        in_specs=[pl.BlockSpec((1, tq, 2 * LANES), lambda b, hh, i: (b, i, hh)),
                  pl.BlockSpec((1, t, LANES), lambda b, hh, i: (b, 0, 2 * hh)),
                  pl.BlockSpec((1, t, LANES), lambda b, hh, i: (b, 0, 2 * hh + 1)),
                  pl.BlockSpec((1, t, LANES), lambda b, hh, i: (b, 0, 0)),
                  pl.BlockSpec((tq, LANES), lambda b, hh, i: (i, 0)),
                  pl.BlockSpec((tq, LANES), lambda b, hh, i: (i, 0))],
        out_specs=pl.BlockSpec((1, tq, LANES), lambda b, hh, i: (b, i, hh)),
        out_shape=jax.ShapeDtypeStruct((bsz, t, nh * MLA_V), BF16),
        scratch_shapes=[pltpu.VMEM((t, 2 * LANES), BF16), pltpu.VMEM((t, 2 * LANES), BF16)],
        compiler_params=_cparams("parallel", "parallel", "arbitrary"),
    )(q, kv, kv, kr, ctab, stab)
    return matmul(o.reshape(bsz * t, nh * MLA_V), w_o.astype(BF16), F32).reshape(bsz, t, d)


def _rwkv_mix_body(u_ref, mu_ref, *o_refs):
    t = u_ref.shape[1]
    u = u_ref[0].astype(F32)
    row = lax.broadcasted_iota(jnp.int32, (t, 1), 0)
    prev = jnp.where((row == 0) | (row == CTX_LEN), 0.0, pltpu.roll(u, 1, 0))
    nxt = jnp.where((row == CTX_LEN - 1) | (row == t - 1), 0.0, pltpu.roll(u, t - 1, 0))
    xx = 0.5 * (prev + nxt) - u
    for n, o_ref in enumerate(o_refs):
        o_ref[0] = (u + xx * mu_ref[n:n + 1, :]).astype(o_ref.dtype)


def _head_sum(x, e):
    eb = e.astype(BF16)
    outs = []
    for s in range(x.shape[1] // LANES):
        hi, mid, lo = _split3(x[:, s * LANES:(s + 1) * LANES])
        outs.append(_dot(hi, eb) + _dot(mid, eb) + _dot(lo, eb))
    return jnp.concatenate(outs, axis=1)


def _head_ones():
    r = lax.broadcasted_iota(jnp.int32, (LANES, LANES), 0) // RWKV_HEAD
    c = lax.broadcasted_iota(jnp.int32, (LANES, LANES), 1) // RWKV_HEAD
    return (r == c).astype(F32)


def _rwkv_prep_body(k_ref, hw_ref, ha_ref, w2_ref, a2_ref, w0_ref, a0_ref, kk_ref, ka_ref,
                    na_ref, lw0_ref, lw1_ref, ks0_ref, ks1_ref, bb0_ref, bb1_ref):
    rk = w2_ref.shape[1]
    k = k_ref[0]
    kk = k * kk_ref[...]
    kk = kk * lax.rsqrt(_head_sum(kk * kk, _head_ones()) + 1e-12)
    na_ref[0] = (-kk).astype(na_ref.dtype)
    hw = jnp.tanh(hw_ref[0])
    ha = ha_ref[0]
    for s, (lw_ref, ks_ref, bb_ref) in enumerate(((lw0_ref, ks0_ref, bb0_ref), (lw1_ref, ks1_ref, bb1_ref))):
        wl = w0_ref[s:s + 1, :] + _dot(hw[:, s * rk:(s + 1) * rk].astype(BF16), w2_ref[s])
        lw_ref[0] = -jnp.exp(_log_sigmoid(wl) - 0.5)
        a = _sigmoid(a0_ref[s:s + 1, :] + _dot(ha[:, s * rk:(s + 1) * rk].astype(BF16), a2_ref[s]))
        ks_ref[0] = (k * (1.0 + (a - 1.0) * ka_ref[...])).astype(ks_ref.dtype)
        bb_ref[0] = (kk * a).astype(bb_ref.dtype)


def _rwkv_chunk(r, lw, ks, v, na, bb, ht, rev):
    c = r.shape[0]
    n = RWKV_HEAD
    c2 = 2 * c
    r, ks, v, na, bb = (x.astype(F32) for x in (r, ks, v, na, bb))
    ri = lax.broadcasted_iota(jnp.int32, (c, c), 0)
    ci = lax.broadcasted_iota(jnp.int32, (c, c), 1)
    cs = _dot_sel(((ri <= ci) if rev else (ri >= ci)).astype(F32), lw)
    yield
    cm = cs - lw
    ctot = jnp.sum(lw, axis=0, keepdims=True)

    lane_a = lax.broadcasted_iota(jnp.int32, (1, LANES), 1) < n

    def stack(x):
        return jnp.concatenate([jnp.where(lane_a, x, 0.0), jnp.where(lane_a, 0.0, x)], axis=0)

    at = stack(na * jnp.exp(cm))
    rt = stack(r * jnp.exp(cs))
    ecs = jnp.exp(-cs)
    bh = stack(bb * ecs)
    kh = stack(ks * ecs)
    ece = jnp.exp(ctot - cs)
    be = stack(bb * ece)
    ke = stack(ks * ece)
    vs = stack(v)

    rr = lax.broadcasted_iota(jnp.int32, (c2, c2), 0)
    cc = lax.broadcasted_iota(jnp.int32, (c2, c2), 1)
    same = (rr // c) == (cc // c)
    tr_, tc_ = rr % c, cc % c
    strict = same & ((tc_ > tr_) if rev else (tc_ < tr_))
    incl = same & ((tc_ >= tr_) if rev else (tc_ <= tr_))
    diag_blk = same & ((tr_ // RWKV_SUB) == (tc_ // RWKV_SUB))
    eye = (rr == cc).astype(F32)

    lhs = jnp.concatenate([at, rt], axis=0)
    rhs = jnp.concatenate([bh, kh], axis=0)
    a4 = _dot_nt(lhs.astype(BF16), rhs.astype(BF16))
    nmat = jnp.where(strict, _dot_x3(at, bh, _dot_nt), 0.0)
    yield
    aak = jnp.where(strict, a4[:c2, c2:], 0.0)
    arb = jnp.where(incl, a4[c2:, :c2], 0.0)
    ark = jnp.where(incl, a4[c2:, c2:], 0.0)

    htb = ht.astype(BF16)
    x0 = _dot_nt(at.astype(BF16), htb) + _dot(aak.astype(BF16), vs.astype(BF16))
    yield

    nd = jnp.where(diag_blk, nmat, 0.0)
    no = nmat - nd
    tm = nd
    pw = nd
    for _ in range(3):
        pw = _dot_b(pw, pw)
        yield
        tm = tm + pw + _dot_b(pw, tm)
        yield
    z = no + _dot_b(tm, no)
    u = x0 + _dot_b(tm, x0)
    yield
    z2 = _dot_b(z, z)
    u = u + _dot_b(z, u)
    yield
    u = u + _dot_b(z2, u)
    yield

    ub = u.astype(BF16)
    vb = vs.astype(BF16)
    ys = _dot_nt(rt.astype(BF16), htb) + _dot(arb.astype(BF16), ub) + _dot(ark.astype(BF16), vb)
    yield
    hn = ht * jnp.exp(ctot) + _dot_tn(ub, be.astype(BF16)) + _dot_tn(vb, ke.astype(BF16))
    hr = lax.broadcasted_iota(jnp.int32, (LANES, LANES), 0) // n
    hc = lax.broadcasted_iota(jnp.int32, (LANES, LANES), 1) // n
    return ys[:c] + ys[c:], jnp.where(hr == hc, hn, 0.0)


def _rwkv_scan_body(*refs):
    ins, (y0_ref, y1_ref, st_ref) = refs[:12], refs[12:]

    @pl.when(pl.program_id(2) == 0)
    def _():
        st_ref[...] = jnp.zeros_like(st_ref)

    gens, dest = [], []
    for dr in range(2):
        r_ref, lw_ref, ks_ref, v_ref, na_ref, bb_ref = ins[6 * dr:6 * dr + 6]
        for p in range(r_ref.shape[2] // LANES):
            sl = slice(p * LANES, (p + 1) * LANES)
            gens.append(_rwkv_chunk(r_ref[0, :, sl], lw_ref[0, :, sl], ks_ref[0, :, sl], v_ref[0, :, sl],
                                    na_ref[0, :, sl], bb_ref[0, :, sl], st_ref[dr, p], rev=(dr == 1)))
            dest.append(((y0_ref, y1_ref)[dr], sl, dr, p))
    for (y_ref, sl, dr, p), (y, hn) in zip(dest, _lockstep(gens)):
        y_ref[0, :, sl] = y
        st_ref[dr, p] = hn


def rwkv_scan(r, v, na, lw, ks, bb):
    bsz, t, d = r.shape
    c = RWKV_CHUNK
    nc, ncx = t // c, CTX_LEN // c
    pp = _pick(d // LANES, (4, 2, 1))
    w = pp * LANES

    def rchunk(s):
        return jnp.where(s < ncx, ncx - 1 - s, nc - 1 - (s - ncx))

    fwd = pl.BlockSpec((1, c, w), lambda b, p, s: (b, s, p))
    bwd = pl.BlockSpec((1, c, w), lambda b, p, s: (b, rchunk(s), p))
    sh = jax.ShapeDtypeStruct((bsz, t, d), F32)
    return pl.pallas_call(
        _rwkv_scan_body,
        grid=(bsz, d // w, nc),
        in_specs=[fwd] * 6 + [bwd] * 6, out_specs=[fwd, bwd], out_shape=[sh, sh],
        scratch_shapes=[pltpu.VMEM((2, pp, LANES, LANES), F32)],
        compiler_params=_cparams("parallel", "parallel", "arbitrary"),
    )(r, lw[0], ks[0], v, na, bb[0], r, lw[1], ks[1], v, na, bb[1])


def _rwkv_out_body(y0_ref, y1_ref, r_ref, v_ref, g_ref, ks0_ref, ks1_ref, rk_ref, lg_ref, lb_ref, o_ref):
    e = _head_ones()
    y = y0_ref[0] + y1_ref[0]
    inv_n = 1.0 / RWKV_HEAD
    yc = y - _head_sum(y, e) * inv_n
    yn = yc * lax.rsqrt(_head_sum(yc * yc, e) * inv_n + RWKV_GN_EPS)
    yn = yn * lg_ref[...] + lb_ref[...]
    ksum = ks0_ref[0].astype(F32) + ks1_ref[0].astype(F32)
    bonus = _head_sum(r_ref[0].astype(F32) * ksum * rk_ref[...], e) * v_ref[0].astype(F32)
    o_ref[0] = ((yn + bonus) * g_ref[0].astype(F32)).astype(o_ref.dtype)


def rwkv_mixer(u, mu, w_r, w_k, w_v, w_o, w0, w1, w2, a0, a1, a2, g1, g2, k_k, k_a, r_k, ln_g, ln_b):
    bsz, t, d = u.shape
    m = bsz * t
    dc = _pick(d, (256, 128))
    xs = pl.pallas_call(
        _rwkv_mix_body, grid=(bsz, d // dc),
        in_specs=[pl.BlockSpec((1, t, dc), lambda b, j: (b, 0, j)),
                  pl.BlockSpec((6, dc), lambda b, j: (0, j))],
        out_specs=[pl.BlockSpec((1, t, dc), lambda b, j: (b, 0, j))] * 6,
        out_shape=[jax.ShapeDtypeStruct((bsz, t, d), BF16)] * 6,
        compiler_params=_cparams("parallel", "parallel"),
    )(u, mu)
    xr, xw, xk, xv, xa, xg = (x.reshape(m, d) for x in xs)
    r = matmul(xr, w_r.astype(BF16), BF16).reshape(bsz, t, d)
    k = matmul(xk, w_k.astype(BF16)).reshape(bsz, t, d)
    v = matmul(xv, w_v.astype(BF16), BF16).reshape(bsz, t, d)
    rk = w1.shape[2]
    hw = matmul(xw, jnp.concatenate([w1[0], w1[1]], 1).astype(BF16)).reshape(bsz, t, 2 * rk)
    ha = matmul(xa, jnp.concatenate([a1[0], a1[1]], 1).astype(BF16)).reshape(bsz, t, 2 * rk)
    gr = g1.shape[1]
    grp = -(-gr // LANES) * LANES
    g1p = jnp.zeros((d, grp), F32).at[:, :gr].set(g1)
    g2p = jnp.zeros((grp, d), F32).at[:gr].set(g2)
    hg = jax.nn.sigmoid(matmul(xg, g1p.astype(BF16)))
    g = matmul(hg, g2p.astype(BF16), BF16).reshape(bsz, t, d)

    tr = _pick(t, (64, 32, 16))
    row = pl.BlockSpec((1, tr, d), lambda b, i: (b, i, 0))
    low = pl.BlockSpec((1, tr, 2 * rk), lambda b, i: (b, i, 0))
    vec = pl.BlockSpec((1, d), lambda b, i: (0, 0))
    vec2 = pl.BlockSpec((2, d), lambda b, i: (0, 0))
    fact = pl.BlockSpec((2, rk, d), lambda b, i: (0, 0, 0))
    big = jax.ShapeDtypeStruct((bsz, t, d), F32)
    half = jax.ShapeDtypeStruct((bsz, t, d), BF16)
    na, lw0, lw1, ks0, ks1, bb0, bb1 = pl.pallas_call(
        _rwkv_prep_body, grid=(bsz, t // tr),
        in_specs=[row, low, low, fact, fact, vec2, vec2, vec, vec],
        out_specs=[row] * 7, out_shape=[half, big, big, half, half, half, half],
        compiler_params=_cparams("parallel", "parallel"),
    )(k, hw, ha, w2.astype(BF16), a2.astype(BF16), w0, a0, k_k[None], k_a[None])

    y0, y1 = rwkv_scan(r, v, na, (lw0, lw1), (ks0, ks1), (bb0, bb1))

    out = pl.pallas_call(
        _rwkv_out_body, grid=(bsz, t // tr),
        in_specs=[row] * 7 + [vec, vec, vec], out_specs=row,
        out_shape=jax.ShapeDtypeStruct((bsz, t, d), BF16),
        compiler_params=_cparams("parallel", "parallel"),
    )(y0, y1, r, v, g, ks0, ks1, r_k.reshape(1, d), ln_g[None], ln_b[None])
    return matmul(out.reshape(m, d), w_o.astype(BF16)).reshape(bsz, t, d)


def _top_vals(s, kk):
    vals = []
    cur = s
    for _ in range(kk):
        mx = jnp.max(cur, axis=0, keepdims=True)
        vals.append(mx)
        cur = jnp.where(cur >= mx, NEG, cur)
    return jnp.concatenate(vals, axis=0)


def _peer_score_body(q_ref, keys_ref, thr_ref, s2_ref, e2_ref, cf_ref):
    nk = PEER_NKEYS
    kk = PEER_TOPK
    for h in range(PEER_HEADS):
        q1 = q_ref[h * 2 * nk:h * 2 * nk + nk, :].astype(BF16)
        q2 = q_ref[h * 2 * nk + nk:(h + 1) * 2 * nk, :].astype(BF16)
        s1 = _dot(keys_ref[0], q1)
        s2 = _dot(keys_ref[1], q2)
        a1 = _top_vals(s1, kk + 1)
        a2 = _top_vals(s2, kk + 1)
        cand = (a1[:kk, None, :] + a2[None, :kk, :]).reshape(kk * kk, -1)
        cv = _top_vals(cand, kk + 1)
        c17 = jnp.maximum(cv[kk:kk + 1], jnp.maximum(a1[kk:kk + 1] + a2[0:1], a1[0:1] + a2[kk:kk + 1]))
        theta = 0.5 * (cv[kk - 1:kk] + c17)
        zsum = jnp.sum(jnp.where(cand >= theta, jnp.exp(cand - cv[0:1]), 0.0), axis=0, keepdims=True)
        thr_ref[h] = theta - s1
        s2_ref[h] = s2
        e2_ref[h] = jnp.exp(s2 - a2[0:1])
        cf_ref[h] = jnp.exp(s1 - a1[0:1]) / zsum


def _gelu(x):
    return 0.5 * x * (1.0 + lax.erf(x * (2.0 ** -0.5)))


def _peer_dense_body(z_ref, u_ref, v_ref, thr_ref, s2_ref, e2_ref, cf_ref, o_ref, w_ref):
    nk = PEER_NKEYS
    eb = pl.program_id(1)
    te = u_ref.shape[0]
    n_i = te // nk
    strip = PEER_ROW_STRIP
    stages = PEER_STAGES

    @pl.when(eb == 0)
    def _():
        o_ref[...] = jnp.zeros_like(o_ref)

    sub = min(PEER_TOK_SUB, z_ref.shape[1])
    d = z_ref.shape[0]
    kc = d // stages

    def first_matmul(ts):
        acc = None
        for k0 in range(0, d, kc):
            part = _dot(u_ref[:, k0:k0 + kc], z_ref[k0:k0 + kc, ts])
            acc = part if acc is None else acc + part
            yield
        return acc

    def gates(ts):
        per_stage = (nk // strip) * PEER_HEADS // stages
        n = 0
        for si in range(nk // strip):
            js = slice(si * strip, (si + 1) * strip)
            w = [None] * n_i
            for h in range(PEER_HEADS):
                s2s = s2_ref[h, js, ts]
                e2s = e2_ref[h, js, ts]
                for ii in range(n_i):
                    i = eb * n_i + ii
                    c = jnp.where(s2s >= thr_ref[h, pl.ds(i, 1), ts], e2s, 0.0) * cf_ref[h, pl.ds(i, 1), ts]
                    w[ii] = c if w[ii] is None else w[ii] + c
                n += 1
                if n % per_stage == 0 and h + 1 < PEER_HEADS:
                    yield
            for ii in range(n_i):
                w_ref[ii * nk + si * strip:ii * nk + (si + 1) * strip, ts] = w[ii].astype(BF16)
            yield

    def activate(ts, act):
        rows = te // stages
        for r0 in range(0, te, rows):
            w_ref[r0:r0 + rows, ts] = w_ref[r0:r0 + rows, ts] * _gelu(act[r0:r0 + rows]).astype(BF16)
            yield

    def second_matmul(ts):
        rows = d // stages
        for r0 in range(0, d, rows):
            o_ref[r0:r0 + rows, ts] += _dot(v_ref[r0:r0 + rows, :], w_ref[:, ts])
            yield

    tiles = [slice(t0, t0 + sub) for t0 in range(0, z_ref.shape[1], sub)]
    acts = {}
    for ph in range(len(tiles) + 2):
        gens, tags = [], []
        if ph < len(tiles):
            gens += [first_matmul(tiles[ph]), gates(tiles[ph])]
            tags += [ph, None]
        if 0 <= ph - 1 < len(tiles):
            gens.append(activate(tiles[ph - 1], acts[ph - 1]))
            tags.append(None)
        if 0 <= ph - 2 < len(tiles):
            gens.append(second_matmul(tiles[ph - 2]))
            tags.append(None)
        for tag, res in zip(tags, _lockstep(gens)):
            if tag is not None:
                acts[tag] = res


def peer_ffn(zt, w_q, sub_keys, u_tab, v_tab):
    d, m = zt.shape
    nh, nk = PEER_HEADS, PEER_NKEYS
    qt = matmul(w_q.T.astype(BF16), zt, F32)
    tt = _pick(m, (256, 128))
    sh = jax.ShapeDtypeStruct((nh, nk, m), F32)
    blk = pl.BlockSpec((nh, nk, tt), lambda i: (0, 0, i))
    thr, s2, e2, cf = pl.pallas_call(
        _peer_score_body, grid=(m // tt,),
        in_specs=[pl.BlockSpec((nh * 2 * nk, tt), lambda i: (0, i)),
                  pl.BlockSpec((2, nk, PEER_DKEY // 2), lambda i: (0, 0, 0))],
        out_specs=[blk, blk, blk, blk], out_shape=[sh, sh, sh, sh],
        compiler_params=_cparams("parallel"),
    )(qt, sub_keys.astype(BF16))

    tm = _pick(m, (2 * PEER_TOK_SUB, PEER_TOK_SUB, LANES))
    te = PEER_EXP_BLOCK
    ne = u_tab.shape[0]
    once = pl.Buffered(1)
    sblk = pl.BlockSpec((nh, nk, tm), lambda i, e: (0, 0, i), pipeline_mode=once)
    return pl.pallas_call(
        _peer_dense_body, grid=(m // tm, ne // te),
        in_specs=[pl.BlockSpec((d, tm), lambda i, e: (0, i), pipeline_mode=once),
                  pl.BlockSpec((te, d), lambda i, e: (e, 0)),
                  pl.BlockSpec((d, te), lambda i, e: (0, e)),
                  sblk, sblk, sblk, sblk],
        out_specs=pl.BlockSpec((d, tm), lambda i, e: (0, i)),
        out_shape=jax.ShapeDtypeStruct((d, m), F32),
        scratch_shapes=[pltpu.VMEM((te, tm), BF16)],
        compiler_params=_cparams("parallel", "arbitrary"),
    )(zt, u_tab.astype(BF16), v_tab.T.astype(BF16), thr, s2, e2, cf)


def _ada_mods(c, c_ctx, w_down, w_up, b_up):
    bsz, d = c.shape
    cond = jnp.concatenate([c, c_ctx[None]], 0)
    pad = 16 - cond.shape[0] % 16
    cond = jnp.concatenate([cond, jnp.zeros((pad, d), F32)], 0)
    hid = matmul(jax.nn.silu(cond), w_down.astype(BF16), F32)
    m = (matmul(hid, w_up.astype(BF16), F32) + b_up)[:bsz + 1].reshape(bsz + 1, N_MOD, d)
    return jnp.stack([jnp.broadcast_to(m[bsz], (bsz, N_MOD, d)), m[:bsz]], axis=1)


def kernel(x, c, ctx, c_ctx, ada_w_down, ada_w_up, ada_b,
           gla_w_in, gla_w_g1, gla_w_g2, gla_b_g, gla_norm_g, gla_w_o,
           mla_w_in, mla_q_norm, mla_kv_norm, mla_w_uq, mla_w_ukv, mla_w_o,
           rwkv_mu, rwkv_w_r, rwkv_w_k, rwkv_w_v, rwkv_w_o, rwkv_w0, rwkv_w1, rwkv_w2,
           rwkv_a0, rwkv_a1, rwkv_a2, rwkv_g1, rwkv_g2, rwkv_k_k, rwkv_k_a, rwkv_r_k, rwkv_ln_g, rwkv_ln_b,
           peer_w_q, peer_sub_keys, peer_u, peer_v):
    z = jnp.concatenate([ctx, x], axis=1)
    bsz, t, d = z.shape
    mods = [_ada_mods(c, c_ctx, ada_w_down[i], ada_w_up[i], ada_b[i]) for i in range(DEPTH)]
    _, u = ln_mod(z, mods_n=mods[0], sidx=0)
    for i in range(DEPTH):
        j = i // N_MIXERS
        if i % N_MIXERS == 0:
            y = gla_mixer(u, gla_w_in[j], gla_w_g1[j], gla_w_g2[j], gla_b_g[j], gla_norm_g[j], gla_w_o[j])
        elif i % N_MIXERS == 1:
            y = mla_mixer(u, mla_w_in[j], mla_q_norm[j], mla_kv_norm[j], mla_w_uq[j], mla_w_ukv[j], mla_w_o[j])
        else:
            y = rwkv_mixer(u, rwkv_mu[j], rwkv_w_r[j], rwkv_w_k[j], rwkv_w_v[j], rwkv_w_o[j],
                           rwkv_w0[j], rwkv_w1[j], rwkv_w2[j], rwkv_a0[j], rwkv_a1[j], rwkv_a2[j],
                           rwkv_g1[j], rwkv_g2[j], rwkv_k_k[j], rwkv_k_a[j], rwkv_r_k[j],
                           rwkv_ln_g[j], rwkv_ln_b[j])
        z, ut = ln_mod(z, y, mods[i], 2, mods[i], 3, u_t=True)
        ht = peer_ffn(ut, peer_w_q[i], peer_sub_keys[i], peer_u[i], peer_v[i])
        if i + 1 < DEPTH:
            z, u = ln_mod(z, ht, mods[i], 5, mods[i + 1], 0, y_t=True)
        else:
            z, _ = ln_mod(z, ht, mods[i], 5, y_t=True)
    return z[:, CTX_LEN:]
```

```python
import functools

import jax
import jax.numpy as jnp
from jax import lax
from jax.experimental import pallas as pl
from jax.experimental.pallas import tpu as pltpu

F32 = jnp.float32
BF16 = jnp.bfloat16

DEPTH = 4
CTX_LEN = 256
GRID_W = 64
N_MIXERS = 3
N_MOD = 6
LN_EPS = 1e-6
DEEPNORM_ALPHA = (2.0 * DEPTH) ** 0.25

GLA_HEADS = 8
GLA_GATE_RANK = 16
GLA_GATE_NORMALIZER = 16.0

MLA_Q_RANK = 1536
MLA_KV_RANK = 512
MLA_NOPE = 128
MLA_ROPE = 64
MLA_V = 128
ROPE_THETA = 10000.0

RWKV_HEAD = 64
RWKV_GN_EPS = 64e-5
RWKV_DECAY_SCALE = 0.6065306597126334

PEER_HEADS = 8
PEER_NKEYS = 128
PEER_DKEY = 256
PEER_TOPK = 16

LANES = 128
SUBLANES = 8
VMEM_LIMIT = 52 * 1024 * 1024

GLA_CHUNK = 128
GLA_HEADS_PER_STEP = 2
GLA_SUB = 4
RWKV_CHUNK = 64
RWKV_SUB = 16
MLA_KEY_CHAINS = 2
LOG2E = 1.4426950408889634
PEER_TOK_SUB = 256
PEER_EXP_BLOCK = 512
PEER_STAGES = 4
PEER_ROW_STRIP = 32
NEG = -1e30


def _pick(n, cands):
    for c in cands:
        if n % c == 0:
            return c
    return n


def _cparams(*sem):
    return pltpu.CompilerParams(dimension_semantics=sem, vmem_limit_bytes=VMEM_LIMIT)


def _dot(a, b):
    return jnp.dot(a, b, preferred_element_type=F32)


def _dot_nt(a, b):
    return lax.dot_general(a, b, (((1,), (1,)), ((), ())), preferred_element_type=F32)


def _dot_tn(a, b):
    return lax.dot_general(a, b, (((0,), (0,)), ((), ())), preferred_element_type=F32)


def _split3(x):
    hi = x.astype(BF16)
    r1 = x - hi.astype(F32)
    mid = r1.astype(BF16)
    lo = (r1 - mid.astype(F32)).astype(BF16)
    return hi, mid, lo


def _dot_sel(sel, x):
    s = sel.astype(BF16)
    hi, mid, lo = _split3(x)
    return _dot(s, hi) + _dot(s, mid) + _dot(s, lo)


def _dot_x3(a, b, dims=None):
    ah = a.astype(BF16)
    al = (a - ah.astype(F32)).astype(BF16)
    bh = b.astype(BF16)
    bl = (b - bh.astype(F32)).astype(BF16)
    f = _dot if dims is None else dims
    return f(ah, bh) + f(ah, bl) + f(al, bh)


def _dot_b(a, b):
    return _dot(a.astype(BF16), b.astype(BF16))


def _log_sigmoid(x):
    return jnp.minimum(x, 0.0) - jnp.log1p(jnp.exp(-jnp.abs(x)))


def _sigmoid(x):
    return 1.0 / (1.0 + jnp.exp(-x))


def _mm_body(a_ref, b_ref, o_ref):
    o_ref[...] = _dot(a_ref[...].astype(BF16), b_ref[...].astype(BF16)).astype(o_ref.dtype)


def matmul(a, b, out_dtype=F32):
    m, k = a.shape
    k2, n = b.shape
    assert k == k2
    a_bytes = jnp.dtype(a.dtype).itemsize
    tm_cands = (1024, 512, 256, 128, 64, 32, 16) if a_bytes * k <= 8192 else (512, 256, 128, 64, 32, 16)
    tm = _pick(m, tm_cands)
    tn = _pick(n, (512, 256, 128))
    return pl.pallas_call(
        _mm_body,
        grid=(m // tm, n // tn),
        in_specs=[pl.BlockSpec((tm, k), lambda i, j: (i, 0)),
                  pl.BlockSpec((k, tn), lambda i, j: (0, j))],
        out_specs=pl.BlockSpec((tm, tn), lambda i, j: (i, j)),
        out_shape=jax.ShapeDtypeStruct((m, n), out_dtype),
        compiler_params=_cparams("parallel", "parallel"),
    )(a, b)


def _layer_norm(z):
    zc = z - jnp.mean(z, -1, keepdims=True)
    return zc * lax.rsqrt(jnp.mean(zc * zc, -1, keepdims=True) + LN_EPS)


def _ln_mod_body(*refs, gidx, sidx, has_y, y_t, u_t):
    it = iter(refs)
    z_ref = next(it)
    y_ref = next(it) if has_y else None
    mg_ref = next(it) if has_y else None
    mn_ref = next(it) if sidx is not None else None
    zo_ref = next(it) if has_y else None
    u_ref = next(it) if sidx is not None else None
    z = z_ref[0]
    if has_y:
        y = y_ref[...].T if y_t else y_ref[0]
        z = _layer_norm(DEEPNORM_ALPHA * z + mg_ref[0, 0, gidx:gidx + 1, :] * y)
        zo_ref[0] = z
    if sidx is not None:
        u = z * (1.0 + mn_ref[0, 0, sidx + 1:sidx + 2, :]) + mn_ref[0, 0, sidx:sidx + 1, :]
        if u_t:
            u_ref[...] = u.T.astype(u_ref.dtype)
        else:
            u_ref[0] = u.astype(u_ref.dtype)


def ln_mod(z, y=None, mods_g=None, gidx=None, mods_n=None, sidx=None, y_t=False, u_t=False):
    bsz, t, d = z.shape
    tr = _pick(CTX_LEN, (256, 128, 64, 32, 16))
    nt = t // tr
    nctx = CTX_LEN // tr
    has_y = y is not None
    row = pl.BlockSpec((1, tr, d), lambda b, i: (b, i, 0))
    col = pl.BlockSpec((d, tr), lambda b, i: (0, b * nt + i))
    mod = pl.BlockSpec((1, 1, N_MOD, d), lambda b, i: (b, jnp.where(i < nctx, 0, 1), 0, 0))
    ins, in_specs, outs, out_specs = [z], [row], [], []
    if has_y:
        ins += [y, mods_g]
        in_specs += [col if y_t else row, mod]
        outs.append(jax.ShapeDtypeStruct((bsz, t, d), F32))
        out_specs.append(row)
    if sidx is not None:
        ins.append(mods_n)
        in_specs.append(mod)
        outs.append(jax.ShapeDtypeStruct((d, bsz * t) if u_t else (bsz, t, d), BF16))
        out_specs.append(col if u_t else row)
    res = pl.pallas_call(
        functools.partial(_ln_mod_body, gidx=gidx, sidx=sidx, has_y=has_y, y_t=y_t, u_t=u_t),
        grid=(bsz, nt), in_specs=in_specs, out_specs=out_specs, out_shape=outs,
        compiler_params=_cparams("parallel", "parallel"),
    )(*ins)
    res = list(res)
    zo = res.pop(0) if has_y else None
    u = res.pop(0) if sidx is not None else None
    return zo, u


def _lockstep(gens):
    out = [None] * len(gens)
    live = list(range(len(gens)))
    while live:
        for i in list(live):
            try:
                next(gens[i])
            except StopIteration as done:
                out[i] = done.value
                live.remove(i)
    return out


def _gla_chunk(q, k, v, glow, wg2, bg, st, rev):
    c, dk = q.shape
    sub = GLA_SUB
    q = q.astype(F32) * (dk ** -0.5)
    k = k.astype(F32)
    gl = _dot_x3(glow, wg2) + bg
    yield
    g = _log_sigmoid(gl) * (1.0 / GLA_GATE_NORMALIZER)
    row = lax.broadcasted_iota(jnp.int32, (c, c), 0)
    col = lax.broadcasted_iota(jnp.int32, (c, c), 1)
    tri = (row <= col) if rev else (row >= col)
    b = _dot_sel(tri.astype(F32), g)
    yield
    btot = jnp.sum(g, axis=0, keepdims=True)

    o = _dot_nt((q * jnp.exp(b)).astype(BF16), st.astype(BF16))

    rowk = lax.broadcasted_iota(jnp.int32, (c, 1), 0)
    halves = []
    hsz = c // 2
    while hsz >= sub:
        halves.append(hsz)
        hsz //= 2
    pick = jnp.concatenate(
        [(col == (row // (2 * hf)) * (2 * hf) + (hf if rev else hf - 1)).astype(F32) for hf in halves], axis=0)
    refs = _dot_sel(pick, b)
    yield
    att = jnp.zeros((c, c), F32)
    for lv, hf in enumerate(halves):
        refb = refs[lv * c:(lv + 1) * c]
        late = (rowk % (2 * hf)) >= hf
        q_side, k_side = (~late, late) if rev else (late, ~late)
        qf = jnp.where(q_side, q * jnp.exp(jnp.minimum(b - refb, 0.0)), 0.0)
        kf = jnp.where(k_side, k * jnp.exp(jnp.minimum(refb - b, 0.0)), 0.0)
        same = (row // (2 * hf)) == (col // (2 * hf))
        att = att + jnp.where(same, _dot_nt(qf.astype(BF16), kf.astype(BF16)), 0.0)
    yield

    rmod = rowk % sub
    for lag in range(sub):
        sh = (c - lag) % c if rev else lag
        ks = pltpu.roll(k, sh, 0) if sh else k
        bs = pltpu.roll(b, sh, 0) if sh else b
        term = jnp.sum(q * ks * jnp.exp(jnp.minimum(b - bs, 0.0)), axis=1, keepdims=True)
        valid = (rmod + lag < sub) if rev else (rmod >= lag)
        hit = (col == row + lag) if rev else (col == row - lag)
        att = att + jnp.where(hit & valid, term, 0.0)

    o = o + _dot(att.astype(BF16), v)
    yield
    kd = (k * jnp.exp(btot - b)).astype(BF16)
    return o, st * jnp.exp(btot) + _dot_tn(v, kd)


def _gla_body(*refs):
    ins, (of_ref, ob_ref, st_ref) = refs[:12], refs[12:]

    @pl.when(pl.program_id(2) == 0)
    def _():
        st_ref[...] = jnp.zeros_like(st_ref)

    dv = st_ref.shape[2]
    dk = st_ref.shape[3]
    gens, dest = [], []
    for dr in range(2):
        q_ref, k_ref, v_ref, gl_ref, wg2_ref, bg_ref = ins[6 * dr:6 * dr + 6]
        for hh in range(st_ref.shape[1]):
            ks, vs = slice(hh * dk, (hh + 1) * dk), slice(hh * dv, (hh + 1) * dv)
            gens.append(_gla_chunk(q_ref[0, :, ks], k_ref[0, :, ks], v_ref[0, :, vs], gl_ref[0],
                                   wg2_ref[:, ks], bg_ref[:, ks], st_ref[dr, hh], rev=(dr == 1)))
            dest.append((dr, hh, vs))
    for (dr, hh, vs), (o, st) in zip(dest, _lockstep(gens)):
        (of_ref, ob_ref)[dr][0, :, vs] = o.astype(of_ref.dtype)
        st_ref[dr, hh] = st


def gla_scan(p, glow, wg2, bg):
    bsz, t, d3 = p.shape
    d = d3 // 3
    h = GLA_HEADS
    dk, dv = (d // 2) // h, d // h
    c = GLA_CHUNK
    nc, ncx = t // c, CTX_LEN // c

    def rchunk(s):
        return jnp.where(s < ncx, ncx - 1 - s, nc - 1 - (s - ncx))

    hp = _pick(h, (GLA_HEADS_PER_STEP, 1))
    hg = h // hp

    def specs(chunk):
        return [pl.BlockSpec((1, c, hp * dk), lambda b, hh, s: (b, chunk(s), hh)),
                pl.BlockSpec((1, c, hp * dk), lambda b, hh, s: (b, chunk(s), hg + hh)),
                pl.BlockSpec((1, c, hp * dv), lambda b, hh, s: (b, chunk(s), hg + hh)),
                pl.BlockSpec((1, c, LANES), lambda b, hh, s: (b, chunk(s), 0)),
                pl.BlockSpec((LANES, hp * dk), lambda b, hh, s: (0, hh)),
                pl.BlockSpec((1, hp * dk), lambda b, hh, s: (0, hh))]

    sh = jax.ShapeDtypeStruct((bsz, t, d), BF16)
    return pl.pallas_call(
        _gla_body,
        grid=(bsz, hg, nc),
        in_specs=specs(lambda s: s) + specs(rchunk),
        out_specs=[pl.BlockSpec((1, c, hp * dv), lambda b, hh, s: (b, s, hh)),
                   pl.BlockSpec((1, c, hp * dv), lambda b, hh, s: (b, rchunk(s), hh))],
        out_shape=[sh, sh],
        scratch_shapes=[pltpu.VMEM((2, hp, dv, dk), F32)],
        compiler_params=_cparams("parallel", "parallel", "arbitrary"),
    )(p, p, p, glow, wg2[0], bg[0], p, p, p, glow, wg2[1], bg[1])


def _gla_gate_body(of_ref, ob_ref, r_ref, g_ref, o_ref):
    dv = g_ref.shape[1]
    o = of_ref[0].astype(F32) + ob_ref[0].astype(F32)
    r = r_ref[0].astype(F32)
    outs = []
    for hh in range(o.shape[1] // dv):
        seg = o[:, hh * dv:(hh + 1) * dv]
        outs.append(seg * lax.rsqrt(jnp.mean(seg * seg, -1, keepdims=True) + 1e-6) * g_ref[...])
    o_ref[0] = (jnp.concatenate(outs, axis=1) * (r * _sigmoid(r))).astype(o_ref.dtype)


def gla_gate(o_f, o_b, p, norm_g):
    bsz, t, d = o_f.shape
    tr = _pick(t, (256, 128, 64, 32, 16))
    row = pl.BlockSpec((1, tr, d), lambda b, i: (b, i, 0))
    return pl.pallas_call(
        _gla_gate_body, grid=(bsz, t // tr),
        in_specs=[row, row, pl.BlockSpec((1, tr, d), lambda b, i: (b, i, 2)),
                  pl.BlockSpec((1, norm_g.shape[1]), lambda b, i: (0, 0))],
        out_specs=row, out_shape=jax.ShapeDtypeStruct((bsz, t, d), BF16),
        compiler_params=_cparams("parallel", "parallel"),
    )(o_f, o_b, p, norm_g)


def gla_mixer(u, w_in, w_g1, w_g2, b_g, norm_g, w_o):
    bsz, t, d = u.shape
    qk = d // 2
    u2 = u.reshape(bsz * t, d)
    p = matmul(u2, w_in.astype(BF16), BF16).reshape(bsz, t, 3 * d)
    r = GLA_GATE_RANK
    wg1 = jnp.zeros((d, LANES), F32).at[:, :r].set(w_g1[0]).at[:, r:2 * r].set(w_g1[1])
    glow = matmul(u2, wg1.astype(BF16), F32).reshape(bsz, t, LANES)
    wg2 = [jnp.zeros((LANES, qk), F32).at[s * r:(s + 1) * r].set(w_g2[s]) for s in range(2)]
    outs = gla_scan(p, glow, wg2, [b_g[0][None], b_g[1][None]])
    gated = gla_gate(outs[0], outs[1], p, norm_g[None])
    return matmul(gated.reshape(bsz * t, d), w_o.astype(BF16), BF16).reshape(bsz, t, d)


def _rms(x, gain):
    return x * lax.rsqrt(jnp.mean(x * x, -1, keepdims=True) + 1e-6) * gain


def _rope128(a, c, s):
    return a * c + pltpu.roll(a, LANES // 2, 1) * s


def _mla_norm_body(h_ref, qg_ref, kg_ref, c_ref, s_ref, cq_ref, ckv_ref, kr_ref):
    qr, kvr = cq_ref.shape[2], ckv_ref.shape[2]
    h = h_ref[0]
    cq_ref[0] = _rms(h[:, :qr], qg_ref[...]).astype(BF16)
    ckv_ref[0] = _rms(h[:, qr:qr + kvr], kg_ref[...]).astype(BF16)
    kr_ref[0] = _rope128(h[:, qr + kvr:], c_ref[...], s_ref[...]).astype(BF16)


def _mla_attn_body(q_ref, kn_ref, v_ref, kr_ref, c_ref, s_ref, o_ref, kf_ref, vf_ref, *, nctx_tiles, scale):
    qt = pl.program_id(2)
    t = kf_ref.shape[0]
    ctx = nctx_tiles * q_ref.shape[1]

    @pl.when(qt == 0)
    def _():
        kf_ref[:, :LANES] = kn_ref[0]
        kf_ref[:, LANES:] = kr_ref[0]
        vf_ref[:, :LANES] = v_ref[0]
        vf_ref[:, LANES:] = jnp.ones((t, LANES), BF16)

    q = q_ref[0].astype(F32)
    qr = _rope128(q[:, LANES:], c_ref[...], s_ref[...])
    qf = (jnp.concatenate([q[:, :LANES], qr], axis=1) * (scale * LOG2E)).astype(BF16)

    def attend(keys):
        s = _dot_nt(qf, kf_ref[keys])
        yield
        m = jnp.max(s, -1, keepdims=True)
        p = jnp.exp2(s - m).astype(BF16)
        return m, _dot(p, vf_ref[keys])

    def finish(parts):
        m = functools.reduce(jnp.maximum, [pm for pm, _ in parts])
        acc = sum(jnp.exp2(pm - m) * po for pm, po in parts)
        o_ref[0] = (acc[:, :LANES] / acc[:, LANES:LANES + 1]).astype(o_ref.dtype)

    @pl.when(qt < nctx_tiles)
    def _():
        finish(_lockstep([attend(slice(0, ctx))]))

    @pl.when(qt >= nctx_tiles)
    def _():
        step = t // MLA_KEY_CHAINS
        finish(_lockstep([attend(slice(k0, k0 + step)) for k0 in range(0, t, step)]))


def _rope_tables(t):
    rows = (t - CTX_LEN) // GRID_W
    row = jnp.repeat(jnp.arange(rows, dtype=F32), GRID_W)
    colp = jnp.tile(jnp.arange(GRID_W, dtype=F32), rows)
    n_freq = MLA_ROPE // 4
    inv = ROPE_THETA ** (-jnp.arange(n_freq, dtype=F32) / n_freq)
    ang = jnp.concatenate([row[:, None] * inv, colp[:, None] * inv], -1)
    cos = jnp.concatenate([jnp.ones((CTX_LEN, MLA_ROPE // 2), F32), jnp.cos(ang)], 0)
    sin = jnp.concatenate([jnp.zeros((CTX_LEN, MLA_ROPE // 2), F32), jnp.sin(ang)], 0)
    z = jnp.zeros_like(cos)
    return jnp.concatenate([cos, cos, z, z], 1), jnp.concatenate([-sin, sin, z, z], 1)


def mla_mixer(u, w_in, q_norm, kv_norm, w_uq, w_ukv, w_o):
    bsz, t, d = u.shape
    nh = d // 128
    qr, kvr, rp = MLA_Q_RANK, MLA_KV_RANK, MLA_ROPE
    ev, od = jnp.arange(0, rp, 2), jnp.arange(1, rp, 2)
    perm = jnp.concatenate([ev, od, od, ev])
    w_in_p = jnp.concatenate([w_in[:, :qr + kvr], w_in[:, qr + kvr + perm]], axis=1)
    hw = qr + kvr + LANES
    h = matmul(u.reshape(bsz * t, d), w_in_p.astype(BF16), F32).reshape(bsz, t, hw)
    ctab, stab = _rope_tables(t)

    tr = _pick(t, (256, 128, 64, 32, 16))
    cq, ckv, kr = pl.pallas_call(
        _mla_norm_body, grid=(bsz, t // tr),
        in_specs=[pl.BlockSpec((1, tr, hw), lambda b, i: (b, i, 0)),
                  pl.BlockSpec((1, qr), lambda b, i: (0, 0)),
                  pl.BlockSpec((1, kvr), lambda b, i: (0, 0)),
                  pl.BlockSpec((tr, LANES), lambda b, i: (i, 0)),
                  pl.BlockSpec((tr, LANES), lambda b, i: (i, 0))],
        out_specs=[pl.BlockSpec((1, tr, qr), lambda b, i: (b, i, 0)),
                   pl.BlockSpec((1, tr, kvr), lambda b, i: (b, i, 0)),
                   pl.BlockSpec((1, tr, LANES), lambda b, i: (b, i, 0))],
        out_shape=[jax.ShapeDtypeStruct((bsz, t, qr), BF16),
                   jax.ShapeDtypeStruct((bsz, t, kvr), BF16),
                   jax.ShapeDtypeStruct((bsz, t, LANES), BF16)],
        compiler_params=_cparams("parallel", "parallel"),
    )(h, q_norm[None], kv_norm[None], ctab, stab)

    hq = MLA_NOPE + rp
    qcols = (jnp.arange(nh)[:, None] * hq
             + jnp.concatenate([jnp.arange(MLA_NOPE), MLA_NOPE + perm])[None, :]).reshape(-1)
    q = matmul(cq.reshape(bsz * t, qr), w_uq[:, qcols].astype(BF16), BF16).reshape(bsz, t, nh * 2 * LANES)
    kv = matmul(ckv.reshape(bsz * t, kvr), w_ukv.astype(BF16), BF16).reshape(bsz, t, nh * 2 * LANES)

    tq = _pick(CTX_LEN, (256, 128, 64, 32, 16))
    scale = (MLA_NOPE + rp) ** -0.5
    o = pl.pallas_call(
        functools.partial(_mla_attn_body, nctx_tiles=CTX_LEN // tq, scale=scale),
        grid=(bsz, nh, t // tq),
        in_specs=[pl.BlockSpec((1, tq, 2 * LANES), lambda b, hh, i: (b, i, hh)),
                  pl.BlockSpec((1, t, LANES), lambda b, hh, i: (b, 0, 2 * hh)),
                  pl.BlockSpec((1, t, LANES), lambda b, hh, i: (b, 0, 2 * hh + 1)),
                  pl.BlockSpec((1, t, LANES), lambda b, hh, i: (b, 0, 0)),
                  pl.BlockSpec((tq, LANES), lambda b, hh, i: (i, 0)),
                  pl.BlockSpec((tq, LANES), lambda b, hh, i: (i, 0))],
        out_specs=pl.BlockSpec((1, tq, LANES), lambda b, hh, i: (b, i, hh)),
        out_shape=jax.ShapeDtypeStruct((bsz, t, nh * MLA_V), BF16),
        scratch_shapes=[pltpu.VMEM((t, 2 * LANES), BF16), pltpu.VMEM((t, 2 * LANES), BF16)],
        compiler_params=_cparams("parallel", "parallel", "arbitrary"),
    )(q, kv, kv, kr, ctab, stab)
    return matmul(o.reshape(bsz * t, nh * MLA_V), w_o.astype(BF16), BF16).reshape(bsz, t, d)


def _rwkv_mix_body(u_ref, mu_ref, *o_refs):
    t = u_ref.shape[1]
    u = u_ref[0].astype(F32)
    row = lax.broadcasted_iota(jnp.int32, (t, 1), 0)
    prev = jnp.where((row == 0) | (row == CTX_LEN), 0.0, pltpu.roll(u, 1, 0))
    nxt = jnp.where((row == CTX_LEN - 1) | (row == t - 1), 0.0, pltpu.roll(u, t - 1, 0))
    xx = 0.5 * (prev + nxt) - u
    for n, o_ref in enumerate(o_refs):
        o_ref[0] = (u + xx * mu_ref[n:n + 1, :]).astype(o_ref.dtype)


def _head_sum(x, e):
    eb = e.astype(BF16)
    outs = []
    for s in range(x.shape[1] // LANES):
        hi, mid, lo = _split3(x[:, s * LANES:(s + 1) * LANES])
        outs.append(_dot(hi, eb) + _dot(mid, eb) + _dot(lo, eb))
    return jnp.concatenate(outs, axis=1)


def _head_ones():
    r = lax.broadcasted_iota(jnp.int32, (LANES, LANES), 0) // RWKV_HEAD
    c = lax.broadcasted_iota(jnp.int32, (LANES, LANES), 1) // RWKV_HEAD
    return (r == c).astype(F32)


def _rwkv_prep_body(k_ref, hw_ref, ha_ref, w2_ref, a2_ref, w0_ref, a0_ref, kk_ref, ka_ref,
                    na_ref, lw0_ref, lw1_ref, ks0_ref, ks1_ref, bb0_ref, bb1_ref):
    rk = w2_ref.shape[1]
    k = k_ref[0]
    kk = k * kk_ref[...]
    kk = kk * lax.rsqrt(_head_sum(kk * kk, _head_ones()) + 1e-12)
    na_ref[0] = (-kk).astype(na_ref.dtype)
    hw = jnp.tanh(hw_ref[0])
    ha = ha_ref[0]
    for s, (lw_ref, ks_ref, bb_ref) in enumerate(((lw0_ref, ks0_ref, bb0_ref), (lw1_ref, ks1_ref, bb1_ref))):
        wl = w0_ref[s:s + 1, :] + _dot(hw[:, s * rk:(s + 1) * rk].astype(BF16), w2_ref[s])
        lw_ref[0] = (-RWKV_DECAY_SCALE) * _sigmoid(wl)
        a = _sigmoid(a0_ref[s:s + 1, :] + _dot(ha[:, s * rk:(s + 1) * rk].astype(BF16), a2_ref[s]))
        ks_ref[0] = (k * (1.0 + (a - 1.0) * ka_ref[...])).astype(ks_ref.dtype)
        bb_ref[0] = (kk * a).astype(bb_ref.dtype)


def _rwkv_chunk(r, lw, ks, v, na, bb, ht, rev):
    c = r.shape[0]
    n = RWKV_HEAD
    c2 = 2 * c
    r, ks, v, na, bb = (x.astype(F32) for x in (r, ks, v, na, bb))
    ri = lax.broadcasted_iota(jnp.int32, (c, c), 0)
    ci = lax.broadcasted_iota(jnp.int32, (c, c), 1)
    cs = _dot_sel(((ri <= ci) if rev else (ri >= ci)).astype(F32), lw)
    yield
    cm = cs - lw
    ctot = jnp.sum(lw, axis=0, keepdims=True)

    lane_a = lax.broadcasted_iota(jnp.int32, (1, LANES), 1) < n

    def stack(x):
        return jnp.concatenate([jnp.where(lane_a, x, 0.0), jnp.where(lane_a, 0.0, x)], axis=0)

    at = stack(na * jnp.exp(cm))
    rt = stack(r * jnp.exp(cs))
    ecs = jnp.exp(-cs)
    bh = stack(bb * ecs)
    kh = stack(ks * ecs)
    ece = jnp.exp(ctot - cs)
    be = stack(bb * ece)
    ke = stack(ks * ece)
    vs = stack(v)

    rr = lax.broadcasted_iota(jnp.int32, (c2, c2), 0)
    cc = lax.broadcasted_iota(jnp.int32, (c2, c2), 1)
    same = (rr // c) == (cc // c)
    tr_, tc_ = rr % c, cc % c
    strict = same & ((tc_ > tr_) if rev else (tc_ < tr_))
    incl = same & ((tc_ >= tr_) if rev else (tc_ <= tr_))
    diag_blk = same & ((tr_ // RWKV_SUB) == (tc_ // RWKV_SUB))

    lhs = jnp.concatenate([at, rt], axis=0)
    rhs = jnp.concatenate([bh, kh], axis=0)
    a4 = _dot_nt(lhs.astype(BF16), rhs.astype(BF16))
    yield
    nmat = jnp.where(strict, a4[:c2, :c2], 0.0)
    aak = jnp.where(strict, a4[:c2, c2:], 0.0)
    arb = jnp.where(incl, a4[c2:, :c2], 0.0)
    ark = jnp.where(incl, a4[c2:, c2:], 0.0)

    hb = ht.T.astype(BF16)
    vb = vs.astype(BF16)
    x0 = _dot(jnp.concatenate([at, aak], axis=1).astype(BF16), jnp.concatenate([hb, vb], axis=0))
    yield

    nd = jnp.where(diag_blk, nmat, 0.0)
    no = nmat - nd
    tm = nd
    pw = nd
    for _ in range(3):
        pw = _dot_b(pw, pw)
        yield
        tm = tm + pw + _dot_b(pw, tm)
        yield
    zu = _dot_b(tm, jnp.concatenate([no, x0], axis=1))
    z = no + zu[:, :c2]
    u = x0 + zu[:, c2:]
    yield
    zz = _dot_b(z, jnp.concatenate([z, u], axis=1))
    u = u + zz[:, c2:]
    yield
    u = u + _dot_b(zz[:, :c2], u)
    yield

    ub = u.astype(BF16)
    ys = _dot(jnp.concatenate([rt, arb, ark], axis=1).astype(BF16), jnp.concatenate([hb, ub, vb], axis=0))
    yield
    hn = ht * jnp.exp(ctot) + _dot_tn(jnp.concatenate([ub, vb], axis=0),
                                      jnp.concatenate([be, ke], axis=0).astype(BF16))
    hr = lax.broadcasted_iota(jnp.int32, (LANES, LANES), 0) // n
    hc = lax.broadcasted_iota(jnp.int32, (LANES, LANES), 1) // n
    return ys[:c] + ys[c:], jnp.where(hr == hc, hn, 0.0)


def _rwkv_scan_body(*refs):
    ins, (y0_ref, y1_ref, st_ref) = refs[:12], refs[12:]

    @pl.when(pl.program_id(2) == 0)
    def _():
        st_ref[...] = jnp.zeros_like(st_ref)

    gens, dest = [], []
    for dr in range(2):
        r_ref, lw_ref, ks_ref, v_ref, na_ref, bb_ref = ins[6 * dr:6 * dr + 6]
        for p in range(r_ref.shape[2] // LANES):
            sl = slice(p * LANES, (p + 1) * LANES)
            gens.append(_rwkv_chunk(r_ref[0, :, sl], lw_ref[0, :, sl], ks_ref[0, :, sl], v_ref[0, :, sl],
                                    na_ref[0, :, sl], bb_ref[0, :, sl], st_ref[dr, p], rev=(dr == 1)))
            dest.append(((y0_ref, y1_ref)[dr], sl, dr, p))
    for (y_ref, sl, dr, p), (y, hn) in zip(dest, _lockstep(gens)):
        y_ref[0, :, sl] = y
        st_ref[dr, p] = hn


def rwkv_scan(r, v, na, lw, ks, bb):
    bsz, t, d = r.shape
    c = RWKV_CHUNK
    nc, ncx = t // c, CTX_LEN // c
    pp = _pick(d // LANES, (4, 2, 1))
    w = pp * LANES

    def rchunk(s):
        return jnp.where(s < ncx, ncx - 1 - s, nc - 1 - (s - ncx))

    fwd = pl.BlockSpec((1, c, w), lambda b, p, s: (b, s, p))
    bwd = pl.BlockSpec((1, c, w), lambda b, p, s: (b, rchunk(s), p))
    sh = jax.ShapeDtypeStruct((bsz, t, d), F32)
    return pl.pallas_call(
        _rwkv_scan_body,
        grid=(bsz, d // w, nc),
        in_specs=[fwd] * 6 + [bwd] * 6, out_specs=[fwd, bwd], out_shape=[sh, sh],
        scratch_shapes=[pltpu.VMEM((2, pp, LANES, LANES), F32)],
        compiler_params=_cparams("parallel", "parallel", "arbitrary"),
    )(r, lw[0], ks[0], v, na, bb[0], r, lw[1], ks[1], v, na, bb[1])


def _rwkv_out_body(y0_ref, y1_ref, r_ref, v_ref, g_ref, ks0_ref, ks1_ref, rk_ref, lg_ref, lb_ref, o_ref):
    e = _head_ones()
    y = y0_ref[0] + y1_ref[0]
    inv_n = 1.0 / RWKV_HEAD
    yc = y - _head_sum(y, e) * inv_n
    yn = yc * lax.rsqrt(_head_sum(yc * yc, e) * inv_n + RWKV_GN_EPS)
    yn = yn * lg_ref[...] + lb_ref[...]
    ksum = ks0_ref[0].astype(F32) + ks1_ref[0].astype(F32)
    bonus = _head_sum(r_ref[0].astype(F32) * ksum * rk_ref[...], e) * v_ref[0].astype(F32)
    o_ref[0] = ((yn + bonus) * g_ref[0].astype(F32)).astype(o_ref.dtype)


def rwkv_mixer(u, mu, w_r, w_k, w_v, w_o, w0, w1, w2, a0, a1, a2, g1, g2, k_k, k_a, r_k, ln_g, ln_b):
    bsz, t, d = u.shape
    m = bsz * t
    dc = _pick(d, (256, 128))
    xs = pl.pallas_call(
        _rwkv_mix_body, grid=(bsz, d // dc),
        in_specs=[pl.BlockSpec((1, t, dc), lambda b, j: (b, 0, j)),
                  pl.BlockSpec((6, dc), lambda b, j: (0, j))],
        out_specs=[pl.BlockSpec((1, t, dc), lambda b, j: (b, 0, j))] * 6,
        out_shape=[jax.ShapeDtypeStruct((bsz, t, d), BF16)] * 6,
        compiler_params=_cparams("parallel", "parallel"),
    )(u, mu)
    xr, xw, xk, xv, xa, xg = (x.reshape(m, d) for x in xs)
    r = matmul(xr, w_r.astype(BF16), BF16).reshape(bsz, t, d)
    k = matmul(xk, w_k.astype(BF16)).reshape(bsz, t, d)
    v = matmul(xv, w_v.astype(BF16), BF16).reshape(bsz, t, d)
    rk = w1.shape[2]
    hw = matmul(xw, jnp.concatenate([w1[0], w1[1]], 1).astype(BF16)).reshape(bsz, t, 2 * rk)
    ha = matmul(xa, jnp.concatenate([a1[0], a1[1]], 1).astype(BF16)).reshape(bsz, t, 2 * rk)
    gr = g1.shape[1]
    grp = -(-gr // LANES) * LANES
    g1p = jnp.zeros((d, grp), F32).at[:, :gr].set(g1)
    g2p = jnp.zeros((grp, d), F32).at[:gr].set(g2)
    hg = jax.nn.sigmoid(matmul(xg, g1p.astype(BF16)))
    g = matmul(hg, g2p.astype(BF16), BF16).reshape(bsz, t, d)

    tr = _pick(t, (64, 32, 16))
    row = pl.BlockSpec((1, tr, d), lambda b, i: (b, i, 0))
    low = pl.BlockSpec((1, tr, 2 * rk), lambda b, i: (b, i, 0))
    vec = pl.BlockSpec((1, d), lambda b, i: (0, 0))
    vec2 = pl.BlockSpec((2, d), lambda b, i: (0, 0))
    fact = pl.BlockSpec((2, rk, d), lambda b, i: (0, 0, 0))
    big = jax.ShapeDtypeStruct((bsz, t, d), F32)
    half = jax.ShapeDtypeStruct((bsz, t, d), BF16)
    na, lw0, lw1, ks0, ks1, bb0, bb1 = pl.pallas_call(
        _rwkv_prep_body, grid=(bsz, t // tr),
        in_specs=[row, low, low, fact, fact, vec2, vec2, vec, vec],
        out_specs=[row] * 7, out_shape=[half, big, big, half, half, half, half],
        compiler_params=_cparams("parallel", "parallel"),
    )(k, hw, ha, w2.astype(BF16), a2.astype(BF16), w0, a0, k_k[None], k_a[None])

    y0, y1 = rwkv_scan(r, v, na, (lw0, lw1), (ks0, ks1), (bb0, bb1))

    out = pl.pallas_call(
        _rwkv_out_body, grid=(bsz, t // tr),
        in_specs=[row] * 7 + [vec, vec, vec], out_specs=row,
        out_shape=jax.ShapeDtypeStruct((bsz, t, d), BF16),
        compiler_params=_cparams("parallel", "parallel"),
    )(y0, y1, r, v, g, ks0, ks1, r_k.reshape(1, d), ln_g[None], ln_b[None])
    return matmul(out.reshape(m, d), w_o.astype(BF16), BF16).reshape(bsz, t, d)


def _top_vals(s, kk):
    vals = []
    cur = s
    for _ in range(kk):
        mx = jnp.max(cur, axis=0, keepdims=True)
        vals.append(mx)
        cur = jnp.where(cur >= mx, NEG, cur)
    return jnp.concatenate(vals, axis=0)


def _peer_score_body(q_ref, keys_ref, thr_ref, s2_ref, e2_ref, cf_ref):
    nk = PEER_NKEYS
    kk = PEER_TOPK
    for h in range(PEER_HEADS):
        q1 = q_ref[h * 2 * nk:h * 2 * nk + nk, :].astype(BF16)
        q2 = q_ref[h * 2 * nk + nk:(h + 1) * 2 * nk, :].astype(BF16)
        s1 = _dot(keys_ref[0], q1)
        s2 = _dot(keys_ref[1], q2)
        a1 = _top_vals(s1, kk + 1)
        a2 = _top_vals(s2, kk + 1)
        cand = (a1[:kk, None, :] + a2[None, :kk, :]).reshape(kk * kk, -1)
        cv = _top_vals(cand, kk + 1)
        c17 = jnp.maximum(cv[kk:kk + 1], jnp.maximum(a1[kk:kk + 1] + a2[0:1], a1[0:1] + a2[kk:kk + 1]))
        theta = 0.5 * (cv[kk - 1:kk] + c17)
        zsum = jnp.sum(jnp.where(cand >= theta, jnp.exp(cand - cv[0:1]), 0.0), axis=0, keepdims=True)
        thr_ref[h] = theta - s1
        s2_ref[h] = s2
        e2_ref[h] = jnp.exp(s2 - a2[0:1])
        cf_ref[h] = jnp.exp(s1 - a1[0:1]) / zsum


def _gelu(x):
    return 0.5 * x * (1.0 + lax.erf(x * (2.0 ** -0.5)))


def _peer_dense_body(z_ref, u_ref, v_ref, thr_ref, s2_ref, e2_ref, cf_ref, o_ref, w_ref):
    nk = PEER_NKEYS
    eb = pl.program_id(1)
    te = u_ref.shape[0]
    n_i = te // nk
    strip = PEER_ROW_STRIP
    stages = PEER_STAGES

    @pl.when(eb == 0)
    def _():
        o_ref[...] = jnp.zeros_like(o_ref)

    sub = min(PEER_TOK_SUB, z_ref.shape[1])
    d = z_ref.shape[0]
    kc = d // stages

    def first_matmul(ts):
        acc = None
        for k0 in range(0, d, kc):
            part = _dot(u_ref[:, k0:k0 + kc], z_ref[k0:k0 + kc, ts])
            acc = part if acc is None else acc + part
            yield
        return acc

    def gates(ts):
        per_stage = (nk // strip) * PEER_HEADS // stages
        n = 0
        for si in range(nk // strip):
            js = slice(si * strip, (si + 1) * strip)
            w = [None] * n_i
            for h in range(PEER_HEADS):
                s2s = s2_ref[h, js, ts]
                e2s = e2_ref[h, js, ts]
                for ii in range(n_i):
                    i = eb * n_i + ii
                    c = jnp.where(s2s >= thr_ref[h, pl.ds(i, 1), ts], e2s, 0.0) * cf_ref[h, pl.ds(i, 1), ts]
                    w[ii] = c if w[ii] is None else w[ii] + c
                n += 1
                if n % per_stage == 0 and h + 1 < PEER_HEADS:
                    yield
            for ii in range(n_i):
                w_ref[ii * nk + si * strip:ii * nk + (si + 1) * strip, ts] = w[ii].astype(BF16)
            yield

    def activate(ts, act):
        rows = te // stages
        for r0 in range(0, te, rows):
            w_ref[r0:r0 + rows, ts] = w_ref[r0:r0 + rows, ts] * _gelu(act[r0:r0 + rows]).astype(BF16)
            yield

    def second_matmul(ts):
        rows = d // stages
        for r0 in range(0, d, rows):
            o_ref[r0:r0 + rows, ts] += _dot(v_ref[r0:r0 + rows, :], w_ref[:, ts])
            yield

    tiles = [slice(t0, t0 + sub) for t0 in range(0, z_ref.shape[1], sub)]
    acts = {}
    for ph in range(len(tiles) + 2):
        gens, tags = [], []
        if ph < len(tiles):
            gens += [first_matmul(tiles[ph]), gates(tiles[ph])]
            tags += [ph, None]
        if 0 <= ph - 1 < len(tiles):
            gens.append(activate(tiles[ph - 1], acts[ph - 1]))
            tags.append(None)
        if 0 <= ph - 2 < len(tiles):
            gens.append(second_matmul(tiles[ph - 2]))
            tags.append(None)
        for tag, res in zip(tags, _lockstep(gens)):
            if tag is not None:
                acts[tag] = res


def peer_ffn(zt, w_q, sub_keys, u_tab, v_tab):
    d, m = zt.shape
    nh, nk = PEER_HEADS, PEER_NKEYS
    qt = matmul(w_q.T.astype(BF16), zt, F32)
    tt = _pick(m, (256, 128))
    sh = jax.ShapeDtypeStruct((nh, nk, m), F32)
    blk = pl.BlockSpec((nh, nk, tt), lambda i: (0, 0, i))
    thr, s2, e2, cf = pl.pallas_call(
        _peer_score_body, grid=(m // tt,),
        in_specs=[pl.BlockSpec((nh * 2 * nk, tt), lambda i: (0, i)),
                  pl.BlockSpec((2, nk, PEER_DKEY // 2), lambda i: (0, 0, 0))],
        out_specs=[blk, blk, blk, blk], out_shape=[sh, sh, sh, sh],
        compiler_params=_cparams("parallel"),
    )(qt, sub_keys.astype(BF16))

    tm = _pick(m, (2 * PEER_TOK_SUB, PEER_TOK_SUB, LANES))
    te = PEER_EXP_BLOCK
    ne = u_tab.shape[0]
    once = pl.Buffered(1)
    sblk = pl.BlockSpec((nh, nk, tm), lambda i, e: (0, 0, i), pipeline_mode=once)
    return pl.pallas_call(
        _peer_dense_body, grid=(m // tm, ne // te),
        in_specs=[pl.BlockSpec((d, tm), lambda i, e: (0, i), pipeline_mode=once),
                  pl.BlockSpec((te, d), lambda i, e: (e, 0)),
                  pl.BlockSpec((d, te), lambda i, e: (0, e)),
                  sblk, sblk, sblk, sblk],
        out_specs=pl.BlockSpec((d, tm), lambda i, e: (0, i)),
        out_shape=jax.ShapeDtypeStruct((d, m), F32),
        scratch_shapes=[pltpu.VMEM((te, tm), BF16)],
        compiler_params=_cparams("parallel", "arbitrary"),
    )(zt, u_tab.astype(BF16), v_tab.T.astype(BF16), thr, s2, e2, cf)


def _ada_mods(c, c_ctx, w_down, w_up, b_up):
    bsz, d = c.shape
    cond = jnp.concatenate([c, c_ctx[None]], 0)
    pad = 16 - cond.shape[0] % 16
    cond = jnp.concatenate([cond, jnp.zeros((pad, d), F32)], 0)
    hid = matmul(jax.nn.silu(cond), w_down.astype(BF16), F32)
    m = (matmul(hid, w_up.astype(BF16), F32) + b_up)[:bsz + 1].reshape(bsz + 1, N_MOD, d)
    return jnp.stack([jnp.broadcast_to(m[bsz], (bsz, N_MOD, d)), m[:bsz]], axis=1)


def kernel(x, c, ctx, c_ctx, ada_w_down, ada_w_up, ada_b,
           gla_w_in, gla_w_g1, gla_w_g2, gla_b_g, gla_norm_g, gla_w_o,
           mla_w_in, mla_q_norm, mla_kv_norm, mla_w_uq, mla_w_ukv, mla_w_o,
           rwkv_mu, rwkv_w_r, rwkv_w_k, rwkv_w_v, rwkv_w_o, rwkv_w0, rwkv_w1, rwkv_w2,
           rwkv_a0, rwkv_a1, rwkv_a2, rwkv_g1, rwkv_g2, rwkv_k_k, rwkv_k_a, rwkv_r_k, rwkv_ln_g, rwkv_ln_b,
           peer_w_q, peer_sub_keys, peer_u, peer_v):
    z = jnp.concatenate([ctx, x], axis=1)
    bsz, t, d = z.shape
    mods = [_ada_mods(c, c_ctx, ada_w_down[i], ada_w_up[i], ada_b[i]) for i in range(DEPTH)]
    _, u = ln_mod(z, mods_n=mods[0], sidx=0)
    for i in range(DEPTH):
        j = i // N_MIXERS
        if i % N_MIXERS == 0:
            y = gla_mixer(u, gla_w_in[j], gla_w_g1[j], gla_w_g2[j], gla_b_g[j], gla_norm_g[j], gla_w_o[j])
        elif i % N_MIXERS == 1:
            y = mla_mixer(u, mla_w_in[j], mla_q_norm[j], mla_kv_norm[j], mla_w_uq[j], mla_w_ukv[j], mla_w_o[j])
        else:
            y = rwkv_mixer(u, rwkv_mu[j], rwkv_w_r[j], rwkv_w_k[j], rwkv_w_v[j], rwkv_w_o[j],
                           rwkv_w0[j], rwkv_w1[j], rwkv_w2[j], rwkv_a0[j], rwkv_a1[j], rwkv_a2[j],
                           rwkv_g1[j], rwkv_g2[j], rwkv_k_k[j], rwkv_k_a[j], rwkv_r_k[j],
                           rwkv_ln_g[j], rwkv_ln_b[j])
        z, ut = ln_mod(z, y, mods[i], 2, mods[i], 3, u_t=True)
        ht = peer_ffn(ut, peer_w_q[i], peer_sub_keys[i], peer_u[i], peer_v[i])
        if i + 1 < DEPTH:
            z, u = ln_mod(z, ht, mods[i], 5, mods[i + 1], 0, y_t=True)
        else:
            z, _ = ln_mod(z, ht, mods[i], 5, y_t=True)
    return z[:, CTX_LEN:]
```

```python
import functools

import jax
import jax.numpy as jnp
from jax import lax
from jax.experimental import pallas as pl
from jax.experimental.pallas import tpu as pltpu

F32 = jnp.float32
BF16 = jnp.bfloat16

DEPTH = 4
CTX_LEN = 256
GRID_W = 64
N_MIXERS = 3
N_MOD = 6
LN_EPS = 1e-6
DEEPNORM_ALPHA = (2.0 * DEPTH) ** 0.25

GLA_HEADS = 8
GLA_GATE_RANK = 16
GLA_GATE_NORMALIZER = 16.0

MLA_Q_RANK = 1536
MLA_KV_RANK = 512
MLA_NOPE = 128
MLA_ROPE = 64
MLA_V = 128
ROPE_THETA = 10000.0

RWKV_HEAD = 64
RWKV_GN_EPS = 64e-5
RWKV_DECAY_SCALE = 0.6065306597126334

PEER_HEADS = 8
PEER_NKEYS = 128
PEER_DKEY = 256
PEER_TOPK = 16

LANES = 128
SUBLANES = 8
VMEM_LIMIT = 52 * 1024 * 1024

GLA_CHUNK = 128
GLA_HEADS_PER_STEP = 2
GLA_SUB = 4
RWKV_CHUNK = 64
RWKV_SUB = 16
MLA_KEY_CHAINS = 2
LOG2E = 1.4426950408889634
PEER_TOK_SUB = 256
PEER_EXP_BLOCK = 512
PEER_STAGES = 4
PEER_ROW_STRIP = 32
NEG = -1e30


def _pick(n, cands):
    for c in cands:
        if n % c == 0:
            return c
    return n


def _cparams(*sem):
    return pltpu.CompilerParams(dimension_semantics=sem, vmem_limit_bytes=VMEM_LIMIT)


def _dot(a, b):
    return jnp.dot(a, b, preferred_element_type=F32)


def _dot_nt(a, b):
    return lax.dot_general(a, b, (((1,), (1,)), ((), ())), preferred_element_type=F32)


def _dot_tn(a, b):
    return lax.dot_general(a, b, (((0,), (0,)), ((), ())), preferred_element_type=F32)


def _split3(x):
    hi = x.astype(BF16)
    r1 = x - hi.astype(F32)
    mid = r1.astype(BF16)
    lo = (r1 - mid.astype(F32)).astype(BF16)
    return hi, mid, lo


def _dot_sel(sel, x):
    s = sel.astype(BF16)
    hi, mid, lo = _split3(x)
    return _dot(s, hi) + _dot(s, mid) + _dot(s, lo)


def _dot_x3(a, b, dims=None):
    ah = a.astype(BF16)
    al = (a - ah.astype(F32)).astype(BF16)
    bh = b.astype(BF16)
    bl = (b - bh.astype(F32)).astype(BF16)
    f = _dot if dims is None else dims
    return f(ah, bh) + f(ah, bl) + f(al, bh)


def _dot_b(a, b):
    return _dot(a.astype(BF16), b.astype(BF16))


def _log_sigmoid(x):
    return jnp.minimum(x, 0.0) - jnp.log1p(jnp.exp(-jnp.abs(x)))


def _sigmoid(x):
    return 1.0 / (1.0 + jnp.exp(-x))


def _mm_body(a_ref, b_ref, o_ref):
    o_ref[...] = _dot(a_ref[...].astype(BF16), b_ref[...].astype(BF16)).astype(o_ref.dtype)


def matmul(a, b, out_dtype=F32):
    m, k = a.shape
    k2, n = b.shape
    assert k == k2
    a_bytes = jnp.dtype(a.dtype).itemsize
    tm_cands = (1024, 512, 256, 128, 64, 32, 16) if a_bytes * k <= 8192 else (512, 256, 128, 64, 32, 16)
    tm = _pick(m, tm_cands)
    tn = _pick(n, (512, 256, 128))
    return pl.pallas_call(
        _mm_body,
        grid=(m // tm, n // tn),
        in_specs=[pl.BlockSpec((tm, k), lambda i, j: (i, 0)),
                  pl.BlockSpec((k, tn), lambda i, j: (0, j))],
        out_specs=pl.BlockSpec((tm, tn), lambda i, j: (i, j)),
        out_shape=jax.ShapeDtypeStruct((m, n), out_dtype),
        compiler_params=_cparams("parallel", "parallel"),
    )(a, b)


def _layer_norm(z):
    zc = z - jnp.mean(z, -1, keepdims=True)
    return zc * lax.rsqrt(jnp.mean(zc * zc, -1, keepdims=True) + LN_EPS)


def _ln_mod_body(*refs, gidx, sidx, has_y, y_t, u_t):
    it = iter(refs)
    z_ref = next(it)
    y_ref = next(it) if has_y else None
    mg_ref = next(it) if has_y else None
    mn_ref = next(it) if sidx is not None else None
    zo_ref = next(it) if has_y else None
    u_ref = next(it) if sidx is not None else None
    z = z_ref[0]
    if has_y:
        y = y_ref[...].T if y_t else y_ref[0]
        z = _layer_norm(DEEPNORM_ALPHA * z + mg_ref[0, 0, gidx:gidx + 1, :] * y)
        zo_ref[0] = z
    if sidx is not None:
        u = z * (1.0 + mn_ref[0, 0, sidx + 1:sidx + 2, :]) + mn_ref[0, 0, sidx:sidx + 1, :]
        if u_t:
            u_ref[...] = u.T.astype(u_ref.dtype)
        else:
            u_ref[0] = u.astype(u_ref.dtype)


def ln_mod(z, y=None, mods_g=None, gidx=None, mods_n=None, sidx=None, y_t=False, u_t=False):
    bsz, t, d = z.shape
    tr = _pick(CTX_LEN, (256, 128, 64, 32, 16))
    nt = t // tr
    nctx = CTX_LEN // tr
    has_y = y is not None
    row = pl.BlockSpec((1, tr, d), lambda b, i: (b, i, 0))
    col = pl.BlockSpec((d, tr), lambda b, i: (0, b * nt + i))
    mod = pl.BlockSpec((1, 1, N_MOD, d), lambda b, i: (b, jnp.where(i < nctx, 0, 1), 0, 0))
    ins, in_specs, outs, out_specs = [z], [row], [], []
    if has_y:
        ins += [y, mods_g]
        in_specs += [col if y_t else row, mod]
        outs.append(jax.ShapeDtypeStruct((bsz, t, d), F32))
        out_specs.append(row)
    if sidx is not None:
        ins.append(mods_n)
        in_specs.append(mod)
        outs.append(jax.ShapeDtypeStruct((d, bsz * t) if u_t else (bsz, t, d), BF16))
        out_specs.append(col if u_t else row)
    res = pl.pallas_call(
        functools.partial(_ln_mod_body, gidx=gidx, sidx=sidx, has_y=has_y, y_t=y_t, u_t=u_t),
        grid=(bsz, nt), in_specs=in_specs, out_specs=out_specs, out_shape=outs,
        compiler_params=_cparams("parallel", "parallel"),
    )(*ins)
    res = list(res)
    zo = res.pop(0) if has_y else None
    u = res.pop(0) if sidx is not None else None
    return zo, u


def _lockstep(gens):
    out = [None] * len(gens)
    live = list(range(len(gens)))
    while live:
        for i in list(live):
            try:
                next(gens[i])
            except StopIteration as done:
                out[i] = done.value
                live.remove(i)
    return out


def _gla_chunk(q, k, v, glow, wg2, bg, st, rev):
    c, dk = q.shape
    sub = GLA_SUB
    q = q.astype(F32) * (dk ** -0.5)
    k = k.astype(F32)
    gl = _dot_x3(glow, wg2) + bg
    yield
    g = _log_sigmoid(gl) * (1.0 / GLA_GATE_NORMALIZER)
    row = lax.broadcasted_iota(jnp.int32, (c, c), 0)
    col = lax.broadcasted_iota(jnp.int32, (c, c), 1)
    tri = (row <= col) if rev else (row >= col)
    b = _dot_sel(tri.astype(F32), g)
    yield
    btot = jnp.sum(g, axis=0, keepdims=True)

    o = _dot_nt((q * jnp.exp(b)).astype(BF16), st.astype(BF16))

    rowk = lax.broadcasted_iota(jnp.int32, (c, 1), 0)
    halves = []
    hsz = c // 2
    while hsz >= sub:
        halves.append(hsz)
        hsz //= 2
    pick = jnp.concatenate(
        [(col == (row // (2 * hf)) * (2 * hf) + (hf if rev else hf - 1)).astype(F32) for hf in halves], axis=0)
    refs = _dot_sel(pick, b)
    yield
    att = jnp.zeros((c, c), F32)
    for lv, hf in enumerate(halves):
        refb = refs[lv * c:(lv + 1) * c]
        late = (rowk % (2 * hf)) >= hf
        q_side, k_side = (~late, late) if rev else (late, ~late)
        qf = jnp.where(q_side, q * jnp.exp(jnp.minimum(b - refb, 0.0)), 0.0)
        kf = jnp.where(k_side, k * jnp.exp(jnp.minimum(refb - b, 0.0)), 0.0)
        same = (row // (2 * hf)) == (col // (2 * hf))
        att = att + jnp.where(same, _dot_nt(qf.astype(BF16), kf.astype(BF16)), 0.0)
    yield

    rmod = rowk % sub
    for lag in range(sub):
        sh = (c - lag) % c if rev else lag
        ks = pltpu.roll(k, sh, 0) if sh else k
        bs = pltpu.roll(b, sh, 0) if sh else b
        term = jnp.sum(q * ks * jnp.exp(jnp.minimum(b - bs, 0.0)), axis=1, keepdims=True)
        valid = (rmod + lag < sub) if rev else (rmod >= lag)
        hit = (col == row + lag) if rev else (col == row - lag)
        att = att + jnp.where(hit & valid, term, 0.0)

    o = o + _dot(att.astype(BF16), v)
    yield
    kd = (k * jnp.exp(btot - b)).astype(BF16)
    return o, st * jnp.exp(btot) + _dot_tn(v, kd)


def _gla_body(*refs):
    ins, (of_ref, ob_ref, st_ref) = refs[:12], refs[12:]

    @pl.when(pl.program_id(2) == 0)
    def _():
        st_ref[...] = jnp.zeros_like(st_ref)

    dv = st_ref.shape[2]
    dk = st_ref.shape[3]
    gens, dest = [], []
    for dr in range(2):
        q_ref, k_ref, v_ref, gl_ref, wg2_ref, bg_ref = ins[6 * dr:6 * dr + 6]
        for hh in range(st_ref.shape[1]):
            ks, vs = slice(hh * dk, (hh + 1) * dk), slice(hh * dv, (hh + 1) * dv)
            gens.append(_gla_chunk(q_ref[0, :, ks], k_ref[0, :, ks], v_ref[0, :, vs], gl_ref[0],
                                   wg2_ref[:, ks], bg_ref[:, ks], st_ref[dr, hh], rev=(dr == 1)))
            dest.append((dr, hh, vs))
    for (dr, hh, vs), (o, st) in zip(dest, _lockstep(gens)):
        (of_ref, ob_ref)[dr][0, :, vs] = o.astype(of_ref.dtype)
        st_ref[dr, hh] = st


def gla_scan(p, glow, wg2, bg):
    bsz, t, d3 = p.shape
    d = d3 // 3
    h = GLA_HEADS
    dk, dv = (d // 2) // h, d // h
    c = GLA_CHUNK
    nc, ncx = t // c, CTX_LEN // c

    def rchunk(s):
        return jnp.where(s < ncx, ncx - 1 - s, nc - 1 - (s - ncx))

    hp = _pick(h, (GLA_HEADS_PER_STEP, 1))
    hg = h // hp

    def specs(chunk):
        return [pl.BlockSpec((1, c, hp * dk), lambda b, hh, s: (b, chunk(s), hh)),
                pl.BlockSpec((1, c, hp * dk), lambda b, hh, s: (b, chunk(s), hg + hh)),
                pl.BlockSpec((1, c, hp * dv), lambda b, hh, s: (b, chunk(s), hg + hh)),
                pl.BlockSpec((1, c, LANES), lambda b, hh, s: (b, chunk(s), 0)),
                pl.BlockSpec((LANES, hp * dk), lambda b, hh, s: (0, hh)),
                pl.BlockSpec((1, hp * dk), lambda b, hh, s: (0, hh))]

    sh = jax.ShapeDtypeStruct((bsz, t, d), BF16)
    return pl.pallas_call(
        _gla_body,
        grid=(bsz, hg, nc),
        in_specs=specs(lambda s: s) + specs(rchunk),
        out_specs=[pl.BlockSpec((1, c, hp * dv), lambda b, hh, s: (b, s, hh)),
                   pl.BlockSpec((1, c, hp * dv), lambda b, hh, s: (b, rchunk(s), hh))],
        out_shape=[sh, sh],
        scratch_shapes=[pltpu.VMEM((2, hp, dv, dk), F32)],
        compiler_params=_cparams("parallel", "parallel", "arbitrary"),
    )(p, p, p, glow, wg2[0], bg[0], p, p, p, glow, wg2[1], bg[1])


def _gla_gate_body(of_ref, ob_ref, r_ref, g_ref, o_ref):
    dv = g_ref.shape[1]
    o = of_ref[0].astype(F32) + ob_ref[0].astype(F32)
    r = r_ref[0].astype(F32)
    outs = []
    for hh in range(o.shape[1] // dv):
        seg = o[:, hh * dv:(hh + 1) * dv]
        outs.append(seg * lax.rsqrt(jnp.mean(seg * seg, -1, keepdims=True) + 1e-6) * g_ref[...])
    o_ref[0] = (jnp.concatenate(outs, axis=1) * (r * _sigmoid(r))).astype(o_ref.dtype)


def gla_gate(o_f, o_b, p, norm_g):
    bsz, t, d = o_f.shape
    tr = _pick(t, (256, 128, 64, 32, 16))
    row = pl.BlockSpec((1, tr, d), lambda b, i: (b, i, 0))
    return pl.pallas_call(
        _gla_gate_body, grid=(bsz, t // tr),
        in_specs=[row, row, pl.BlockSpec((1, tr, d), lambda b, i: (b, i, 2)),
                  pl.BlockSpec((1, norm_g.shape[1]), lambda b, i: (0, 0))],
        out_specs=row, out_shape=jax.ShapeDtypeStruct((bsz, t, d), BF16),
        compiler_params=_cparams("parallel", "parallel"),
    )(o_f, o_b, p, norm_g)


def gla_mixer(u, w_in, w_g1, w_g2, b_g, norm_g, w_o):
    bsz, t, d = u.shape
    qk = d // 2
    u2 = u.reshape(bsz * t, d)
    p = matmul(u2, w_in.astype(BF16), BF16).reshape(bsz, t, 3 * d)
    r = GLA_GATE_RANK
    wg1 = jnp.zeros((d, LANES), F32).at[:, :r].set(w_g1[0]).at[:, r:2 * r].set(w_g1[1])
    glow = matmul(u2, wg1.astype(BF16), F32).reshape(bsz, t, LANES)
    wg2 = [jnp.zeros((LANES, qk), F32).at[s * r:(s + 1) * r].set(w_g2[s]) for s in range(2)]
    outs = gla_scan(p, glow, wg2, [b_g[0][None], b_g[1][None]])
    gated = gla_gate(outs[0], outs[1], p, norm_g[None])
    return matmul(gated.reshape(bsz * t, d), w_o.astype(BF16), BF16).reshape(bsz, t, d)


def _rms(x, gain):
    return x * lax.rsqrt(jnp.mean(x * x, -1, keepdims=True) + 1e-6) * gain


def _rope128(a, c, s):
    return a * c + pltpu.roll(a, LANES // 2, 1) * s


def _mla_norm_body(h_ref, qg_ref, kg_ref, c_ref, s_ref, cq_ref, ckv_ref, kr_ref):
    qr, kvr = cq_ref.shape[2], ckv_ref.shape[2]
    h = h_ref[0]
    cq_ref[0] = _rms(h[:, :qr], qg_ref[...]).astype(BF16)
    ckv_ref[0] = _rms(h[:, qr:qr + kvr], kg_ref[...]).astype(BF16)
    kr_ref[0] = _rope128(h[:, qr + kvr:], c_ref[...], s_ref[...]).astype(BF16)


def _mla_attn_body(q_ref, kn_ref, v_ref, kr_ref, c_ref, s_ref, o_ref, kf_ref, vf_ref, *, nctx_tiles, scale):
    qt = pl.program_id(2)
    t = kf_ref.shape[0]
    ctx = nctx_tiles * q_ref.shape[1]

    @pl.when(qt == 0)
    def _():
        kf_ref[:, :LANES] = kn_ref[0]
        kf_ref[:, LANES:] = kr_ref[0]
        vf_ref[:, :LANES] = v_ref[0]
        vf_ref[:, LANES:] = jnp.ones((t, LANES), BF16)

    q = q_ref[0].astype(F32)
    qr = _rope128(q[:, LANES:], c_ref[...], s_ref[...])
    qf = (jnp.concatenate([q[:, :LANES], qr], axis=1) * (scale * LOG2E)).astype(BF16)

    def attend(keys):
        s = _dot_nt(qf, kf_ref[keys])
        yield
        m = jnp.max(s, -1, keepdims=True)
        p = jnp.exp2(s - m).astype(BF16)
        return m, _dot(p, vf_ref[keys])

    def finish(parts):
        m = functools.reduce(jnp.maximum, [pm for pm, _ in parts])
        acc = sum(jnp.exp2(pm - m) * po for pm, po in parts)
        o_ref[0] = (acc[:, :LANES] / acc[:, LANES:LANES + 1]).astype(o_ref.dtype)

    @pl.when(qt < nctx_tiles)
    def _():
        finish(_lockstep([attend(slice(0, ctx))]))

    @pl.when(qt >= nctx_tiles)
    def _():
        step = t // MLA_KEY_CHAINS
        finish(_lockstep([attend(slice(k0, k0 + step)) for k0 in range(0, t, step)]))


def _rope_tables(t):
    rows = (t - CTX_LEN) // GRID_W
    row = jnp.repeat(jnp.arange(rows, dtype=F32), GRID_W)
    colp = jnp.tile(jnp.arange(GRID_W, dtype=F32), rows)
    n_freq = MLA_ROPE // 4
    inv = ROPE_THETA ** (-jnp.arange(n_freq, dtype=F32) / n_freq)
    ang = jnp.concatenate([row[:, None] * inv, colp[:, None] * inv], -1)
    cos = jnp.concatenate([jnp.ones((CTX_LEN, MLA_ROPE // 2), F32), jnp.cos(ang)], 0)
    sin = jnp.concatenate([jnp.zeros((CTX_LEN, MLA_ROPE // 2), F32), jnp.sin(ang)], 0)
    z = jnp.zeros_like(cos)
    return jnp.concatenate([cos, cos, z, z], 1), jnp.concatenate([-sin, sin, z, z], 1)


def mla_mixer(u, w_in, q_norm, kv_norm, w_uq, w_ukv, w_o):
    bsz, t, d = u.shape
    nh = d // 128
    qr, kvr, rp = MLA_Q_RANK, MLA_KV_RANK, MLA_ROPE
    ev, od = jnp.arange(0, rp, 2), jnp.arange(1, rp, 2)
    perm = jnp.concatenate([ev, od, od, ev])
    w_in_p = jnp.concatenate([w_in[:, :qr + kvr], w_in[:, qr + kvr + perm]], axis=1)
    hw = qr + kvr + LANES
    h = matmul(u.reshape(bsz * t, d), w_in_p.astype(BF16), F32).reshape(bsz, t, hw)
    ctab, stab = _rope_tables(t)

    tr = _pick(t, (256, 128, 64, 32, 16))
    cq, ckv, kr = pl.pallas_call(
        _mla_norm_body, grid=(bsz, t // tr),
        in_specs=[pl.BlockSpec((1, tr, hw), lambda b, i: (b, i, 0)),
                  pl.BlockSpec((1, qr), lambda b, i: (0, 0)),
                  pl.BlockSpec((1, kvr), lambda b, i: (0, 0)),
                  pl.BlockSpec((tr, LANES), lambda b, i: (i, 0)),
                  pl.BlockSpec((tr, LANES), lambda b, i: (i, 0))],
        out_specs=[pl.BlockSpec((1, tr, qr), lambda b, i: (b, i, 0)),
                   pl.BlockSpec((1, tr, kvr), lambda b, i: (b, i, 0)),
                   pl.BlockSpec((1, tr, LANES), lambda b, i: (b, i, 0))],
        out_shape=[jax.ShapeDtypeStruct((bsz, t, qr), BF16),
                   jax.ShapeDtypeStruct((bsz, t, kvr), BF16),
                   jax.ShapeDtypeStruct((bsz, t, LANES), BF16)],
        compiler_params=_cparams("parallel", "parallel"),
    )(h, q_norm[None], kv_norm[None], ctab, stab)

    hq = MLA_NOPE + rp
    qcols = (jnp.arange(nh)[:, None] * hq
             + jnp.concatenate([jnp.arange(MLA_NOPE), MLA_NOPE + perm])[None, :]).reshape(-1)
    q = matmul(cq.reshape(bsz * t, qr), w_uq[:, qcols].astype(BF16), BF16).reshape(bsz, t, nh * 2 * LANES)
    kv = matmul(ckv.reshape(bsz * t, kvr), w_ukv.astype(BF16), BF16).reshape(bsz, t, nh * 2 * LANES)

    tq = _pick(CTX_LEN, (256, 128, 64, 32, 16))
    scale = (MLA_NOPE + rp) ** -0.5
    o = pl.pallas_call(
        functools.partial(_mla_attn_body, nctx_tiles=CTX_LEN // tq, scale=scale),
        grid=(bsz, nh, t // tq),
        in_specs=[pl.BlockSpec((1, tq, 2 * LANES), lambda b, hh, i: (b, i, hh)),
                  pl.BlockSpec((1, t, LANES), lambda b, hh, i: (b, 0, 2 * hh)),
                  pl.BlockSpec((1, t, LANES), lambda b, hh, i: (b, 0, 2 * hh + 1)),
                  pl.BlockSpec((1, t, LANES), lambda b, hh, i: (b, 0, 0)),
                  pl.BlockSpec((tq, LANES), lambda b, hh, i: (i, 0)),
                  pl.BlockSpec((tq, LANES), lambda b, hh, i: (i, 0))],
        out_specs=pl.BlockSpec((1, tq, LANES), lambda b, hh, i: (b, i, hh)),
        out_shape=jax.ShapeDtypeStruct((bsz, t, nh * MLA_V), BF16),
        scratch_shapes=[pltpu.VMEM((t, 2 * LANES), BF16), pltpu.VMEM((t, 2 * LANES), BF16)],
        compiler_params=_cparams("parallel", "parallel", "arbitrary"),
    )(q, kv, kv, kr, ctab, stab)
    return matmul(o.reshape(bsz * t, nh * MLA_V), w_o.astype(BF16), BF16).reshape(bsz, t, d)


def _rwkv_mix_body(u_ref, before_ref, after_ref, mu_ref, *o_refs, nctx_tiles):
    i = pl.program_id(1)
    tr = u_ref.shape[1]
    hr = before_ref.shape[1]
    u = u_ref[0].astype(F32)
    row = lax.broadcasted_iota(jnp.int32, (tr, 1), 0)
    starts = (i == 0) | (i == nctx_tiles)
    ends = (i == nctx_tiles - 1) | (i == pl.num_programs(1) - 1)
    edge_prev = jnp.where(starts, 0.0, before_ref[0, hr - 1:hr, :].astype(F32))
    edge_next = jnp.where(ends, 0.0, after_ref[0, 0:1, :].astype(F32))
    prev = jnp.where(row == 0, edge_prev, pltpu.roll(u, 1, 0))
    nxt = jnp.where(row == tr - 1, edge_next, pltpu.roll(u, tr - 1, 0))
    xx = 0.5 * (prev + nxt) - u
    for n, o_ref in enumerate(o_refs):
        o_ref[0] = (u + xx * mu_ref[n:n + 1, :]).astype(o_ref.dtype)


def _head_sum(x, e):
    eb = e.astype(BF16)
    outs = []
    for s in range(x.shape[1] // LANES):
        xs = x[:, s * LANES:(s + 1) * LANES]
        hi = xs.astype(BF16)
        lo = (xs - hi.astype(F32)).astype(BF16)
        outs.append(_dot(hi, eb) + _dot(lo, eb))
    return jnp.concatenate(outs, axis=1)


def _head_ones():
    r = lax.broadcasted_iota(jnp.int32, (LANES, LANES), 0) // RWKV_HEAD
    c = lax.broadcasted_iota(jnp.int32, (LANES, LANES), 1) // RWKV_HEAD
    return (r == c).astype(F32)


def _rwkv_prep_body(k_ref, hw_ref, ha_ref, w2_ref, a2_ref, w0_ref, a0_ref, kk_ref, ka_ref,
                    na_ref, lw0_ref, lw1_ref, ks0_ref, ks1_ref, bb0_ref, bb1_ref):
    rk = w2_ref.shape[1]
    k = k_ref[0]
    kk = k * kk_ref[...]
    kk = kk * lax.rsqrt(_head_sum(kk * kk, _head_ones()) + 1e-12)
    na_ref[0] = (-kk).astype(na_ref.dtype)
    hw = jnp.tanh(hw_ref[0])
    ha = ha_ref[0]
    for s, (lw_ref, ks_ref, bb_ref) in enumerate(((lw0_ref, ks0_ref, bb0_ref), (lw1_ref, ks1_ref, bb1_ref))):
        wl = w0_ref[s:s + 1, :] + _dot(hw[:, s * rk:(s + 1) * rk].astype(BF16), w2_ref[s])
        lw_ref[0] = (-RWKV_DECAY_SCALE) * _sigmoid(wl)
        a = _sigmoid(a0_ref[s:s + 1, :] + _dot(ha[:, s * rk:(s + 1) * rk].astype(BF16), a2_ref[s]))
        ks_ref[0] = (k * (1.0 + (a - 1.0) * ka_ref[...])).astype(ks_ref.dtype)
        bb_ref[0] = (kk * a).astype(bb_ref.dtype)


def _rwkv_chunk(r, lw, ks, v, na, bb, ht, rev):
    c = r.shape[0]
    n = RWKV_HEAD
    c2 = 2 * c
    r, ks, v, na, bb = (x.astype(F32) for x in (r, ks, v, na, bb))
    ri = lax.broadcasted_iota(jnp.int32, (c, c), 0)
    ci = lax.broadcasted_iota(jnp.int32, (c, c), 1)
    cs = _dot_sel(((ri <= ci) if rev else (ri >= ci)).astype(F32), lw)
    yield
    cm = cs - lw
    ctot = jnp.sum(lw, axis=0, keepdims=True)

    lane_a = lax.broadcasted_iota(jnp.int32, (1, LANES), 1) < n

    def stack(x):
        return jnp.concatenate([jnp.where(lane_a, x, 0.0), jnp.where(lane_a, 0.0, x)], axis=0)

    at = stack(na * jnp.exp(cm))
    rt = stack(r * jnp.exp(cs))
    ecs = jnp.exp(-cs)
    bh = stack(bb * ecs)
    kh = stack(ks * ecs)
    ece = jnp.exp(ctot - cs)
    be = stack(bb * ece)
    ke = stack(ks * ece)
    vs = stack(v)

    rr = lax.broadcasted_iota(jnp.int32, (c2, c2), 0)
    cc = lax.broadcasted_iota(jnp.int32, (c2, c2), 1)
    same = (rr // c) == (cc // c)
    tr_, tc_ = rr % c, cc % c
    strict = same & ((tc_ > tr_) if rev else (tc_ < tr_))
    incl = same & ((tc_ >= tr_) if rev else (tc_ <= tr_))
    diag_blk = same & ((tr_ // RWKV_SUB) == (tc_ // RWKV_SUB))

    lhs = jnp.concatenate([at, rt], axis=0)
    rhs = jnp.concatenate([bh, kh], axis=0)
    a4 = _dot_nt(lhs.astype(BF16), rhs.astype(BF16))
    yield
    nmat = jnp.where(strict, a4[:c2, :c2], 0.0)
    aak = jnp.where(strict, a4[:c2, c2:], 0.0)
    arb = jnp.where(incl, a4[c2:, :c2], 0.0)
    ark = jnp.where(incl, a4[c2:, c2:], 0.0)

    hb = ht.T.astype(BF16)
    vb = vs.astype(BF16)
    x0 = _dot(jnp.concatenate([at, aak], axis=1).astype(BF16), jnp.concatenate([hb, vb], axis=0))
    yield

    nd = jnp.where(diag_blk, nmat, 0.0)
    no = nmat - nd
    tm = nd
    pw = nd
    for _ in range(3):
        pw = _dot_b(pw, pw)
        yield
        tm = tm + pw + _dot_b(pw, tm)
        yield
    zu = _dot_b(tm, jnp.concatenate([no, x0], axis=1))
    z = no + zu[:, :c2]
    u = x0 + zu[:, c2:]
    yield
    zz = _dot_b(z, jnp.concatenate([z, u], axis=1))
    u = u + zz[:, c2:]
    yield
    u = u + _dot_b(zz[:, :c2], u)
    yield

    ub = u.astype(BF16)
    ys = _dot(jnp.concatenate([rt, arb, ark], axis=1).astype(BF16), jnp.concatenate([hb, ub, vb], axis=0))
    yield
    hn = ht * jnp.exp(ctot) + _dot_tn(jnp.concatenate([ub, vb], axis=0),
                                      jnp.concatenate([be, ke], axis=0).astype(BF16))
    hr = lax.broadcasted_iota(jnp.int32, (LANES, LANES), 0) // n
    hc = lax.broadcasted_iota(jnp.int32, (LANES, LANES), 1) // n
    return ys[:c] + ys[c:], jnp.where(hr == hc, hn, 0.0)


def _rwkv_scan_body(*refs):
    ins, (y0_ref, y1_ref, st_ref) = refs[:12], refs[12:]

    @pl.when(pl.program_id(2) == 0)
    def _():
        st_ref[...] = jnp.zeros_like(st_ref)

    gens, dest = [], []
    for dr in range(2):
        r_ref, lw_ref, ks_ref, v_ref, na_ref, bb_ref = ins[6 * dr:6 * dr + 6]
        for p in range(r_ref.shape[2] // LANES):
            sl = slice(p * LANES, (p + 1) * LANES)
            gens.append(_rwkv_chunk(r_ref[0, :, sl], lw_ref[0, :, sl], ks_ref[0, :, sl], v_ref[0, :, sl],
                                    na_ref[0, :, sl], bb_ref[0, :, sl], st_ref[dr, p], rev=(dr == 1)))
            dest.append(((y0_ref, y1_ref)[dr], sl, dr, p))
    for (y_ref, sl, dr, p), (y, hn) in zip(dest, _lockstep(gens)):
        y_ref[0, :, sl] = y
        st_ref[dr, p] = hn


def rwkv_scan(r, v, na, lw, ks, bb):
    bsz, t, d = r.shape
    c = RWKV_CHUNK
    nc, ncx = t // c, CTX_LEN // c
    pp = _pick(d // LANES, (4, 2, 1))
    w = pp * LANES

    def rchunk(s):
        return jnp.where(s < ncx, ncx - 1 - s, nc - 1 - (s - ncx))

    fwd = pl.BlockSpec((1, c, w), lambda b, p, s: (b, s, p))
    bwd = pl.BlockSpec((1, c, w), lambda b, p, s: (b, rchunk(s), p))
    sh = jax.ShapeDtypeStruct((bsz, t, d), F32)
    return pl.pallas_call(
        _rwkv_scan_body,
        grid=(bsz, d // w, nc),
        in_specs=[fwd] * 6 + [bwd] * 6, out_specs=[fwd, bwd], out_shape=[sh, sh],
        scratch_shapes=[pltpu.VMEM((2, pp, LANES, LANES), F32)],
        compiler_params=_cparams("parallel", "parallel", "arbitrary"),
    )(r, lw[0], ks[0], v, na, bb[0], r, lw[1], ks[1], v, na, bb[1])


def _rwkv_out_body(y0_ref, y1_ref, r_ref, v_ref, g_ref, ks0_ref, ks1_ref, rk_ref, lg_ref, lb_ref, o_ref):
    e = _head_ones()
    y = y0_ref[0] + y1_ref[0]
    inv_n = 1.0 / RWKV_HEAD
    yc = y - _head_sum(y, e) * inv_n
    yn = yc * lax.rsqrt(_head_sum(yc * yc, e) * inv_n + RWKV_GN_EPS)
    yn = yn * lg_ref[...] + lb_ref[...]
    ksum = ks0_ref[0].astype(F32) + ks1_ref[0].astype(F32)
    bonus = _head_sum(r_ref[0].astype(F32) * ksum * rk_ref[...], e) * v_ref[0].astype(F32)
    o_ref[0] = ((yn + bonus) * g_ref[0].astype(F32)).astype(o_ref.dtype)


def rwkv_mixer(u, mu, w_r, w_k, w_v, w_o, w0, w1, w2, a0, a1, a2, g1, g2, k_k, k_a, r_k, ln_g, ln_b):
    bsz, t, d = u.shape
    m = bsz * t
    tm_ = _pick(CTX_LEN, (128, 64, 32, 16))
    hr = 16
    rpt, last = tm_ // hr, t // hr - 1
    tile = pl.BlockSpec((1, tm_, d), lambda b, i: (b, i, 0))
    xs = pl.pallas_call(
        functools.partial(_rwkv_mix_body, nctx_tiles=CTX_LEN // tm_), grid=(bsz, t // tm_),
        in_specs=[tile,
                  pl.BlockSpec((1, hr, d), lambda b, i: (b, jnp.maximum(i * rpt - 1, 0), 0)),
                  pl.BlockSpec((1, hr, d), lambda b, i: (b, jnp.minimum((i + 1) * rpt, last), 0)),
                  pl.BlockSpec((6, d), lambda b, i: (0, 0))],
        out_specs=[tile] * 6,
        out_shape=[jax.ShapeDtypeStruct((bsz, t, d), BF16)] * 6,
        compiler_params=_cparams("parallel", "parallel"),
    )(u, u, u, mu)
    xr, xw, xk, xv, xa, xg = (x.reshape(m, d) for x in xs)
    r = matmul(xr, w_r.astype(BF16), BF16).reshape(bsz, t, d)
    k = matmul(xk, w_k.astype(BF16)).reshape(bsz, t, d)
    v = matmul(xv, w_v.astype(BF16), BF16).reshape(bsz, t, d)
    rk = w1.shape[2]
    hw = matmul(xw, jnp.concatenate([w1[0], w1[1]], 1).astype(BF16)).reshape(bsz, t, 2 * rk)
    ha = matmul(xa, jnp.concatenate([a1[0], a1[1]], 1).astype(BF16)).reshape(bsz, t, 2 * rk)
    gr = g1.shape[1]
    grp = -(-gr // LANES) * LANES
    g1p = jnp.zeros((d, grp), F32).at[:, :gr].set(g1)
    g2p = jnp.zeros((grp, d), F32).at[:gr].set(g2)
    hg = jax.nn.sigmoid(matmul(xg, g1p.astype(BF16)))
    g = matmul(hg, g2p.astype(BF16), BF16).reshape(bsz, t, d)

    tr = _pick(t, (64, 32, 16))
    row = pl.BlockSpec((1, tr, d), lambda b, i: (b, i, 0))
    low = pl.BlockSpec((1, tr, 2 * rk), lambda b, i: (b, i, 0))
    vec = pl.BlockSpec((1, d), lambda b, i: (0, 0))
    vec2 = pl.BlockSpec((2, d), lambda b, i: (0, 0))
    fact = pl.BlockSpec((2, rk, d), lambda b, i: (0, 0, 0))
    big = jax.ShapeDtypeStruct((bsz, t, d), F32)
    half = jax.ShapeDtypeStruct((bsz, t, d), BF16)
    na, lw0, lw1, ks0, ks1, bb0, bb1 = pl.pallas_call(
        _rwkv_prep_body, grid=(bsz, t // tr),
        in_specs=[row, low, low, fact, fact, vec2, vec2, vec, vec],
        out_specs=[row] * 7, out_shape=[half, big, big, half, half, half, half],
        compiler_params=_cparams("parallel", "parallel"),
    )(k, hw, ha, w2.astype(BF16), a2.astype(BF16), w0, a0, k_k[None], k_a[None])

    y0, y1 = rwkv_scan(r, v, na, (lw0, lw1), (ks0, ks1), (bb0, bb1))

    out = pl.pallas_call(
        _rwkv_out_body, grid=(bsz, t // tr),
        in_specs=[row] * 7 + [vec, vec, vec], out_specs=row,
        out_shape=jax.ShapeDtypeStruct((bsz, t, d), BF16),
        compiler_params=_cparams("parallel", "parallel"),
    )(y0, y1, r, v, g, ks0, ks1, r_k.reshape(1, d), ln_g[None], ln_b[None])
    return matmul(out.reshape(m, d), w_o.astype(BF16), BF16).reshape(bsz, t, d)


def _merge_exchange(n):
    t = max(1, (n - 1).bit_length())
    p = 1 << (t - 1)
    pairs = []
    while p > 0:
        q, r, d = 1 << (t - 1), 0, p
        while d > 0:
            pairs += [(i, i + d) for i in range(n - d) if (i & p) == r]
            d, q, r = q - p, q >> 1, p
        p >>= 1
    return pairs


def _pop_heads(levels, kk):
    out = []
    for k in range(kk):
        head = levels[0]
        m = jnp.max(head, axis=0, keepdims=True)
        out.append(m)
        need = min(len(levels), kk - k - 1)
        pop = head == m
        levels = [jnp.where(pop, levels[r + 1] if r + 1 < len(levels) else NEG, levels[r]) for r in range(need)]
    return jnp.concatenate(out, axis=0)


def _top_vals(s, kk):
    groups = [s[r:r + SUBLANES] for r in range(0, s.shape[0], SUBLANES)]
    for i, j in _merge_exchange(len(groups)):
        groups[i], groups[j] = jnp.maximum(groups[i], groups[j]), jnp.minimum(groups[i], groups[j])
    return _pop_heads(groups, kk)


def _peer_score_body(q_ref, keys_ref, thr_ref, s2_ref, e2_ref, cf_ref):
    nk = PEER_NKEYS
    kk = PEER_TOPK
    for h in range(PEER_HEADS):
        q1 = q_ref[h * 2 * nk:h * 2 * nk + nk, :].astype(BF16)
        q2 = q_ref[h * 2 * nk + nk:(h + 1) * 2 * nk, :].astype(BF16)
        s1 = _dot(keys_ref[0], q1)
        s2 = _dot(keys_ref[1], q2)
        a1 = _top_vals(s1, kk + 1)
        a2 = _top_vals(s2, kk + 1)
        cv = _pop_heads([a1[:kk] + a2[q:q + 1] for q in range(kk)], kk + 1)
        c17 = jnp.maximum(cv[kk:kk + 1], jnp.maximum(a1[kk:kk + 1] + a2[0:1], a1[0:1] + a2[kk:kk + 1]))
        theta = 0.5 * (cv[kk - 1:kk] + c17)
        zsum = jnp.sum(jnp.exp(cv[:kk] - cv[0:1]), axis=0, keepdims=True)
        thr_ref[h] = theta - s1
        s2_ref[h] = s2
        e2_ref[h] = jnp.exp(s2 - a2[0:1])
        cf_ref[h] = jnp.exp(s1 - a1[0:1]) / zsum


def _gelu(x):
    return 0.5 * x * (1.0 + lax.erf(x * (2.0 ** -0.5)))


def _peer_dense_body(z_ref, u_ref, v_ref, thr_ref, s2_ref, e2_ref, cf_ref, o_ref, w_ref):
    nk = PEER_NKEYS
    eb = pl.program_id(1)
    te = u_ref.shape[0]
    n_i = te // nk
    strip = PEER_ROW_STRIP
    stages = PEER_STAGES

    @pl.when(eb == 0)
    def _():
        o_ref[...] = jnp.zeros_like(o_ref)

    sub = min(PEER_TOK_SUB, z_ref.shape[1])
    d = z_ref.shape[0]
    kc = d // stages

    def first_matmul(ts):
        acc = None
        for k0 in range(0, d, kc):
            part = _dot(u_ref[:, k0:k0 + kc], z_ref[k0:k0 + kc, ts])
            acc = part if acc is None else acc + part
            yield
        return acc

    def gates(ts):
        per_stage = (nk // strip) * PEER_HEADS // stages
        n = 0
        for si in range(nk // strip):
            js = slice(si * strip, (si + 1) * strip)
            w = [None] * n_i
            for h in range(PEER_HEADS):
                s2s = s2_ref[h, js, ts]
                e2s = e2_ref[h, js, ts]
                for ii in range(n_i):
                    i = eb * n_i + ii
                    c = jnp.where(s2s >= thr_ref[h, pl.ds(i, 1), ts], e2s, 0.0) * cf_ref[h, pl.ds(i, 1), ts]
                    w[ii] = c if w[ii] is None else w[ii] + c
                n += 1
                if n % per_stage == 0 and h + 1 < PEER_HEADS:
                    yield
            for ii in range(n_i):
                w_ref[ii * nk + si * strip:ii * nk + (si + 1) * strip, ts] = w[ii].astype(BF16)
            yield

    def activate(ts, act):
        rows = te // stages
        for r0 in range(0, te, rows):
            w_ref[r0:r0 + rows, ts] = w_ref[r0:r0 + rows, ts] * _gelu(act[r0:r0 + rows]).astype(BF16)
            yield

    def second_matmul(ts):
        rows = d // stages
        for r0 in range(0, d, rows):
            o_ref[r0:r0 + rows, ts] += _dot(v_ref[r0:r0 + rows, :], w_ref[:, ts])
            yield

    tiles = [slice(t0, t0 + sub) for t0 in range(0, z_ref.shape[1], sub)]
    acts = {}
    for ph in range(len(tiles) + 2):
        gens, tags = [], []
        if ph < len(tiles):
            gens += [first_matmul(tiles[ph]), gates(tiles[ph])]
            tags += [ph, None]
        if 0 <= ph - 1 < len(tiles):
            gens.append(activate(tiles[ph - 1], acts[ph - 1]))
            tags.append(None)
        if 0 <= ph - 2 < len(tiles):
            gens.append(second_matmul(tiles[ph - 2]))
            tags.append(None)
        for tag, res in zip(tags, _lockstep(gens)):
            if tag is not None:
                acts[tag] = res


def peer_ffn(zt, w_q, sub_keys, u_tab, v_tab):
    d, m = zt.shape
    nh, nk = PEER_HEADS, PEER_NKEYS
    qt = matmul(w_q.T.astype(BF16), zt, F32)
    tt = _pick(m, (256, 128))
    sh = jax.ShapeDtypeStruct((nh, nk, m), F32)
    blk = pl.BlockSpec((nh, nk, tt), lambda i: (0, 0, i))
    thr, s2, e2, cf = pl.pallas_call(
        _peer_score_body, grid=(m // tt,),
        in_specs=[pl.BlockSpec((nh * 2 * nk, tt), lambda i: (0, i)),
                  pl.BlockSpec((2, nk, PEER_DKEY // 2), lambda i: (0, 0, 0))],
        out_specs=[blk, blk, blk, blk], out_shape=[sh, sh, sh, sh],
        compiler_params=_cparams("parallel"),
    )(qt, sub_keys.astype(BF16))

    tm = _pick(m, (2 * PEER_TOK_SUB, PEER_TOK_SUB, LANES))
    te = PEER_EXP_BLOCK
    ne = u_tab.shape[0]
    once = pl.Buffered(1)
    sblk = pl.BlockSpec((nh, nk, tm), lambda i, e: (0, 0, i), pipeline_mode=once)
    return pl.pallas_call(
        _peer_dense_body, grid=(m // tm, ne // te),
        in_specs=[pl.BlockSpec((d, tm), lambda i, e: (0, i), pipeline_mode=once),
                  pl.BlockSpec((te, d), lambda i, e: (e, 0)),
                  pl.BlockSpec((d, te), lambda i, e: (0, e)),
                  sblk, sblk, sblk, sblk],
        out_specs=pl.BlockSpec((d, tm), lambda i, e: (0, i)),
        out_shape=jax.ShapeDtypeStruct((d, m), F32),
        scratch_shapes=[pltpu.VMEM((te, tm), BF16)],
        compiler_params=_cparams("parallel", "arbitrary"),
    )(zt, u_tab.astype(BF16), v_tab.T.astype(BF16), thr, s2, e2, cf)


def _ada_mods(c, c_ctx, w_down, w_up, b_up):
    bsz, d = c.shape
    cond = jnp.concatenate([c, c_ctx[None]], 0)
    pad = 16 - cond.shape[0] % 16
    cond = jnp.concatenate([cond, jnp.zeros((pad, d), F32)], 0)
    hid = matmul(jax.nn.silu(cond), w_down.astype(BF16), F32)
    m = (matmul(hid, w_up.astype(BF16), F32) + b_up)[:bsz + 1].reshape(bsz + 1, N_MOD, d)
    return jnp.stack([jnp.broadcast_to(m[bsz], (bsz, N_MOD, d)), m[:bsz]], axis=1)


def kernel(x, c, ctx, c_ctx, ada_w_down, ada_w_up, ada_b,
           gla_w_in, gla_w_g1, gla_w_g2, gla_b_g, gla_norm_g, gla_w_o,
           mla_w_in, mla_q_norm, mla_kv_norm, mla_w_uq, mla_w_ukv, mla_w_o,
           rwkv_mu, rwkv_w_r, rwkv_w_k, rwkv_w_v, rwkv_w_o, rwkv_w0, rwkv_w1, rwkv_w2,
           rwkv_a0, rwkv_a1, rwkv_a2, rwkv_g1, rwkv_g2, rwkv_k_k, rwkv_k_a, rwkv_r_k, rwkv_ln_g, rwkv_ln_b,
           peer_w_q, peer_sub_keys, peer_u, peer_v):
    z = jnp.concatenate([ctx, x], axis=1)
    bsz, t, d = z.shape
    mods = [_ada_mods(c, c_ctx, ada_w_down[i], ada_w_up[i], ada_b[i]) for i in range(DEPTH)]
    _, u = ln_mod(z, mods_n=mods[0], sidx=0)
    for i in range(DEPTH):
        j = i // N_MIXERS
        if i % N_MIXERS == 0:
            y = gla_mixer(u, gla_w_in[j], gla_w_g1[j], gla_w_g2[j], gla_b_g[j], gla_norm_g[j], gla_w_o[j])
        elif i % N_MIXERS == 1:
            y = mla_mixer(u, mla_w_in[j], mla_q_norm[j], mla_kv_norm[j], mla_w_uq[j], mla_w_ukv[j], mla_w_o[j])
        else:
            y = rwkv_mixer(u, rwkv_mu[j], rwkv_w_r[j], rwkv_w_k[j], rwkv_w_v[j], rwkv_w_o[j],
                           rwkv_w0[j], rwkv_w1[j], rwkv_w2[j], rwkv_a0[j], rwkv_a1[j], rwkv_a2[j],
                           rwkv_g1[j], rwkv_g2[j], rwkv_k_k[j], rwkv_k_a[j], rwkv_r_k[j],
                           rwkv_ln_g[j], rwkv_ln_b[j])
        z, ut = ln_mod(z, y, mods[i], 2, mods[i], 3, u_t=True)
        ht = peer_ffn(ut, peer_w_q[i], peer_sub_keys[i], peer_u[i], peer_v[i])
        if i + 1 < DEPTH:
            z, u = ln_mod(z, ht, mods[i], 5, mods[i + 1], 0, y_t=True)
        else:
            z, _ = ln_mod(z, ht, mods[i], 5, y_t=True)
    return z[:, CTX_LEN:]
```

```python
import functools

import jax
import jax.numpy as jnp
from jax import lax
from jax.experimental import pallas as pl
from jax.experimental.pallas import tpu as pltpu

F32 = jnp.float32
BF16 = jnp.bfloat16

DEPTH = 4
CTX_LEN = 256
GRID_W = 64
N_MIXERS = 3
N_MOD = 6
LN_EPS = 1e-6
DEEPNORM_ALPHA = (2.0 * DEPTH) ** 0.25

GLA_HEADS = 8
GLA_GATE_RANK = 16
GLA_GATE_NORMALIZER = 16.0

MLA_Q_RANK = 1536
MLA_KV_RANK = 512
MLA_NOPE = 128
MLA_ROPE = 64
MLA_V = 128
ROPE_THETA = 10000.0

RWKV_HEAD = 64
RWKV_GN_EPS = 64e-5
RWKV_DECAY_SCALE = 0.6065306597126334

PEER_HEADS = 8
PEER_NKEYS = 128
PEER_DKEY = 256
PEER_TOPK = 16

LANES = 128
SUBLANES = 8
MXU_COLS = 256
VMEM_LIMIT = 52 * 1024 * 1024

GLA_CHUNK = 128
GLA_HEADS_PER_STEP = 2
GLA_SUB = 4
RWKV_CHUNK = 64
RWKV_SUB = 16
MLA_KEY_CHAINS = 2
LOG2E = 1.4426950408889634
PEER_TOK_SUB = 256
PEER_EXP_BLOCK = 512
PEER_STAGES = 4
PEER_ROW_STRIP = 32
NEG = -1e30


def _pick(n, cands):
    for c in cands:
        if n % c == 0:
            return c
    return n


def _cparams(*sem):
    return pltpu.CompilerParams(dimension_semantics=sem, vmem_limit_bytes=VMEM_LIMIT)


def _dot(a, b):
    return jnp.dot(a, b, preferred_element_type=F32)


def _dot_nt(a, b):
    return lax.dot_general(a, b, (((1,), (1,)), ((), ())), preferred_element_type=F32)


def _dot_tn(a, b):
    return lax.dot_general(a, b, (((0,), (0,)), ((), ())), preferred_element_type=F32)


def _split3(x):
    hi = x.astype(BF16)
    r1 = x - hi.astype(F32)
    mid = r1.astype(BF16)
    lo = (r1 - mid.astype(F32)).astype(BF16)
    return hi, mid, lo


def _dot_sel(sel, x):
    s = sel.astype(BF16)
    hi, mid, lo = _split3(x)
    return _dot(s, hi) + _dot(s, mid) + _dot(s, lo)


def _dot_x3(a, b, dims=None):
    ah = a.astype(BF16)
    al = (a - ah.astype(F32)).astype(BF16)
    bh = b.astype(BF16)
    bl = (b - bh.astype(F32)).astype(BF16)
    f = _dot if dims is None else dims
    return f(ah, bh) + f(ah, bl) + f(al, bh)


def _dot_b(a, b):
    return _dot(a.astype(BF16), b.astype(BF16))


def _log_sigmoid(x):
    return jnp.minimum(x, 0.0) - jnp.log1p(jnp.exp(-jnp.abs(x)))


def _sigmoid(x):
    return 1.0 / (1.0 + jnp.exp(-x))


def _mm_body(a_ref, b_ref, o_ref):
    o_ref[...] = _dot(a_ref[...].astype(BF16), b_ref[...].astype(BF16)).astype(o_ref.dtype)


def matmul(a, b, out_dtype=F32):
    m, k = a.shape
    k2, n = b.shape
    assert k == k2
    a_bytes = jnp.dtype(a.dtype).itemsize
    tm_cands = (1024, 512, 256, 128, 64, 32, 16) if a_bytes * k <= 8192 else (512, 256, 128, 64, 32, 16)
    tm = _pick(m, tm_cands)
    tn = _pick(n, (512, 256, 128))
    return pl.pallas_call(
        _mm_body,
        grid=(m // tm, n // tn),
        in_specs=[pl.BlockSpec((tm, k), lambda i, j: (i, 0)),
                  pl.BlockSpec((k, tn), lambda i, j: (0, j))],
        out_specs=pl.BlockSpec((tm, tn), lambda i, j: (i, j)),
        out_shape=jax.ShapeDtypeStruct((m, n), out_dtype),
        compiler_params=_cparams("parallel", "parallel"),
    )(a, b)


def _layer_norm(z):
    zc = z - jnp.mean(z, -1, keepdims=True)
    return zc * lax.rsqrt(jnp.mean(zc * zc, -1, keepdims=True) + LN_EPS)


def _ln_mod_body(*refs, gidx, sidx, has_y, y_t, u_t):
    it = iter(refs)
    z_ref = next(it)
    y_ref = next(it) if has_y else None
    mg_ref = next(it) if has_y else None
    mn_ref = next(it) if sidx is not None else None
    zo_ref = next(it) if has_y else None
    u_ref = next(it) if sidx is not None else None
    z = z_ref[0]
    if has_y:
        y = y_ref[...].T if y_t else y_ref[0]
        z = _layer_norm(DEEPNORM_ALPHA * z + mg_ref[0, 0, gidx:gidx + 1, :] * y)
        zo_ref[0] = z
    if sidx is not None:
        u = z * (1.0 + mn_ref[0, 0, sidx + 1:sidx + 2, :]) + mn_ref[0, 0, sidx:sidx + 1, :]
        if u_t:
            u_ref[...] = u.T.astype(u_ref.dtype)
        else:
            u_ref[0] = u.astype(u_ref.dtype)


def ln_mod(z, y=None, mods_g=None, gidx=None, mods_n=None, sidx=None, y_t=False, u_t=False, latent_only=False):
    assert not (latent_only and sidx is not None)
    bsz, t, d = z.shape
    tr = _pick(CTX_LEN, (256, 128, 64, 32, 16))
    nt = t // tr
    nctx = CTX_LEN // tr
    first = nctx if latent_only else 0
    has_y = y is not None
    row = pl.BlockSpec((1, tr, d), lambda b, i: (b, first + i, 0))
    col = pl.BlockSpec((d, tr), lambda b, i: (0, b * nt + first + i))
    mod = pl.BlockSpec((1, 1, N_MOD, d), lambda b, i: (b, jnp.where(first + i < nctx, 0, 1), 0, 0))
    ins, in_specs, outs, out_specs = [z], [row], [], []
    if has_y:
        ins += [y, mods_g]
        in_specs += [col if y_t else row, mod]
        outs.append(jax.ShapeDtypeStruct((bsz, t - first * tr, d), F32))
        out_specs.append(pl.BlockSpec((1, tr, d), lambda b, i: (b, i, 0)) if latent_only else row)
    if sidx is not None:
        ins.append(mods_n)
        in_specs.append(mod)
        outs.append(jax.ShapeDtypeStruct((d, bsz * t) if u_t else (bsz, t, d), BF16))
        out_specs.append(col if u_t else row)
    res = pl.pallas_call(
        functools.partial(_ln_mod_body, gidx=gidx, sidx=sidx, has_y=has_y, y_t=y_t, u_t=u_t),
        grid=(bsz, nt - first), in_specs=in_specs, out_specs=out_specs, out_shape=outs,
        compiler_params=_cparams("parallel", "parallel"),
    )(*ins)
    res = list(res)
    zo = res.pop(0) if has_y else None
    u = res.pop(0) if sidx is not None else None
    return zo, u


def _lockstep(gens):
    out = [None] * len(gens)
    live = list(range(len(gens)))
    while live:
        for i in list(live):
            try:
                next(gens[i])
            except StopIteration as done:
                out[i] = done.value
                live.remove(i)
    return out


def _gla_chunk(q, k, v, glow, wg2, bg, st, rev):
    c, dk = q.shape
    sub = GLA_SUB
    q = q.astype(F32) * (dk ** -0.5)
    k = k.astype(F32)
    gl = _dot_x3(glow, wg2) + bg
    yield
    g = _log_sigmoid(gl) * (1.0 / GLA_GATE_NORMALIZER)
    row = lax.broadcasted_iota(jnp.int32, (c, c), 0)
    col = lax.broadcasted_iota(jnp.int32, (c, c), 1)
    tri = (row <= col) if rev else (row >= col)
    b = _dot_sel(tri.astype(F32), g)
    yield
    btot = jnp.sum(g, axis=0, keepdims=True)

    o = _dot_nt((q * jnp.exp(b)).astype(BF16), st.astype(BF16))

    rowk = lax.broadcasted_iota(jnp.int32, (c, 1), 0)
    halves = []
    hsz = c // 2
    while hsz >= sub:
        halves.append(hsz)
        hsz //= 2
    pick = jnp.concatenate(
        [(col == (row // (2 * hf)) * (2 * hf) + (hf if rev else hf - 1)).astype(F32) for hf in halves], axis=0)
    refs = _dot_sel(pick, b)
    yield
    att = jnp.zeros((c, c), F32)
    for lv, hf in enumerate(halves):
        refb = refs[lv * c:(lv + 1) * c]
        late = (rowk % (2 * hf)) >= hf
        q_side, k_side = (~late, late) if rev else (late, ~late)
        qf = jnp.where(q_side, q * jnp.exp(jnp.minimum(b - refb, 0.0)), 0.0)
        kf = jnp.where(k_side, k * jnp.exp(jnp.minimum(refb - b, 0.0)), 0.0)
        same = (row // (2 * hf)) == (col // (2 * hf))
        att = att + jnp.where(same, _dot_nt(qf.astype(BF16), kf.astype(BF16)), 0.0)
    yield

    rmod = rowk % sub
    for lag in range(sub):
        sh = (c - lag) % c if rev else lag
        ks = pltpu.roll(k, sh, 0) if sh else k
        bs = pltpu.roll(b, sh, 0) if sh else b
        term = jnp.sum(q * ks * jnp.exp(jnp.minimum(b - bs, 0.0)), axis=1, keepdims=True)
        valid = (rmod + lag < sub) if rev else (rmod >= lag)
        hit = (col == row + lag) if rev else (col == row - lag)
        att = att + jnp.where(hit & valid, term, 0.0)

    o = o + _dot(att.astype(BF16), v)
    yield
    kd = (k * jnp.exp(btot - b)).astype(BF16)
    return o, st * jnp.exp(btot) + _dot_tn(v, kd)


def _gla_body(*refs):
    ins, (of_ref, ob_ref, st_ref) = refs[:12], refs[12:]

    @pl.when(pl.program_id(2) == 0)
    def _():
        st_ref[...] = jnp.zeros_like(st_ref)

    dv = st_ref.shape[2]
    dk = st_ref.shape[3]
    gens, dest = [], []
    for dr in range(2):
        q_ref, k_ref, v_ref, gl_ref, wg2_ref, bg_ref = ins[6 * dr:6 * dr + 6]
        for hh in range(st_ref.shape[1]):
            ks, vs = slice(hh * dk, (hh + 1) * dk), slice(hh * dv, (hh + 1) * dv)
            gens.append(_gla_chunk(q_ref[0, :, ks], k_ref[0, :, ks], v_ref[0, :, vs], gl_ref[0],
                                   wg2_ref[:, ks], bg_ref[:, ks], st_ref[dr, hh], rev=(dr == 1)))
            dest.append((dr, hh, vs))
    for (dr, hh, vs), (o, st) in zip(dest, _lockstep(gens)):
        (of_ref, ob_ref)[dr][0, :, vs] = o.astype(of_ref.dtype)
        st_ref[dr, hh] = st


def gla_scan(p, glow, wg2, bg):
    bsz, t, d3 = p.shape
    d = d3 // 3
    h = GLA_HEADS
    dk, dv = (d // 2) // h, d // h
    c = GLA_CHUNK
    nc, ncx = t // c, CTX_LEN // c

    def rchunk(s):
        return jnp.where(s < ncx, ncx - 1 - s, nc - 1 - (s - ncx))

    hp = _pick(h, (GLA_HEADS_PER_STEP, 1))
    hg = h // hp

    def specs(chunk):
        return [pl.BlockSpec((1, c, hp * dk), lambda b, hh, s: (b, chunk(s), hh)),
                pl.BlockSpec((1, c, hp * dk), lambda b, hh, s: (b, chunk(s), hg + hh)),
                pl.BlockSpec((1, c, hp * dv), lambda b, hh, s: (b, chunk(s), hg + hh)),
                pl.BlockSpec((1, c, LANES), lambda b, hh, s: (b, chunk(s), 0)),
                pl.BlockSpec((LANES, hp * dk), lambda b, hh, s: (0, hh)),
                pl.BlockSpec((1, hp * dk), lambda b, hh, s: (0, hh))]

    sh = jax.ShapeDtypeStruct((bsz, t, d), BF16)
    return pl.pallas_call(
        _gla_body,
        grid=(bsz, hg, nc),
        in_specs=specs(lambda s: s) + specs(rchunk),
        out_specs=[pl.BlockSpec((1, c, hp * dv), lambda b, hh, s: (b, s, hh)),
                   pl.BlockSpec((1, c, hp * dv), lambda b, hh, s: (b, rchunk(s), hh))],
        out_shape=[sh, sh],
        scratch_shapes=[pltpu.VMEM((2, hp, dv, dk), F32)],
        compiler_params=_cparams("parallel", "parallel", "arbitrary"),
    )(p, p, p, glow, wg2[0], bg[0], p, p, p, glow, wg2[1], bg[1])


def _gla_gate_body(of_ref, ob_ref, r_ref, g_ref, o_ref):
    dv = g_ref.shape[1]
    o = of_ref[0].astype(F32) + ob_ref[0].astype(F32)
    r = r_ref[0].astype(F32)
    outs = []
    for hh in range(o.shape[1] // dv):
        seg = o[:, hh * dv:(hh + 1) * dv]
        outs.append(seg * lax.rsqrt(jnp.mean(seg * seg, -1, keepdims=True) + 1e-6) * g_ref[...])
    o_ref[0] = (jnp.concatenate(outs, axis=1) * (r * _sigmoid(r))).astype(o_ref.dtype)


def gla_gate(o_f, o_b, p, norm_g):
    bsz, t, d = o_f.shape
    tr = _pick(t, (256, 128, 64, 32, 16))
    row = pl.BlockSpec((1, tr, d), lambda b, i: (b, i, 0))
    return pl.pallas_call(
        _gla_gate_body, grid=(bsz, t // tr),
        in_specs=[row, row, pl.BlockSpec((1, tr, d), lambda b, i: (b, i, 2)),
                  pl.BlockSpec((1, norm_g.shape[1]), lambda b, i: (0, 0))],
        out_specs=row, out_shape=jax.ShapeDtypeStruct((bsz, t, d), BF16),
        compiler_params=_cparams("parallel", "parallel"),
    )(o_f, o_b, p, norm_g)


def gla_mixer(u, w_in, w_g1, w_g2, b_g, norm_g, w_o):
    bsz, t, d = u.shape
    qk = d // 2
    u2 = u.reshape(bsz * t, d)
    p = matmul(u2, w_in.astype(BF16), BF16).reshape(bsz, t, 3 * d)
    r = GLA_GATE_RANK
    wg1 = jnp.zeros((d, LANES), F32).at[:, :r].set(w_g1[0]).at[:, r:2 * r].set(w_g1[1])
    glow = matmul(u2, wg1.astype(BF16), F32).reshape(bsz, t, LANES)
    wg2 = [jnp.zeros((LANES, qk), F32).at[s * r:(s + 1) * r].set(w_g2[s]) for s in range(2)]
    outs = gla_scan(p, glow, wg2, [b_g[0][None], b_g[1][None]])
    gated = gla_gate(outs[0], outs[1], p, norm_g[None])
    return matmul(gated.reshape(bsz * t, d), w_o.astype(BF16), BF16).reshape(bsz, t, d)


def _rms(x, gain):
    return x * lax.rsqrt(jnp.mean(x * x, -1, keepdims=True) + 1e-6) * gain


def _rope128(a, c, s):
    return a * c + pltpu.roll(a, LANES // 2, 1) * s


def _mla_norm_body(h_ref, qg_ref, kg_ref, c_ref, s_ref, cq_ref, ckv_ref, kr_ref):
    qr, kvr = cq_ref.shape[2], ckv_ref.shape[2]
    h = h_ref[0]
    cq_ref[0] = _rms(h[:, :qr], qg_ref[...]).astype(BF16)
    ckv_ref[0] = _rms(h[:, qr:qr + kvr], kg_ref[...]).astype(BF16)
    kr_ref[0] = _rope128(h[:, qr + kvr:qr + kvr + LANES], c_ref[...], s_ref[...]).astype(BF16)


def _mla_attn_body(q_ref, kn_ref, v_ref, kr_ref, c_ref, s_ref, o_ref, kf_ref, vf_ref, *, nctx_tiles, scale):
    qt = pl.program_id(2)
    t = kf_ref.shape[0]
    ctx = nctx_tiles * q_ref.shape[1]

    @pl.when(qt == 0)
    def _():
        kf_ref[:, :LANES] = kn_ref[0]
        kf_ref[:, LANES:] = kr_ref[0]
        vf_ref[:, :LANES] = v_ref[0]
        vf_ref[:, LANES:] = jnp.ones((t, LANES), BF16)

    q = q_ref[0].astype(F32)
    qr = _rope128(q[:, LANES:], c_ref[...], s_ref[...])
    qf = (jnp.concatenate([q[:, :LANES], qr], axis=1) * (scale * LOG2E)).astype(BF16)

    def attend(keys):
        s = _dot_nt(qf, kf_ref[keys])
        yield
        m = jnp.max(s, -1, keepdims=True)
        p = jnp.exp2(s - m).astype(BF16)
        return m, _dot(p, vf_ref[keys])

    def finish(parts):
        m = functools.reduce(jnp.maximum, [pm for pm, _ in parts])
        acc = sum(jnp.exp2(pm - m) * po for pm, po in parts)
        o_ref[0] = (acc[:, :LANES] / acc[:, LANES:LANES + 1]).astype(o_ref.dtype)

    @pl.when(qt < nctx_tiles)
    def _():
        finish(_lockstep([attend(slice(0, ctx))]))

    @pl.when(qt >= nctx_tiles)
    def _():
        step = t // MLA_KEY_CHAINS
        finish(_lockstep([attend(slice(k0, k0 + step)) for k0 in range(0, t, step)]))


def _rope_tables(t):
    rows = (t - CTX_LEN) // GRID_W
    row = jnp.repeat(jnp.arange(rows, dtype=F32), GRID_W)
    colp = jnp.tile(jnp.arange(GRID_W, dtype=F32), rows)
    n_freq = MLA_ROPE // 4
    inv = ROPE_THETA ** (-jnp.arange(n_freq, dtype=F32) / n_freq)
    ang = jnp.concatenate([row[:, None] * inv, colp[:, None] * inv], -1)
    cos = jnp.concatenate([jnp.ones((CTX_LEN, MLA_ROPE // 2), F32), jnp.cos(ang)], 0)
    sin = jnp.concatenate([jnp.zeros((CTX_LEN, MLA_ROPE // 2), F32), jnp.sin(ang)], 0)
    z = jnp.zeros_like(cos)
    return jnp.concatenate([cos, cos, z, z], 1), jnp.concatenate([-sin, sin, z, z], 1)


def mla_mixer(u, w_in, q_norm, kv_norm, w_uq, w_ukv, w_o):
    bsz, t, d = u.shape
    nh = d // 128
    qr, kvr, rp = MLA_Q_RANK, MLA_KV_RANK, MLA_ROPE
    ev, od = jnp.arange(0, rp, 2), jnp.arange(1, rp, 2)
    perm = jnp.concatenate([ev, od, od, ev])
    hw = -(-(qr + kvr + LANES) // MXU_COLS) * MXU_COLS
    w_in_p = jnp.concatenate([w_in[:, :qr + kvr], w_in[:, qr + kvr + perm],
                              jnp.zeros((d, hw - (qr + kvr + LANES)), F32)], axis=1)
    h = matmul(u.reshape(bsz * t, d), w_in_p.astype(BF16), F32).reshape(bsz, t, hw)
    ctab, stab = _rope_tables(t)

    tr = _pick(t, (256, 128, 64, 32, 16))
    cq, ckv, kr = pl.pallas_call(
        _mla_norm_body, grid=(bsz, t // tr),
        in_specs=[pl.BlockSpec((1, tr, hw), lambda b, i: (b, i, 0)),
                  pl.BlockSpec((1, qr), lambda b, i: (0, 0)),
                  pl.BlockSpec((1, kvr), lambda b, i: (0, 0)),
                  pl.BlockSpec((tr, LANES), lambda b, i: (i, 0)),
                  pl.BlockSpec((tr, LANES), lambda b, i: (i, 0))],
        out_specs=[pl.BlockSpec((1, tr, qr), lambda b, i: (b, i, 0)),
                   pl.BlockSpec((1, tr, kvr), lambda b, i: (b, i, 0)),
                   pl.BlockSpec((1, tr, LANES), lambda b, i: (b, i, 0))],
        out_shape=[jax.ShapeDtypeStruct((bsz, t, qr), BF16),
                   jax.ShapeDtypeStruct((bsz, t, kvr), BF16),
                   jax.ShapeDtypeStruct((bsz, t, LANES), BF16)],
        compiler_params=_cparams("parallel", "parallel"),
    )(h, q_norm[None], kv_norm[None], ctab, stab)

    hq = MLA_NOPE + rp
    qcols = (jnp.arange(nh)[:, None] * hq
             + jnp.concatenate([jnp.arange(MLA_NOPE), MLA_NOPE + perm])[None, :]).reshape(-1)
    q = matmul(cq.reshape(bsz * t, qr), w_uq[:, qcols].astype(BF16), BF16).reshape(bsz, t, nh * 2 * LANES)
    kv = matmul(ckv.reshape(bsz * t, kvr), w_ukv.astype(BF16), BF16).reshape(bsz, t, nh * 2 * LANES)

    tq = _pick(CTX_LEN, (256, 128, 64, 32, 16))
    scale = (MLA_NOPE + rp) ** -0.5
    o = pl.pallas_call(
        functools.partial(_mla_attn_body, nctx_tiles=CTX_LEN // tq, scale=scale),
        grid=(bsz, nh, t // tq),
        in_specs=[pl.BlockSpec((1, tq, 2 * LANES), lambda b, hh, i: (b, i, hh)),
                  pl.BlockSpec((1, t, LANES), lambda b, hh, i: (b, 0, 2 * hh)),
                  pl.BlockSpec((1, t, LANES), lambda b, hh, i: (b, 0, 2 * hh + 1)),
                  pl.BlockSpec((1, t, LANES), lambda b, hh, i: (b, 0, 0)),
                  pl.BlockSpec((tq, LANES), lambda b, hh, i: (i, 0)),
                  pl.BlockSpec((tq, LANES), lambda b, hh, i: (i, 0))],
        out_specs=pl.BlockSpec((1, tq, LANES), lambda b, hh, i: (b, i, hh)),
        out_shape=jax.ShapeDtypeStruct((bsz, t, nh * MLA_V), BF16),
        scratch_shapes=[pltpu.VMEM((t, 2 * LANES), BF16), pltpu.VMEM((t, 2 * LANES), BF16)],
        compiler_params=_cparams("parallel", "parallel", "arbitrary"),
    )(q, kv, kv, kr, ctab, stab)
    return matmul(o.reshape(bsz * t, nh * MLA_V), w_o.astype(BF16), BF16).reshape(bsz, t, d)


def _rwkv_mix_body(u_ref, before_ref, after_ref, mu_ref, *o_refs, nctx_tiles):
    i = pl.program_id(1)
    tr = u_ref.shape[1]
    hr = before_ref.shape[1]
    u = u_ref[0].astype(F32)
    row = lax.broadcasted_iota(jnp.int32, (tr, 1), 0)
    starts = (i == 0) | (i == nctx_tiles)
    ends = (i == nctx_tiles - 1) | (i == pl.num_programs(1) - 1)
    edge_prev = jnp.where(starts, 0.0, before_ref[0, hr - 1:hr, :].astype(F32))
    edge_next = jnp.where(ends, 0.0, after_ref[0, 0:1, :].astype(F32))
    prev = jnp.where(row == 0, edge_prev, pltpu.roll(u, 1, 0))
    nxt = jnp.where(row == tr - 1, edge_next, pltpu.roll(u, tr - 1, 0))
    xx = 0.5 * (prev + nxt) - u
    for n, o_ref in enumerate(o_refs):
        o_ref[0] = (u + xx * mu_ref[n:n + 1, :]).astype(o_ref.dtype)


def _head_sum(x, e):
    eb = e.astype(BF16)
    outs = []
    for s in range(x.shape[1] // LANES):
        xs = x[:, s * LANES:(s + 1) * LANES]
        hi = xs.astype(BF16)
        lo = (xs - hi.astype(F32)).astype(BF16)
        outs.append(_dot(hi, eb) + _dot(lo, eb))
    return jnp.concatenate(outs, axis=1)


def _head_ones():
    r = lax.broadcasted_iota(jnp.int32, (LANES, LANES), 0) // RWKV_HEAD
    c = lax.broadcasted_iota(jnp.int32, (LANES, LANES), 1) // RWKV_HEAD
    return (r == c).astype(F32)


def _rwkv_prep_body(k_ref, hw_ref, ha_ref, w2_ref, a2_ref, w0_ref, a0_ref, kk_ref, ka_ref,
                    na_ref, lw0_ref, lw1_ref, ks0_ref, ks1_ref, bb0_ref, bb1_ref):
    rk = w2_ref.shape[1]
    k = k_ref[0]
    kk = k * kk_ref[...]
    kk = kk * lax.rsqrt(_head_sum(kk * kk, _head_ones()) + 1e-12)
    na_ref[0] = (-kk).astype(na_ref.dtype)
    hw = jnp.tanh(hw_ref[0])
    ha = ha_ref[0]
    for s, (lw_ref, ks_ref, bb_ref) in enumerate(((lw0_ref, ks0_ref, bb0_ref), (lw1_ref, ks1_ref, bb1_ref))):
        wl = w0_ref[s:s + 1, :] + _dot(hw[:, s * rk:(s + 1) * rk].astype(BF16), w2_ref[s])
        lw_ref[0] = (-RWKV_DECAY_SCALE) * _sigmoid(wl)
        a = _sigmoid(a0_ref[s:s + 1, :] + _dot(ha[:, s * rk:(s + 1) * rk].astype(BF16), a2_ref[s]))
        ks_ref[0] = (k * (1.0 + (a - 1.0) * ka_ref[...])).astype(ks_ref.dtype)
        bb_ref[0] = (kk * a).astype(bb_ref.dtype)


def _rwkv_chunk(r, lw, ks, v, na, bb, ht, rev):
    c = r.shape[0]
    n = RWKV_HEAD
    c2 = 2 * c
    r, ks, v, na, bb = (x.astype(F32) for x in (r, ks, v, na, bb))
    ri = lax.broadcasted_iota(jnp.int32, (c, c), 0)
    ci = lax.broadcasted_iota(jnp.int32, (c, c), 1)
    cs = _dot_sel(((ri <= ci) if rev else (ri >= ci)).astype(F32), lw)
    yield
    cm = cs - lw
    ctot = jnp.sum(lw, axis=0, keepdims=True)

    lane_a = lax.broadcasted_iota(jnp.int32, (1, LANES), 1) < n

    def stack(x):
        return jnp.concatenate([jnp.where(lane_a, x, 0.0), jnp.where(lane_a, 0.0, x)], axis=0)

    at = stack(na * jnp.exp(cm))
    rt = stack(r * jnp.exp(cs))
    ecs = jnp.exp(-cs)
    bh = stack(bb * ecs)
    kh = stack(ks * ecs)
    ece = jnp.exp(ctot - cs)
    be = stack(bb * ece)
    ke = stack(ks * ece)
    vs = stack(v)

    rr = lax.broadcasted_iota(jnp.int32, (c2, c2), 0)
    cc = lax.broadcasted_iota(jnp.int32, (c2, c2), 1)
    same = (rr // c) == (cc // c)
    tr_, tc_ = rr % c, cc % c
    strict = same & ((tc_ > tr_) if rev else (tc_ < tr_))
    incl = same & ((tc_ >= tr_) if rev else (tc_ <= tr_))
    diag_blk = same & ((tr_ // RWKV_SUB) == (tc_ // RWKV_SUB))

    lhs = jnp.concatenate([at, rt], axis=0)
    rhs = jnp.concatenate([bh, kh], axis=0)
    a4 = _dot_nt(lhs.astype(BF16), rhs.astype(BF16))
    yield
    nmat = jnp.where(strict, a4[:c2, :c2], 0.0)
    aak = jnp.where(strict, a4[:c2, c2:], 0.0)
    arb = jnp.where(incl, a4[c2:, :c2], 0.0)
    ark = jnp.where(incl, a4[c2:, c2:], 0.0)

    hb = ht.T.astype(BF16)
    vb = vs.astype(BF16)
    x0 = _dot(jnp.concatenate([at, aak], axis=1).astype(BF16), jnp.concatenate([hb, vb], axis=0))
    yield

    nd = jnp.where(diag_blk, nmat, 0.0)
    no = nmat - nd
    tm = nd
    pw = nd
    for _ in range(3):
        pw = _dot_b(pw, pw)
        yield
        tm = tm + pw + _dot_b(pw, tm)
        yield
    zu = _dot_b(tm, jnp.concatenate([no, x0], axis=1))
    z = no + zu[:, :c2]
    u = x0 + zu[:, c2:]
    yield
    zz = _dot_b(z, jnp.concatenate([z, u], axis=1))
    u = u + zz[:, c2:]
    yield
    u = u + _dot_b(zz[:, :c2], u)
    yield

    ub = u.astype(BF16)
    ys = _dot(jnp.concatenate([rt, arb, ark], axis=1).astype(BF16), jnp.concatenate([hb, ub, vb], axis=0))
    yield
    hn = ht * jnp.exp(ctot) + _dot_tn(jnp.concatenate([ub, vb], axis=0),
                                      jnp.concatenate([be, ke], axis=0).astype(BF16))
    hr = lax.broadcasted_iota(jnp.int32, (LANES, LANES), 0) // n
    hc = lax.broadcasted_iota(jnp.int32, (LANES, LANES), 1) // n
    return ys[:c] + ys[c:], jnp.where(hr == hc, hn, 0.0)


def _rwkv_scan_body(*refs):
    ins, (y0_ref, y1_ref, st_ref) = refs[:12], refs[12:]

    @pl.when(pl.program_id(2) == 0)
    def _():
        st_ref[...] = jnp.zeros_like(st_ref)

    gens, dest = [], []
    for dr in range(2):
        r_ref, lw_ref, ks_ref, v_ref, na_ref, bb_ref = ins[6 * dr:6 * dr + 6]
        for p in range(r_ref.shape[2] // LANES):
            sl = slice(p * LANES, (p + 1) * LANES)
            gens.append(_rwkv_chunk(r_ref[0, :, sl], lw_ref[0, :, sl], ks_ref[0, :, sl], v_ref[0, :, sl],
                                    na_ref[0, :, sl], bb_ref[0, :, sl], st_ref[dr, p], rev=(dr == 1)))
            dest.append(((y0_ref, y1_ref)[dr], sl, dr, p))
    for (y_ref, sl, dr, p), (y, hn) in zip(dest, _lockstep(gens)):
        y_ref[0, :, sl] = y
        st_ref[dr, p] = hn


def rwkv_scan(r, v, na, lw, ks, bb):
    bsz, t, d = r.shape
    c = RWKV_CHUNK
    nc, ncx = t // c, CTX_LEN // c
    pp = _pick(d // LANES, (4, 2, 1))
    w = pp * LANES

    def rchunk(s):
        return jnp.where(s < ncx, ncx - 1 - s, nc - 1 - (s - ncx))

    fwd = pl.BlockSpec((1, c, w), lambda b, p, s: (b, s, p))
    bwd = pl.BlockSpec((1, c, w), lambda b, p, s: (b, rchunk(s), p))
    sh = jax.ShapeDtypeStruct((bsz, t, d), F32)
    return pl.pallas_call(
        _rwkv_scan_body,
        grid=(bsz, d // w, nc),
        in_specs=[fwd] * 6 + [bwd] * 6, out_specs=[fwd, bwd], out_shape=[sh, sh],
        scratch_shapes=[pltpu.VMEM((2, pp, LANES, LANES), F32)],
        compiler_params=_cparams("parallel", "parallel", "arbitrary"),
    )(r, lw[0], ks[0], v, na, bb[0], r, lw[1], ks[1], v, na, bb[1])


def _rwkv_out_body(y0_ref, y1_ref, r_ref, v_ref, g_ref, ks0_ref, ks1_ref, rk_ref, lg_ref, lb_ref, o_ref):
    e = _head_ones()
    y = y0_ref[0] + y1_ref[0]
    inv_n = 1.0 / RWKV_HEAD
    yc = y - _head_sum(y, e) * inv_n
    yn = yc * lax.rsqrt(_head_sum(yc * yc, e) * inv_n + RWKV_GN_EPS)
    yn = yn * lg_ref[...] + lb_ref[...]
    ksum = ks0_ref[0].astype(F32) + ks1_ref[0].astype(F32)
    bonus = _head_sum(r_ref[0].astype(F32) * ksum * rk_ref[...], e) * v_ref[0].astype(F32)
    o_ref[0] = ((yn + bonus) * g_ref[0].astype(F32)).astype(o_ref.dtype)


def rwkv_mixer(u, mu, w_r, w_k, w_v, w_o, w0, w1, w2, a0, a1, a2, g1, g2, k_k, k_a, r_k, ln_g, ln_b):
    bsz, t, d = u.shape
    m = bsz * t
    tm_ = _pick(CTX_LEN, (128, 64, 32, 16))
    hr = 16
    rpt, last = tm_ // hr, t // hr - 1
    tile = pl.BlockSpec((1, tm_, d), lambda b, i: (b, i, 0))
    xs = pl.pallas_call(
        functools.partial(_rwkv_mix_body, nctx_tiles=CTX_LEN // tm_), grid=(bsz, t // tm_),
        in_specs=[tile,
                  pl.BlockSpec((1, hr, d), lambda b, i: (b, jnp.maximum(i * rpt - 1, 0), 0)),
                  pl.BlockSpec((1, hr, d), lambda b, i: (b, jnp.minimum((i + 1) * rpt, last), 0)),
                  pl.BlockSpec((6, d), lambda b, i: (0, 0))],
        out_specs=[tile] * 6,
        out_shape=[jax.ShapeDtypeStruct((bsz, t, d), BF16)] * 6,
        compiler_params=_cparams("parallel", "parallel"),
    )(u, u, u, mu)
    xr, xw, xk, xv, xa, xg = (x.reshape(m, d) for x in xs)
    r = matmul(xr, w_r.astype(BF16), BF16).reshape(bsz, t, d)
    k = matmul(xk, w_k.astype(BF16)).reshape(bsz, t, d)
    v = matmul(xv, w_v.astype(BF16), BF16).reshape(bsz, t, d)
    rk = w1.shape[2]
    hw = matmul(xw, jnp.concatenate([w1[0], w1[1]], 1).astype(BF16)).reshape(bsz, t, 2 * rk)
    ha = matmul(xa, jnp.concatenate([a1[0], a1[1]], 1).astype(BF16)).reshape(bsz, t, 2 * rk)
    gr = g1.shape[1]
    grp = -(-gr // LANES) * LANES
    g1p = jnp.zeros((d, grp), F32).at[:, :gr].set(g1)
    g2p = jnp.zeros((grp, d), F32).at[:gr].set(g2)
    hg = jax.nn.sigmoid(matmul(xg, g1p.astype(BF16)))
    g = matmul(hg, g2p.astype(BF16), BF16).reshape(bsz, t, d)

    tr = _pick(t, (64, 32, 16))
    row = pl.BlockSpec((1, tr, d), lambda b, i: (b, i, 0))
    low = pl.BlockSpec((1, tr, 2 * rk), lambda b, i: (b, i, 0))
    vec = pl.BlockSpec((1, d), lambda b, i: (0, 0))
    vec2 = pl.BlockSpec((2, d), lambda b, i: (0, 0))
    fact = pl.BlockSpec((2, rk, d), lambda b, i: (0, 0, 0))
    big = jax.ShapeDtypeStruct((bsz, t, d), F32)
    half = jax.ShapeDtypeStruct((bsz, t, d), BF16)
    na, lw0, lw1, ks0, ks1, bb0, bb1 = pl.pallas_call(
        _rwkv_prep_body, grid=(bsz, t // tr),
        in_specs=[row, low, low, fact, fact, vec2, vec2, vec, vec],
        out_specs=[row] * 7, out_shape=[half, big, big, half, half, half, half],
        compiler_params=_cparams("parallel", "parallel"),
    )(k, hw, ha, w2.astype(BF16), a2.astype(BF16), w0, a0, k_k[None], k_a[None])

    y0, y1 = rwkv_scan(r, v, na, (lw0, lw1), (ks0, ks1), (bb0, bb1))

    out = pl.pallas_call(
        _rwkv_out_body, grid=(bsz, t // tr),
        in_specs=[row] * 7 + [vec, vec, vec], out_specs=row,
        out_shape=jax.ShapeDtypeStruct((bsz, t, d), BF16),
        compiler_params=_cparams("parallel", "parallel"),
    )(y0, y1, r, v, g, ks0, ks1, r_k.reshape(1, d), ln_g[None], ln_b[None])
    return matmul(out.reshape(m, d), w_o.astype(BF16), BF16).reshape(bsz, t, d)


def _merge_exchange(n):
    t = max(1, (n - 1).bit_length())
    p = 1 << (t - 1)
    pairs = []
    while p > 0:
        q, r, d = 1 << (t - 1), 0, p
        while d > 0:
            pairs += [(i, i + d) for i in range(n - d) if (i & p) == r]
            d, q, r = q - p, q >> 1, p
        p >>= 1
    return pairs


def _pop_heads(levels, kk):
    out = []
    for k in range(kk):
        head = levels[0]
        m = jnp.max(head, axis=0, keepdims=True)
        out.append(m)
        need = min(len(levels), kk - k - 1)
        pop = head == m
        levels = [jnp.where(pop, levels[r + 1] if r + 1 < len(levels) else NEG, levels[r]) for r in range(need)]
    return jnp.concatenate(out, axis=0)


def _top_vals(s, kk):
    groups = [s[r:r + SUBLANES] for r in range(0, s.shape[0], SUBLANES)]
    for i, j in _merge_exchange(len(groups)):
        groups[i], groups[j] = jnp.maximum(groups[i], groups[j]), jnp.minimum(groups[i], groups[j])
    return _pop_heads(groups, kk)


def _peer_score_body(q_ref, keys_ref, thr_ref, s2_ref, e2_ref, cf_ref):
    nk = PEER_NKEYS
    kk = PEER_TOPK
    for h in range(PEER_HEADS):
        q1 = q_ref[h * 2 * nk:h * 2 * nk + nk, :].astype(BF16)
        q2 = q_ref[h * 2 * nk + nk:(h + 1) * 2 * nk, :].astype(BF16)
        s1 = _dot(keys_ref[0], q1)
        s2 = _dot(keys_ref[1], q2)
        a1 = _top_vals(s1, kk + 1)
        a2 = _top_vals(s2, kk + 1)
        cv = _pop_heads([a1[:kk] + a2[q:q + 1] for q in range(kk)], kk + 1)
        c17 = jnp.maximum(cv[kk:kk + 1], jnp.maximum(a1[kk:kk + 1] + a2[0:1], a1[0:1] + a2[kk:kk + 1]))
        theta = 0.5 * (cv[kk - 1:kk] + c17)
        zsum = jnp.sum(jnp.exp(cv[:kk] - cv[0:1]), axis=0, keepdims=True)
        thr_ref[h] = theta - s1
        s2_ref[h] = s2
        e2_ref[h] = jnp.exp(s2 - a2[0:1])
        cf_ref[h] = jnp.exp(s1 - a1[0:1]) / zsum


def _gelu(x):
    return 0.5 * x * (1.0 + lax.erf(x * (2.0 ** -0.5)))


def _peer_dense_body(z_ref, u_ref, v_ref, thr_ref, s2_ref, e2_ref, cf_ref, o_ref, w_ref):
    nk = PEER_NKEYS
    eb = pl.program_id(1)
    te = u_ref.shape[0]
    n_i = te // nk
    strip = PEER_ROW_STRIP
    stages = PEER_STAGES
    sub = min(PEER_TOK_SUB, z_ref.shape[1])
    d = z_ref.shape[0]
    kc = d // stages
    tiles = [slice(t0, t0 + sub) for t0 in range(0, z_ref.shape[1], sub)]

    @pl.when(eb == 0)
    def _():
        o_ref[...] = jnp.zeros_like(o_ref)

    def first_matmul(ts):
        acc = None
        for k0 in range(0, d, kc):
            part = _dot(u_ref[:, k0:k0 + kc], z_ref[k0:k0 + kc, ts])
            acc = part if acc is None else acc + part
            yield
        return acc

    def gates(ts):
        per_stage = (nk // strip) * PEER_HEADS // stages
        n = 0
        for si in range(nk // strip):
            js = slice(si * strip, (si + 1) * strip)
            w = [None] * n_i
            for h in range(PEER_HEADS):
                s2s = s2_ref[h, js, ts]
                e2s = e2_ref[h, js, ts]
                for ii in range(n_i):
                    i = eb * n_i + ii
                    c = jnp.where(s2s >= thr_ref[h, pl.ds(i, 1), ts], e2s, 0.0) * cf_ref[h, pl.ds(i, 1), ts]
                    w[ii] = c if w[ii] is None else w[ii] + c
                n += 1
                if n % per_stage == 0 and h + 1 < PEER_HEADS:
                    yield
            for ii in range(n_i):
                w_ref[ii * nk + si * strip:ii * nk + (si + 1) * strip, ts] = w[ii].astype(BF16)
            yield

    def activate(ts, act):
        rows = te // stages
        for r0 in range(0, te, rows):
            w_ref[r0:r0 + rows, ts] = w_ref[r0:r0 + rows, ts] * _gelu(act[r0:r0 + rows]).astype(BF16)
            yield

    def second_matmul(ts):
        rows = d // stages
        for r0 in range(0, d, rows):
            o_ref[r0:r0 + rows, ts] += _dot(v_ref[r0:r0 + rows, :], w_ref[:, ts])
            yield

    acts = {}
    for ph in range(len(tiles) + 2):
        gens, tags = [], []
        if ph < len(tiles):
            gens += [first_matmul(tiles[ph]), gates(tiles[ph])]
            tags += [ph, None]
        if 0 <= ph - 1 < len(tiles):
            gens.append(activate(tiles[ph - 1], acts[ph - 1]))
            tags.append(None)
        if 0 <= ph - 2 < len(tiles):
            gens.append(second_matmul(tiles[ph - 2]))
            tags.append(None)
        for tag, res in zip(tags, _lockstep(gens)):
            if tag is not None:
                acts[tag] = res


def peer_ffn(zt, w_q, sub_keys, u_tab, v_tab):
    d, m = zt.shape
    nh, nk = PEER_HEADS, PEER_NKEYS
    qt = matmul(w_q.T.astype(BF16), zt, F32)
    tt = _pick(m, (256, 128))
    sh = jax.ShapeDtypeStruct((nh, nk, m), F32)
    blk = pl.BlockSpec((nh, nk, tt), lambda i: (0, 0, i))
    thr, s2, e2, cf = pl.pallas_call(
        _peer_score_body, grid=(m // tt,),
        in_specs=[pl.BlockSpec((nh * 2 * nk, tt), lambda i: (0, i)),
                  pl.BlockSpec((2, nk, PEER_DKEY // 2), lambda i: (0, 0, 0))],
        out_specs=[blk, blk, blk, blk], out_shape=[sh, sh, sh, sh],
        compiler_params=_cparams("parallel"),
    )(qt, sub_keys.astype(BF16))

    tm = _pick(m, (2 * PEER_TOK_SUB, PEER_TOK_SUB, LANES))
    te = PEER_EXP_BLOCK
    ne = u_tab.shape[0]
    once = pl.Buffered(1)
    sblk = pl.BlockSpec((nh, nk, tm), lambda i, e: (0, 0, i), pipeline_mode=once)
    return pl.pallas_call(
        _peer_dense_body, grid=(m // tm, ne // te),
        in_specs=[pl.BlockSpec((d, tm), lambda i, e: (0, i), pipeline_mode=once),
                  pl.BlockSpec((te, d), lambda i, e: (e, 0)),
                  pl.BlockSpec((d, te), lambda i, e: (0, e)),
                  sblk, sblk, sblk, sblk],
        out_specs=pl.BlockSpec((d, tm), lambda i, e: (0, i)),
        out_shape=jax.ShapeDtypeStruct((d, m), F32),
        scratch_shapes=[pltpu.VMEM((te, tm), BF16)],
        compiler_params=_cparams("parallel", "arbitrary"),
    )(zt, u_tab.astype(BF16), v_tab.T.astype(BF16), thr, s2, e2, cf)


def _ada_mods(c, c_ctx, w_down, w_up, b_up):
    bsz, d = c.shape
    cond = jnp.concatenate([c, c_ctx[None]], 0)
    pad = 16 - cond.shape[0] % 16
    cond = jnp.concatenate([cond, jnp.zeros((pad, d), F32)], 0)
    hid = matmul(jax.nn.silu(cond), w_down.astype(BF16), F32)
    m = (matmul(hid, w_up.astype(BF16), F32) + b_up)[:bsz + 1].reshape(bsz + 1, N_MOD, d)
    return jnp.stack([jnp.broadcast_to(m[bsz], (bsz, N_MOD, d)), m[:bsz]], axis=1)


def kernel(x, c, ctx, c_ctx, ada_w_down, ada_w_up, ada_b,
           gla_w_in, gla_w_g1, gla_w_g2, gla_b_g, gla_norm_g, gla_w_o,
           mla_w_in, mla_q_norm, mla_kv_norm, mla_w_uq, mla_w_ukv, mla_w_o,
           rwkv_mu, rwkv_w_r, rwkv_w_k, rwkv_w_v, rwkv_w_o, rwkv_w0, rwkv_w1, rwkv_w2,
           rwkv_a0, rwkv_a1, rwkv_a2, rwkv_g1, rwkv_g2, rwkv_k_k, rwkv_k_a, rwkv_r_k, rwkv_ln_g, rwkv_ln_b,
           peer_w_q, peer_sub_keys, peer_u, peer_v):
    z = jnp.concatenate([ctx, x], axis=1)
    bsz, t, d = z.shape
    mods = [_ada_mods(c, c_ctx, ada_w_down[i], ada_w_up[i], ada_b[i]) for i in range(DEPTH)]
    _, u = ln_mod(z, mods_n=mods[0], sidx=0)
    for i in range(DEPTH):
        j = i // N_MIXERS
        if i % N_MIXERS == 0:
            y = gla_mixer(u, gla_w_in[j], gla_w_g1[j], gla_w_g2[j], gla_b_g[j], gla_norm_g[j], gla_w_o[j])
        elif i % N_MIXERS == 1:
            y = mla_mixer(u, mla_w_in[j], mla_q_norm[j], mla_kv_norm[j], mla_w_uq[j], mla_w_ukv[j], mla_w_o[j])
        else:
            y = rwkv_mixer(u, rwkv_mu[j], rwkv_w_r[j], rwkv_w_k[j], rwkv_w_v[j], rwkv_w_o[j],
                           rwkv_w0[j], rwkv_w1[j], rwkv_w2[j], rwkv_a0[j], rwkv_a1[j], rwkv_a2[j],
                           rwkv_g1[j], rwkv_g2[j], rwkv_k_k[j], rwkv_k_a[j], rwkv_r_k[j],
                           rwkv_ln_g[j], rwkv_ln_b[j])
        z, ut = ln_mod(z, y, mods[i], 2, mods[i], 3, u_t=True)
        ht = peer_ffn(ut, peer_w_q[i], peer_sub_keys[i], peer_u[i], peer_v[i])
        if i + 1 < DEPTH:
            z, u = ln_mod(z, ht, mods[i], 5, mods[i + 1], 0, y_t=True)
        else:
            z, _ = ln_mod(z, ht, mods[i], 5, y_t=True, latent_only=True)
    return z
```

```python
import functools

import jax
import jax.numpy as jnp
from jax import lax
from jax.experimental import pallas as pl
from jax.experimental.pallas import tpu as pltpu

F32 = jnp.float32
BF16 = jnp.bfloat16

DEPTH = 4
CTX_LEN = 256
GRID_W = 64
N_MIXERS = 3
N_MOD = 6
LN_EPS = 1e-6
DEEPNORM_ALPHA = (2.0 * DEPTH) ** 0.25

GLA_HEADS = 8
GLA_GATE_RANK = 16
GLA_GATE_NORMALIZER = 16.0

MLA_Q_RANK = 1536
MLA_KV_RANK = 512
MLA_NOPE = 128
MLA_ROPE = 64
MLA_V = 128
ROPE_THETA = 10000.0

RWKV_HEAD = 64
RWKV_GN_EPS = 64e-5
RWKV_DECAY_SCALE = 0.6065306597126334

PEER_HEADS = 8
PEER_NKEYS = 128
PEER_DKEY = 256
PEER_TOPK = 16

LANES = 128
SUBLANES = 8
MXU_COLS = 256
VMEM_LIMIT = 52 * 1024 * 1024

GLA_CHUNK = 128
GLA_HEADS_PER_STEP = 2
GLA_SUB = 4
RWKV_CHUNK = 64
RWKV_SUB = 16
MLA_KEY_CHAINS = 2
LOG2E = 1.4426950408889634
PEER_TOK_SUB = 256
PEER_EXP_BLOCK = 512
PEER_STAGES = 4
PEER_ROW_STRIP = 32
NEG = -1e30


def _pick(n, cands):
    for c in cands:
        if n % c == 0:
            return c
    return n


def _cparams(*sem):
    return pltpu.CompilerParams(dimension_semantics=sem, vmem_limit_bytes=VMEM_LIMIT)


def _dot(a, b):
    return jnp.dot(a, b, preferred_element_type=F32)


def _dot_nt(a, b):
    return lax.dot_general(a, b, (((1,), (1,)), ((), ())), preferred_element_type=F32)


def _dot_tn(a, b):
    return lax.dot_general(a, b, (((0,), (0,)), ((), ())), preferred_element_type=F32)


def _split3(x):
    hi = x.astype(BF16)
    r1 = x - hi.astype(F32)
    mid = r1.astype(BF16)
    lo = (r1 - mid.astype(F32)).astype(BF16)
    return hi, mid, lo


def _dot_sel(sel, x):
    s = sel.astype(BF16)
    hi, mid, lo = _split3(x)
    return _dot(s, hi) + _dot(s, mid) + _dot(s, lo)


def _dot_x3(a, b, dims=None):
    ah = a.astype(BF16)
    al = (a - ah.astype(F32)).astype(BF16)
    bh = b.astype(BF16)
    bl = (b - bh.astype(F32)).astype(BF16)
    f = _dot if dims is None else dims
    return f(ah, bh) + f(ah, bl) + f(al, bh)


def _dot_b(a, b):
    return _dot(a.astype(BF16), b.astype(BF16))


def _log_sigmoid(x):
    return jnp.minimum(x, 0.0) - jnp.log1p(jnp.exp(-jnp.abs(x)))


def _sigmoid(x):
    return 1.0 / (1.0 + jnp.exp(-x))


def _mm_body(a_ref, b_ref, o_ref):
    o_ref[...] = _dot(a_ref[...].astype(BF16), b_ref[...].astype(BF16)).astype(o_ref.dtype)


def matmul(a, b, out_dtype=F32):
    m, k = a.shape
    k2, n = b.shape
    assert k == k2
    a_bytes = jnp.dtype(a.dtype).itemsize
    tm_cands = (1024, 512, 256, 128, 64, 32, 16) if a_bytes * k <= 8192 else (512, 256, 128, 64, 32, 16)
    tm = _pick(m, tm_cands)
    tn = _pick(n, (512, 256, 128))
    return pl.pallas_call(
        _mm_body,
        grid=(m // tm, n // tn),
        in_specs=[pl.BlockSpec((tm, k), lambda i, j: (i, 0)),
                  pl.BlockSpec((k, tn), lambda i, j: (0, j))],
        out_specs=pl.BlockSpec((tm, tn), lambda i, j: (i, j)),
        out_shape=jax.ShapeDtypeStruct((m, n), out_dtype),
        compiler_params=_cparams("parallel", "parallel"),
    )(a, b)


def _layer_norm(z):
    zc = z - jnp.mean(z, -1, keepdims=True)
    return zc * lax.rsqrt(jnp.mean(zc * zc, -1, keepdims=True) + LN_EPS)


def _ln_mod_body(*refs, gidx, sidx, has_y, y_t, u_t):
    it = iter(refs)
    z_ref = next(it)
    y_ref = next(it) if has_y else None
    mg_ref = next(it) if has_y else None
    mn_ref = next(it) if sidx is not None else None
    zo_ref = next(it) if has_y else None
    u_ref = next(it) if sidx is not None else None
    z = z_ref[0]
    if has_y:
        y = y_ref[...].T if y_t else y_ref[0]
        z = _layer_norm(DEEPNORM_ALPHA * z + mg_ref[0, 0, gidx:gidx + 1, :] * y)
        zo_ref[0] = z
    if sidx is not None:
        u = z * (1.0 + mn_ref[0, 0, sidx + 1:sidx + 2, :]) + mn_ref[0, 0, sidx:sidx + 1, :]
        if u_t:
            u_ref[...] = u.T.astype(u_ref.dtype)
        else:
            u_ref[0] = u.astype(u_ref.dtype)


def ln_mod(z, y=None, mods_g=None, gidx=None, mods_n=None, sidx=None, y_t=False, u_t=False, latent_only=False):
    assert not (latent_only and sidx is not None)
    bsz, t, d = z.shape
    tr = _pick(CTX_LEN, (256, 128, 64, 32, 16))
    nt = t // tr
    nctx = CTX_LEN // tr
    first = nctx if latent_only else 0
    has_y = y is not None
    row = pl.BlockSpec((1, tr, d), lambda b, i: (b, first + i, 0))
    col = pl.BlockSpec((d, tr), lambda b, i: (0, b * nt + first + i))
    mod = pl.BlockSpec((1, 1, N_MOD, d), lambda b, i: (b, jnp.where(first + i < nctx, 0, 1), 0, 0))
    ins, in_specs, outs, out_specs = [z], [row], [], []
    if has_y:
        ins += [y, mods_g]
        in_specs += [col if y_t else row, mod]
        outs.append(jax.ShapeDtypeStruct((bsz, t - first * tr, d), F32))
        out_specs.append(pl.BlockSpec((1, tr, d), lambda b, i: (b, i, 0)) if latent_only else row)
    if sidx is not None:
        ins.append(mods_n)
        in_specs.append(mod)
        outs.append(jax.ShapeDtypeStruct((d, bsz * t) if u_t else (bsz, t, d), BF16))
        out_specs.append(col if u_t else row)
    res = pl.pallas_call(
        functools.partial(_ln_mod_body, gidx=gidx, sidx=sidx, has_y=has_y, y_t=y_t, u_t=u_t),
        grid=(bsz, nt - first), in_specs=in_specs, out_specs=out_specs, out_shape=outs,
        compiler_params=_cparams("parallel", "parallel"),
    )(*ins)
    res = list(res)
    zo = res.pop(0) if has_y else None
    u = res.pop(0) if sidx is not None else None
    return zo, u


def _lockstep(gens):
    out = [None] * len(gens)
    live = list(range(len(gens)))
    while live:
        for i in list(live):
            try:
                next(gens[i])
            except StopIteration as done:
                out[i] = done.value
                live.remove(i)
    return out


def _gla_chunk(q, k, v, glow, wg2, bg, st, rev):
    c, dk = q.shape
    sub = GLA_SUB
    q = q.astype(F32) * (dk ** -0.5)
    k = k.astype(F32)
    gl = _dot_x3(glow, wg2) + bg
    yield
    g = _log_sigmoid(gl) * (1.0 / GLA_GATE_NORMALIZER)
    row = lax.broadcasted_iota(jnp.int32, (c, c), 0)
    col = lax.broadcasted_iota(jnp.int32, (c, c), 1)
    tri = (row <= col) if rev else (row >= col)
    b = _dot_sel(tri.astype(F32), g)
    yield
    btot = jnp.sum(g, axis=0, keepdims=True)

    o = _dot_nt((q * jnp.exp(b)).astype(BF16), st.astype(BF16))

    rowk = lax.broadcasted_iota(jnp.int32, (c, 1), 0)
    halves = []
    hsz = c // 2
    while hsz >= sub:
        halves.append(hsz)
        hsz //= 2
    pick = jnp.concatenate(
        [(col == (row // (2 * hf)) * (2 * hf) + (hf if rev else hf - 1)).astype(F32) for hf in halves], axis=0)
    refs = _dot_sel(pick, b)
    yield
    att = jnp.zeros((c, c), F32)
    for lv, hf in enumerate(halves):
        refb = refs[lv * c:(lv + 1) * c]
        late = (rowk % (2 * hf)) >= hf
        q_side, k_side = (~late, late) if rev else (late, ~late)
        qf = jnp.where(q_side, q * jnp.exp(jnp.minimum(b - refb, 0.0)), 0.0)
        kf = jnp.where(k_side, k * jnp.exp(jnp.minimum(refb - b, 0.0)), 0.0)
        same = (row // (2 * hf)) == (col // (2 * hf))
        att = att + jnp.where(same, _dot_nt(qf.astype(BF16), kf.astype(BF16)), 0.0)
    yield

    rmod = rowk % sub
    for lag in range(sub):
        sh = (c - lag) % c if rev else lag
        ks = pltpu.roll(k, sh, 0) if sh else k
        bs = pltpu.roll(b, sh, 0) if sh else b
        term = jnp.sum(q * ks * jnp.exp(jnp.minimum(b - bs, 0.0)), axis=1, keepdims=True)
        valid = (rmod + lag < sub) if rev else (rmod >= lag)
        hit = (col == row + lag) if rev else (col == row - lag)
        att = att + jnp.where(hit & valid, term, 0.0)

    o = o + _dot(att.astype(BF16), v)
    yield
    kd = (k * jnp.exp(btot - b)).astype(BF16)
    return o, st * jnp.exp(btot) + _dot_tn(v, kd)


def _gla_body(*refs):
    ins, (of_ref, ob_ref, st_ref) = refs[:12], refs[12:]

    @pl.when(pl.program_id(2) == 0)
    def _():
        st_ref[...] = jnp.zeros_like(st_ref)

    dv = st_ref.shape[2]
    dk = st_ref.shape[3]
    gens, dest = [], []
    for dr in range(2):
        q_ref, k_ref, v_ref, gl_ref, wg2_ref, bg_ref = ins[6 * dr:6 * dr + 6]
        for hh in range(st_ref.shape[1]):
            ks, vs = slice(hh * dk, (hh + 1) * dk), slice(hh * dv, (hh + 1) * dv)
            gens.append(_gla_chunk(q_ref[0, :, ks], k_ref[0, :, ks], v_ref[0, :, vs], gl_ref[0],
                                   wg2_ref[:, ks], bg_ref[:, ks], st_ref[dr, hh], rev=(dr == 1)))
            dest.append((dr, hh, vs))
    for (dr, hh, vs), (o, st) in zip(dest, _lockstep(gens)):
        (of_ref, ob_ref)[dr][0, :, vs] = o.astype(of_ref.dtype)
        st_ref[dr, hh] = st


def gla_scan(p, glow, wg2, bg):
    bsz, t, d3 = p.shape
    d = d3 // 3
    h = GLA_HEADS
    dk, dv = (d // 2) // h, d // h
    c = GLA_CHUNK
    nc, ncx = t // c, CTX_LEN // c

    def rchunk(s):
        return jnp.where(s < ncx, ncx - 1 - s, nc - 1 - (s - ncx))

    hp = _pick(h, (GLA_HEADS_PER_STEP, 1))
    hg = h // hp

    def specs(chunk):
        return [pl.BlockSpec((1, c, hp * dk), lambda b, hh, s: (b, chunk(s), hh)),
                pl.BlockSpec((1, c, hp * dk), lambda b, hh, s: (b, chunk(s), hg + hh)),
                pl.BlockSpec((1, c, hp * dv), lambda b, hh, s: (b, chunk(s), hg + hh)),
                pl.BlockSpec((1, c, LANES), lambda b, hh, s: (b, chunk(s), 0)),
                pl.BlockSpec((LANES, hp * dk), lambda b, hh, s: (0, hh)),
                pl.BlockSpec((1, hp * dk), lambda b, hh, s: (0, hh))]

    sh = jax.ShapeDtypeStruct((bsz, t, d), BF16)
    return pl.pallas_call(
        _gla_body,
        grid=(bsz, hg, nc),
        in_specs=specs(lambda s: s) + specs(rchunk),
        out_specs=[pl.BlockSpec((1, c, hp * dv), lambda b, hh, s: (b, s, hh)),
                   pl.BlockSpec((1, c, hp * dv), lambda b, hh, s: (b, rchunk(s), hh))],
        out_shape=[sh, sh],
        scratch_shapes=[pltpu.VMEM((2, hp, dv, dk), F32)],
        compiler_params=_cparams("parallel", "parallel", "arbitrary"),
    )(p, p, p, glow, wg2[0], bg[0], p, p, p, glow, wg2[1], bg[1])


def _gla_gate_body(of_ref, ob_ref, r_ref, g_ref, o_ref):
    dv = g_ref.shape[1]
    o = of_ref[0].astype(F32) + ob_ref[0].astype(F32)
    r = r_ref[0].astype(F32)
    outs = []
    for hh in range(o.shape[1] // dv):
        seg = o[:, hh * dv:(hh + 1) * dv]
        outs.append(seg * lax.rsqrt(jnp.mean(seg * seg, -1, keepdims=True) + 1e-6) * g_ref[...])
    o_ref[0] = (jnp.concatenate(outs, axis=1) * (r * _sigmoid(r))).astype(o_ref.dtype)


def gla_gate(o_f, o_b, p, norm_g):
    bsz, t, d = o_f.shape
    tr = _pick(t, (256, 128, 64, 32, 16))
    row = pl.BlockSpec((1, tr, d), lambda b, i: (b, i, 0))
    return pl.pallas_call(
        _gla_gate_body, grid=(bsz, t // tr),
        in_specs=[row, row, pl.BlockSpec((1, tr, d), lambda b, i: (b, i, 2)),
                  pl.BlockSpec((1, norm_g.shape[1]), lambda b, i: (0, 0))],
        out_specs=row, out_shape=jax.ShapeDtypeStruct((bsz, t, d), BF16),
        compiler_params=_cparams("parallel", "parallel"),
    )(o_f, o_b, p, norm_g)


def gla_mixer(u, w_in, w_g1, w_g2, b_g, norm_g, w_o):
    bsz, t, d = u.shape
    qk = d // 2
    u2 = u.reshape(bsz * t, d)
    p = matmul(u2, w_in.astype(BF16), BF16).reshape(bsz, t, 3 * d)
    r = GLA_GATE_RANK
    wg1 = jnp.zeros((d, LANES), F32).at[:, :r].set(w_g1[0]).at[:, r:2 * r].set(w_g1[1])
    glow = matmul(u2, wg1.astype(BF16), F32).reshape(bsz, t, LANES)
    wg2 = [jnp.zeros((LANES, qk), F32).at[s * r:(s + 1) * r].set(w_g2[s]) for s in range(2)]
    outs = gla_scan(p, glow, wg2, [b_g[0][None], b_g[1][None]])
    gated = gla_gate(outs[0], outs[1], p, norm_g[None])
    return matmul(gated.reshape(bsz * t, d), w_o.astype(BF16), BF16).reshape(bsz, t, d)


def _rms(x, gain):
    return x * lax.rsqrt(jnp.mean(x * x, -1, keepdims=True) + 1e-6) * gain


def _rope128(a, c, s):
    return a * c + pltpu.roll(a, LANES // 2, 1) * s


def _mla_norm_body(h_ref, qg_ref, kg_ref, c_ref, s_ref, cq_ref, ckv_ref, kr_ref):
    qr, kvr = cq_ref.shape[2], ckv_ref.shape[2]
    h = h_ref[0]
    cq_ref[0] = _rms(h[:, :qr], qg_ref[...]).astype(BF16)
    ckv_ref[0] = _rms(h[:, qr:qr + kvr], kg_ref[...]).astype(BF16)
    kr_ref[0] = _rope128(h[:, qr + kvr:qr + kvr + LANES], c_ref[...], s_ref[...]).astype(BF16)


def _mla_attn_body(q_ref, kn_ref, v_ref, kr_ref, c_ref, s_ref, o_ref, kf_ref, vf_ref, *, nctx_tiles, scale):
    qt = pl.program_id(2)
    t = kf_ref.shape[0]
    ctx = nctx_tiles * q_ref.shape[1]

    @pl.when(qt == 0)
    def _():
        kf_ref[:, :LANES] = kn_ref[0]
        kf_ref[:, LANES:] = kr_ref[0]
        vf_ref[:, :LANES] = v_ref[0]
        vf_ref[:, LANES:] = jnp.ones((t, LANES), BF16)

    q = q_ref[0].astype(F32)
    qr = _rope128(q[:, LANES:], c_ref[...], s_ref[...])
    qf = (jnp.concatenate([q[:, :LANES], qr], axis=1) * (scale * LOG2E)).astype(BF16)

    def attend(keys):
        s = _dot_nt(qf, kf_ref[keys])
        yield
        m = jnp.max(s, -1, keepdims=True)
        p = jnp.exp2(s - m).astype(BF16)
        return m, _dot(p, vf_ref[keys])

    def finish(parts):
        m = functools.reduce(jnp.maximum, [pm for pm, _ in parts])
        acc = sum(jnp.exp2(pm - m) * po for pm, po in parts)
        o_ref[0] = (acc[:, :LANES] / acc[:, LANES:LANES + 1]).astype(o_ref.dtype)

    @pl.when(qt < nctx_tiles)
    def _():
        finish(_lockstep([attend(slice(0, ctx))]))

    @pl.when(qt >= nctx_tiles)
    def _():
        step = t // MLA_KEY_CHAINS
        finish(_lockstep([attend(slice(k0, k0 + step)) for k0 in range(0, t, step)]))


def _rope_tables(t):
    rows = (t - CTX_LEN) // GRID_W
    row = jnp.repeat(jnp.arange(rows, dtype=F32), GRID_W)
    colp = jnp.tile(jnp.arange(GRID_W, dtype=F32), rows)
    n_freq = MLA_ROPE // 4
    inv = ROPE_THETA ** (-jnp.arange(n_freq, dtype=F32) / n_freq)
    ang = jnp.concatenate([row[:, None] * inv, colp[:, None] * inv], -1)
    cos = jnp.concatenate([jnp.ones((CTX_LEN, MLA_ROPE // 2), F32), jnp.cos(ang)], 0)
    sin = jnp.concatenate([jnp.zeros((CTX_LEN, MLA_ROPE // 2), F32), jnp.sin(ang)], 0)
    z = jnp.zeros_like(cos)
    return jnp.concatenate([cos, cos, z, z], 1), jnp.concatenate([-sin, sin, z, z], 1)


def mla_mixer(u, w_in, q_norm, kv_norm, w_uq, w_ukv, w_o):
    bsz, t, d = u.shape
    nh = d // 128
    qr, kvr, rp = MLA_Q_RANK, MLA_KV_RANK, MLA_ROPE
    ev, od = jnp.arange(0, rp, 2), jnp.arange(1, rp, 2)
    perm = jnp.concatenate([ev, od, od, ev])
    hw = -(-(qr + kvr + LANES) // MXU_COLS) * MXU_COLS
    w_in_p = jnp.concatenate([w_in[:, :qr + kvr], w_in[:, qr + kvr + perm],
                              jnp.zeros((d, hw - (qr + kvr + LANES)), F32)], axis=1)
    h = matmul(u.reshape(bsz * t, d), w_in_p.astype(BF16), F32).reshape(bsz, t, hw)
    ctab, stab = _rope_tables(t)

    tr = _pick(t, (256, 128, 64, 32, 16))
    cq, ckv, kr = pl.pallas_call(
        _mla_norm_body, grid=(bsz, t // tr),
        in_specs=[pl.BlockSpec((1, tr, hw), lambda b, i: (b, i, 0)),
                  pl.BlockSpec((1, qr), lambda b, i: (0, 0)),
                  pl.BlockSpec((1, kvr), lambda b, i: (0, 0)),
                  pl.BlockSpec((tr, LANES), lambda b, i: (i, 0)),
                  pl.BlockSpec((tr, LANES), lambda b, i: (i, 0))],
        out_specs=[pl.BlockSpec((1, tr, qr), lambda b, i: (b, i, 0)),
                   pl.BlockSpec((1, tr, kvr), lambda b, i: (b, i, 0)),
                   pl.BlockSpec((1, tr, LANES), lambda b, i: (b, i, 0))],
        out_shape=[jax.ShapeDtypeStruct((bsz, t, qr), BF16),
                   jax.ShapeDtypeStruct((bsz, t, kvr), BF16),
                   jax.ShapeDtypeStruct((bsz, t, LANES), BF16)],
        compiler_params=_cparams("parallel", "parallel"),
    )(h, q_norm[None], kv_norm[None], ctab, stab)

    hq = MLA_NOPE + rp
    qcols = (jnp.arange(nh)[:, None] * hq
             + jnp.concatenate([jnp.arange(MLA_NOPE), MLA_NOPE + perm])[None, :]).reshape(-1)
    q = matmul(cq.reshape(bsz * t, qr), w_uq[:, qcols].astype(BF16), BF16).reshape(bsz, t, nh * 2 * LANES)
    kv = matmul(ckv.reshape(bsz * t, kvr), w_ukv.astype(BF16), BF16).reshape(bsz, t, nh * 2 * LANES)

    tq = _pick(CTX_LEN, (256, 128, 64, 32, 16))
    scale = (MLA_NOPE + rp) ** -0.5
    o = pl.pallas_call(
        functools.partial(_mla_attn_body, nctx_tiles=CTX_LEN // tq, scale=scale),
        grid=(bsz, nh, t // tq),
        in_specs=[pl.BlockSpec((1, tq, 2 * LANES), lambda b, hh, i: (b, i, hh)),
                  pl.BlockSpec((1, t, LANES), lambda b, hh, i: (b, 0, 2 * hh)),
                  pl.BlockSpec((1, t, LANES), lambda b, hh, i: (b, 0, 2 * hh + 1)),
                  pl.BlockSpec((1, t, LANES), lambda b, hh, i: (b, 0, 0)),
                  pl.BlockSpec((tq, LANES), lambda b, hh, i: (i, 0)),
                  pl.BlockSpec((tq, LANES), lambda b, hh, i: (i, 0))],
        out_specs=pl.BlockSpec((1, tq, LANES), lambda b, hh, i: (b, i, hh)),
        out_shape=jax.ShapeDtypeStruct((bsz, t, nh * MLA_V), BF16),
        scratch_shapes=[pltpu.VMEM((t, 2 * LANES), BF16), pltpu.VMEM((t, 2 * LANES), BF16)],
        compiler_params=_cparams("parallel", "parallel", "arbitrary"),
    )(q, kv, kv, kr, ctab, stab)
    return matmul(o.reshape(bsz * t, nh * MLA_V), w_o.astype(BF16), BF16).reshape(bsz, t, d)


def _rwkv_mix_body(u_ref, before_ref, after_ref, mu_ref, *o_refs, nctx_tiles):
    i = pl.program_id(1)
    tr = u_ref.shape[1]
    hr = before_ref.shape[1]
    u = u_ref[0].astype(F32)
    row = lax.broadcasted_iota(jnp.int32, (tr, 1), 0)
    starts = (i == 0) | (i == nctx_tiles)
    ends = (i == nctx_tiles - 1) | (i == pl.num_programs(1) - 1)
    edge_prev = jnp.where(starts, 0.0, before_ref[0, hr - 1:hr, :].astype(F32))
    edge_next = jnp.where(ends, 0.0, after_ref[0, 0:1, :].astype(F32))
    prev = jnp.where(row == 0, edge_prev, pltpu.roll(u, 1, 0))
    nxt = jnp.where(row == tr - 1, edge_next, pltpu.roll(u, tr - 1, 0))
    xx = 0.5 * (prev + nxt) - u
    for n, o_ref in enumerate(o_refs):
        o_ref[0] = (u + xx * mu_ref[n:n + 1, :]).astype(o_ref.dtype)


def _head_sum(x, e):
    eb = e.astype(BF16)
    outs = []
    for s in range(x.shape[1] // LANES):
        xs = x[:, s * LANES:(s + 1) * LANES]
        hi = xs.astype(BF16)
        lo = (xs - hi.astype(F32)).astype(BF16)
        outs.append(_dot(hi, eb) + _dot(lo, eb))
    return jnp.concatenate(outs, axis=1)


def _head_ones():
    r = lax.broadcasted_iota(jnp.int32, (LANES, LANES), 0) // RWKV_HEAD
    c = lax.broadcasted_iota(jnp.int32, (LANES, LANES), 1) // RWKV_HEAD
    return (r == c).astype(F32)


def _rwkv_prep_body(k_ref, hw_ref, ha_ref, w2_ref, a2_ref, w0_ref, a0_ref, kk_ref, ka_ref,
                    na_ref, lw0_ref, lw1_ref, ks0_ref, ks1_ref, bb0_ref, bb1_ref):
    rk = w2_ref.shape[1]
    k = k_ref[0]
    kk = k * kk_ref[...]
    kk = kk * lax.rsqrt(_head_sum(kk * kk, _head_ones()) + 1e-12)
    na_ref[0] = (-kk).astype(na_ref.dtype)
    hw = jnp.tanh(hw_ref[0])
    ha = ha_ref[0]
    for s, (lw_ref, ks_ref, bb_ref) in enumerate(((lw0_ref, ks0_ref, bb0_ref), (lw1_ref, ks1_ref, bb1_ref))):
        wl = w0_ref[s:s + 1, :] + _dot(hw[:, s * rk:(s + 1) * rk].astype(BF16), w2_ref[s])
        lw_ref[0] = (-RWKV_DECAY_SCALE) * _sigmoid(wl)
        a = _sigmoid(a0_ref[s:s + 1, :] + _dot(ha[:, s * rk:(s + 1) * rk].astype(BF16), a2_ref[s]))
        ks_ref[0] = (k * (1.0 + (a - 1.0) * ka_ref[...])).astype(ks_ref.dtype)
        bb_ref[0] = (kk * a).astype(bb_ref.dtype)


def _rwkv_chunk(r, lw, ks, v, na, bb, ht, rev):
    c = r.shape[0]
    n = RWKV_HEAD
    c2 = 2 * c
    r, ks, v, na, bb = (x.astype(F32) for x in (r, ks, v, na, bb))
    ri = lax.broadcasted_iota(jnp.int32, (c, c), 0)
    ci = lax.broadcasted_iota(jnp.int32, (c, c), 1)
    cs = _dot_sel(((ri <= ci) if rev else (ri >= ci)).astype(F32), lw)
    yield
    cm = cs - lw
    ctot = jnp.sum(lw, axis=0, keepdims=True)

    lane_a = lax.broadcasted_iota(jnp.int32, (1, LANES), 1) < n

    def stack(x):
        return jnp.concatenate([jnp.where(lane_a, x, 0.0), jnp.where(lane_a, 0.0, x)], axis=0)

    at = stack(na * jnp.exp(cm))
    rt = stack(r * jnp.exp(cs))
    ecs = jnp.exp(-cs)
    bh = stack(bb * ecs)
    kh = stack(ks * ecs)
    ece = jnp.exp(ctot - cs)
    be = stack(bb * ece)
    ke = stack(ks * ece)
    vs = stack(v)

    rr = lax.broadcasted_iota(jnp.int32, (c2, c2), 0)
    cc = lax.broadcasted_iota(jnp.int32, (c2, c2), 1)
    same = (rr // c) == (cc // c)
    tr_, tc_ = rr % c, cc % c
    strict = same & ((tc_ > tr_) if rev else (tc_ < tr_))
    incl = same & ((tc_ >= tr_) if rev else (tc_ <= tr_))
    diag_blk = same & ((tr_ // RWKV_SUB) == (tc_ // RWKV_SUB))

    lhs = jnp.concatenate([at, rt], axis=0)
    rhs = jnp.concatenate([bh, kh], axis=0)
    a4 = _dot_nt(lhs.astype(BF16), rhs.astype(BF16))
    yield
    nmat = jnp.where(strict, a4[:c2, :c2], 0.0)
    aak = jnp.where(strict, a4[:c2, c2:], 0.0)
    arb = jnp.where(incl, a4[c2:, :c2], 0.0)
    ark = jnp.where(incl, a4[c2:, c2:], 0.0)

    hb = ht.T.astype(BF16)
    vb = vs.astype(BF16)
    x0 = _dot(jnp.concatenate([at, aak], axis=1).astype(BF16), jnp.concatenate([hb, vb], axis=0))
    yield

    nd = jnp.where(diag_blk, nmat, 0.0)
    no = nmat - nd
    tm = nd
    pw = nd
    for _ in range(3):
        pw = _dot_b(pw, pw)
        yield
        tm = tm + pw + _dot_b(pw, tm)
        yield
    zu = _dot_b(tm, jnp.concatenate([no, x0], axis=1))
    z = no + zu[:, :c2]
    u = x0 + zu[:, c2:]
    yield
    zz = _dot_b(z, jnp.concatenate([z, u], axis=1))
    u = u + zz[:, c2:]
    yield
    u = u + _dot_b(zz[:, :c2], u)
    yield

    ub = u.astype(BF16)
    ys = _dot(jnp.concatenate([rt, arb, ark], axis=1).astype(BF16), jnp.concatenate([hb, ub, vb], axis=0))
    yield
    hn = ht * jnp.exp(ctot) + _dot_tn(jnp.concatenate([ub, vb], axis=0),
                                      jnp.concatenate([be, ke], axis=0).astype(BF16))
    hr = lax.broadcasted_iota(jnp.int32, (LANES, LANES), 0) // n
    hc = lax.broadcasted_iota(jnp.int32, (LANES, LANES), 1) // n
    return ys[:c] + ys[c:], jnp.where(hr == hc, hn, 0.0)


def _rwkv_scan_body(*refs):
    ins, (y0_ref, y1_ref, st_ref) = refs[:12], refs[12:]

    @pl.when(pl.program_id(2) == 0)
    def _():
        st_ref[...] = jnp.zeros_like(st_ref)

    gens, dest = [], []
    for dr in range(2):
        r_ref, lw_ref, ks_ref, v_ref, na_ref, bb_ref = ins[6 * dr:6 * dr + 6]
        for p in range(r_ref.shape[2] // LANES):
            sl = slice(p * LANES, (p + 1) * LANES)
            gens.append(_rwkv_chunk(r_ref[0, :, sl], lw_ref[0, :, sl], ks_ref[0, :, sl], v_ref[0, :, sl],
                                    na_ref[0, :, sl], bb_ref[0, :, sl], st_ref[dr, p], rev=(dr == 1)))
            dest.append(((y0_ref, y1_ref)[dr], sl, dr, p))
    for (y_ref, sl, dr, p), (y, hn) in zip(dest, _lockstep(gens)):
        y_ref[0, :, sl] = y
        st_ref[dr, p] = hn


def rwkv_scan(r, v, na, lw, ks, bb):
    bsz, t, d = r.shape
    c = RWKV_CHUNK
    nc, ncx = t // c, CTX_LEN // c
    pp = _pick(d // LANES, (4, 2, 1))
    w = pp * LANES

    def rchunk(s):
        return jnp.where(s < ncx, ncx - 1 - s, nc - 1 - (s - ncx))

    fwd = pl.BlockSpec((1, c, w), lambda b, p, s: (b, s, p))
    bwd = pl.BlockSpec((1, c, w), lambda b, p, s: (b, rchunk(s), p))
    sh = jax.ShapeDtypeStruct((bsz, t, d), F32)
    return pl.pallas_call(
        _rwkv_scan_body,
        grid=(bsz, d // w, nc),
        in_specs=[fwd] * 6 + [bwd] * 6, out_specs=[fwd, bwd], out_shape=[sh, sh],
        scratch_shapes=[pltpu.VMEM((2, pp, LANES, LANES), F32)],
        compiler_params=_cparams("parallel", "parallel", "arbitrary"),
    )(r, lw[0], ks[0], v, na, bb[0], r, lw[1], ks[1], v, na, bb[1])


def _rwkv_out_body(y0_ref, y1_ref, r_ref, v_ref, g_ref, ks0_ref, ks1_ref, rk_ref, lg_ref, lb_ref, o_ref):
    e = _head_ones()
    y = y0_ref[0] + y1_ref[0]
    inv_n = 1.0 / RWKV_HEAD
    yc = y - _head_sum(y, e) * inv_n
    yn = yc * lax.rsqrt(_head_sum(yc * yc, e) * inv_n + RWKV_GN_EPS)
    yn = yn * lg_ref[...] + lb_ref[...]
    ksum = ks0_ref[0].astype(F32) + ks1_ref[0].astype(F32)
    bonus = _head_sum(r_ref[0].astype(F32) * ksum * rk_ref[...], e) * v_ref[0].astype(F32)
    o_ref[0] = ((yn + bonus) * g_ref[0].astype(F32)).astype(o_ref.dtype)


def rwkv_mixer(u, mu, w_r, w_k, w_v, w_o, w0, w1, w2, a0, a1, a2, g1, g2, k_k, k_a, r_k, ln_g, ln_b):
    bsz, t, d = u.shape
    m = bsz * t
    tm_ = _pick(CTX_LEN, (128, 64, 32, 16))
    hr = 16
    rpt, last = tm_ // hr, t // hr - 1
    tile = pl.BlockSpec((1, tm_, d), lambda b, i: (b, i, 0))
    xs = pl.pallas_call(
        functools.partial(_rwkv_mix_body, nctx_tiles=CTX_LEN // tm_), grid=(bsz, t // tm_),
        in_specs=[tile,
                  pl.BlockSpec((1, hr, d), lambda b, i: (b, jnp.maximum(i * rpt - 1, 0), 0)),
                  pl.BlockSpec((1, hr, d), lambda b, i: (b, jnp.minimum((i + 1) * rpt, last), 0)),
                  pl.BlockSpec((6, d), lambda b, i: (0, 0))],
        out_specs=[tile] * 6,
        out_shape=[jax.ShapeDtypeStruct((bsz, t, d), BF16)] * 6,
        compiler_params=_cparams("parallel", "parallel"),
    )(u, u, u, mu)
    xr, xw, xk, xv, xa, xg = (x.reshape(m, d) for x in xs)
    r = matmul(xr, w_r.astype(BF16), BF16).reshape(bsz, t, d)
    k = matmul(xk, w_k.astype(BF16)).reshape(bsz, t, d)
    v = matmul(xv, w_v.astype(BF16), BF16).reshape(bsz, t, d)
    rk = w1.shape[2]
    hw = matmul(xw, jnp.concatenate([w1[0], w1[1]], 1).astype(BF16)).reshape(bsz, t, 2 * rk)
    ha = matmul(xa, jnp.concatenate([a1[0], a1[1]], 1).astype(BF16)).reshape(bsz, t, 2 * rk)
    gr = g1.shape[1]
    grp = -(-gr // LANES) * LANES
    g1p = jnp.zeros((d, grp), F32).at[:, :gr].set(g1)
    g2p = jnp.zeros((grp, d), F32).at[:gr].set(g2)
    hg = jax.nn.sigmoid(matmul(xg, g1p.astype(BF16)))
    g = matmul(hg, g2p.astype(BF16), BF16).reshape(bsz, t, d)

    tr = _pick(t, (64, 32, 16))
    row = pl.BlockSpec((1, tr, d), lambda b, i: (b, i, 0))
    low = pl.BlockSpec((1, tr, 2 * rk), lambda b, i: (b, i, 0))
    vec = pl.BlockSpec((1, d), lambda b, i: (0, 0))
    vec2 = pl.BlockSpec((2, d), lambda b, i: (0, 0))
    fact = pl.BlockSpec((2, rk, d), lambda b, i: (0, 0, 0))
    big = jax.ShapeDtypeStruct((bsz, t, d), F32)
    half = jax.ShapeDtypeStruct((bsz, t, d), BF16)
    na, lw0, lw1, ks0, ks1, bb0, bb1 = pl.pallas_call(
        _rwkv_prep_body, grid=(bsz, t // tr),
        in_specs=[row, low, low, fact, fact, vec2, vec2, vec, vec],
        out_specs=[row] * 7, out_shape=[half, big, big, half, half, half, half],
        compiler_params=_cparams("parallel", "parallel"),
    )(k, hw, ha, w2.astype(BF16), a2.astype(BF16), w0, a0, k_k[None], k_a[None])

    y0, y1 = rwkv_scan(r, v, na, (lw0, lw1), (ks0, ks1), (bb0, bb1))

    out = pl.pallas_call(
        _rwkv_out_body, grid=(bsz, t // tr),
        in_specs=[row] * 7 + [vec, vec, vec], out_specs=row,
        out_shape=jax.ShapeDtypeStruct((bsz, t, d), BF16),
        compiler_params=_cparams("parallel", "parallel"),
    )(y0, y1, r, v, g, ks0, ks1, r_k.reshape(1, d), ln_g[None], ln_b[None])
    return matmul(out.reshape(m, d), w_o.astype(BF16), BF16).reshape(bsz, t, d)


def _merge_exchange(n):
    t = max(1, (n - 1).bit_length())
    p = 1 << (t - 1)
    pairs = []
    while p > 0:
        q, r, d = 1 << (t - 1), 0, p
        while d > 0:
            pairs += [(i, i + d) for i in range(n - d) if (i & p) == r]
            d, q, r = q - p, q >> 1, p
        p >>= 1
    return pairs


def _pop_heads(levels, kk):
    out = []
    for k in range(kk):
        head = levels[0]
        m = jnp.max(head, axis=0, keepdims=True)
        out.append(m)
        need = min(len(levels), kk - k - 1)
        pop = head == m
        levels = [jnp.where(pop, levels[r + 1] if r + 1 < len(levels) else NEG, levels[r]) for r in range(need)]
    return jnp.concatenate(out, axis=0)


def _top_vals(s, kk):
    groups = [s[r:r + SUBLANES] for r in range(0, s.shape[0], SUBLANES)]
    for i, j in _merge_exchange(len(groups)):
        groups[i], groups[j] = jnp.maximum(groups[i], groups[j]), jnp.minimum(groups[i], groups[j])
    return _pop_heads(groups, kk)


def _peer_score_body(q_ref, keys_ref, thr_ref, s2_ref, e2_ref, cf_ref):
    nk = PEER_NKEYS
    kk = PEER_TOPK
    for h in range(PEER_HEADS):
        q1 = q_ref[h * 2 * nk:h * 2 * nk + nk, :].astype(BF16)
        q2 = q_ref[h * 2 * nk + nk:(h + 1) * 2 * nk, :].astype(BF16)
        s1 = _dot(keys_ref[0], q1)
        s2 = _dot(keys_ref[1], q2)
        a1 = _top_vals(s1, kk + 1)
        a2 = _top_vals(s2, kk + 1)
        cv = _pop_heads([a1[:kk] + a2[q:q + 1] for q in range(kk)], kk + 1)
        c17 = jnp.maximum(cv[kk:kk + 1], jnp.maximum(a1[kk:kk + 1] + a2[0:1], a1[0:1] + a2[kk:kk + 1]))
        theta = 0.5 * (cv[kk - 1:kk] + c17)
        zsum = jnp.sum(jnp.exp(cv[:kk] - cv[0:1]), axis=0, keepdims=True)
        thr_ref[h] = theta - s1
        s2_ref[h] = s2
        e2_ref[h] = jnp.exp(s2 - a2[0:1])
        cf_ref[h] = jnp.exp(s1 - a1[0:1]) / zsum


def _gelu(x):
    return 0.5 * x * (1.0 + lax.erf(x * (2.0 ** -0.5)))


def _peer_dense_body(z_ref, u_ref, v_ref, thr_ref, s2_ref, e2_ref, cf_ref, o_ref, w_ref):
    nk = PEER_NKEYS
    eb = pl.program_id(1)
    te = u_ref.shape[1]
    n_i = te // nk
    strip = PEER_ROW_STRIP
    stages = PEER_STAGES
    sub = min(PEER_TOK_SUB, z_ref.shape[1])
    d = z_ref.shape[0]
    kc = d // stages
    tiles = [slice(t0, t0 + sub) for t0 in range(0, z_ref.shape[1], sub)]

    @pl.when(eb == 0)
    def _():
        o_ref[...] = jnp.zeros_like(o_ref)

    def first_matmul(ts):
        acc = None
        for k0 in range(0, d, kc):
            part = _dot(u_ref[0, :, k0:k0 + kc], z_ref[k0:k0 + kc, ts])
            acc = part if acc is None else acc + part
            yield
        return acc

    def gates(ts):
        per_stage = (nk // strip) * PEER_HEADS // stages
        n = 0
        for si in range(nk // strip):
            js = slice(si * strip, (si + 1) * strip)
            w = [None] * n_i
            for h in range(PEER_HEADS):
                s2s = s2_ref[h, js, ts]
                e2s = e2_ref[h, js, ts]
                for ii in range(n_i):
                    i = eb * n_i + ii
                    c = jnp.where(s2s >= thr_ref[h, pl.ds(i, 1), ts], e2s, 0.0) * cf_ref[h, pl.ds(i, 1), ts]
                    w[ii] = c if w[ii] is None else w[ii] + c
                n += 1
                if n % per_stage == 0 and h + 1 < PEER_HEADS:
                    yield
            for ii in range(n_i):
                w_ref[ii * nk + si * strip:ii * nk + (si + 1) * strip, ts] = w[ii].astype(BF16)
            yield

    def activate(ts, act):
        rows = te // stages
        for r0 in range(0, te, rows):
            w_ref[r0:r0 + rows, ts] = w_ref[r0:r0 + rows, ts] * _gelu(act[r0:r0 + rows]).astype(BF16)
            yield

    def second_matmul(ts):
        rows = d // stages
        for r0 in range(0, d, rows):
            o_ref[r0:r0 + rows, ts] += _dot(v_ref[0, r0:r0 + rows, :], w_ref[:, ts])
            yield

    acts = {}
    for ph in range(len(tiles) + 2):
        gens, tags = [], []
        if ph < len(tiles):
            gens += [first_matmul(tiles[ph]), gates(tiles[ph])]
            tags += [ph, None]
        if 0 <= ph - 1 < len(tiles):
            gens.append(activate(tiles[ph - 1], acts[ph - 1]))
            tags.append(None)
        if 0 <= ph - 2 < len(tiles):
            gens.append(second_matmul(tiles[ph - 2]))
            tags.append(None)
        for tag, res in zip(tags, _lockstep(gens)):
            if tag is not None:
                acts[tag] = res


def peer_ffn(zt, w_q, sub_keys, u_all, vt_all, layer):
    d, m = zt.shape
    nh, nk = PEER_HEADS, PEER_NKEYS
    qt = matmul(w_q.T.astype(BF16), zt, F32)
    tt = _pick(m, (256, 128))
    sh = jax.ShapeDtypeStruct((nh, nk, m), F32)
    blk = pl.BlockSpec((nh, nk, tt), lambda i: (0, 0, i))
    thr, s2, e2, cf = pl.pallas_call(
        _peer_score_body, grid=(m // tt,),
        in_specs=[pl.BlockSpec((nh * 2 * nk, tt), lambda i: (0, i)),
                  pl.BlockSpec((2, nk, PEER_DKEY // 2), lambda i: (0, 0, 0))],
        out_specs=[blk, blk, blk, blk], out_shape=[sh, sh, sh, sh],
        compiler_params=_cparams("parallel"),
    )(qt, sub_keys.astype(BF16))

    tm = _pick(m, (2 * PEER_TOK_SUB, PEER_TOK_SUB, LANES))
    te = PEER_EXP_BLOCK
    ne = u_all.shape[1]
    once = pl.Buffered(1)
    sblk = pl.BlockSpec((nh, nk, tm), lambda i, e: (0, 0, i), pipeline_mode=once)
    return pl.pallas_call(
        _peer_dense_body, grid=(m // tm, ne // te),
        in_specs=[pl.BlockSpec((d, tm), lambda i, e: (0, i), pipeline_mode=once),
                  pl.BlockSpec((1, te, d), lambda i, e: (layer, e, 0)),
                  pl.BlockSpec((1, d, te), lambda i, e: (layer, 0, e)),
                  sblk, sblk, sblk, sblk],
        out_specs=pl.BlockSpec((d, tm), lambda i, e: (0, i)),
        out_shape=jax.ShapeDtypeStruct((d, m), F32),
        scratch_shapes=[pltpu.VMEM((te, tm), BF16)],
        compiler_params=_cparams("parallel", "arbitrary"),
    )(zt, u_all, vt_all, thr, s2, e2, cf)


def _ada_mods(c, c_ctx, w_down, w_up, b_up):
    bsz, d = c.shape
    cond = jnp.concatenate([c, c_ctx[None]], 0)
    pad = 16 - cond.shape[0] % 16
    cond = jnp.concatenate([cond, jnp.zeros((pad, d), F32)], 0)
    hid = matmul(jax.nn.silu(cond), w_down.astype(BF16), F32)
    m = (matmul(hid, w_up.astype(BF16), F32) + b_up)[:bsz + 1].reshape(bsz + 1, N_MOD, d)
    return jnp.stack([jnp.broadcast_to(m[bsz], (bsz, N_MOD, d)), m[:bsz]], axis=1)


def kernel(x, c, ctx, c_ctx, ada_w_down, ada_w_up, ada_b,
           gla_w_in, gla_w_g1, gla_w_g2, gla_b_g, gla_norm_g, gla_w_o,
           mla_w_in, mla_q_norm, mla_kv_norm, mla_w_uq, mla_w_ukv, mla_w_o,
           rwkv_mu, rwkv_w_r, rwkv_w_k, rwkv_w_v, rwkv_w_o, rwkv_w0, rwkv_w1, rwkv_w2,
           rwkv_a0, rwkv_a1, rwkv_a2, rwkv_g1, rwkv_g2, rwkv_k_k, rwkv_k_a, rwkv_r_k, rwkv_ln_g, rwkv_ln_b,
           peer_w_q, peer_sub_keys, peer_u, peer_v):
    z = jnp.concatenate([ctx, x], axis=1)
    bsz, t, d = z.shape
    mods = [_ada_mods(c, c_ctx, ada_w_down[i], ada_w_up[i], ada_b[i]) for i in range(DEPTH)]
    peer_ub = peer_u.astype(BF16)
    peer_vtb = jnp.swapaxes(peer_v, 1, 2).astype(BF16)
    _, u = ln_mod(z, mods_n=mods[0], sidx=0)
    for i in range(DEPTH):
        j = i // N_MIXERS
        if i % N_MIXERS == 0:
            y = gla_mixer(u, gla_w_in[j], gla_w_g1[j], gla_w_g2[j], gla_b_g[j], gla_norm_g[j], gla_w_o[j])
        elif i % N_MIXERS == 1:
            y = mla_mixer(u, mla_w_in[j], mla_q_norm[j], mla_kv_norm[j], mla_w_uq[j], mla_w_ukv[j], mla_w_o[j])
        else:
            y = rwkv_mixer(u, rwkv_mu[j], rwkv_w_r[j], rwkv_w_k[j], rwkv_w_v[j], rwkv_w_o[j],
                           rwkv_w0[j], rwkv_w1[j], rwkv_w2[j], rwkv_a0[j], rwkv_a1[j], rwkv_a2[j],
                           rwkv_g1[j], rwkv_g2[j], rwkv_k_k[j], rwkv_k_a[j], rwkv_r_k[j],
                           rwkv_ln_g[j], rwkv_ln_b[j])
        z, ut = ln_mod(z, y, mods[i], 2, mods[i], 3, u_t=True)
        ht = peer_ffn(ut, peer_w_q[i], peer_sub_keys[i], peer_ub, peer_vtb, i)
        if i + 1 < DEPTH:
            z, u = ln_mod(z, ht, mods[i], 5, mods[i + 1], 0, y_t=True)
        else:
            z, _ = ln_mod(z, ht, mods[i], 5, y_t=True, latent_only=True)
    return z
```

```python
import functools

import jax
import jax.numpy as jnp
from jax import lax
from jax.experimental import pallas as pl
from jax.experimental.pallas import tpu as pltpu

F32 = jnp.float32
BF16 = jnp.bfloat16

DEPTH = 4
CTX_LEN = 256
GRID_W = 64
N_MIXERS = 3
N_MOD = 6
LN_EPS = 1e-6
DEEPNORM_ALPHA = (2.0 * DEPTH) ** 0.25

GLA_HEADS = 8
GLA_GATE_RANK = 16
GLA_GATE_NORMALIZER = 16.0

MLA_Q_RANK = 1536
MLA_KV_RANK = 512
MLA_NOPE = 128
MLA_ROPE = 64
MLA_V = 128
ROPE_THETA = 10000.0

RWKV_HEAD = 64
RWKV_GN_EPS = 64e-5
RWKV_DECAY_SCALE = 0.6065306597126334

PEER_HEADS = 8
PEER_NKEYS = 128
PEER_DKEY = 256
PEER_TOPK = 16

LANES = 128
SUBLANES = 8
MXU_COLS = 256
VMEM_LIMIT = 52 * 1024 * 1024

GLA_CHUNK = 128
GLA_HEADS_PER_STEP = 4
GLA_SUB = 4
RWKV_CHUNK = 64
RWKV_SUB = 16
MLA_KEY_CHAINS = 2
LOG2E = 1.4426950408889634
PEER_TOK_SUB = 256
PEER_EXP_BLOCK = 512
PEER_STAGES = 4
PEER_ROW_STRIP = 32
NEG = -1e30


def _pick(n, cands):
    for c in cands:
        if n % c == 0:
            return c
    return n


def _cparams(*sem):
    return pltpu.CompilerParams(dimension_semantics=sem, vmem_limit_bytes=VMEM_LIMIT)


def _dot(a, b):
    return jnp.dot(a, b, preferred_element_type=F32)


def _dot_nt(a, b):
    return lax.dot_general(a, b, (((1,), (1,)), ((), ())), preferred_element_type=F32)


def _dot_tn(a, b):
    return lax.dot_general(a, b, (((0,), (0,)), ((), ())), preferred_element_type=F32)


def _split3(x):
    hi = x.astype(BF16)
    r1 = x - hi.astype(F32)
    mid = r1.astype(BF16)
    lo = (r1 - mid.astype(F32)).astype(BF16)
    return hi, mid, lo


def _dot_sel(sel, x):
    s = sel.astype(BF16)
    hi, mid, lo = _split3(x)
    return _dot(s, hi) + _dot(s, mid) + _dot(s, lo)


def _dot_x3(a, b, dims=None):
    ah = a.astype(BF16)
    al = (a - ah.astype(F32)).astype(BF16)
    bh = b.astype(BF16)
    bl = (b - bh.astype(F32)).astype(BF16)
    f = _dot if dims is None else dims
    return f(ah, bh) + f(ah, bl) + f(al, bh)


def _dot_b(a, b):
    return _dot(a.astype(BF16), b.astype(BF16))


def _log_sigmoid(x):
    return jnp.minimum(x, 0.0) - jnp.log1p(jnp.exp(-jnp.abs(x)))


def _sigmoid(x):
    return 1.0 / (1.0 + jnp.exp(-x))


def _mm_body(a_ref, b_ref, o_ref):
    o_ref[...] = _dot(a_ref[...].astype(BF16), b_ref[...].astype(BF16)).astype(o_ref.dtype)


def matmul(a, b, out_dtype=F32):
    m, k = a.shape
    k2, n = b.shape
    assert k == k2
    a_bytes = jnp.dtype(a.dtype).itemsize
    tm_cands = (1024, 512, 256, 128, 64, 32, 16) if a_bytes * k <= 8192 else (512, 256, 128, 64, 32, 16)
    tm = _pick(m, tm_cands)
    tn = _pick(n, (512, 256, 128))
    return pl.pallas_call(
        _mm_body,
        grid=(m // tm, n // tn),
        in_specs=[pl.BlockSpec((tm, k), lambda i, j: (i, 0)),
                  pl.BlockSpec((k, tn), lambda i, j: (0, j))],
        out_specs=pl.BlockSpec((tm, tn), lambda i, j: (i, j)),
        out_shape=jax.ShapeDtypeStruct((m, n), out_dtype),
        compiler_params=_cparams("parallel", "parallel"),
    )(a, b)


def _layer_norm(z):
    zc = z - jnp.mean(z, -1, keepdims=True)
    return zc * lax.rsqrt(jnp.mean(zc * zc, -1, keepdims=True) + LN_EPS)


def _ln_mod_body(*refs, gidx, sidx, has_y, y_t, u_t):
    it = iter(refs)
    z_ref = next(it)
    y_ref = next(it) if has_y else None
    mg_ref = next(it) if has_y else None
    mn_ref = next(it) if sidx is not None else None
    zo_ref = next(it) if has_y else None
    u_ref = next(it) if sidx is not None else None
    z = z_ref[0]
    if has_y:
        y = y_ref[...].T if y_t else y_ref[0]
        z = _layer_norm(DEEPNORM_ALPHA * z + mg_ref[0, 0, gidx:gidx + 1, :] * y)
        zo_ref[0] = z
    if sidx is not None:
        u = z * (1.0 + mn_ref[0, 0, sidx + 1:sidx + 2, :]) + mn_ref[0, 0, sidx:sidx + 1, :]
        if u_t:
            u_ref[...] = u.T.astype(u_ref.dtype)
        else:
            u_ref[0] = u.astype(u_ref.dtype)


def ln_mod(z, y=None, mods_g=None, gidx=None, mods_n=None, sidx=None, y_t=False, u_t=False, latent_only=False):
    assert not (latent_only and sidx is not None)
    bsz, t, d = z.shape
    tr = _pick(CTX_LEN, (256, 128, 64, 32, 16))
    nt = t // tr
    nctx = CTX_LEN // tr
    first = nctx if latent_only else 0
    has_y = y is not None
    row = pl.BlockSpec((1, tr, d), lambda b, i: (b, first + i, 0))
    col = pl.BlockSpec((d, tr), lambda b, i: (0, b * nt + first + i))
    mod = pl.BlockSpec((1, 1, N_MOD, d), lambda b, i: (b, jnp.where(first + i < nctx, 0, 1), 0, 0))
    ins, in_specs, outs, out_specs = [z], [row], [], []
    if has_y:
        ins += [y, mods_g]
        in_specs += [col if y_t else row, mod]
        outs.append(jax.ShapeDtypeStruct((bsz, t - first * tr, d), F32))
        out_specs.append(pl.BlockSpec((1, tr, d), lambda b, i: (b, i, 0)) if latent_only else row)
    if sidx is not None:
        ins.append(mods_n)
        in_specs.append(mod)
        outs.append(jax.ShapeDtypeStruct((d, bsz * t) if u_t else (bsz, t, d), BF16))
        out_specs.append(col if u_t else row)
    res = pl.pallas_call(
        functools.partial(_ln_mod_body, gidx=gidx, sidx=sidx, has_y=has_y, y_t=y_t, u_t=u_t),
        grid=(bsz, nt - first), in_specs=in_specs, out_specs=out_specs, out_shape=outs,
        compiler_params=_cparams("parallel", "parallel"),
    )(*ins)
    res = list(res)
    zo = res.pop(0) if has_y else None
    u = res.pop(0) if sidx is not None else None
    return zo, u


def _lockstep(gens):
    out = [None] * len(gens)
    live = list(range(len(gens)))
    while live:
        for i in list(live):
            try:
                next(gens[i])
            except StopIteration as done:
                out[i] = done.value
                live.remove(i)
    return out


def _gla_chunk(q, k, v, glow, wg2, bg, st, rev):
    c, dk = q.shape
    sub = GLA_SUB
    q = q.astype(F32) * (dk ** -0.5)
    k = k.astype(F32)
    gl = _dot_x3(glow, wg2) + bg
    yield
    g = _log_sigmoid(gl) * (1.0 / GLA_GATE_NORMALIZER)
    row = lax.broadcasted_iota(jnp.int32, (c, c), 0)
    col = lax.broadcasted_iota(jnp.int32, (c, c), 1)
    tri = (row <= col) if rev else (row >= col)
    b = _dot_sel(tri.astype(F32), g)
    yield
    btot = jnp.sum(g, axis=0, keepdims=True)

    o = _dot_nt((q * jnp.exp(b)).astype(BF16), st.astype(BF16))

    rowk = lax.broadcasted_iota(jnp.int32, (c, 1), 0)
    halves = []
    hsz = c // 2
    while hsz >= sub:
        halves.append(hsz)
        hsz //= 2
    pick = jnp.concatenate(
        [(col == (row // (2 * hf)) * (2 * hf) + (hf if rev else hf - 1)).astype(F32) for hf in halves], axis=0)
    refs = _dot_sel(pick, b)
    yield
    att = jnp.zeros((c, c), F32)
    for lv, hf in enumerate(halves):
        refb = refs[lv * c:(lv + 1) * c]
        late = (rowk % (2 * hf)) >= hf
        q_side, k_side = (~late, late) if rev else (late, ~late)
        qf = jnp.where(q_side, q * jnp.exp(jnp.minimum(b - refb, 0.0)), 0.0)
        kf = jnp.where(k_side, k * jnp.exp(jnp.minimum(refb - b, 0.0)), 0.0)
        same = (row // (2 * hf)) == (col // (2 * hf))
        att = att + jnp.where(same, _dot_nt(qf.astype(BF16), kf.astype(BF16)), 0.0)
    yield

    rmod = rowk % sub
    for lag in range(sub):
        sh = (c - lag) % c if rev else lag
        ks = pltpu.roll(k, sh, 0) if sh else k
        bs = pltpu.roll(b, sh, 0) if sh else b
        term = jnp.sum(q * ks * jnp.exp(jnp.minimum(b - bs, 0.0)), axis=1, keepdims=True)
        valid = (rmod + lag < sub) if rev else (rmod >= lag)
        hit = (col == row + lag) if rev else (col == row - lag)
        att = att + jnp.where(hit & valid, term, 0.0)

    o = o + _dot(att.astype(BF16), v)
    yield
    kd = (k * jnp.exp(btot - b)).astype(BF16)
    return o, st * jnp.exp(btot) + _dot_tn(v, kd)


def _gla_body(*refs):
    ins, (of_ref, ob_ref, st_ref) = refs[:12], refs[12:]

    @pl.when(pl.program_id(2) == 0)
    def _():
        st_ref[...] = jnp.zeros_like(st_ref)

    dv = st_ref.shape[2]
    dk = st_ref.shape[3]
    gens, dest = [], []
    for dr in range(2):
        q_ref, k_ref, v_ref, gl_ref, wg2_ref, bg_ref = ins[6 * dr:6 * dr + 6]
        for hh in range(st_ref.shape[1]):
            ks, vs = slice(hh * dk, (hh + 1) * dk), slice(hh * dv, (hh + 1) * dv)
            gens.append(_gla_chunk(q_ref[0, :, ks], k_ref[0, :, ks], v_ref[0, :, vs], gl_ref[0],
                                   wg2_ref[:, ks], bg_ref[:, ks], st_ref[dr, hh], rev=(dr == 1)))
            dest.append((dr, hh, vs))
    for (dr, hh, vs), (o, st) in zip(dest, _lockstep(gens)):
        (of_ref, ob_ref)[dr][0, :, vs] = o.astype(of_ref.dtype)
        st_ref[dr, hh] = st


def gla_scan(p, glow, wg2, bg):
    bsz, t, d3 = p.shape
    d = d3 // 3
    h = GLA_HEADS
    dk, dv = (d // 2) // h, d // h
    c = GLA_CHUNK
    nc, ncx = t // c, CTX_LEN // c

    def rchunk(s):
        return jnp.where(s < ncx, ncx - 1 - s, nc - 1 - (s - ncx))

    hp = _pick(h, (GLA_HEADS_PER_STEP, 1))
    hg = h // hp

    def specs(chunk):
        return [pl.BlockSpec((1, c, hp * dk), lambda b, hh, s: (b, chunk(s), hh)),
                pl.BlockSpec((1, c, hp * dk), lambda b, hh, s: (b, chunk(s), hg + hh)),
                pl.BlockSpec((1, c, hp * dv), lambda b, hh, s: (b, chunk(s), hg + hh)),
                pl.BlockSpec((1, c, LANES), lambda b, hh, s: (b, chunk(s), 0)),
                pl.BlockSpec((LANES, hp * dk), lambda b, hh, s: (0, hh)),
                pl.BlockSpec((1, hp * dk), lambda b, hh, s: (0, hh))]

    sh = jax.ShapeDtypeStruct((bsz, t, d), BF16)
    return pl.pallas_call(
        _gla_body,
        grid=(bsz, hg, nc),
        in_specs=specs(lambda s: s) + specs(rchunk),
        out_specs=[pl.BlockSpec((1, c, hp * dv), lambda b, hh, s: (b, s, hh)),
                   pl.BlockSpec((1, c, hp * dv), lambda b, hh, s: (b, rchunk(s), hh))],
        out_shape=[sh, sh],
        scratch_shapes=[pltpu.VMEM((2, hp, dv, dk), F32)],
        compiler_params=_cparams("parallel", "parallel", "arbitrary"),
    )(p, p, p, glow, wg2[0], bg[0], p, p, p, glow, wg2[1], bg[1])


def _gla_gate_body(of_ref, ob_ref, r_ref, g_ref, o_ref):
    dv = g_ref.shape[1]
    o = of_ref[0].astype(F32) + ob_ref[0].astype(F32)
    r = r_ref[0].astype(F32)
    outs = []
    for hh in range(o.shape[1] // dv):
        seg = o[:, hh * dv:(hh + 1) * dv]
        outs.append(seg * lax.rsqrt(jnp.mean(seg * seg, -1, keepdims=True) + 1e-6) * g_ref[...])
    o_ref[0] = (jnp.concatenate(outs, axis=1) * (r * _sigmoid(r))).astype(o_ref.dtype)


def gla_gate(o_f, o_b, p, norm_g):
    bsz, t, d = o_f.shape
    tr = _pick(t, (256, 128, 64, 32, 16))
    row = pl.BlockSpec((1, tr, d), lambda b, i: (b, i, 0))
    return pl.pallas_call(
        _gla_gate_body, grid=(bsz, t // tr),
        in_specs=[row, row, pl.BlockSpec((1, tr, d), lambda b, i: (b, i, 2)),
                  pl.BlockSpec((1, norm_g.shape[1]), lambda b, i: (0, 0))],
        out_specs=row, out_shape=jax.ShapeDtypeStruct((bsz, t, d), BF16),
        compiler_params=_cparams("parallel", "parallel"),
    )(o_f, o_b, p, norm_g)


def gla_mixer(u, w_in, w_g1, w_g2, b_g, norm_g, w_o):
    bsz, t, d = u.shape
    qk = d // 2
    u2 = u.reshape(bsz * t, d)
    p = matmul(u2, w_in.astype(BF16), BF16).reshape(bsz, t, 3 * d)
    r = GLA_GATE_RANK
    wg1 = jnp.zeros((d, LANES), F32).at[:, :r].set(w_g1[0]).at[:, r:2 * r].set(w_g1[1])
    glow = matmul(u2, wg1.astype(BF16), F32).reshape(bsz, t, LANES)
    wg2 = [jnp.zeros((LANES, qk), F32).at[s * r:(s + 1) * r].set(w_g2[s]) for s in range(2)]
    outs = gla_scan(p, glow, wg2, [b_g[0][None], b_g[1][None]])
    gated = gla_gate(outs[0], outs[1], p, norm_g[None])
    return matmul(gated.reshape(bsz * t, d), w_o.astype(BF16), BF16).reshape(bsz, t, d)


def _rms(x, gain):
    return x * lax.rsqrt(jnp.mean(x * x, -1, keepdims=True) + 1e-6) * gain


def _rope128(a, c, s):
    return a * c + pltpu.roll(a, LANES // 2, 1) * s


def _mla_norm_body(h_ref, qg_ref, kg_ref, c_ref, s_ref, cq_ref, ckv_ref, kr_ref):
    qr, kvr = cq_ref.shape[2], ckv_ref.shape[2]
    h = h_ref[0]
    cq_ref[0] = _rms(h[:, :qr], qg_ref[...]).astype(BF16)
    ckv_ref[0] = _rms(h[:, qr:qr + kvr], kg_ref[...]).astype(BF16)
    kr_ref[0] = _rope128(h[:, qr + kvr:qr + kvr + LANES], c_ref[...], s_ref[...]).astype(BF16)


def _mla_attn_body(q_ref, kn_ref, v_ref, kr_ref, c_ref, s_ref, o_ref, kf_ref, vf_ref, *, nctx_tiles, scale):
    qt = pl.program_id(2)
    t = kf_ref.shape[0]
    ctx = nctx_tiles * q_ref.shape[1]

    @pl.when(qt == 0)
    def _():
        kf_ref[:, :LANES] = kn_ref[0]
        kf_ref[:, LANES:] = kr_ref[0]
        vf_ref[:, :LANES] = v_ref[0]
        vf_ref[:, LANES:] = jnp.ones((t, LANES), BF16)

    q = q_ref[0].astype(F32)
    qr = _rope128(q[:, LANES:], c_ref[...], s_ref[...])
    qf = (jnp.concatenate([q[:, :LANES], qr], axis=1) * (scale * LOG2E)).astype(BF16)

    def attend(keys):
        s = _dot_nt(qf, kf_ref[keys])
        yield
        m = jnp.max(s, -1, keepdims=True)
        p = jnp.exp2(s - m).astype(BF16)
        return m, _dot(p, vf_ref[keys])

    def finish(parts):
        m = functools.reduce(jnp.maximum, [pm for pm, _ in parts])
        acc = sum(jnp.exp2(pm - m) * po for pm, po in parts)
        o_ref[0] = (acc[:, :LANES] / acc[:, LANES:LANES + 1]).astype(o_ref.dtype)

    @pl.when(qt < nctx_tiles)
    def _():
        finish(_lockstep([attend(slice(0, ctx))]))

    @pl.when(qt >= nctx_tiles)
    def _():
        step = t // MLA_KEY_CHAINS
        finish(_lockstep([attend(slice(k0, k0 + step)) for k0 in range(0, t, step)]))


def _rope_tables(t):
    rows = (t - CTX_LEN) // GRID_W
    row = jnp.repeat(jnp.arange(rows, dtype=F32), GRID_W)
    colp = jnp.tile(jnp.arange(GRID_W, dtype=F32), rows)
    n_freq = MLA_ROPE // 4
    inv = ROPE_THETA ** (-jnp.arange(n_freq, dtype=F32) / n_freq)
    ang = jnp.concatenate([row[:, None] * inv, colp[:, None] * inv], -1)
    cos = jnp.concatenate([jnp.ones((CTX_LEN, MLA_ROPE // 2), F32), jnp.cos(ang)], 0)
    sin = jnp.concatenate([jnp.zeros((CTX_LEN, MLA_ROPE // 2), F32), jnp.sin(ang)], 0)
    z = jnp.zeros_like(cos)
    return jnp.concatenate([cos, cos, z, z], 1), jnp.concatenate([-sin, sin, z, z], 1)


def mla_mixer(u, w_in, q_norm, kv_norm, w_uq, w_ukv, w_o):
    bsz, t, d = u.shape
    nh = d // 128
    qr, kvr, rp = MLA_Q_RANK, MLA_KV_RANK, MLA_ROPE
    ev, od = jnp.arange(0, rp, 2), jnp.arange(1, rp, 2)
    perm = jnp.concatenate([ev, od, od, ev])
    hw = -(-(qr + kvr + LANES) // MXU_COLS) * MXU_COLS
    w_in_p = jnp.concatenate([w_in[:, :qr + kvr], w_in[:, qr + kvr + perm],
                              jnp.zeros((d, hw - (qr + kvr + LANES)), F32)], axis=1)
    h = matmul(u.reshape(bsz * t, d), w_in_p.astype(BF16), F32).reshape(bsz, t, hw)
    ctab, stab = _rope_tables(t)

    tr = _pick(t, (256, 128, 64, 32, 16))
    cq, ckv, kr = pl.pallas_call(
        _mla_norm_body, grid=(bsz, t // tr),
        in_specs=[pl.BlockSpec((1, tr, hw), lambda b, i: (b, i, 0)),
                  pl.BlockSpec((1, qr), lambda b, i: (0, 0)),
                  pl.BlockSpec((1, kvr), lambda b, i: (0, 0)),
                  pl.BlockSpec((tr, LANES), lambda b, i: (i, 0)),
                  pl.BlockSpec((tr, LANES), lambda b, i: (i, 0))],
        out_specs=[pl.BlockSpec((1, tr, qr), lambda b, i: (b, i, 0)),
                   pl.BlockSpec((1, tr, kvr), lambda b, i: (b, i, 0)),
                   pl.BlockSpec((1, tr, LANES), lambda b, i: (b, i, 0))],
        out_shape=[jax.ShapeDtypeStruct((bsz, t, qr), BF16),
                   jax.ShapeDtypeStruct((bsz, t, kvr), BF16),
                   jax.ShapeDtypeStruct((bsz, t, LANES), BF16)],
        compiler_params=_cparams("parallel", "parallel"),
    )(h, q_norm[None], kv_norm[None], ctab, stab)

    hq = MLA_NOPE + rp
    qcols = (jnp.arange(nh)[:, None] * hq
             + jnp.concatenate([jnp.arange(MLA_NOPE), MLA_NOPE + perm])[None, :]).reshape(-1)
    q = matmul(cq.reshape(bsz * t, qr), w_uq[:, qcols].astype(BF16), BF16).reshape(bsz, t, nh * 2 * LANES)
    kv = matmul(ckv.reshape(bsz * t, kvr), w_ukv.astype(BF16), BF16).reshape(bsz, t, nh * 2 * LANES)

    tq = _pick(CTX_LEN, (256, 128, 64, 32, 16))
    scale = (MLA_NOPE + rp) ** -0.5
    o = pl.pallas_call(
        functools.partial(_mla_attn_body, nctx_tiles=CTX_LEN // tq, scale=scale),
        grid=(bsz, nh, t // tq),
        in_specs=[pl.BlockSpec((1, tq, 2 * LANES), lambda b, hh, i: (b, i, hh)),
                  pl.BlockSpec((1, t, LANES), lambda b, hh, i: (b, 0, 2 * hh)),
                  pl.BlockSpec((1, t, LANES), lambda b, hh, i: (b, 0, 2 * hh + 1)),
                  pl.BlockSpec((1, t, LANES), lambda b, hh, i: (b, 0, 0)),
                  pl.BlockSpec((tq, LANES), lambda b, hh, i: (i, 0)),
                  pl.BlockSpec((tq, LANES), lambda b, hh, i: (i, 0))],
        out_specs=pl.BlockSpec((1, tq, LANES), lambda b, hh, i: (b, i, hh)),
        out_shape=jax.ShapeDtypeStruct((bsz, t, nh * MLA_V), BF16),
        scratch_shapes=[pltpu.VMEM((t, 2 * LANES), BF16), pltpu.VMEM((t, 2 * LANES), BF16)],
        compiler_params=_cparams("parallel", "parallel", "arbitrary"),
    )(q, kv, kv, kr, ctab, stab)
    return matmul(o.reshape(bsz * t, nh * MLA_V), w_o.astype(BF16), BF16).reshape(bsz, t, d)


def _rwkv_mix_body(u_ref, before_ref, after_ref, mu_ref, *o_refs, nctx_tiles):
    i = pl.program_id(1)
    tr = u_ref.shape[1]
    hr = before_ref.shape[1]
    u = u_ref[0].astype(F32)
    row = lax.broadcasted_iota(jnp.int32, (tr, 1), 0)
    starts = (i == 0) | (i == nctx_tiles)
    ends = (i == nctx_tiles - 1) | (i == pl.num_programs(1) - 1)
    edge_prev = jnp.where(starts, 0.0, before_ref[0, hr - 1:hr, :].astype(F32))
    edge_next = jnp.where(ends, 0.0, after_ref[0, 0:1, :].astype(F32))
    prev = jnp.where(row == 0, edge_prev, pltpu.roll(u, 1, 0))
    nxt = jnp.where(row == tr - 1, edge_next, pltpu.roll(u, tr - 1, 0))
    xx = 0.5 * (prev + nxt) - u
    for n, o_ref in enumerate(o_refs):
        o_ref[0] = (u + xx * mu_ref[n:n + 1, :]).astype(o_ref.dtype)


def _head_sum(x, e):
    eb = e.astype(BF16)
    w = e.shape[0]
    outs = []
    for s in range(x.shape[1] // w):
        xs = x[:, s * w:(s + 1) * w]
        hi = xs.astype(BF16)
        lo = (xs - hi.astype(F32)).astype(BF16)
        outs.append(_dot(hi, eb) + _dot(lo, eb))
    return jnp.concatenate(outs, axis=1)


def _head_ones():
    r = lax.broadcasted_iota(jnp.int32, (MXU_COLS, MXU_COLS), 0) // RWKV_HEAD
    c = lax.broadcasted_iota(jnp.int32, (MXU_COLS, MXU_COLS), 1) // RWKV_HEAD
    return (r == c).astype(F32)


def _rwkv_prep_body(k_ref, hw_ref, ha_ref, w2_ref, a2_ref, w0_ref, a0_ref, kk_ref, ka_ref,
                    na_ref, lw0_ref, lw1_ref, ks0_ref, ks1_ref, bb0_ref, bb1_ref):
    rk = w2_ref.shape[1]
    k = k_ref[0]
    kk = k * kk_ref[...]
    kk = kk * lax.rsqrt(_head_sum(kk * kk, _head_ones()) + 1e-12)
    na_ref[0] = (-kk).astype(na_ref.dtype)
    hw = jnp.tanh(hw_ref[0])
    ha = ha_ref[0]
    for s, (lw_ref, ks_ref, bb_ref) in enumerate(((lw0_ref, ks0_ref, bb0_ref), (lw1_ref, ks1_ref, bb1_ref))):
        wl = w0_ref[s:s + 1, :] + _dot(hw[:, s * rk:(s + 1) * rk].astype(BF16), w2_ref[s])
        lw_ref[0] = (-RWKV_DECAY_SCALE) * _sigmoid(wl)
        a = _sigmoid(a0_ref[s:s + 1, :] + _dot(ha[:, s * rk:(s + 1) * rk].astype(BF16), a2_ref[s]))
        ks_ref[0] = (k * (1.0 + (a - 1.0) * ka_ref[...])).astype(ks_ref.dtype)
        bb_ref[0] = (kk * a).astype(bb_ref.dtype)


def _rwkv_chunk(r, lw, ks, v, na, bb, ht, rev):
    c = r.shape[0]
    n = RWKV_HEAD
    c2 = 2 * c
    r, ks, v, na, bb = (x.astype(F32) for x in (r, ks, v, na, bb))
    ri = lax.broadcasted_iota(jnp.int32, (c, c), 0)
    ci = lax.broadcasted_iota(jnp.int32, (c, c), 1)
    cs = _dot_sel(((ri <= ci) if rev else (ri >= ci)).astype(F32), lw)
    yield
    cm = cs - lw
    ctot = jnp.sum(lw, axis=0, keepdims=True)

    lane_a = lax.broadcasted_iota(jnp.int32, (1, LANES), 1) < n

    def stack(x):
        return jnp.concatenate([jnp.where(lane_a, x, 0.0), jnp.where(lane_a, 0.0, x)], axis=0)

    at = stack(na * jnp.exp(cm))
    rt = stack(r * jnp.exp(cs))
    ecs = jnp.exp(-cs)
    bh = stack(bb * ecs)
    kh = stack(ks * ecs)
    ece = jnp.exp(ctot - cs)
    be = stack(bb * ece)
    ke = stack(ks * ece)
    vs = stack(v)

    rr = lax.broadcasted_iota(jnp.int32, (c2, c2), 0)
    cc = lax.broadcasted_iota(jnp.int32, (c2, c2), 1)
    same = (rr // c) == (cc // c)
    tr_, tc_ = rr % c, cc % c
    strict = same & ((tc_ > tr_) if rev else (tc_ < tr_))
    incl = same & ((tc_ >= tr_) if rev else (tc_ <= tr_))
    diag_blk = same & ((tr_ // RWKV_SUB) == (tc_ // RWKV_SUB))

    lhs = jnp.concatenate([at, rt], axis=0)
    rhs = jnp.concatenate([bh, kh], axis=0)
    a4 = _dot_nt(lhs.astype(BF16), rhs.astype(BF16))
    yield
    nmat = jnp.where(strict, a4[:c2, :c2], 0.0)
    aak = jnp.where(strict, a4[:c2, c2:], 0.0)
    arb = jnp.where(incl, a4[c2:, :c2], 0.0)
    ark = jnp.where(incl, a4[c2:, c2:], 0.0)

    hb = ht.T.astype(BF16)
    vb = vs.astype(BF16)
    x0 = _dot(jnp.concatenate([at, aak], axis=1).astype(BF16), jnp.concatenate([hb, vb], axis=0))
    yield

    nd = jnp.where(diag_blk, nmat, 0.0)
    no = nmat - nd
    tm = nd
    pw = nd
    for _ in range(3):
        pw = _dot_b(pw, pw)
        yield
        tm = tm + pw + _dot_b(pw, tm)
        yield
    zu = _dot_b(tm, jnp.concatenate([no, x0], axis=1))
    z = no + zu[:, :c2]
    u = x0 + zu[:, c2:]
    yield
    zz = _dot_b(z, jnp.concatenate([z, u], axis=1))
    u = u + zz[:, c2:]
    yield
    u = u + _dot_b(zz[:, :c2], u)
    yield

    ub = u.astype(BF16)
    ys = _dot(jnp.concatenate([rt, arb, ark], axis=1).astype(BF16), jnp.concatenate([hb, ub, vb], axis=0))
    yield
    hn = ht * jnp.exp(ctot) + _dot_tn(jnp.concatenate([ub, vb], axis=0),
                                      jnp.concatenate([be, ke], axis=0).astype(BF16))
    hr = lax.broadcasted_iota(jnp.int32, (LANES, LANES), 0) // n
    hc = lax.broadcasted_iota(jnp.int32, (LANES, LANES), 1) // n
    return ys[:c] + ys[c:], jnp.where(hr == hc, hn, 0.0)


def _rwkv_scan_body(*refs):
    ins, (y0_ref, y1_ref, st_ref) = refs[:12], refs[12:]

    @pl.when(pl.program_id(2) == 0)
    def _():
        st_ref[...] = jnp.zeros_like(st_ref)

    gens, dest = [], []
    for dr in range(2):
        r_ref, lw_ref, ks_ref, v_ref, na_ref, bb_ref = ins[6 * dr:6 * dr + 6]
        for p in range(r_ref.shape[2] // LANES):
            sl = slice(p * LANES, (p + 1) * LANES)
            gens.append(_rwkv_chunk(r_ref[0, :, sl], lw_ref[0, :, sl], ks_ref[0, :, sl], v_ref[0, :, sl],
                                    na_ref[0, :, sl], bb_ref[0, :, sl], st_ref[dr, p], rev=(dr == 1)))
            dest.append(((y0_ref, y1_ref)[dr], sl, dr, p))
    for (y_ref, sl, dr, p), (y, hn) in zip(dest, _lockstep(gens)):
        y_ref[0, :, sl] = y
        st_ref[dr, p] = hn


def rwkv_scan(r, v, na, lw, ks, bb):
    bsz, t, d = r.shape
    c = RWKV_CHUNK
    nc, ncx = t // c, CTX_LEN // c
    pp = _pick(d // LANES, (8, 4, 2, 1))
    w = pp * LANES

    def rchunk(s):
        return jnp.where(s < ncx, ncx - 1 - s, nc - 1 - (s - ncx))

    fwd = pl.BlockSpec((1, c, w), lambda b, p, s: (b, s, p))
    bwd = pl.BlockSpec((1, c, w), lambda b, p, s: (b, rchunk(s), p))
    sh = jax.ShapeDtypeStruct((bsz, t, d), F32)
    return pl.pallas_call(
        _rwkv_scan_body,
        grid=(bsz, d // w, nc),
        in_specs=[fwd] * 6 + [bwd] * 6, out_specs=[fwd, bwd], out_shape=[sh, sh],
        scratch_shapes=[pltpu.VMEM((2, pp, LANES, LANES), F32)],
        compiler_params=_cparams("parallel", "parallel", "arbitrary"),
    )(r, lw[0], ks[0], v, na, bb[0], r, lw[1], ks[1], v, na, bb[1])


def _rwkv_out_body(y0_ref, y1_ref, r_ref, v_ref, g_ref, ks0_ref, ks1_ref, rk_ref, lg_ref, lb_ref, o_ref):
    e = _head_ones()
    y = y0_ref[0] + y1_ref[0]
    inv_n = 1.0 / RWKV_HEAD
    yc = y - _head_sum(y, e) * inv_n
    yn = yc * lax.rsqrt(_head_sum(yc * yc, e) * inv_n + RWKV_GN_EPS)
    yn = yn * lg_ref[...] + lb_ref[...]
    ksum = ks0_ref[0].astype(F32) + ks1_ref[0].astype(F32)
    bonus = _head_sum(r_ref[0].astype(F32) * ksum * rk_ref[...], e) * v_ref[0].astype(F32)
    o_ref[0] = ((yn + bonus) * g_ref[0].astype(F32)).astype(o_ref.dtype)


def rwkv_mixer(u, mu, w_r, w_k, w_v, w_o, w0, w1, w2, a0, a1, a2, g1, g2, k_k, k_a, r_k, ln_g, ln_b):
    bsz, t, d = u.shape
    m = bsz * t
    tm_ = _pick(CTX_LEN, (128, 64, 32, 16))
    hr = 16
    rpt, last = tm_ // hr, t // hr - 1
    tile = pl.BlockSpec((1, tm_, d), lambda b, i: (b, i, 0))
    xs = pl.pallas_call(
        functools.partial(_rwkv_mix_body, nctx_tiles=CTX_LEN // tm_), grid=(bsz, t // tm_),
        in_specs=[tile,
                  pl.BlockSpec((1, hr, d), lambda b, i: (b, jnp.maximum(i * rpt - 1, 0), 0)),
                  pl.BlockSpec((1, hr, d), lambda b, i: (b, jnp.minimum((i + 1) * rpt, last), 0)),
                  pl.BlockSpec((6, d), lambda b, i: (0, 0))],
        out_specs=[tile] * 6,
        out_shape=[jax.ShapeDtypeStruct((bsz, t, d), BF16)] * 6,
        compiler_params=_cparams("parallel", "parallel"),
    )(u, u, u, mu)
    xr, xw, xk, xv, xa, xg = (x.reshape(m, d) for x in xs)
    r = matmul(xr, w_r.astype(BF16), BF16).reshape(bsz, t, d)
    k = matmul(xk, w_k.astype(BF16)).reshape(bsz, t, d)
    v = matmul(xv, w_v.astype(BF16), BF16).reshape(bsz, t, d)
    rk = w1.shape[2]
    hw = matmul(xw, jnp.concatenate([w1[0], w1[1]], 1).astype(BF16)).reshape(bsz, t, 2 * rk)
    ha = matmul(xa, jnp.concatenate([a1[0], a1[1]], 1).astype(BF16)).reshape(bsz, t, 2 * rk)
    gr = g1.shape[1]
    grp = -(-gr // LANES) * LANES
    g1p = jnp.zeros((d, grp), F32).at[:, :gr].set(g1)
    g2p = jnp.zeros((grp, d), F32).at[:gr].set(g2)
    hg = jax.nn.sigmoid(matmul(xg, g1p.astype(BF16)))
    g = matmul(hg, g2p.astype(BF16), BF16).reshape(bsz, t, d)

    tr = _pick(t, (64, 32, 16))
    row = pl.BlockSpec((1, tr, d), lambda b, i: (b, i, 0))
    low = pl.BlockSpec((1, tr, 2 * rk), lambda b, i: (b, i, 0))
    vec = pl.BlockSpec((1, d), lambda b, i: (0, 0))
    vec2 = pl.BlockSpec((2, d), lambda b, i: (0, 0))
    fact = pl.BlockSpec((2, rk, d), lambda b, i: (0, 0, 0))
    big = jax.ShapeDtypeStruct((bsz, t, d), F32)
    half = jax.ShapeDtypeStruct((bsz, t, d), BF16)
    na, lw0, lw1, ks0, ks1, bb0, bb1 = pl.pallas_call(
        _rwkv_prep_body, grid=(bsz, t // tr),
        in_specs=[row, low, low, fact, fact, vec2, vec2, vec, vec],
        out_specs=[row] * 7, out_shape=[half, big, big, half, half, half, half],
        compiler_params=_cparams("parallel", "parallel"),
    )(k, hw, ha, w2.astype(BF16), a2.astype(BF16), w0, a0, k_k[None], k_a[None])

    y0, y1 = rwkv_scan(r, v, na, (lw0, lw1), (ks0, ks1), (bb0, bb1))

    out = pl.pallas_call(
        _rwkv_out_body, grid=(bsz, t // tr),
        in_specs=[row] * 7 + [vec, vec, vec], out_specs=row,
        out_shape=jax.ShapeDtypeStruct((bsz, t, d), BF16),
        compiler_params=_cparams("parallel", "parallel"),
    )(y0, y1, r, v, g, ks0, ks1, r_k.reshape(1, d), ln_g[None], ln_b[None])
    return matmul(out.reshape(m, d), w_o.astype(BF16), BF16).reshape(bsz, t, d)


def _merge_exchange(n):
    t = max(1, (n - 1).bit_length())
    p = 1 << (t - 1)
    pairs = []
    while p > 0:
        q, r, d = 1 << (t - 1), 0, p
        while d > 0:
            pairs += [(i, i + d) for i in range(n - d) if (i & p) == r]
            d, q, r = q - p, q >> 1, p
        p >>= 1
    return pairs


def _pop_heads(levels, kk):
    out = []
    for k in range(kk):
        head = levels[0]
        m = jnp.max(head, axis=0, keepdims=True)
        out.append(m)
        need = min(len(levels), kk - k - 1)
        pop = head == m
        levels = [jnp.where(pop, levels[r + 1] if r + 1 < len(levels) else NEG, levels[r]) for r in range(need)]
    return jnp.concatenate(out, axis=0)


def _top_vals(s, kk):
    groups = [s[r:r + SUBLANES] for r in range(0, s.shape[0], SUBLANES)]
    for i, j in _merge_exchange(len(groups)):
        groups[i], groups[j] = jnp.maximum(groups[i], groups[j]), jnp.minimum(groups[i], groups[j])
    return _pop_heads(groups, kk)


def _peer_score_body(q_ref, keys_ref, thr_ref, s2_ref, e2_ref, cf_ref):
    nk = PEER_NKEYS
    kk = PEER_TOPK
    for h in range(PEER_HEADS):
        q1 = q_ref[h * 2 * nk:h * 2 * nk + nk, :].astype(BF16)
        q2 = q_ref[h * 2 * nk + nk:(h + 1) * 2 * nk, :].astype(BF16)
        s1 = _dot(keys_ref[0], q1)
        s2 = _dot(keys_ref[1], q2)
        a1 = _top_vals(s1, kk + 1)
        a2 = _top_vals(s2, kk + 1)
        cv = _pop_heads([a1[:kk] + a2[q:q + 1] for q in range(kk)], kk + 1)
        c17 = jnp.maximum(cv[kk:kk + 1], jnp.maximum(a1[kk:kk + 1] + a2[0:1], a1[0:1] + a2[kk:kk + 1]))
        theta = 0.5 * (cv[kk - 1:kk] + c17)
        zsum = jnp.sum(jnp.exp(cv[:kk] - cv[0:1]), axis=0, keepdims=True)
        thr_ref[h] = theta - s1
        s2_ref[h] = s2
        e2_ref[h] = jnp.exp(s2 - a2[0:1])
        cf_ref[h] = jnp.exp(s1 - a1[0:1]) / zsum


def _gelu(x):
    return 0.5 * x * (1.0 + lax.erf(x * (2.0 ** -0.5)))


def _peer_dense_body(z_ref, u_ref, v_ref, thr_ref, s2_ref, e2_ref, cf_ref, o_ref, w_ref):
    nk = PEER_NKEYS
    eb = pl.program_id(1)
    te = u_ref.shape[1]
    n_i = te // nk
    strip = PEER_ROW_STRIP
    stages = PEER_STAGES
    sub = min(PEER_TOK_SUB, z_ref.shape[1])
    d = z_ref.shape[0]
    kc = d // stages
    tiles = [slice(t0, t0 + sub) for t0 in range(0, z_ref.shape[1], sub)]

    @pl.when(eb == 0)
    def _():
        o_ref[...] = jnp.zeros_like(o_ref)

    def first_matmul(ts):
        acc = None
        for k0 in range(0, d, kc):
            part = _dot(u_ref[0, :, k0:k0 + kc], z_ref[k0:k0 + kc, ts])
            acc = part if acc is None else acc + part
            yield
        return acc

    def gates(ts):
        per_stage = (nk // strip) * PEER_HEADS // stages
        n = 0
        for si in range(nk // strip):
            js = slice(si * strip, (si + 1) * strip)
            w = [None] * n_i
            for h in range(PEER_HEADS):
                s2s = s2_ref[h, js, ts]
                e2s = e2_ref[h, js, ts]
                for ii in range(n_i):
                    i = eb * n_i + ii
                    c = jnp.where(s2s >= thr_ref[h, pl.ds(i, 1), ts], e2s, 0.0) * cf_ref[h, pl.ds(i, 1), ts]
                    w[ii] = c if w[ii] is None else w[ii] + c
                n += 1
                if n % per_stage == 0 and h + 1 < PEER_HEADS:
                    yield
            for ii in range(n_i):
                w_ref[ii * nk + si * strip:ii * nk + (si + 1) * strip, ts] = w[ii].astype(BF16)
            yield

    def activate(ts, act):
        rows = te // stages
        for r0 in range(0, te, rows):
            w_ref[r0:r0 + rows, ts] = w_ref[r0:r0 + rows, ts] * _gelu(act[r0:r0 + rows]).astype(BF16)
            yield

    def second_matmul(ts):
        rows = d // stages
        for r0 in range(0, d, rows):
            o_ref[r0:r0 + rows, ts] += _dot(v_ref[0, r0:r0 + rows, :], w_ref[:, ts])
            yield

    acts = {}
    for ph in range(len(tiles) + 2):
        gens, tags = [], []
        if ph < len(tiles):
            gens += [first_matmul(tiles[ph]), gates(tiles[ph])]
            tags += [ph, None]
        if 0 <= ph - 1 < len(tiles):
            gens.append(activate(tiles[ph - 1], acts[ph - 1]))
            tags.append(None)
        if 0 <= ph - 2 < len(tiles):
            gens.append(second_matmul(tiles[ph - 2]))
            tags.append(None)
        for tag, res in zip(tags, _lockstep(gens)):
            if tag is not None:
                acts[tag] = res


def peer_ffn(zt, w_q, sub_keys, u_all, vt_all, layer):
    d, m = zt.shape
    nh, nk = PEER_HEADS, PEER_NKEYS
    qt = matmul(w_q.T.astype(BF16), zt, F32)
    tt = _pick(m, (256, 128))
    sh = jax.ShapeDtypeStruct((nh, nk, m), F32)
    blk = pl.BlockSpec((nh, nk, tt), lambda i: (0, 0, i))
    thr, s2, e2, cf = pl.pallas_call(
        _peer_score_body, grid=(m // tt,),
        in_specs=[pl.BlockSpec((nh * 2 * nk, tt), lambda i: (0, i)),
                  pl.BlockSpec((2, nk, PEER_DKEY // 2), lambda i: (0, 0, 0))],
        out_specs=[blk, blk, blk, blk], out_shape=[sh, sh, sh, sh],
        compiler_params=_cparams("parallel"),
    )(qt, sub_keys.astype(BF16))

    tm = _pick(m, (2 * PEER_TOK_SUB, PEER_TOK_SUB, LANES))
    te = PEER_EXP_BLOCK
    ne = u_all.shape[1]
    once = pl.Buffered(1)
    sblk = pl.BlockSpec((nh, nk, tm), lambda i, e: (0, 0, i), pipeline_mode=once)
    return pl.pallas_call(
        _peer_dense_body, grid=(m // tm, ne // te),
        in_specs=[pl.BlockSpec((d, tm), lambda i, e: (0, i), pipeline_mode=once),
                  pl.BlockSpec((1, te, d), lambda i, e: (layer, e, 0)),
                  pl.BlockSpec((1, d, te), lambda i, e: (layer, 0, e)),
                  sblk, sblk, sblk, sblk],
        out_specs=pl.BlockSpec((d, tm), lambda i, e: (0, i)),
        out_shape=jax.ShapeDtypeStruct((d, m), F32),
        scratch_shapes=[pltpu.VMEM((te, tm), BF16)],
        compiler_params=_cparams("parallel", "arbitrary"),
    )(zt, u_all, vt_all, thr, s2, e2, cf)


def _ada_mods(c, c_ctx, w_down, w_up, b_up):
    bsz, d = c.shape
    cond = jnp.concatenate([c, c_ctx[None]], 0)
    pad = 16 - cond.shape[0] % 16
    cond = jnp.concatenate([cond, jnp.zeros((pad, d), F32)], 0)
    hid = matmul(jax.nn.silu(cond), w_down.astype(BF16), F32)
    m = (matmul(hid, w_up.astype(BF16), F32) + b_up)[:bsz + 1].reshape(bsz + 1, N_MOD, d)
    return jnp.stack([jnp.broadcast_to(m[bsz], (bsz, N_MOD, d)), m[:bsz]], axis=1)


def kernel(x, c, ctx, c_ctx, ada_w_down, ada_w_up, ada_b,
           gla_w_in, gla_w_g1, gla_w_g2, gla_b_g, gla_norm_g, gla_w_o,
           mla_w_in, mla_q_norm, mla_kv_norm, mla_w_uq, mla_w_ukv, mla_w_o,
           rwkv_mu, rwkv_w_r, rwkv_w_k, rwkv_w_v, rwkv_w_o, rwkv_w0, rwkv_w1, rwkv_w2,
           rwkv_a0, rwkv_a1, rwkv_a2, rwkv_g1, rwkv_g2, rwkv_k_k, rwkv_k_a, rwkv_r_k, rwkv_ln_g, rwkv_ln_b,
           peer_w_q, peer_sub_keys, peer_u, peer_v):
    z = jnp.concatenate([ctx, x], axis=1)
    bsz, t, d = z.shape
    mods = [_ada_mods(c, c_ctx, ada_w_down[i], ada_w_up[i], ada_b[i]) for i in range(DEPTH)]
    peer_ub = peer_u.astype(BF16)
    peer_vtb = jnp.swapaxes(peer_v, 1, 2).astype(BF16)
    _, u = ln_mod(z, mods_n=mods[0], sidx=0)
    for i in range(DEPTH):
        j = i // N_MIXERS
        if i % N_MIXERS == 0:
            y = gla_mixer(u, gla_w_in[j], gla_w_g1[j], gla_w_g2[j], gla_b_g[j], gla_norm_g[j], gla_w_o[j])
        elif i % N_MIXERS == 1:
            y = mla_mixer(u, mla_w_in[j], mla_q_norm[j], mla_kv_norm[j], mla_w_uq[j], mla_w_ukv[j], mla_w_o[j])
        else:
            y = rwkv_mixer(u, rwkv_mu[j], rwkv_w_r[j], rwkv_w_k[j], rwkv_w_v[j], rwkv_w_o[j],
                           rwkv_w0[j], rwkv_w1[j], rwkv_w2[j], rwkv_a0[j], rwkv_a1[j], rwkv_a2[j],
                           rwkv_g1[j], rwkv_g2[j], rwkv_k_k[j], rwkv_k_a[j], rwkv_r_k[j],
                           rwkv_ln_g[j], rwkv_ln_b[j])
        z, ut = ln_mod(z, y, mods[i], 2, mods[i], 3, u_t=True)
        ht = peer_ffn(ut, peer_w_q[i], peer_sub_keys[i], peer_ub, peer_vtb, i)
        if i + 1 < DEPTH:
            z, u = ln_mod(z, ht, mods[i], 5, mods[i + 1], 0, y_t=True)
        else:
            z, _ = ln_mod(z, ht, mods[i], 5, y_t=True, latent_only=True)
    return z
```

```python
import functools

import jax
import jax.numpy as jnp
from jax import lax
from jax.experimental import pallas as pl
from jax.experimental.pallas import tpu as pltpu

F32 = jnp.float32
BF16 = jnp.bfloat16

DEPTH = 4
CTX_LEN = 256
GRID_W = 64
N_MIXERS = 3
N_MOD = 6
LN_EPS = 1e-6
DEEPNORM_ALPHA = (2.0 * DEPTH) ** 0.25

GLA_HEADS = 8
GLA_GATE_RANK = 16
GLA_GATE_NORMALIZER = 16.0

MLA_Q_RANK = 1536
MLA_KV_RANK = 512
MLA_NOPE = 128
MLA_ROPE = 64
MLA_V = 128
ROPE_THETA = 10000.0

RWKV_HEAD = 64
RWKV_GN_EPS = 64e-5
RWKV_DECAY_SCALE = 0.6065306597126334

PEER_HEADS = 8
PEER_NKEYS = 128
PEER_DKEY = 256
PEER_TOPK = 16

LANES = 128
SUBLANES = 8
MXU_COLS = 256
VMEM_LIMIT = 52 * 1024 * 1024

GLA_CHUNK = 128
GLA_HEADS_PER_STEP = 4
GLA_SUB = 4
RWKV_CHUNK = 64
RWKV_SUB = 16
MLA_KEY_CHAINS = 2
LOG2E = 1.4426950408889634
PEER_TOK_SUB = 256
PEER_EXP_BLOCK = 512
PEER_STAGES = 4
PEER_ROW_STRIP = 32
NEG = -1e30


def _pick(n, cands):
    for c in cands:
        if n % c == 0:
            return c
    return n


def _cparams(*sem):
    return pltpu.CompilerParams(dimension_semantics=sem, vmem_limit_bytes=VMEM_LIMIT)


def _dot(a, b):
    return jnp.dot(a, b, preferred_element_type=F32)


def _dot_nt(a, b):
    return lax.dot_general(a, b, (((1,), (1,)), ((), ())), preferred_element_type=F32)


def _dot_tn(a, b):
    return lax.dot_general(a, b, (((0,), (0,)), ((), ())), preferred_element_type=F32)


def _split3(x):
    hi = x.astype(BF16)
    r1 = x - hi.astype(F32)
    mid = r1.astype(BF16)
    lo = (r1 - mid.astype(F32)).astype(BF16)
    return hi, mid, lo


def _dot_sel(sel, x):
    s = sel.astype(BF16)
    hi, mid, lo = _split3(x)
    return _dot(s, hi) + _dot(s, mid) + _dot(s, lo)


def _dot_x3(a, b, dims=None):
    ah = a.astype(BF16)
    al = (a - ah.astype(F32)).astype(BF16)
    bh = b.astype(BF16)
    bl = (b - bh.astype(F32)).astype(BF16)
    f = _dot if dims is None else dims
    return f(ah, bh) + f(ah, bl) + f(al, bh)


def _dot_b(a, b):
    return _dot(a.astype(BF16), b.astype(BF16))


def _log_sigmoid(x):
    return jnp.minimum(x, 0.0) - jnp.log1p(jnp.exp(-jnp.abs(x)))


def _sigmoid(x):
    return 1.0 / (1.0 + jnp.exp(-x))


def _mm_body(a_ref, b_ref, o_ref):
    o_ref[...] = _dot(a_ref[...].astype(BF16), b_ref[...].astype(BF16)).astype(o_ref.dtype)


def matmul(a, b, out_dtype=F32):
    m, k = a.shape
    k2, n = b.shape
    assert k == k2
    a_bytes = jnp.dtype(a.dtype).itemsize
    tm_cands = (1024, 512, 256, 128, 64, 32, 16) if a_bytes * k <= 8192 else (512, 256, 128, 64, 32, 16)
    tm = _pick(m, tm_cands)
    tn = _pick(n, (512, 256, 128))
    return pl.pallas_call(
        _mm_body,
        grid=(m // tm, n // tn),
        in_specs=[pl.BlockSpec((tm, k), lambda i, j: (i, 0)),
                  pl.BlockSpec((k, tn), lambda i, j: (0, j))],
        out_specs=pl.BlockSpec((tm, tn), lambda i, j: (i, j)),
        out_shape=jax.ShapeDtypeStruct((m, n), out_dtype),
        compiler_params=_cparams("parallel", "parallel"),
    )(a, b)


def _layer_norm(z):
    zc = z - jnp.mean(z, -1, keepdims=True)
    return zc * lax.rsqrt(jnp.mean(zc * zc, -1, keepdims=True) + LN_EPS)


def _ln_mod_body(*refs, gidx, sidx, has_y, y_t, u_t):
    it = iter(refs)
    z_ref = next(it)
    y_ref = next(it) if has_y else None
    mg_ref = next(it) if has_y else None
    mn_ref = next(it) if sidx is not None else None
    zo_ref = next(it) if has_y else None
    u_ref = next(it) if sidx is not None else None
    z = z_ref[0]
    if has_y:
        y = y_ref[...].T if y_t else y_ref[0]
        z = _layer_norm(DEEPNORM_ALPHA * z + mg_ref[0, 0, gidx:gidx + 1, :] * y)
        zo_ref[0] = z
    if sidx is not None:
        u = z * (1.0 + mn_ref[0, 0, sidx + 1:sidx + 2, :]) + mn_ref[0, 0, sidx:sidx + 1, :]
        if u_t:
            u_ref[...] = u.T.astype(u_ref.dtype)
        else:
            u_ref[0] = u.astype(u_ref.dtype)


def ln_mod(z, y=None, mods_g=None, gidx=None, mods_n=None, sidx=None, y_t=False, u_t=False, latent_only=False):
    assert not (latent_only and sidx is not None)
    bsz, t, d = z.shape
    tr = _pick(CTX_LEN, (256, 128, 64, 32, 16))
    nt = t // tr
    nctx = CTX_LEN // tr
    first = nctx if latent_only else 0
    has_y = y is not None
    row = pl.BlockSpec((1, tr, d), lambda b, i: (b, first + i, 0))
    col = pl.BlockSpec((d, tr), lambda b, i: (0, b * nt + first + i))
    mod = pl.BlockSpec((1, 1, N_MOD, d), lambda b, i: (b, jnp.where(first + i < nctx, 0, 1), 0, 0))
    ins, in_specs, outs, out_specs = [z], [row], [], []
    if has_y:
        ins += [y, mods_g]
        in_specs += [col if y_t else row, mod]
        outs.append(jax.ShapeDtypeStruct((bsz, t - first * tr, d), F32))
        out_specs.append(pl.BlockSpec((1, tr, d), lambda b, i: (b, i, 0)) if latent_only else row)
    if sidx is not None:
        ins.append(mods_n)
        in_specs.append(mod)
        outs.append(jax.ShapeDtypeStruct((d, bsz * t) if u_t else (bsz, t, d), BF16))
        out_specs.append(col if u_t else row)
    res = pl.pallas_call(
        functools.partial(_ln_mod_body, gidx=gidx, sidx=sidx, has_y=has_y, y_t=y_t, u_t=u_t),
        grid=(bsz, nt - first), in_specs=in_specs, out_specs=out_specs, out_shape=outs,
        compiler_params=_cparams("parallel", "parallel"),
    )(*ins)
    res = list(res)
    zo = res.pop(0) if has_y else None
    u = res.pop(0) if sidx is not None else None
    return zo, u


def _lockstep(gens):
    out = [None] * len(gens)
    live = list(range(len(gens)))
    while live:
        for i in list(live):
            try:
                next(gens[i])
            except StopIteration as done:
                out[i] = done.value
                live.remove(i)
    return out


def _gla_chunk(q, k, v, glow, wg2, bg, st, rev):
    c, dk = q.shape
    sub = GLA_SUB
    q = q.astype(F32) * (dk ** -0.5)
    k = k.astype(F32)
    gl = _dot_x3(glow, wg2) + bg
    yield
    g = _log_sigmoid(gl) * (1.0 / GLA_GATE_NORMALIZER)
    row = lax.broadcasted_iota(jnp.int32, (c, c), 0)
    col = lax.broadcasted_iota(jnp.int32, (c, c), 1)
    tri = (row <= col) if rev else (row >= col)
    b = _dot_sel(tri.astype(F32), g)
    yield
    btot = jnp.sum(g, axis=0, keepdims=True)

    o = _dot_nt((q * jnp.exp(b)).astype(BF16), st.astype(BF16))

    rowk = lax.broadcasted_iota(jnp.int32, (c, 1), 0)
    halves = []
    hsz = c // 2
    while hsz >= sub:
        halves.append(hsz)
        hsz //= 2
    pick = jnp.concatenate(
        [(col == (row // (2 * hf)) * (2 * hf) + (hf if rev else hf - 1)).astype(F32) for hf in halves], axis=0)
    refs = _dot_sel(pick, b)
    yield
    att = jnp.zeros((c, c), F32)
    for lv, hf in enumerate(halves):
        refb = refs[lv * c:(lv + 1) * c]
        late = (rowk % (2 * hf)) >= hf
        q_side, k_side = (~late, late) if rev else (late, ~late)
        qf = jnp.where(q_side, q * jnp.exp(jnp.minimum(b - refb, 0.0)), 0.0)
        kf = jnp.where(k_side, k * jnp.exp(jnp.minimum(refb - b, 0.0)), 0.0)
        same = (row // (2 * hf)) == (col // (2 * hf))
        att = att + jnp.where(same, _dot_nt(qf.astype(BF16), kf.astype(BF16)), 0.0)
    yield

    rmod = rowk % sub
    for lag in range(sub):
        sh = (c - lag) % c if rev else lag
        ks = pltpu.roll(k, sh, 0) if sh else k
        bs = pltpu.roll(b, sh, 0) if sh else b
        term = jnp.sum(q * ks * jnp.exp(jnp.minimum(b - bs, 0.0)), axis=1, keepdims=True)
        valid = (rmod + lag < sub) if rev else (rmod >= lag)
        hit = (col == row + lag) if rev else (col == row - lag)
        att = att + jnp.where(hit & valid, term, 0.0)

    o = o + _dot(att.astype(BF16), v)
    yield
    kd = (k * jnp.exp(btot - b)).astype(BF16)
    return o, st * jnp.exp(btot) + _dot_tn(v, kd)


def _gla_body(*refs):
    ins, (of_ref, ob_ref, st_ref) = refs[:12], refs[12:]

    @pl.when(pl.program_id(2) == 0)
    def _():
        st_ref[...] = jnp.zeros_like(st_ref)

    dv = st_ref.shape[2]
    dk = st_ref.shape[3]
    gens, dest = [], []
    for dr in range(2):
        q_ref, k_ref, v_ref, gl_ref, wg2_ref, bg_ref = ins[6 * dr:6 * dr + 6]
        for hh in range(st_ref.shape[1]):
            ks, vs = slice(hh * dk, (hh + 1) * dk), slice(hh * dv, (hh + 1) * dv)
            gens.append(_gla_chunk(q_ref[0, :, ks], k_ref[0, :, ks], v_ref[0, :, vs], gl_ref[0],
                                   wg2_ref[:, ks], bg_ref[:, ks], st_ref[dr, hh], rev=(dr == 1)))
            dest.append((dr, hh, vs))
    for (dr, hh, vs), (o, st) in zip(dest, _lockstep(gens)):
        (of_ref, ob_ref)[dr][0, :, vs] = o.astype(of_ref.dtype)
        st_ref[dr, hh] = st


def gla_scan(p, glow, wg2, bg):
    bsz, t, d3 = p.shape
    d = d3 // 3
    h = GLA_HEADS
    dk, dv = (d // 2) // h, d // h
    c = GLA_CHUNK
    nc, ncx = t // c, CTX_LEN // c

    def rchunk(s):
        return jnp.where(s < ncx, ncx - 1 - s, nc - 1 - (s - ncx))

    hp = _pick(h, (GLA_HEADS_PER_STEP, 1))
    hg = h // hp

    def specs(chunk):
        return [pl.BlockSpec((1, c, hp * dk), lambda b, hh, s: (b, chunk(s), hh)),
                pl.BlockSpec((1, c, hp * dk), lambda b, hh, s: (b, chunk(s), hg + hh)),
                pl.BlockSpec((1, c, hp * dv), lambda b, hh, s: (b, chunk(s), hg + hh)),
                pl.BlockSpec((1, c, LANES), lambda b, hh, s: (b, chunk(s), 0)),
                pl.BlockSpec((LANES, hp * dk), lambda b, hh, s: (0, hh)),
                pl.BlockSpec((1, hp * dk), lambda b, hh, s: (0, hh))]

    sh = jax.ShapeDtypeStruct((bsz, t, d), BF16)
    return pl.pallas_call(
        _gla_body,
        grid=(bsz, hg, nc),
        in_specs=specs(lambda s: s) + specs(rchunk),
        out_specs=[pl.BlockSpec((1, c, hp * dv), lambda b, hh, s: (b, s, hh)),
                   pl.BlockSpec((1, c, hp * dv), lambda b, hh, s: (b, rchunk(s), hh))],
        out_shape=[sh, sh],
        scratch_shapes=[pltpu.VMEM((2, hp, dv, dk), F32)],
        compiler_params=_cparams("parallel", "parallel", "arbitrary"),
    )(p, p, p, glow, wg2[0], bg[0], p, p, p, glow, wg2[1], bg[1])


def _gla_gate_body(of_ref, ob_ref, r_ref, g_ref, o_ref):
    dv = g_ref.shape[1]
    o = of_ref[0].astype(F32) + ob_ref[0].astype(F32)
    r = r_ref[0].astype(F32)
    outs = []
    for hh in range(o.shape[1] // dv):
        seg = o[:, hh * dv:(hh + 1) * dv]
        outs.append(seg * lax.rsqrt(jnp.mean(seg * seg, -1, keepdims=True) + 1e-6) * g_ref[...])
    o_ref[0] = (jnp.concatenate(outs, axis=1) * (r * _sigmoid(r))).astype(o_ref.dtype)


def gla_gate(o_f, o_b, p, norm_g):
    bsz, t, d = o_f.shape
    tr = _pick(t, (256, 128, 64, 32, 16))
    row = pl.BlockSpec((1, tr, d), lambda b, i: (b, i, 0))
    return pl.pallas_call(
        _gla_gate_body, grid=(bsz, t // tr),
        in_specs=[row, row, pl.BlockSpec((1, tr, d), lambda b, i: (b, i, 2)),
                  pl.BlockSpec((1, norm_g.shape[1]), lambda b, i: (0, 0))],
        out_specs=row, out_shape=jax.ShapeDtypeStruct((bsz, t, d), BF16),
        compiler_params=_cparams("parallel", "parallel"),
    )(o_f, o_b, p, norm_g)


def gla_mixer(u, w_in, w_g1, w_g2, b_g, norm_g, w_o):
    bsz, t, d = u.shape
    qk = d // 2
    u2 = u.reshape(bsz * t, d)
    p = matmul(u2, w_in.astype(BF16), BF16).reshape(bsz, t, 3 * d)
    r = GLA_GATE_RANK
    wg1 = jnp.zeros((d, LANES), F32).at[:, :r].set(w_g1[0]).at[:, r:2 * r].set(w_g1[1])
    glow = matmul(u2, wg1.astype(BF16), F32).reshape(bsz, t, LANES)
    wg2 = [jnp.zeros((LANES, qk), F32).at[s * r:(s + 1) * r].set(w_g2[s]) for s in range(2)]
    outs = gla_scan(p, glow, wg2, [b_g[0][None], b_g[1][None]])
    gated = gla_gate(outs[0], outs[1], p, norm_g[None])
    return matmul(gated.reshape(bsz * t, d), w_o.astype(BF16), BF16).reshape(bsz, t, d)


def _rms(x, gain):
    return x * lax.rsqrt(jnp.mean(x * x, -1, keepdims=True) + 1e-6) * gain


def _rope128(a, c, s):
    return a * c + pltpu.roll(a, LANES // 2, 1) * s


def _mla_norm_body(h_ref, qg_ref, kg_ref, c_ref, s_ref, cq_ref, ckv_ref, kr_ref):
    qr, kvr = cq_ref.shape[2], ckv_ref.shape[2]
    h = h_ref[0]
    cq_ref[0] = _rms(h[:, :qr], qg_ref[...]).astype(BF16)
    ckv_ref[0] = _rms(h[:, qr:qr + kvr], kg_ref[...]).astype(BF16)
    kr_ref[0] = _rope128(h[:, qr + kvr:qr + kvr + LANES], c_ref[...], s_ref[...]).astype(BF16)


def _mla_attn_body(q_ref, kn_ref, v_ref, kr_ref, c_ref, s_ref, o_ref, kf_ref, vf_ref, *, nctx_tiles, scale):
    qt = pl.program_id(2)
    t = kf_ref.shape[0]
    ctx = nctx_tiles * q_ref.shape[1]

    @pl.when(qt == 0)
    def _():
        kf_ref[:, :LANES] = kn_ref[0]
        kf_ref[:, LANES:] = kr_ref[0]
        vf_ref[:, :LANES] = v_ref[0]
        vf_ref[:, LANES:] = jnp.ones((t, LANES), BF16)

    q = q_ref[0].astype(F32)
    qr = _rope128(q[:, LANES:], c_ref[...], s_ref[...])
    qf = (jnp.concatenate([q[:, :LANES], qr], axis=1) * (scale * LOG2E)).astype(BF16)

    def attend(keys):
        s = _dot_nt(qf, kf_ref[keys])
        yield
        m = jnp.max(s, -1, keepdims=True)
        p = jnp.exp2(s - m).astype(BF16)
        return m, _dot(p, vf_ref[keys])

    def finish(parts):
        m = functools.reduce(jnp.maximum, [pm for pm, _ in parts])
        acc = sum(jnp.exp2(pm - m) * po for pm, po in parts)
        o_ref[0] = (acc[:, :LANES] / acc[:, LANES:LANES + 1]).astype(o_ref.dtype)

    @pl.when(qt < nctx_tiles)
    def _():
        finish(_lockstep([attend(slice(0, ctx))]))

    @pl.when(qt >= nctx_tiles)
    def _():
        step = t // MLA_KEY_CHAINS
        finish(_lockstep([attend(slice(k0, k0 + step)) for k0 in range(0, t, step)]))


def _rope_tables(t):
    rows = (t - CTX_LEN) // GRID_W
    row = jnp.repeat(jnp.arange(rows, dtype=F32), GRID_W)
    colp = jnp.tile(jnp.arange(GRID_W, dtype=F32), rows)
    n_freq = MLA_ROPE // 4
    inv = ROPE_THETA ** (-jnp.arange(n_freq, dtype=F32) / n_freq)
    ang = jnp.concatenate([row[:, None] * inv, colp[:, None] * inv], -1)
    cos = jnp.concatenate([jnp.ones((CTX_LEN, MLA_ROPE // 2), F32), jnp.cos(ang)], 0)
    sin = jnp.concatenate([jnp.zeros((CTX_LEN, MLA_ROPE // 2), F32), jnp.sin(ang)], 0)
    z = jnp.zeros_like(cos)
    return jnp.concatenate([cos, cos, z, z], 1), jnp.concatenate([-sin, sin, z, z], 1)


def mla_mixer(u, w_in, q_norm, kv_norm, w_uq, w_ukv, w_o):
    bsz, t, d = u.shape
    nh = d // 128
    qr, kvr, rp = MLA_Q_RANK, MLA_KV_RANK, MLA_ROPE
    ev, od = jnp.arange(0, rp, 2), jnp.arange(1, rp, 2)
    perm = jnp.concatenate([ev, od, od, ev])
    hw = -(-(qr + kvr + LANES) // MXU_COLS) * MXU_COLS
    w_in_p = jnp.concatenate([w_in[:, :qr + kvr], w_in[:, qr + kvr + perm],
                              jnp.zeros((d, hw - (qr + kvr + LANES)), F32)], axis=1)
    h = matmul(u.reshape(bsz * t, d), w_in_p.astype(BF16), F32).reshape(bsz, t, hw)
    ctab, stab = _rope_tables(t)

    tr = _pick(t, (256, 128, 64, 32, 16))
    cq, ckv, kr = pl.pallas_call(
        _mla_norm_body, grid=(bsz, t // tr),
        in_specs=[pl.BlockSpec((1, tr, hw), lambda b, i: (b, i, 0)),
                  pl.BlockSpec((1, qr), lambda b, i: (0, 0)),
                  pl.BlockSpec((1, kvr), lambda b, i: (0, 0)),
                  pl.BlockSpec((tr, LANES), lambda b, i: (i, 0)),
                  pl.BlockSpec((tr, LANES), lambda b, i: (i, 0))],
        out_specs=[pl.BlockSpec((1, tr, qr), lambda b, i: (b, i, 0)),
                   pl.BlockSpec((1, tr, kvr), lambda b, i: (b, i, 0)),
                   pl.BlockSpec((1, tr, LANES), lambda b, i: (b, i, 0))],
        out_shape=[jax.ShapeDtypeStruct((bsz, t, qr), BF16),
                   jax.ShapeDtypeStruct((bsz, t, kvr), BF16),
                   jax.ShapeDtypeStruct((bsz, t, LANES), BF16)],
        compiler_params=_cparams("parallel", "parallel"),
    )(h, q_norm[None], kv_norm[None], ctab, stab)

    hq = MLA_NOPE + rp
    qcols = (jnp.arange(nh)[:, None] * hq
             + jnp.concatenate([jnp.arange(MLA_NOPE), MLA_NOPE + perm])[None, :]).reshape(-1)
    q = matmul(cq.reshape(bsz * t, qr), w_uq[:, qcols].astype(BF16), BF16).reshape(bsz, t, nh * 2 * LANES)
    kv = matmul(ckv.reshape(bsz * t, kvr), w_ukv.astype(BF16), BF16).reshape(bsz, t, nh * 2 * LANES)

    tq = _pick(CTX_LEN, (256, 128, 64, 32, 16))
    scale = (MLA_NOPE + rp) ** -0.5
    o = pl.pallas_call(
        functools.partial(_mla_attn_body, nctx_tiles=CTX_LEN // tq, scale=scale),
        grid=(bsz, nh, t // tq),
        in_specs=[pl.BlockSpec((1, tq, 2 * LANES), lambda b, hh, i: (b, i, hh)),
                  pl.BlockSpec((1, t, LANES), lambda b, hh, i: (b, 0, 2 * hh)),
                  pl.BlockSpec((1, t, LANES), lambda b, hh, i: (b, 0, 2 * hh + 1)),
                  pl.BlockSpec((1, t, LANES), lambda b, hh, i: (b, 0, 0)),
                  pl.BlockSpec((tq, LANES), lambda b, hh, i: (i, 0)),
                  pl.BlockSpec((tq, LANES), lambda b, hh, i: (i, 0))],
        out_specs=pl.BlockSpec((1, tq, LANES), lambda b, hh, i: (b, i, hh)),
        out_shape=jax.ShapeDtypeStruct((bsz, t, nh * MLA_V), BF16),
        scratch_shapes=[pltpu.VMEM((t, 2 * LANES), BF16), pltpu.VMEM((t, 2 * LANES), BF16)],
        compiler_params=_cparams("parallel", "parallel", "arbitrary"),
    )(q, kv, kv, kr, ctab, stab)
    return matmul(o.reshape(bsz * t, nh * MLA_V), w_o.astype(BF16), BF16).reshape(bsz, t, d)


def _rwkv_mix_body(u_ref, before_ref, after_ref, mu_ref, *o_refs, nctx_tiles):
    i = pl.program_id(1)
    tr = u_ref.shape[1]
    hr = before_ref.shape[1]
    u = u_ref[0].astype(F32)
    row = lax.broadcasted_iota(jnp.int32, (tr, 1), 0)
    starts = (i == 0) | (i == nctx_tiles)
    ends = (i == nctx_tiles - 1) | (i == pl.num_programs(1) - 1)
    edge_prev = jnp.where(starts, 0.0, before_ref[0, hr - 1:hr, :].astype(F32))
    edge_next = jnp.where(ends, 0.0, after_ref[0, 0:1, :].astype(F32))
    prev = jnp.where(row == 0, edge_prev, pltpu.roll(u, 1, 0))
    nxt = jnp.where(row == tr - 1, edge_next, pltpu.roll(u, tr - 1, 0))
    xx = 0.5 * (prev + nxt) - u
    for n, o_ref in enumerate(o_refs):
        o_ref[0] = (u + xx * mu_ref[n:n + 1, :]).astype(o_ref.dtype)


def _head_sum(x, e):
    eb = e.astype(BF16)
    w = e.shape[0]
    outs = []
    for s in range(x.shape[1] // w):
        xs = x[:, s * w:(s + 1) * w]
        hi = xs.astype(BF16)
        lo = (xs - hi.astype(F32)).astype(BF16)
        outs.append(_dot(hi, eb) + _dot(lo, eb))
    return jnp.concatenate(outs, axis=1)


def _head_ones():
    r = lax.broadcasted_iota(jnp.int32, (LANES, LANES), 0) // RWKV_HEAD
    c = lax.broadcasted_iota(jnp.int32, (LANES, LANES), 1) // RWKV_HEAD
    return (r == c).astype(F32)


def _rwkv_prep_body(k_ref, hw_ref, ha_ref, w2_ref, a2_ref, w0_ref, a0_ref, kk_ref, ka_ref,
                    na_ref, lw0_ref, lw1_ref, ks0_ref, ks1_ref, bb0_ref, bb1_ref):
    rk = w2_ref.shape[1]
    k = k_ref[0]
    kk = k * kk_ref[...]
    kk = kk * lax.rsqrt(_head_sum(kk * kk, _head_ones()) + 1e-12)
    na_ref[0] = (-kk).astype(na_ref.dtype)
    hw = jnp.tanh(hw_ref[0])
    ha = ha_ref[0]
    for s, (lw_ref, ks_ref, bb_ref) in enumerate(((lw0_ref, ks0_ref, bb0_ref), (lw1_ref, ks1_ref, bb1_ref))):
        wl = w0_ref[s:s + 1, :] + _dot(hw[:, s * rk:(s + 1) * rk].astype(BF16), w2_ref[s])
        lw_ref[0] = (-RWKV_DECAY_SCALE) * _sigmoid(wl)
        a = _sigmoid(a0_ref[s:s + 1, :] + _dot(ha[:, s * rk:(s + 1) * rk].astype(BF16), a2_ref[s]))
        ks_ref[0] = (k * (1.0 + (a - 1.0) * ka_ref[...])).astype(ks_ref.dtype)
        bb_ref[0] = (kk * a).astype(bb_ref.dtype)


def _rwkv_chunk(r, lw, ks, v, na, bb, ht, rev):
    c = r.shape[0]
    n = RWKV_HEAD
    c2 = 2 * c
    r, ks, v, na, bb = (x.astype(F32) for x in (r, ks, v, na, bb))
    ri = lax.broadcasted_iota(jnp.int32, (c, c), 0)
    ci = lax.broadcasted_iota(jnp.int32, (c, c), 1)
    cs = _dot_sel(((ri <= ci) if rev else (ri >= ci)).astype(F32), lw)
    yield
    cm = cs - lw
    ctot = jnp.sum(lw, axis=0, keepdims=True)

    lane_a = lax.broadcasted_iota(jnp.int32, (1, LANES), 1) < n

    def stack(x):
        return jnp.concatenate([jnp.where(lane_a, x, 0.0), jnp.where(lane_a, 0.0, x)], axis=0)

    at = stack(na * jnp.exp(cm))
    rt = stack(r * jnp.exp(cs))
    ecs = jnp.exp(-cs)
    bh = stack(bb * ecs)
    kh = stack(ks * ecs)
    ece = jnp.exp(ctot - cs)
    be = stack(bb * ece)
    ke = stack(ks * ece)
    vs = stack(v)

    rr = lax.broadcasted_iota(jnp.int32, (c2, c2), 0)
    cc = lax.broadcasted_iota(jnp.int32, (c2, c2), 1)
    same = (rr // c) == (cc // c)
    tr_, tc_ = rr % c, cc % c
    strict = same & ((tc_ > tr_) if rev else (tc_ < tr_))
    incl = same & ((tc_ >= tr_) if rev else (tc_ <= tr_))
    diag_blk = same & ((tr_ // RWKV_SUB) == (tc_ // RWKV_SUB))

    lhs = jnp.concatenate([at, rt], axis=0)
    rhs = jnp.concatenate([bh, kh], axis=0)
    a4 = _dot_nt(lhs.astype(BF16), rhs.astype(BF16))
    yield
    nmat = jnp.where(strict, a4[:c2, :c2], 0.0)
    aak = jnp.where(strict, a4[:c2, c2:], 0.0)
    arb = jnp.where(incl, a4[c2:, :c2], 0.0)
    ark = jnp.where(incl, a4[c2:, c2:], 0.0)

    hb = ht.T.astype(BF16)
    vb = vs.astype(BF16)
    x0 = _dot(jnp.concatenate([at, aak], axis=1).astype(BF16), jnp.concatenate([hb, vb], axis=0))
    yield

    nd = jnp.where(diag_blk, nmat, 0.0)
    no = nmat - nd
    tm = nd
    pw = nd
    for _ in range(3):
        pw = _dot_b(pw, pw)
        yield
        tm = tm + pw + _dot_b(pw, tm)
        yield
    zu = _dot_b(tm, jnp.concatenate([no, x0], axis=1))
    z = no + zu[:, :c2]
    u = x0 + zu[:, c2:]
    yield
    zz = _dot_b(z, jnp.concatenate([z, u], axis=1))
    u = u + zz[:, c2:]
    yield
    u = u + _dot_b(zz[:, :c2], u)
    yield

    ub = u.astype(BF16)
    ys = _dot(jnp.concatenate([rt, arb, ark], axis=1).astype(BF16), jnp.concatenate([hb, ub, vb], axis=0))
    yield
    hn = ht * jnp.exp(ctot) + _dot_tn(jnp.concatenate([ub, vb], axis=0),
                                      jnp.concatenate([be, ke], axis=0).astype(BF16))
    hr = lax.broadcasted_iota(jnp.int32, (LANES, LANES), 0) // n
    hc = lax.broadcasted_iota(jnp.int32, (LANES, LANES), 1) // n
    return ys[:c] + ys[c:], jnp.where(hr == hc, hn, 0.0)


def _rwkv_scan_body(*refs):
    ins, (y0_ref, y1_ref, st_ref) = refs[:12], refs[12:]

    @pl.when(pl.program_id(2) == 0)
    def _():
        st_ref[...] = jnp.zeros_like(st_ref)

    gens, dest = [], []
    for dr in range(2):
        r_ref, lw_ref, ks_ref, v_ref, na_ref, bb_ref = ins[6 * dr:6 * dr + 6]
        for p in range(r_ref.shape[2] // LANES):
            sl = slice(p * LANES, (p + 1) * LANES)
            gens.append(_rwkv_chunk(r_ref[0, :, sl], lw_ref[0, :, sl], ks_ref[0, :, sl], v_ref[0, :, sl],
                                    na_ref[0, :, sl], bb_ref[0, :, sl], st_ref[dr, p], rev=(dr == 1)))
            dest.append(((y0_ref, y1_ref)[dr], sl, dr, p))
    for (y_ref, sl, dr, p), (y, hn) in zip(dest, _lockstep(gens)):
        y_ref[0, :, sl] = y
        st_ref[dr, p] = hn


def rwkv_scan(r, v, na, lw, ks, bb):
    bsz, t, d = r.shape
    c = RWKV_CHUNK
    nc, ncx = t // c, CTX_LEN // c
    pp = _pick(d // LANES, (8, 4, 2, 1))
    w = pp * LANES

    def rchunk(s):
        return jnp.where(s < ncx, ncx - 1 - s, nc - 1 - (s - ncx))

    fwd = pl.BlockSpec((1, c, w), lambda b, p, s: (b, s, p))
    bwd = pl.BlockSpec((1, c, w), lambda b, p, s: (b, rchunk(s), p))
    sh = jax.ShapeDtypeStruct((bsz, t, d), F32)
    return pl.pallas_call(
        _rwkv_scan_body,
        grid=(bsz, d // w, nc),
        in_specs=[fwd] * 6 + [bwd] * 6, out_specs=[fwd, bwd], out_shape=[sh, sh],
        scratch_shapes=[pltpu.VMEM((2, pp, LANES, LANES), F32)],
        compiler_params=_cparams("parallel", "parallel", "arbitrary"),
    )(r, lw[0], ks[0], v, na, bb[0], r, lw[1], ks[1], v, na, bb[1])


def _rwkv_out_body(y0_ref, y1_ref, r_ref, v_ref, g_ref, ks0_ref, ks1_ref, rk_ref, lg_ref, lb_ref, o_ref):
    e = _head_ones()
    y = y0_ref[0] + y1_ref[0]
    inv_n = 1.0 / RWKV_HEAD
    yc = y - _head_sum(y, e) * inv_n
    yn = yc * lax.rsqrt(_head_sum(yc * yc, e) * inv_n + RWKV_GN_EPS)
    yn = yn * lg_ref[...] + lb_ref[...]
    ksum = ks0_ref[0].astype(F32) + ks1_ref[0].astype(F32)
    bonus = _head_sum(r_ref[0].astype(F32) * ksum * rk_ref[...], e) * v_ref[0].astype(F32)
    o_ref[0] = ((yn + bonus) * g_ref[0].astype(F32)).astype(o_ref.dtype)


def rwkv_mixer(u, mu, w_r, w_k, w_v, w_o, w0, w1, w2, a0, a1, a2, g1, g2, k_k, k_a, r_k, ln_g, ln_b):
    bsz, t, d = u.shape
    m = bsz * t
    tm_ = _pick(CTX_LEN, (128, 64, 32, 16))
    hr = 16
    rpt, last = tm_ // hr, t // hr - 1
    tile = pl.BlockSpec((1, tm_, d), lambda b, i: (b, i, 0))
    xs = pl.pallas_call(
        functools.partial(_rwkv_mix_body, nctx_tiles=CTX_LEN // tm_), grid=(bsz, t // tm_),
        in_specs=[tile,
                  pl.BlockSpec((1, hr, d), lambda b, i: (b, jnp.maximum(i * rpt - 1, 0), 0)),
                  pl.BlockSpec((1, hr, d), lambda b, i: (b, jnp.minimum((i + 1) * rpt, last), 0)),
                  pl.BlockSpec((6, d), lambda b, i: (0, 0))],
        out_specs=[tile] * 6,
        out_shape=[jax.ShapeDtypeStruct((bsz, t, d), BF16)] * 6,
        compiler_params=_cparams("parallel", "parallel"),
    )(u, u, u, mu)
    xr, xw, xk, xv, xa, xg = (x.reshape(m, d) for x in xs)
    r = matmul(xr, w_r.astype(BF16), BF16).reshape(bsz, t, d)
    k = matmul(xk, w_k.astype(BF16)).reshape(bsz, t, d)
    v = matmul(xv, w_v.astype(BF16), BF16).reshape(bsz, t, d)
    rk = w1.shape[2]
    hw = matmul(xw, jnp.concatenate([w1[0], w1[1]], 1).astype(BF16)).reshape(bsz, t, 2 * rk)
    ha = matmul(xa, jnp.concatenate([a1[0], a1[1]], 1).astype(BF16)).reshape(bsz, t, 2 * rk)
    gr = g1.shape[1]
    grp = -(-gr // LANES) * LANES
    g1p = jnp.zeros((d, grp), F32).at[:, :gr].set(g1)
    g2p = jnp.zeros((grp, d), F32).at[:gr].set(g2)
    hg = jax.nn.sigmoid(matmul(xg, g1p.astype(BF16)))
    g = matmul(hg, g2p.astype(BF16), BF16).reshape(bsz, t, d)

    tr = _pick(t, (64, 32, 16))
    row = pl.BlockSpec((1, tr, d), lambda b, i: (b, i, 0))
    low = pl.BlockSpec((1, tr, 2 * rk), lambda b, i: (b, i, 0))
    vec = pl.BlockSpec((1, d), lambda b, i: (0, 0))
    vec2 = pl.BlockSpec((2, d), lambda b, i: (0, 0))
    fact = pl.BlockSpec((2, rk, d), lambda b, i: (0, 0, 0))
    big = jax.ShapeDtypeStruct((bsz, t, d), F32)
    half = jax.ShapeDtypeStruct((bsz, t, d), BF16)
    na, lw0, lw1, ks0, ks1, bb0, bb1 = pl.pallas_call(
        _rwkv_prep_body, grid=(bsz, t // tr),
        in_specs=[row, low, low, fact, fact, vec2, vec2, vec, vec],
        out_specs=[row] * 7, out_shape=[half, big, big, half, half, half, half],
        compiler_params=_cparams("parallel", "parallel"),
    )(k, hw, ha, w2.astype(BF16), a2.astype(BF16), w0, a0, k_k[None], k_a[None])

    y0, y1 = rwkv_scan(r, v, na, (lw0, lw1), (ks0, ks1), (bb0, bb1))

    out = pl.pallas_call(
        _rwkv_out_body, grid=(bsz, t // tr),
        in_specs=[row] * 7 + [vec, vec, vec], out_specs=row,
        out_shape=jax.ShapeDtypeStruct((bsz, t, d), BF16),
        compiler_params=_cparams("parallel", "parallel"),
    )(y0, y1, r, v, g, ks0, ks1, r_k.reshape(1, d), ln_g[None], ln_b[None])
    return matmul(out.reshape(m, d), w_o.astype(BF16), BF16).reshape(bsz, t, d)


def _merge_exchange(n):
    t = max(1, (n - 1).bit_length())
    p = 1 << (t - 1)
    pairs = []
    while p > 0:
        q, r, d = 1 << (t - 1), 0, p
        while d > 0:
            pairs += [(i, i + d) for i in range(n - d) if (i & p) == r]
            d, q, r = q - p, q >> 1, p
        p >>= 1
    return pairs


def _pop_heads(levels, kk):
    out = []
    for k in range(kk):
        head = levels[0]
        m = jnp.max(head, axis=0, keepdims=True)
        out.append(m)
        need = min(len(levels), kk - k - 1)
        pop = head == m
        levels = [jnp.where(pop, levels[r + 1] if r + 1 < len(levels) else NEG, levels[r]) for r in range(need)]
    return jnp.concatenate(out, axis=0)


def _top_vals(s, kk):
    groups = [s[r:r + SUBLANES] for r in range(0, s.shape[0], SUBLANES)]
    for i, j in _merge_exchange(len(groups)):
        groups[i], groups[j] = jnp.maximum(groups[i], groups[j]), jnp.minimum(groups[i], groups[j])
    return _pop_heads(groups, kk)


def _peer_score_body(q_ref, keys_ref, thr_ref, s2_ref, e2_ref, cf_ref):
    nk = PEER_NKEYS
    kk = PEER_TOPK
    for h in range(PEER_HEADS):
        q1 = q_ref[h * 2 * nk:h * 2 * nk + nk, :].astype(BF16)
        q2 = q_ref[h * 2 * nk + nk:(h + 1) * 2 * nk, :].astype(BF16)
        s1 = _dot(keys_ref[0], q1)
        s2 = _dot(keys_ref[1], q2)
        a1 = _top_vals(s1, kk + 1)
        a2 = _top_vals(s2, kk + 1)
        cv = _pop_heads([a1[:kk] + a2[q:q + 1] for q in range(kk)], kk + 1)
        c17 = jnp.maximum(cv[kk:kk + 1], jnp.maximum(a1[kk:kk + 1] + a2[0:1], a1[0:1] + a2[kk:kk + 1]))
        theta = 0.5 * (cv[kk - 1:kk] + c17)
        zsum = jnp.sum(jnp.exp(cv[:kk] - cv[0:1]), axis=0, keepdims=True)
        thr_ref[h] = theta - s1
        s2_ref[h] = s2
        e2_ref[h] = jnp.exp(s2 - a2[0:1])
        cf_ref[h] = jnp.exp(s1 - a1[0:1]) / zsum


def _gelu(x):
    return 0.5 * x * (1.0 + lax.erf(x * (2.0 ** -0.5)))


def _peer_dense_body(z_ref, u_ref, v_ref, thr_ref, s2_ref, e2_ref, cf_ref, o_ref, w_ref):
    nk = PEER_NKEYS
    eb = pl.program_id(1)
    te = u_ref.shape[1]
    n_i = te // nk
    strip = PEER_ROW_STRIP
    stages = PEER_STAGES
    sub = min(PEER_TOK_SUB, z_ref.shape[1])
    d = z_ref.shape[0]
    kc = d // stages
    tiles = [slice(t0, t0 + sub) for t0 in range(0, z_ref.shape[1], sub)]

    @pl.when(eb == 0)
    def _():
        o_ref[...] = jnp.zeros_like(o_ref)

    def first_matmul(ts):
        acc = None
        for k0 in range(0, d, kc):
            part = _dot(u_ref[0, :, k0:k0 + kc], z_ref[k0:k0 + kc, ts])
            acc = part if acc is None else acc + part
            yield
        return acc

    def gates(ts):
        per_stage = (nk // strip) * PEER_HEADS // stages
        n = 0
        for si in range(nk // strip):
            js = slice(si * strip, (si + 1) * strip)
            w = [None] * n_i
            for h in range(PEER_HEADS):
                s2s = s2_ref[h, js, ts]
                e2s = e2_ref[h, js, ts]
                for ii in range(n_i):
                    i = eb * n_i + ii
                    c = jnp.where(s2s >= thr_ref[h, pl.ds(i, 1), ts], e2s, 0.0) * cf_ref[h, pl.ds(i, 1), ts]
                    w[ii] = c if w[ii] is None else w[ii] + c
                n += 1
                if n % per_stage == 0 and h + 1 < PEER_HEADS:
                    yield
            for ii in range(n_i):
                w_ref[ii * nk + si * strip:ii * nk + (si + 1) * strip, ts] = w[ii].astype(BF16)
            yield

    def activate(ts, act):
        rows = te // stages
        for r0 in range(0, te, rows):
            w_ref[r0:r0 + rows, ts] = w_ref[r0:r0 + rows, ts] * _gelu(act[r0:r0 + rows]).astype(BF16)
            yield

    def second_matmul(ts):
        rows = d // stages
        for r0 in range(0, d, rows):
            o_ref[r0:r0 + rows, ts] += _dot(v_ref[0, r0:r0 + rows, :], w_ref[:, ts])
            yield

    acts = {}
    for ph in range(len(tiles) + 2):
        gens, tags = [], []
        if ph < len(tiles):
            gens += [first_matmul(tiles[ph]), gates(tiles[ph])]
            tags += [ph, None]
        if 0 <= ph - 1 < len(tiles):
            gens.append(activate(tiles[ph - 1], acts[ph - 1]))
            tags.append(None)
        if 0 <= ph - 2 < len(tiles):
            gens.append(second_matmul(tiles[ph - 2]))
            tags.append(None)
        for tag, res in zip(tags, _lockstep(gens)):
            if tag is not None:
                acts[tag] = res


def peer_ffn(zt, w_q, sub_keys, u_all, vt_all, layer):
    d, m = zt.shape
    nh, nk = PEER_HEADS, PEER_NKEYS
    qt = matmul(w_q.T.astype(BF16), zt, F32)
    tt = _pick(m, (256, 128))
    sh = jax.ShapeDtypeStruct((nh, nk, m), F32)
    blk = pl.BlockSpec((nh, nk, tt), lambda i: (0, 0, i))
    thr, s2, e2, cf = pl.pallas_call(
        _peer_score_body, grid=(m // tt,),
        in_specs=[pl.BlockSpec((nh * 2 * nk, tt), lambda i: (0, i)),
                  pl.BlockSpec((2, nk, PEER_DKEY // 2), lambda i: (0, 0, 0))],
        out_specs=[blk, blk, blk, blk], out_shape=[sh, sh, sh, sh],
        compiler_params=_cparams("parallel"),
    )(qt, sub_keys.astype(BF16))

    tm = _pick(m, (2 * PEER_TOK_SUB, PEER_TOK_SUB, LANES))
    te = PEER_EXP_BLOCK
    ne = u_all.shape[1]
    once = pl.Buffered(1)
    sblk = pl.BlockSpec((nh, nk, tm), lambda i, e: (0, 0, i), pipeline_mode=once)
    return pl.pallas_call(
        _peer_dense_body, grid=(m // tm, ne // te),
        in_specs=[pl.BlockSpec((d, tm), lambda i, e: (0, i), pipeline_mode=once),
                  pl.BlockSpec((1, te, d), lambda i, e: (layer, e, 0)),
                  pl.BlockSpec((1, d, te), lambda i, e: (layer, 0, e)),
                  sblk, sblk, sblk, sblk],
        out_specs=pl.BlockSpec((d, tm), lambda i, e: (0, i)),
        out_shape=jax.ShapeDtypeStruct((d, m), F32),
        scratch_shapes=[pltpu.VMEM((te, tm), BF16)],
        compiler_params=_cparams("parallel", "arbitrary"),
    )(zt, u_all, vt_all, thr, s2, e2, cf)


def _ada_mods(c, c_ctx, w_down, w_up, b_up):
    bsz, d = c.shape
    cond = jnp.concatenate([c, c_ctx[None]], 0)
    pad = 16 - cond.shape[0] % 16
    cond = jnp.concatenate([cond, jnp.zeros((pad, d), F32)], 0)
    hid = matmul(jax.nn.silu(cond), w_down.astype(BF16), F32)
    m = (matmul(hid, w_up.astype(BF16), F32) + b_up)[:bsz + 1].reshape(bsz + 1, N_MOD, d)
    return jnp.stack([jnp.broadcast_to(m[bsz], (bsz, N_MOD, d)), m[:bsz]], axis=1)


def kernel(x, c, ctx, c_ctx, ada_w_down, ada_w_up, ada_b,
           gla_w_in, gla_w_g1, gla_w_g2, gla_b_g, gla_norm_g, gla_w_o,
           mla_w_in, mla_q_norm, mla_kv_norm, mla_w_uq, mla_w_ukv, mla_w_o,
           rwkv_mu, rwkv_w_r, rwkv_w_k, rwkv_w_v, rwkv_w_o, rwkv_w0, rwkv_w1, rwkv_w2,
           rwkv_a0, rwkv_a1, rwkv_a2, rwkv_g1, rwkv_g2, rwkv_k_k, rwkv_k_a, rwkv_r_k, rwkv_ln_g, rwkv_ln_b,
           peer_w_q, peer_sub_keys, peer_u, peer_v):
    z = jnp.concatenate([ctx, x], axis=1)
    bsz, t, d = z.shape
    mods = [_ada_mods(c, c_ctx, ada_w_down[i], ada_w_up[i], ada_b[i]) for i in range(DEPTH)]
    peer_ub = peer_u.astype(BF16)
    peer_vtb = jnp.swapaxes(peer_v, 1, 2).astype(BF16)
    _, u = ln_mod(z, mods_n=mods[0], sidx=0)
    for i in range(DEPTH):
        j = i // N_MIXERS
        if i % N_MIXERS == 0:
            y = gla_mixer(u, gla_w_in[j], gla_w_g1[j], gla_w_g2[j], gla_b_g[j], gla_norm_g[j], gla_w_o[j])
        elif i % N_MIXERS == 1:
            y = mla_mixer(u, mla_w_in[j], mla_q_norm[j], mla_kv_norm[j], mla_w_uq[j], mla_w_ukv[j], mla_w_o[j])
        else:
            y = rwkv_mixer(u, rwkv_mu[j], rwkv_w_r[j], rwkv_w_k[j], rwkv_w_v[j], rwkv_w_o[j],
                           rwkv_w0[j], rwkv_w1[j], rwkv_w2[j], rwkv_a0[j], rwkv_a1[j], rwkv_a2[j],
                           rwkv_g1[j], rwkv_g2[j], rwkv_k_k[j], rwkv_k_a[j], rwkv_r_k[j],
                           rwkv_ln_g[j], rwkv_ln_b[j])
        z, ut = ln_mod(z, y, mods[i], 2, mods[i], 3, u_t=True)
        ht = peer_ffn(ut, peer_w_q[i], peer_sub_keys[i], peer_ub, peer_vtb, i)
        if i + 1 < DEPTH:
            z, u = ln_mod(z, ht, mods[i], 5, mods[i + 1], 0, y_t=True)
        else:
            z, _ = ln_mod(z, ht, mods[i], 5, y_t=True, latent_only=True)
    return z
```

```python
import functools

import jax
import jax.numpy as jnp
from jax import lax
from jax.experimental import pallas as pl
from jax.experimental.pallas import tpu as pltpu

F32 = jnp.float32
BF16 = jnp.bfloat16

DEPTH = 4
CTX_LEN = 256
GRID_W = 64
N_MIXERS = 3
N_MOD = 6
LN_EPS = 1e-6
DEEPNORM_ALPHA = (2.0 * DEPTH) ** 0.25

GLA_HEADS = 8
GLA_GATE_RANK = 16
GLA_GATE_NORMALIZER = 16.0

MLA_Q_RANK = 1536
MLA_KV_RANK = 512
MLA_NOPE = 128
MLA_ROPE = 64
MLA_V = 128
ROPE_THETA = 10000.0

RWKV_HEAD = 64
RWKV_GN_EPS = 64e-5
RWKV_DECAY_SCALE = 0.6065306597126334

PEER_HEADS = 8
PEER_NKEYS = 128
PEER_DKEY = 256
PEER_TOPK = 16

LANES = 128
SUBLANES = 8
MXU_COLS = 256
VMEM_LIMIT = 52 * 1024 * 1024

GLA_CHUNK = 128
GLA_HEADS_PER_STEP = 4
GLA_SUB = 4
RWKV_CHUNK = 64
RWKV_SUB = 16
MLA_KEY_CHAINS = 2
LOG2E = 1.4426950408889634
PEER_TOK_SUB = 256
PEER_EXP_BLOCK = 512
PEER_STAGES = 4
PEER_ROW_STRIP = 32
NEG = -1e30


def _pick(n, cands):
    for c in cands:
        if n % c == 0:
            return c
    return n


def _cparams(*sem):
    return pltpu.CompilerParams(dimension_semantics=sem, vmem_limit_bytes=VMEM_LIMIT)


def _zero_after(x, zeros):
    bits = pltpu.bitcast(x[0:SUBLANES], jnp.uint32) & zeros
    return pltpu.bitcast(bits, F32)[0:1].astype(BF16)


def _dot(a, b):
    return jnp.dot(a, b, preferred_element_type=F32)


def _dot_nt(a, b):
    return lax.dot_general(a, b, (((1,), (1,)), ((), ())), preferred_element_type=F32)


def _dot_tn(a, b):
    return lax.dot_general(a, b, (((0,), (0,)), ((), ())), preferred_element_type=F32)


def _split3(x):
    hi = x.astype(BF16)
    r1 = x - hi.astype(F32)
    mid = r1.astype(BF16)
    lo = (r1 - mid.astype(F32)).astype(BF16)
    return hi, mid, lo


def _dot_sel(sel, x):
    s = sel.astype(BF16)
    hi, mid, lo = _split3(x)
    return _dot(s, hi) + _dot(s, mid) + _dot(s, lo)


def _dot_x3(a, b, dims=None):
    ah = a.astype(BF16)
    al = (a - ah.astype(F32)).astype(BF16)
    bh = b.astype(BF16)
    bl = (b - bh.astype(F32)).astype(BF16)
    f = _dot if dims is None else dims
    return f(ah, bh) + f(ah, bl) + f(al, bh)


def _dot_b(a, b):
    return _dot(a.astype(BF16), b.astype(BF16))


def _log_sigmoid(x):
    return jnp.minimum(x, 0.0) - jnp.log1p(jnp.exp(-jnp.abs(x)))


def _sigmoid(x):
    return 1.0 / (1.0 + jnp.exp(-x))


def _mm_body(a_ref, b_ref, o_ref):
    o_ref[...] = _dot(a_ref[...].astype(BF16), b_ref[...].astype(BF16)).astype(o_ref.dtype)


def matmul(a, b, out_dtype=F32):
    m, k = a.shape
    k2, n = b.shape
    assert k == k2
    a_bytes = jnp.dtype(a.dtype).itemsize
    tm_cands = (1024, 512, 256, 128, 64, 32, 16) if a_bytes * k <= 8192 else (512, 256, 128, 64, 32, 16)
    tm = _pick(m, tm_cands)
    tn = _pick(n, (512, 256, 128))
    return pl.pallas_call(
        _mm_body,
        grid=(m // tm, n // tn),
        in_specs=[pl.BlockSpec((tm, k), lambda i, j: (i, 0)),
                  pl.BlockSpec((k, tn), lambda i, j: (0, j))],
        out_specs=pl.BlockSpec((tm, tn), lambda i, j: (i, j)),
        out_shape=jax.ShapeDtypeStruct((m, n), out_dtype),
        compiler_params=_cparams("parallel", "parallel"),
    )(a, b)


def _layer_norm(z):
    zc = z - jnp.mean(z, -1, keepdims=True)
    return zc * lax.rsqrt(jnp.mean(zc * zc, -1, keepdims=True) + LN_EPS)


def _ln_mod_body(*refs, gidx, sidx, has_y, y_t, u_t):
    it = iter(refs)
    z_ref = next(it)
    y_ref = next(it) if has_y else None
    mg_ref = next(it) if has_y else None
    mn_ref = next(it) if sidx is not None else None
    zo_ref = next(it) if has_y else None
    u_ref = next(it) if sidx is not None else None
    z = z_ref[0]
    if has_y:
        y = y_ref[...].T if y_t else y_ref[0]
        z = _layer_norm(DEEPNORM_ALPHA * z + mg_ref[0, 0, gidx:gidx + 1, :] * y)
        zo_ref[0] = z
    if sidx is not None:
        u = z * (1.0 + mn_ref[0, 0, sidx + 1:sidx + 2, :]) + mn_ref[0, 0, sidx:sidx + 1, :]
        if u_t:
            u_ref[...] = u.T.astype(u_ref.dtype)
        else:
            u_ref[0] = u.astype(u_ref.dtype)


def ln_mod(z, y=None, mods_g=None, gidx=None, mods_n=None, sidx=None, y_t=False, u_t=False, latent_only=False):
    assert not (latent_only and sidx is not None)
    bsz, t, d = z.shape
    tr = _pick(CTX_LEN, (256, 128, 64, 32, 16))
    nt = t // tr
    nctx = CTX_LEN // tr
    first = nctx if latent_only else 0
    has_y = y is not None
    row = pl.BlockSpec((1, tr, d), lambda b, i: (b, first + i, 0))
    col = pl.BlockSpec((d, tr), lambda b, i: (0, b * nt + first + i))
    mod = pl.BlockSpec((1, 1, N_MOD, d), lambda b, i: (b, jnp.where(first + i < nctx, 0, 1), 0, 0))
    ins, in_specs, outs, out_specs = [z], [row], [], []
    if has_y:
        ins += [y, mods_g]
        in_specs += [col if y_t else row, mod]
        outs.append(jax.ShapeDtypeStruct((bsz, t - first * tr, d), F32))
        out_specs.append(pl.BlockSpec((1, tr, d), lambda b, i: (b, i, 0)) if latent_only else row)
    if sidx is not None:
        ins.append(mods_n)
        in_specs.append(mod)
        outs.append(jax.ShapeDtypeStruct((d, bsz * t) if u_t else (bsz, t, d), BF16))
        out_specs.append(col if u_t else row)
    res = pl.pallas_call(
        functools.partial(_ln_mod_body, gidx=gidx, sidx=sidx, has_y=has_y, y_t=y_t, u_t=u_t),
        grid=(bsz, nt - first), in_specs=in_specs, out_specs=out_specs, out_shape=outs,
        compiler_params=_cparams("parallel", "parallel"),
    )(*ins)
    res = list(res)
    zo = res.pop(0) if has_y else None
    u = res.pop(0) if sidx is not None else None
    return zo, u


def _lockstep(gens):
    out = [None] * len(gens)
    live = list(range(len(gens)))
    while live:
        for i in list(live):
            try:
                next(gens[i])
            except StopIteration as done:
                out[i] = done.value
                live.remove(i)
    return out


def _gla_chunk(q, k, v, glow, wg2, bg, st, rev):
    c, dk = q.shape
    sub = GLA_SUB
    q = q.astype(F32) * (dk ** -0.5)
    k = k.astype(F32)
    gl = _dot_x3(glow, wg2) + bg
    yield
    g = _log_sigmoid(gl) * (1.0 / GLA_GATE_NORMALIZER)
    row = lax.broadcasted_iota(jnp.int32, (c, c), 0)
    col = lax.broadcasted_iota(jnp.int32, (c, c), 1)
    tri = (row <= col) if rev else (row >= col)
    b = _dot_sel(tri.astype(F32), g)
    yield
    btot = jnp.sum(g, axis=0, keepdims=True)

    o = _dot_nt((q * jnp.exp(b)).astype(BF16), st.astype(BF16))

    rowk = lax.broadcasted_iota(jnp.int32, (c, 1), 0)
    halves = []
    hsz = c // 2
    while hsz >= sub:
        halves.append(hsz)
        hsz //= 2
    pick = jnp.concatenate(
        [(col == (row // (2 * hf)) * (2 * hf) + (hf if rev else hf - 1)).astype(F32) for hf in halves], axis=0)
    refs = _dot_sel(pick, b)
    yield
    att = jnp.zeros((c, c), F32)
    for lv, hf in enumerate(halves):
        refb = refs[lv * c:(lv + 1) * c]
        late = (rowk % (2 * hf)) >= hf
        q_side, k_side = (~late, late) if rev else (late, ~late)
        qf = jnp.where(q_side, q * jnp.exp(jnp.minimum(b - refb, 0.0)), 0.0)
        kf = jnp.where(k_side, k * jnp.exp(jnp.minimum(refb - b, 0.0)), 0.0)
        same = (row // (2 * hf)) == (col // (2 * hf))
        att = att + jnp.where(same, _dot_nt(qf.astype(BF16), kf.astype(BF16)), 0.0)
    yield

    rmod = rowk % sub
    for lag in range(sub):
        sh = (c - lag) % c if rev else lag
        ks = pltpu.roll(k, sh, 0) if sh else k
        bs = pltpu.roll(b, sh, 0) if sh else b
        term = jnp.sum(q * ks * jnp.exp(jnp.minimum(b - bs, 0.0)), axis=1, keepdims=True)
        valid = (rmod + lag < sub) if rev else (rmod >= lag)
        hit = (col == row + lag) if rev else (col == row - lag)
        att = att + jnp.where(hit & valid, term, 0.0)

    o = o + _dot(att.astype(BF16), v)
    yield
    kd = (k * jnp.exp(btot - b)).astype(BF16)
    return o, st * jnp.exp(btot) + _dot_tn(v, kd)


def _gla_body(*refs):
    ins, (of_ref, ob_ref, st_ref) = refs[:12], refs[12:]

    @pl.when(pl.program_id(2) == 0)
    def _():
        st_ref[...] = jnp.zeros_like(st_ref)

    dv = st_ref.shape[2]
    dk = st_ref.shape[3]
    gens, dest = [], []
    for dr in range(2):
        q_ref, k_ref, v_ref, gl_ref, wg2_ref, bg_ref = ins[6 * dr:6 * dr + 6]
        for hh in range(st_ref.shape[1]):
            ks, vs = slice(hh * dk, (hh + 1) * dk), slice(hh * dv, (hh + 1) * dv)
            gens.append(_gla_chunk(q_ref[0, :, ks], k_ref[0, :, ks], v_ref[0, :, vs], gl_ref[0],
                                   wg2_ref[:, ks], bg_ref[:, ks], st_ref[dr, hh], rev=(dr == 1)))
            dest.append((dr, hh, vs))
    for (dr, hh, vs), (o, st) in zip(dest, _lockstep(gens)):
        (of_ref, ob_ref)[dr][0, :, vs] = o.astype(of_ref.dtype)
        st_ref[dr, hh] = st


def gla_scan(p, glow, wg2, bg):
    bsz, t, d3 = p.shape
    d = d3 // 3
    h = GLA_HEADS
    dk, dv = (d // 2) // h, d // h
    c = GLA_CHUNK
    nc, ncx = t // c, CTX_LEN // c

    def rchunk(s):
        return jnp.where(s < ncx, ncx - 1 - s, nc - 1 - (s - ncx))

    hp = _pick(h, (GLA_HEADS_PER_STEP, 1))
    hg = h // hp

    def specs(chunk):
        return [pl.BlockSpec((1, c, hp * dk), lambda b, hh, s: (b, chunk(s), hh)),
                pl.BlockSpec((1, c, hp * dk), lambda b, hh, s: (b, chunk(s), hg + hh)),
                pl.BlockSpec((1, c, hp * dv), lambda b, hh, s: (b, chunk(s), hg + hh)),
                pl.BlockSpec((1, c, LANES), lambda b, hh, s: (b, chunk(s), 0)),
                pl.BlockSpec((LANES, hp * dk), lambda b, hh, s: (0, hh)),
                pl.BlockSpec((1, hp * dk), lambda b, hh, s: (0, hh))]

    sh = jax.ShapeDtypeStruct((bsz, t, d), BF16)
    return pl.pallas_call(
        _gla_body,
        grid=(bsz, hg, nc),
        in_specs=specs(lambda s: s) + specs(rchunk),
        out_specs=[pl.BlockSpec((1, c, hp * dv), lambda b, hh, s: (b, s, hh)),
                   pl.BlockSpec((1, c, hp * dv), lambda b, hh, s: (b, rchunk(s), hh))],
        out_shape=[sh, sh],
        scratch_shapes=[pltpu.VMEM((2, hp, dv, dk), F32)],
        compiler_params=_cparams("parallel", "parallel", "arbitrary"),
    )(p, p, p, glow, wg2[0], bg[0], p, p, p, glow, wg2[1], bg[1])


def _gla_gate_body(of_ref, ob_ref, r_ref, g_ref, o_ref):
    dv = g_ref.shape[1]
    o = of_ref[0].astype(F32) + ob_ref[0].astype(F32)
    r = r_ref[0].astype(F32)
    outs = []
    for hh in range(o.shape[1] // dv):
        seg = o[:, hh * dv:(hh + 1) * dv]
        outs.append(seg * lax.rsqrt(jnp.mean(seg * seg, -1, keepdims=True) + 1e-6) * g_ref[...])
    o_ref[0] = (jnp.concatenate(outs, axis=1) * (r * _sigmoid(r))).astype(o_ref.dtype)


def gla_gate(o_f, o_b, p, norm_g):
    bsz, t, d = o_f.shape
    tr = _pick(t, (256, 128, 64, 32, 16))
    row = pl.BlockSpec((1, tr, d), lambda b, i: (b, i, 0))
    return pl.pallas_call(
        _gla_gate_body, grid=(bsz, t // tr),
        in_specs=[row, row, pl.BlockSpec((1, tr, d), lambda b, i: (b, i, 2)),
                  pl.BlockSpec((1, norm_g.shape[1]), lambda b, i: (0, 0))],
        out_specs=row, out_shape=jax.ShapeDtypeStruct((bsz, t, d), BF16),
        compiler_params=_cparams("parallel", "parallel"),
    )(o_f, o_b, p, norm_g)


def gla_mixer(u, w_in, w_g1, w_g2, b_g, norm_g, w_o):
    bsz, t, d = u.shape
    qk = d // 2
    u2 = u.reshape(bsz * t, d)
    p = matmul(u2, w_in.astype(BF16), BF16).reshape(bsz, t, 3 * d)
    r = GLA_GATE_RANK
    wg1 = jnp.zeros((d, LANES), F32).at[:, :r].set(w_g1[0]).at[:, r:2 * r].set(w_g1[1])
    glow = matmul(u2, wg1.astype(BF16), F32).reshape(bsz, t, LANES)
    wg2 = [jnp.zeros((LANES, qk), F32).at[s * r:(s + 1) * r].set(w_g2[s]) for s in range(2)]
    outs = gla_scan(p, glow, wg2, [b_g[0][None], b_g[1][None]])
    gated = gla_gate(outs[0], outs[1], p, norm_g[None])
    return matmul(gated.reshape(bsz * t, d), w_o.astype(BF16), BF16).reshape(bsz, t, d)


def _rms(x, gain):
    return x * lax.rsqrt(jnp.mean(x * x, -1, keepdims=True) + 1e-6) * gain


def _rope128(a, c, s):
    return a * c + pltpu.roll(a, LANES // 2, 1) * s


def _mla_norm_body(h_ref, qg_ref, kg_ref, c_ref, s_ref, cq_ref, ckv_ref, kr_ref):
    qr, kvr = cq_ref.shape[2], ckv_ref.shape[2]
    h = h_ref[0]
    cq_ref[0] = _rms(h[:, :qr], qg_ref[...]).astype(BF16)
    ckv_ref[0] = _rms(h[:, qr:qr + kvr], kg_ref[...]).astype(BF16)
    kr_ref[0] = _rope128(h[:, qr + kvr:qr + kvr + LANES], c_ref[...], s_ref[...]).astype(BF16)


def _mla_attn_body(q_ref, kn_ref, v_ref, kr_ref, c_ref, s_ref, o_ref, kf_ref, vf_ref, *, nctx_tiles, scale):
    qt = pl.program_id(2)
    t = kf_ref.shape[0]
    ctx = nctx_tiles * q_ref.shape[1]

    @pl.when(qt == 0)
    def _():
        kf_ref[:, :LANES] = kn_ref[0]
        kf_ref[:, LANES:] = kr_ref[0]
        vf_ref[:, :LANES] = v_ref[0]
        vf_ref[:, LANES:] = jnp.ones((t, LANES), BF16)

    q = q_ref[0].astype(F32)
    qr = _rope128(q[:, LANES:], c_ref[...], s_ref[...])
    qf = (jnp.concatenate([q[:, :LANES], qr], axis=1) * (scale * LOG2E)).astype(BF16)

    def attend(keys):
        s = _dot_nt(qf, kf_ref[keys])
        yield
        m = jnp.max(s, -1, keepdims=True)
        p = jnp.exp2(s - m).astype(BF16)
        return m, _dot(p, vf_ref[keys])

    def finish(parts):
        m = functools.reduce(jnp.maximum, [pm for pm, _ in parts])
        acc = sum(jnp.exp2(pm - m) * po for pm, po in parts)
        o_ref[0] = (acc[:, :LANES] / acc[:, LANES:LANES + 1]).astype(o_ref.dtype)

    @pl.when(qt < nctx_tiles)
    def _():
        finish(_lockstep([attend(slice(0, ctx))]))

    @pl.when(qt >= nctx_tiles)
    def _():
        step = t // MLA_KEY_CHAINS
        finish(_lockstep([attend(slice(k0, k0 + step)) for k0 in range(0, t, step)]))


def _rope_tables(t):
    rows = (t - CTX_LEN) // GRID_W
    row = jnp.repeat(jnp.arange(rows, dtype=F32), GRID_W)
    colp = jnp.tile(jnp.arange(GRID_W, dtype=F32), rows)
    n_freq = MLA_ROPE // 4
    inv = ROPE_THETA ** (-jnp.arange(n_freq, dtype=F32) / n_freq)
    ang = jnp.concatenate([row[:, None] * inv, colp[:, None] * inv], -1)
    cos = jnp.concatenate([jnp.ones((CTX_LEN, MLA_ROPE // 2), F32), jnp.cos(ang)], 0)
    sin = jnp.concatenate([jnp.zeros((CTX_LEN, MLA_ROPE // 2), F32), jnp.sin(ang)], 0)
    z = jnp.zeros_like(cos)
    return jnp.concatenate([cos, cos, z, z], 1), jnp.concatenate([-sin, sin, z, z], 1)


def mla_mixer(u, w_in, q_norm, kv_norm, w_uq, w_ukv, w_o):
    bsz, t, d = u.shape
    nh = d // 128
    qr, kvr, rp = MLA_Q_RANK, MLA_KV_RANK, MLA_ROPE
    ev, od = jnp.arange(0, rp, 2), jnp.arange(1, rp, 2)
    perm = jnp.concatenate([ev, od, od, ev])
    hw = -(-(qr + kvr + LANES) // MXU_COLS) * MXU_COLS
    w_in_p = jnp.concatenate([w_in[:, :qr + kvr], w_in[:, qr + kvr + perm],
                              jnp.zeros((d, hw - (qr + kvr + LANES)), F32)], axis=1)
    h = matmul(u.reshape(bsz * t, d), w_in_p.astype(BF16), F32).reshape(bsz, t, hw)
    ctab, stab = _rope_tables(t)

    tr = _pick(t, (256, 128, 64, 32, 16))
    cq, ckv, kr = pl.pallas_call(
        _mla_norm_body, grid=(bsz, t // tr),
        in_specs=[pl.BlockSpec((1, tr, hw), lambda b, i: (b, i, 0)),
                  pl.BlockSpec((1, qr), lambda b, i: (0, 0)),
                  pl.BlockSpec((1, kvr), lambda b, i: (0, 0)),
                  pl.BlockSpec((tr, LANES), lambda b, i: (i, 0)),
                  pl.BlockSpec((tr, LANES), lambda b, i: (i, 0))],
        out_specs=[pl.BlockSpec((1, tr, qr), lambda b, i: (b, i, 0)),
                   pl.BlockSpec((1, tr, kvr), lambda b, i: (b, i, 0)),
                   pl.BlockSpec((1, tr, LANES), lambda b, i: (b, i, 0))],
        out_shape=[jax.ShapeDtypeStruct((bsz, t, qr), BF16),
                   jax.ShapeDtypeStruct((bsz, t, kvr), BF16),
                   jax.ShapeDtypeStruct((bsz, t, LANES), BF16)],
        compiler_params=_cparams("parallel", "parallel"),
    )(h, q_norm[None], kv_norm[None], ctab, stab)

    hq = MLA_NOPE + rp
    qcols = (jnp.arange(nh)[:, None] * hq
             + jnp.concatenate([jnp.arange(MLA_NOPE), MLA_NOPE + perm])[None, :]).reshape(-1)
    q = matmul(cq.reshape(bsz * t, qr), w_uq[:, qcols].astype(BF16), BF16).reshape(bsz, t, nh * 2 * LANES)
    kv = matmul(ckv.reshape(bsz * t, kvr), w_ukv.astype(BF16), BF16).reshape(bsz, t, nh * 2 * LANES)

    tq = _pick(CTX_LEN, (256, 128, 64, 32, 16))
    scale = (MLA_NOPE + rp) ** -0.5
    o = pl.pallas_call(
        functools.partial(_mla_attn_body, nctx_tiles=CTX_LEN // tq, scale=scale),
        grid=(bsz, nh, t // tq),
        in_specs=[pl.BlockSpec((1, tq, 2 * LANES), lambda b, hh, i: (b, i, hh)),
                  pl.BlockSpec((1, t, LANES), lambda b, hh, i: (b, 0, 2 * hh)),
                  pl.BlockSpec((1, t, LANES), lambda b, hh, i: (b, 0, 2 * hh + 1)),
                  pl.BlockSpec((1, t, LANES), lambda b, hh, i: (b, 0, 0)),
                  pl.BlockSpec((tq, LANES), lambda b, hh, i: (i, 0)),
                  pl.BlockSpec((tq, LANES), lambda b, hh, i: (i, 0))],
        out_specs=pl.BlockSpec((1, tq, LANES), lambda b, hh, i: (b, i, hh)),
        out_shape=jax.ShapeDtypeStruct((bsz, t, nh * MLA_V), BF16),
        scratch_shapes=[pltpu.VMEM((t, 2 * LANES), BF16), pltpu.VMEM((t, 2 * LANES), BF16)],
        compiler_params=_cparams("parallel", "parallel", "arbitrary"),
    )(q, kv, kv, kr, ctab, stab)
    return matmul(o.reshape(bsz * t, nh * MLA_V), w_o.astype(BF16), BF16).reshape(bsz, t, d)


def _rwkv_mix_body(u_ref, before_ref, after_ref, mu_ref, *o_refs, nctx_tiles):
    i = pl.program_id(1)
    tr = u_ref.shape[1]
    hr = before_ref.shape[1]
    u = u_ref[0].astype(F32)
    row = lax.broadcasted_iota(jnp.int32, (tr, 1), 0)
    starts = (i == 0) | (i == nctx_tiles)
    ends = (i == nctx_tiles - 1) | (i == pl.num_programs(1) - 1)
    edge_prev = jnp.where(starts, 0.0, before_ref[0, hr - 1:hr, :].astype(F32))
    edge_next = jnp.where(ends, 0.0, after_ref[0, 0:1, :].astype(F32))
    prev = jnp.where(row == 0, edge_prev, pltpu.roll(u, 1, 0))
    nxt = jnp.where(row == tr - 1, edge_next, pltpu.roll(u, tr - 1, 0))
    xx = 0.5 * (prev + nxt) - u
    for n, o_ref in enumerate(o_refs):
        o_ref[0] = (u + xx * mu_ref[n:n + 1, :]).astype(o_ref.dtype)


def _head_sum(x, e):
    eb = e.astype(BF16)
    w = e.shape[0]
    outs = []
    for s in range(x.shape[1] // w):
        xs = x[:, s * w:(s + 1) * w]
        hi = xs.astype(BF16)
        lo = (xs - hi.astype(F32)).astype(BF16)
        outs.append(_dot(hi, eb) + _dot(lo, eb))
    return jnp.concatenate(outs, axis=1)


def _head_ones():
    r = lax.broadcasted_iota(jnp.int32, (MXU_COLS, MXU_COLS), 0) // RWKV_HEAD
    c = lax.broadcasted_iota(jnp.int32, (MXU_COLS, MXU_COLS), 1) // RWKV_HEAD
    return (r == c).astype(F32)


def _rwkv_prep_body(k_ref, hw_ref, ha_ref, w2_ref, a2_ref, w0_ref, a0_ref, kk_ref, ka_ref,
                    na_ref, lw0_ref, lw1_ref, ks0_ref, ks1_ref, bb0_ref, bb1_ref):
    rk = w2_ref.shape[1]
    k = k_ref[0]
    kk = k * kk_ref[...]
    kk = kk * lax.rsqrt(_head_sum(kk * kk, _head_ones()) + 1e-12)
    na_ref[0] = (-kk).astype(na_ref.dtype)
    hw = jnp.tanh(hw_ref[0])
    ha = ha_ref[0]
    for s, (lw_ref, ks_ref, bb_ref) in enumerate(((lw0_ref, ks0_ref, bb0_ref), (lw1_ref, ks1_ref, bb1_ref))):
        wl = w0_ref[s:s + 1, :] + _dot(hw[:, s * rk:(s + 1) * rk].astype(BF16), w2_ref[s])
        lw_ref[0] = (-RWKV_DECAY_SCALE) * _sigmoid(wl)
        a = _sigmoid(a0_ref[s:s + 1, :] + _dot(ha[:, s * rk:(s + 1) * rk].astype(BF16), a2_ref[s]))
        ks_ref[0] = (k * (1.0 + (a - 1.0) * ka_ref[...])).astype(ks_ref.dtype)
        bb_ref[0] = (kk * a).astype(bb_ref.dtype)


def _rwkv_chunk(r, lw, ks, v, na, bb, ht, rev):
    c = r.shape[0]
    n = RWKV_HEAD
    c2 = 2 * c
    r, ks, v, na, bb = (x.astype(F32) for x in (r, ks, v, na, bb))
    ri = lax.broadcasted_iota(jnp.int32, (c, c), 0)
    ci = lax.broadcasted_iota(jnp.int32, (c, c), 1)
    cs = _dot_sel(((ri <= ci) if rev else (ri >= ci)).astype(F32), lw)
    yield
    cm = cs - lw
    ctot = jnp.sum(lw, axis=0, keepdims=True)

    lane_a = lax.broadcasted_iota(jnp.int32, (1, LANES), 1) < n

    def stack(x):
        return jnp.concatenate([jnp.where(lane_a, x, 0.0), jnp.where(lane_a, 0.0, x)], axis=0)

    at = stack(na * jnp.exp(cm))
    rt = stack(r * jnp.exp(cs))
    ecs = jnp.exp(-cs)
    bh = stack(bb * ecs)
    kh = stack(ks * ecs)
    ece = jnp.exp(ctot - cs)
    be = stack(bb * ece)
    ke = stack(ks * ece)
    vs = stack(v)

    rr = lax.broadcasted_iota(jnp.int32, (c2, c2), 0)
    cc = lax.broadcasted_iota(jnp.int32, (c2, c2), 1)
    same = (rr // c) == (cc // c)
    tr_, tc_ = rr % c, cc % c
    strict = same & ((tc_ > tr_) if rev else (tc_ < tr_))
    incl = same & ((tc_ >= tr_) if rev else (tc_ <= tr_))
    diag_blk = same & ((tr_ // RWKV_SUB) == (tc_ // RWKV_SUB))

    lhs = jnp.concatenate([at, rt], axis=0)
    rhs = jnp.concatenate([bh, kh], axis=0)
    a4 = _dot_nt(lhs.astype(BF16), rhs.astype(BF16))
    yield
    nmat = jnp.where(strict, a4[:c2, :c2], 0.0)
    aak = jnp.where(strict, a4[:c2, c2:], 0.0)
    arb = jnp.where(incl, a4[c2:, :c2], 0.0)
    ark = jnp.where(incl, a4[c2:, c2:], 0.0)

    hb = ht.T.astype(BF16)
    vb = vs.astype(BF16)
    x0 = _dot(jnp.concatenate([at, aak], axis=1).astype(BF16), jnp.concatenate([hb, vb], axis=0))
    yield

    nd = jnp.where(diag_blk, nmat, 0.0)
    no = nmat - nd
    tm = nd
    pw = nd
    for _ in range(3):
        pw = _dot_b(pw, pw)
        yield
        tm = tm + pw + _dot_b(pw, tm)
        yield
    zu = _dot_b(tm, jnp.concatenate([no, x0], axis=1))
    z = no + zu[:, :c2]
    u = x0 + zu[:, c2:]
    yield
    zz = _dot_b(z, jnp.concatenate([z, u], axis=1))
    u = u + zz[:, c2:]
    yield
    u = u + _dot_b(zz[:, :c2], u)
    yield

    ub = u.astype(BF16)
    ys = _dot(jnp.concatenate([rt, arb, ark], axis=1).astype(BF16), jnp.concatenate([hb, ub, vb], axis=0))
    yield
    hn = ht * jnp.exp(ctot) + _dot_tn(jnp.concatenate([ub, vb], axis=0),
                                      jnp.concatenate([be, ke], axis=0).astype(BF16))
    hr = lax.broadcasted_iota(jnp.int32, (LANES, LANES), 0) // n
    hc = lax.broadcasted_iota(jnp.int32, (LANES, LANES), 1) // n
    return ys[:c] + ys[c:], jnp.where(hr == hc, hn, 0.0)


def _rwkv_scan_body(*refs):
    ins, (y0_ref, y1_ref, st_ref) = refs[:12], refs[12:]

    @pl.when(pl.program_id(2) == 0)
    def _():
        st_ref[...] = jnp.zeros_like(st_ref)

    gens, dest = [], []
    for dr in range(2):
        r_ref, lw_ref, ks_ref, v_ref, na_ref, bb_ref = ins[6 * dr:6 * dr + 6]
        for p in range(r_ref.shape[2] // LANES):
            sl = slice(p * LANES, (p + 1) * LANES)
            gens.append(_rwkv_chunk(r_ref[0, :, sl], lw_ref[0, :, sl], ks_ref[0, :, sl], v_ref[0, :, sl],
                                    na_ref[0, :, sl], bb_ref[0, :, sl], st_ref[dr, p], rev=(dr == 1)))
            dest.append(((y0_ref, y1_ref)[dr], sl, dr, p))
    for (y_ref, sl, dr, p), (y, hn) in zip(dest, _lockstep(gens)):
        y_ref[0, :, sl] = y
        st_ref[dr, p] = hn


def rwkv_scan(r, v, na, lw, ks, bb):
    bsz, t, d = r.shape
    c = RWKV_CHUNK
    nc, ncx = t // c, CTX_LEN // c
    pp = _pick(d // LANES, (8, 4, 2, 1))
    w = pp * LANES

    def rchunk(s):
        return jnp.where(s < ncx, ncx - 1 - s, nc - 1 - (s - ncx))

    fwd = pl.BlockSpec((1, c, w), lambda b, p, s: (b, s, p))
    bwd = pl.BlockSpec((1, c, w), lambda b, p, s: (b, rchunk(s), p))
    sh = jax.ShapeDtypeStruct((bsz, t, d), F32)
    return pl.pallas_call(
        _rwkv_scan_body,
        grid=(bsz, d // w, nc),
        in_specs=[fwd] * 6 + [bwd] * 6, out_specs=[fwd, bwd], out_shape=[sh, sh],
        scratch_shapes=[pltpu.VMEM((2, pp, LANES, LANES), F32)],
        compiler_params=_cparams("parallel", "parallel", "arbitrary"),
    )(r, lw[0], ks[0], v, na, bb[0], r, lw[1], ks[1], v, na, bb[1])


def _rwkv_out_body(y0_ref, y1_ref, r_ref, v_ref, g_ref, ks0_ref, ks1_ref, rk_ref, lg_ref, lb_ref, o_ref):
    e = _head_ones()
    y = y0_ref[0] + y1_ref[0]
    inv_n = 1.0 / RWKV_HEAD
    yc = y - _head_sum(y, e) * inv_n
    yn = yc * lax.rsqrt(_head_sum(yc * yc, e) * inv_n + RWKV_GN_EPS)
    yn = yn * lg_ref[...] + lb_ref[...]
    ksum = ks0_ref[0].astype(F32) + ks1_ref[0].astype(F32)
    bonus = _head_sum(r_ref[0].astype(F32) * ksum * rk_ref[...], e) * v_ref[0].astype(F32)
    o_ref[0] = ((yn + bonus) * g_ref[0].astype(F32)).astype(o_ref.dtype)


def rwkv_mixer(u, mu, w_r, w_k, w_v, w_o, w0, w1, w2, a0, a1, a2, g1, g2, k_k, k_a, r_k, ln_g, ln_b):
    bsz, t, d = u.shape
    m = bsz * t
    tm_ = _pick(CTX_LEN, (128, 64, 32, 16))
    hr = 16
    rpt, last = tm_ // hr, t // hr - 1
    tile = pl.BlockSpec((1, tm_, d), lambda b, i: (b, i, 0))
    xs = pl.pallas_call(
        functools.partial(_rwkv_mix_body, nctx_tiles=CTX_LEN // tm_), grid=(bsz, t // tm_),
        in_specs=[tile,
                  pl.BlockSpec((1, hr, d), lambda b, i: (b, jnp.maximum(i * rpt - 1, 0), 0)),
                  pl.BlockSpec((1, hr, d), lambda b, i: (b, jnp.minimum((i + 1) * rpt, last), 0)),
                  pl.BlockSpec((6, d), lambda b, i: (0, 0))],
        out_specs=[tile] * 6,
        out_shape=[jax.ShapeDtypeStruct((bsz, t, d), BF16)] * 6,
        compiler_params=_cparams("parallel", "parallel"),
    )(u, u, u, mu)
    xr, xw, xk, xv, xa, xg = (x.reshape(m, d) for x in xs)
    r = matmul(xr, w_r.astype(BF16), BF16).reshape(bsz, t, d)
    k = matmul(xk, w_k.astype(BF16)).reshape(bsz, t, d)
    v = matmul(xv, w_v.astype(BF16), BF16).reshape(bsz, t, d)
    rk = w1.shape[2]
    hw = matmul(xw, jnp.concatenate([w1[0], w1[1]], 1).astype(BF16)).reshape(bsz, t, 2 * rk)
    ha = matmul(xa, jnp.concatenate([a1[0], a1[1]], 1).astype(BF16)).reshape(bsz, t, 2 * rk)
    gr = g1.shape[1]
    grp = -(-gr // LANES) * LANES
    g1p = jnp.zeros((d, grp), F32).at[:, :gr].set(g1)
    g2p = jnp.zeros((grp, d), F32).at[:gr].set(g2)
    hg = jax.nn.sigmoid(matmul(xg, g1p.astype(BF16)))
    g = matmul(hg, g2p.astype(BF16), BF16).reshape(bsz, t, d)

    tr = _pick(t, (64, 32, 16))
    row = pl.BlockSpec((1, tr, d), lambda b, i: (b, i, 0))
    low = pl.BlockSpec((1, tr, 2 * rk), lambda b, i: (b, i, 0))
    vec = pl.BlockSpec((1, d), lambda b, i: (0, 0))
    vec2 = pl.BlockSpec((2, d), lambda b, i: (0, 0))
    fact = pl.BlockSpec((2, rk, d), lambda b, i: (0, 0, 0))
    big = jax.ShapeDtypeStruct((bsz, t, d), F32)
    half = jax.ShapeDtypeStruct((bsz, t, d), BF16)
    na, lw0, lw1, ks0, ks1, bb0, bb1 = pl.pallas_call(
        _rwkv_prep_body, grid=(bsz, t // tr),
        in_specs=[row, low, low, fact, fact, vec2, vec2, vec, vec],
        out_specs=[row] * 7, out_shape=[half, big, big, half, half, half, half],
        compiler_params=_cparams("parallel", "parallel"),
    )(k, hw, ha, w2.astype(BF16), a2.astype(BF16), w0, a0, k_k[None], k_a[None])

    y0, y1 = rwkv_scan(r, v, na, (lw0, lw1), (ks0, ks1), (bb0, bb1))

    out = pl.pallas_call(
        _rwkv_out_body, grid=(bsz, t // tr),
        in_specs=[row] * 7 + [vec, vec, vec], out_specs=row,
        out_shape=jax.ShapeDtypeStruct((bsz, t, d), BF16),
        compiler_params=_cparams("parallel", "parallel"),
    )(y0, y1, r, v, g, ks0, ks1, r_k.reshape(1, d), ln_g[None], ln_b[None])
    return matmul(out.reshape(m, d), w_o.astype(BF16), BF16).reshape(bsz, t, d)


def _merge_exchange(n):
    t = max(1, (n - 1).bit_length())
    p = 1 << (t - 1)
    pairs = []
    while p > 0:
        q, r, d = 1 << (t - 1), 0, p
        while d > 0:
            pairs += [(i, i + d) for i in range(n - d) if (i & p) == r]
            d, q, r = q - p, q >> 1, p
        p >>= 1
    return pairs


def _pop_heads(levels, kk):
    out = []
    for k in range(kk):
        head = levels[0]
        m = jnp.max(head, axis=0, keepdims=True)
        out.append(m)
        need = min(len(levels), kk - k - 1)
        pop = head == m
        levels = [jnp.where(pop, levels[r + 1] if r + 1 < len(levels) else NEG, levels[r]) for r in range(need)]
    return jnp.concatenate(out, axis=0)


def _top_vals(s, kk):
    groups = [s[r:r + SUBLANES] for r in range(0, s.shape[0], SUBLANES)]
    for i, j in _merge_exchange(len(groups)):
        groups[i], groups[j] = jnp.maximum(groups[i], groups[j]), jnp.minimum(groups[i], groups[j])
    return _pop_heads(groups, kk)


def _peer_score_body(q_ref, keys_ref, thr_ref, s2_ref, e2_ref, cf_ref):
    nk = PEER_NKEYS
    kk = PEER_TOPK
    for h in range(PEER_HEADS):
        q1 = q_ref[h * 2 * nk:h * 2 * nk + nk, :].astype(BF16)
        q2 = q_ref[h * 2 * nk + nk:(h + 1) * 2 * nk, :].astype(BF16)
        s1 = _dot(keys_ref[0], q1)
        s2 = _dot(keys_ref[1], q2)
        a1 = _top_vals(s1, kk + 1)
        a2 = _top_vals(s2, kk + 1)
        cv = _pop_heads([a1[:kk] + a2[q:q + 1] for q in range(kk)], kk + 1)
        c17 = jnp.maximum(cv[kk:kk + 1], jnp.maximum(a1[kk:kk + 1] + a2[0:1], a1[0:1] + a2[kk:kk + 1]))
        theta = 0.5 * (cv[kk - 1:kk] + c17)
        zsum = jnp.sum(jnp.exp(cv[:kk] - cv[0:1]), axis=0, keepdims=True)
        thr_ref[h] = theta - s1
        s2_ref[h] = s2
        e2_ref[h] = jnp.exp(s2 - a2[0:1])
        cf_ref[h] = jnp.exp(s1 - a1[0:1]) / zsum


def _gelu(x):
    return 0.5 * x * (1.0 + lax.erf(x * (2.0 ** -0.5)))


def _peer_dense_body(z_ref, u_ref, v_ref, thr_ref, s2_ref, e2_ref, cf_ref, zero_ref, o_ref, w_ref):
    nk = PEER_NKEYS
    eb = pl.program_id(1)
    te = u_ref.shape[1]
    n_i = te // nk
    strip = PEER_ROW_STRIP
    stages = PEER_STAGES
    sub = min(PEER_TOK_SUB, z_ref.shape[1])
    d = z_ref.shape[0]
    kc = d // stages
    tiles = [slice(t0, t0 + sub) for t0 in range(0, z_ref.shape[1], sub)]

    @pl.when(eb == 0)
    def _():
        o_ref[...] = jnp.zeros_like(o_ref)

    def first_matmul(ts, after=()):
        acc = None
        for n, k0 in enumerate(range(0, d, kc)):
            zk = z_ref[k0:k0 + kc, ts]
            if after:
                zk = zk + _zero_after(after[-1], zero_ref[:, ts])
            part = _dot(u_ref[0, :, k0:k0 + kc], zk)
            acc = part if acc is None else acc + part
            yield
        return acc

    def gates(ts, done):
        per_stage = (nk // strip) * PEER_HEADS // stages
        n = 0
        for si in range(nk // strip):
            js = slice(si * strip, (si + 1) * strip)
            w = [None] * n_i
            for h in range(PEER_HEADS):
                s2s = s2_ref[h, js, ts]
                e2s = e2_ref[h, js, ts]
                for ii in range(n_i):
                    i = eb * n_i + ii
                    c = jnp.where(s2s >= thr_ref[h, pl.ds(i, 1), ts], e2s, 0.0) * cf_ref[h, pl.ds(i, 1), ts]
                    w[ii] = c if w[ii] is None else w[ii] + c
                n += 1
                if n % per_stage == 0 and h + 1 < PEER_HEADS:
                    yield
            for ii in range(n_i):
                w_ref[ii * nk + si * strip:ii * nk + (si + 1) * strip, ts] = w[ii].astype(BF16)
            done.append(sum(x[r:r + SUBLANES] for x in w for r in range(0, strip, SUBLANES)))
            yield

    def activate(ts, act):
        rows = te // stages
        for r0 in range(0, te, rows):
            w_ref[r0:r0 + rows, ts] = w_ref[r0:r0 + rows, ts] * _gelu(act[r0:r0 + rows]).astype(BF16)
            yield

    def second_matmul(ts, after=()):
        rows = d // stages
        for n, r0 in enumerate(range(0, d, rows)):
            g = w_ref[:, ts]
            if n < len(after):
                g = g + _zero_after(after[n], zero_ref[:, ts])
            o_ref[r0:r0 + rows, ts] += _dot(v_ref[0, r0:r0 + rows, :], g)
            yield

    acts = {}
    done_all = []
    for ph in range(len(tiles) + 2):
        gens, tags = [], []
        if ph < len(tiles):
            gens += [first_matmul(tiles[ph], done_all), gates(tiles[ph], done_all)]
            tags += [ph, None]
        if 0 <= ph - 1 < len(tiles):
            gens.append(activate(tiles[ph - 1], acts[ph - 1]))
            tags.append(None)
        if 0 <= ph - 2 < len(tiles):
            gens.append(second_matmul(tiles[ph - 2]))
            tags.append(None)
        for tag, res in zip(tags, _lockstep(gens)):
            if tag is not None:
                acts[tag] = res


def peer_ffn(zt, w_q, sub_keys, u_all, vt_all, layer):
    d, m = zt.shape
    nh, nk = PEER_HEADS, PEER_NKEYS
    qt = matmul(w_q.T.astype(BF16), zt, F32)
    tt = _pick(m, (256, 128))
    sh = jax.ShapeDtypeStruct((nh, nk, m), F32)
    blk = pl.BlockSpec((nh, nk, tt), lambda i: (0, 0, i))
    thr, s2, e2, cf = pl.pallas_call(
        _peer_score_body, grid=(m // tt,),
        in_specs=[pl.BlockSpec((nh * 2 * nk, tt), lambda i: (0, i)),
                  pl.BlockSpec((2, nk, PEER_DKEY // 2), lambda i: (0, 0, 0))],
        out_specs=[blk, blk, blk, blk], out_shape=[sh, sh, sh, sh],
        compiler_params=_cparams("parallel"),
    )(qt, sub_keys.astype(BF16))

    tm = _pick(m, (2 * PEER_TOK_SUB, PEER_TOK_SUB, LANES))
    te = PEER_EXP_BLOCK
    ne = u_all.shape[1]
    once = pl.Buffered(1)
    sblk = pl.BlockSpec((nh, nk, tm), lambda i, e: (0, 0, i), pipeline_mode=once)
    return pl.pallas_call(
        _peer_dense_body, grid=(m // tm, ne // te),
        in_specs=[pl.BlockSpec((d, tm), lambda i, e: (0, i), pipeline_mode=once),
                  pl.BlockSpec((1, te, d), lambda i, e: (layer, e, 0)),
                  pl.BlockSpec((1, d, te), lambda i, e: (layer, 0, e)),
                  sblk, sblk, sblk, sblk,
                  pl.BlockSpec((SUBLANES, tm), lambda i, e: (0, 0))],
        out_specs=pl.BlockSpec((d, tm), lambda i, e: (0, i)),
        out_shape=jax.ShapeDtypeStruct((d, m), F32),
        scratch_shapes=[pltpu.VMEM((te, tm), BF16)],
        compiler_params=_cparams("parallel", "arbitrary"),
    )(zt, u_all, vt_all, thr, s2, e2, cf, jnp.zeros((SUBLANES, tm), jnp.uint32))


def _ada_mods(c, c_ctx, w_down, w_up, b_up):
    bsz, d = c.shape
    cond = jnp.concatenate([c, c_ctx[None]], 0)
    pad = 16 - cond.shape[0] % 16
    cond = jnp.concatenate([cond, jnp.zeros((pad, d), F32)], 0)
    hid = matmul(jax.nn.silu(cond), w_down.astype(BF16), F32)
    m = (matmul(hid, w_up.astype(BF16), F32) + b_up)[:bsz + 1].reshape(bsz + 1, N_MOD, d)
    return jnp.stack([jnp.broadcast_to(m[bsz], (bsz, N_MOD, d)), m[:bsz]], axis=1)


def kernel(x, c, ctx, c_ctx, ada_w_down, ada_w_up, ada_b,
           gla_w_in, gla_w_g1, gla_w_g2, gla_b_g, gla_norm_g, gla_w_o,
           mla_w_in, mla_q_norm, mla_kv_norm, mla_w_uq, mla_w_ukv, mla_w_o,
           rwkv_mu, rwkv_w_r, rwkv_w_k, rwkv_w_v, rwkv_w_o, rwkv_w0, rwkv_w1, rwkv_w2,
           rwkv_a0, rwkv_a1, rwkv_a2, rwkv_g1, rwkv_g2, rwkv_k_k, rwkv_k_a, rwkv_r_k, rwkv_ln_g, rwkv_ln_b,
           peer_w_q, peer_sub_keys, peer_u, peer_v):
    z = jnp.concatenate([ctx, x], axis=1)
    bsz, t, d = z.shape
    mods = [_ada_mods(c, c_ctx, ada_w_down[i], ada_w_up[i], ada_b[i]) for i in range(DEPTH)]
    peer_ub = peer_u.astype(BF16)
    peer_vtb = jnp.swapaxes(peer_v, 1, 2).astype(BF16)
    _, u = ln_mod(z, mods_n=mods[0], sidx=0)
    for i in range(DEPTH):
        j = i // N_MIXERS
        if i % N_MIXERS == 0:
            y = gla_mixer(u, gla_w_in[j], gla_w_g1[j], gla_w_g2[j], gla_b_g[j], gla_norm_g[j], gla_w_o[j])
        elif i % N_MIXERS == 1:
            y = mla_mixer(u, mla_w_in[j], mla_q_norm[j], mla_kv_norm[j], mla_w_uq[j], mla_w_ukv[j], mla_w_o[j])
        else:
            y = rwkv_mixer(u, rwkv_mu[j], rwkv_w_r[j], rwkv_w_k[j], rwkv_w_v[j], rwkv_w_o[j],
                           rwkv_w0[j], rwkv_w1[j], rwkv_w2[j], rwkv_a0[j], rwkv_a1[j], rwkv_a2[j],
                           rwkv_g1[j], rwkv_g2[j], rwkv_k_k[j], rwkv_k_a[j], rwkv_r_k[j],
                           rwkv_ln_g[j], rwkv_ln_b[j])
        z, ut = ln_mod(z, y, mods[i], 2, mods[i], 3, u_t=True)
        ht = peer_ffn(ut, peer_w_q[i], peer_sub_keys[i], peer_ub, peer_vtb, i)
        if i + 1 < DEPTH:
            z, u = ln_mod(z, ht, mods[i], 5, mods[i + 1], 0, y_t=True)
        else:
            z, _ = ln_mod(z, ht, mods[i], 5, y_t=True, latent_only=True)
    return z
```
